```python
import math
import jax, jax.numpy as jnp
from jax import lax
import numpy as np

D_MODEL = 2048
BATCH = 2
SEQ = 8192
DEPTH = 1
DEC_BATCH = 128
DEC_SEQ = 4
PAST_LEN = 16384
PAGE_SIZE = 128

D_MIX = D_MODEL
D_RNN = D_MIX // 2
D_ATTN = D_MIX - D_RNN
HEAD_DIM = 64
N_HEADS = D_ATTN // HEAD_DIM
KV_HEADS = 4
GQA_GROUP = N_HEADS // KV_HEADS
KV_DIM = KV_HEADS * HEAD_DIM
RNN_BLOCKS = 16
RNN_BLOCK = D_RNN // RNN_BLOCKS
CONV_W = 4
RG_C = 8.0
WINDOW = 128
BLOCK = WINDOW
CACHE_WIN = min(WINDOW, PAST_LEN)
N_BUCKETS = 32
MAX_EXACT = N_BUCKETS // 2
REL_MAX_DIST = 128
D_FF = 3 * D_MODEL
FFN_CONV_W = 3
EPS = 1e-6
NEG_INF = -1e30
ATTN_SCALE = HEAD_DIM ** -0.5
D_IN_PROJ = 2 * D_RNN + D_ATTN + 2 * KV_DIM
SPLIT_IDX = (D_RNN, 2 * D_RNN, 2 * D_RNN + D_ATTN, 2 * D_RNN + D_ATTN + KV_DIM)

kernel_name = "hymba_rglru_swa_sink_convffn_step"


def _rmsnorm(x, g):
    xf = x.astype(jnp.float32)
    y = xf * lax.rsqrt(jnp.mean(xf * xf, axis=-1, keepdims=True) + EPS)
    return (y * g.astype(jnp.float32)).astype(x.dtype)


def _causal_dwconv(x, buf, w, b):
    K = w.shape[0]
    T = x.shape[1]
    xp = jnp.concatenate([buf.astype(x.dtype), x], axis=1)
    y = b + sum(xp[:, k:k + T] * w[k] for k in range(K))
    return y, xp[:, xp.shape[1] - (K - 1):]


def _rglru(x, h0, w_a, b_a, w_x, b_x, lam):
    B, T, C = x.shape
    xb = x.reshape(B, T, RNN_BLOCKS, RNN_BLOCK)
    r = jax.nn.sigmoid((jnp.einsum("bthi,hij->bthj", xb, w_a).reshape(B, T, C) + b_a).astype(jnp.float32))
    i = jax.nn.sigmoid((jnp.einsum("bthi,hij->bthj", xb, w_x).reshape(B, T, C) + b_x).astype(jnp.float32))
    log_a = -RG_C * r * jax.nn.softplus(-lam.astype(jnp.float32))
    a = jnp.exp(log_a)
    u = jnp.sqrt(-jnp.expm1(2.0 * log_a)) * (i * x.astype(jnp.float32))

    def step(h, au):
        a_t, u_t = au
        h = a_t * h + u_t
        return h, h

    hT, hs = lax.scan(step, h0.astype(jnp.float32), (jnp.swapaxes(a, 0, 1), jnp.swapaxes(u, 0, 1)))
    return jnp.swapaxes(hs, 0, 1).astype(x.dtype), hT.astype(x.dtype)


def _t5_bucket(d):
    n = jnp.maximum(d, 0)
    nf = jnp.maximum(n, 1).astype(jnp.float32)
    large = MAX_EXACT + (jnp.log(nf / MAX_EXACT) / math.log(REL_MAX_DIST / MAX_EXACT)
                         * (N_BUCKETS - MAX_EXACT)).astype(jnp.int32)
    large = jnp.minimum(large, N_BUCKETS - 1)
    return jnp.where(n < MAX_EXACT, n, large)


def _rel_bias(dist, table):
    b = table[_t5_bucket(dist)].astype(jnp.float32)
    b = jnp.moveaxis(b, -1, 0)
    return b.reshape(KV_HEADS, GQA_GROUP, dist.shape[0], dist.shape[1])


def _attend(q, k, v, bias, valid, sinks):
    lead = q.shape[:-3]
    Tq = q.shape[-3]
    qg = q.reshape(*lead, Tq, KV_HEADS, GQA_GROUP, HEAD_DIM)
    s = jnp.einsum("...qkgd,...skd->...kgqs", qg, k).astype(jnp.float32) * ATTN_SCALE + bias
    s = jnp.where(valid, s, NEG_INF)
    sink = sinks.astype(jnp.float32).reshape(KV_HEADS, GQA_GROUP, 1, 1)
    m = jnp.maximum(jnp.max(s, axis=-1, keepdims=True), sink)
    p = jnp.exp(s - m)
    w = p / (jnp.sum(p, axis=-1, keepdims=True) + jnp.exp(sink - m))
    o = jnp.einsum("...kgqs,...skd->...qkgd", w.astype(v.dtype), v)
    return o.reshape(*lead, Tq, N_HEADS, HEAD_DIM)


def _swa_prompt(q, k, v, table, sinks):
    B, S = q.shape[0], q.shape[1]
    nb = S // BLOCK
    qb = q.reshape(B, nb, BLOCK, N_HEADS, HEAD_DIM)
    pad = jnp.zeros((B, BLOCK, KV_HEADS, HEAD_DIM), k.dtype)
    kprev = jnp.concatenate([pad, k[:, :S - BLOCK]], axis=1).reshape(B, nb, BLOCK, KV_HEADS, HEAD_DIM)
    vprev = jnp.concatenate([pad, v[:, :S - BLOCK]], axis=1).reshape(B, nb, BLOCK, KV_HEADS, HEAD_DIM)
    kband = jnp.concatenate([kprev, k.reshape(B, nb, BLOCK, KV_HEADS, HEAD_DIM)], axis=2)
    vband = jnp.concatenate([vprev, v.reshape(B, nb, BLOCK, KV_HEADS, HEAD_DIM)], axis=2)
    qi = jnp.arange(BLOCK)[:, None]
    kj = jnp.arange(2 * BLOCK)[None, :]
    dist = BLOCK + qi - kj
    k_abs = jnp.arange(nb)[:, None, None] * BLOCK - BLOCK + kj
    valid = (dist >= 0) & (dist < WINDOW) & (k_abs >= 0)
    valid = valid[:, None, None]
    out = _attend(qb, kband, vband, _rel_bias(dist, table), valid, sinks)
    return out.reshape(B, S, N_HEADS, HEAD_DIM), k[:, S - CACHE_WIN:], v[:, S - CACHE_WIN:]


def _swa_sample(q, k, v, k_buf, v_buf, table, sinks):
    T = q.shape[1]
    W = k_buf.shape[1]
    kk = jnp.concatenate([k_buf.astype(k.dtype), k], axis=1)
    vv = jnp.concatenate([v_buf.astype(v.dtype), v], axis=1)
    qi = jnp.arange(T)[:, None]
    kj = jnp.arange(W + T)[None, :]
    dist = W + qi - kj
    valid = (dist >= 0) & (dist < WINDOW)
    out = _attend(q, kk, vv, _rel_bias(dist, table), valid, sinks)
    return out, kk[:, kk.shape[1] - W:], vv[:, vv.shape[1] - W:]


def _layer(x, rnn_conv_buf, rnn_h0, k_buf, v_buf, ffn_buf, lw, rel_bias_table, prompt):
    B, T, _ = x.shape
    xn = _rmsnorm(x, lw["norm_mix_g"])
    proj = jnp.einsum("btd,de->bte", xn, lw["w_in"])
    x_rnn, g_rnn, q, k, v = jnp.split(proj, SPLIT_IDX, axis=-1)
    xc, new_rnn_conv = _causal_dwconv(x_rnn, rnn_conv_buf, lw["rnn_conv_w"], lw["rnn_conv_b"])
    h_rnn, new_h = _rglru(xc, rnn_h0, lw["w_gate_a"], lw["b_gate_a"], lw["w_gate_x"], lw["b_gate_x"], lw["rnn_lambda"])
    y_rnn = jax.nn.gelu(g_rnn) * h_rnn
    q = q.reshape(B, T, N_HEADS, HEAD_DIM)
    k = k.reshape(B, T, KV_HEADS, HEAD_DIM)
    v = v.reshape(B, T, KV_HEADS, HEAD_DIM)
    if prompt:
        y_attn, new_k, new_v = _swa_prompt(q, k, v, rel_bias_table, lw["attn_sinks"])
    else:
        y_attn, new_k, new_v = _swa_sample(q, k, v, k_buf, v_buf, rel_bias_table, lw["attn_sinks"])
    merged = jnp.concatenate([_rmsnorm(y_rnn, lw["gn_rnn_g"]),
                              _rmsnorm(y_attn.reshape(B, T, D_ATTN), lw["gn_attn_g"])], axis=-1)
    h = x + jnp.einsum("bte,ed->btd", merged, lw["w_out"])
    hn = _rmsnorm(h, lw["norm_ffn_g"])
    up = jnp.einsum("btd,df->btf", hn, lw["w_up"])
    upc, new_ffn = _causal_dwconv(up, ffn_buf, lw["ffn_conv_w"], lw["ffn_conv_b"])
    gate, val = jnp.split(upc, 2, axis=-1)
    out = h + jnp.einsum("btf,fd->btd", jax.nn.gelu(gate) * val, lw["w_down"])
    return out, (new_rnn_conv, new_h, new_k, new_v, new_ffn)


def setup_inputs(seed: int = 0) -> dict:
    key = jax.random.key(seed)
    ks = jax.random.split(key, 32)
    f = jnp.float32
    nrm = lambda k, shape, s: jax.random.normal(k, shape, f) * s
    a0 = jax.random.uniform(ks[14], (DEPTH, D_RNN), f, minval=0.9, maxval=0.999)
    return {
        "x_prompt": nrm(ks[0], (BATCH, SEQ, D_MODEL), 1.0),
        "x_sample": nrm(ks[1], (DEC_BATCH, DEC_SEQ, D_MODEL), 1.0),
        "state_rnn_conv": nrm(ks[2], (DEPTH, DEC_BATCH, CONV_W - 1, D_RNN), 1.0),
        "state_rnn_h": nrm(ks[3], (DEPTH, DEC_BATCH, D_RNN), 0.5),
        "cache_win_k": nrm(ks[4], (DEPTH, DEC_BATCH, CACHE_WIN, KV_HEADS, HEAD_DIM), 1.0),
        "cache_win_v": nrm(ks[5], (DEPTH, DEC_BATCH, CACHE_WIN, KV_HEADS, HEAD_DIM), 1.0),
        "state_ffn_conv": nrm(ks[6], (DEPTH, DEC_BATCH, FFN_CONV_W - 1, 2 * D_FF), 1.0),
        "norm_mix_g": 1.0 + nrm(ks[7], (DEPTH, D_MODEL), 0.02),
        "w_in": nrm(ks[8], (DEPTH, D_MODEL, D_IN_PROJ), D_MODEL ** -0.5),
        "rnn_conv_w": nrm(ks[9], (DEPTH, CONV_W, D_RNN), CONV_W ** -0.5),
        "rnn_conv_b": nrm(ks[10], (DEPTH, D_RNN), 0.02),
        "w_gate_a": nrm(ks[11], (DEPTH, RNN_BLOCKS, RNN_BLOCK, RNN_BLOCK), RNN_BLOCK ** -0.5),
        "b_gate_a": nrm(ks[12], (DEPTH, D_RNN), 0.02),
        "w_gate_x": nrm(ks[13], (DEPTH, RNN_BLOCKS, RNN_BLOCK, RNN_BLOCK), RNN_BLOCK ** -0.5),
        "b_gate_x": nrm(ks[15], (DEPTH, D_RNN), 0.02),
        "rnn_lambda": jnp.log(a0) - jnp.log1p(-a0),
        "attn_sinks": nrm(ks[16], (DEPTH, N_HEADS), 0.5),
        "rel_bias_table": nrm(ks[17], (N_BUCKETS, N_HEADS), 0.5),
        "gn_rnn_g": 1.0 + nrm(ks[18], (DEPTH, D_RNN), 0.02),
        "gn_attn_g": 1.0 + nrm(ks[19], (DEPTH, D_ATTN), 0.02),
        "w_out": nrm(ks[20], (DEPTH, D_MIX, D_MODEL), D_MIX ** -0.5),
        "norm_ffn_g": 1.0 + nrm(ks[21], (DEPTH, D_MODEL), 0.02),
        "w_up": nrm(ks[22], (DEPTH, D_MODEL, 2 * D_FF), D_MODEL ** -0.5),
        "ffn_conv_w": nrm(ks[23], (DEPTH, FFN_CONV_W, 2 * D_FF), FFN_CONV_W ** -0.5),
        "ffn_conv_b": nrm(ks[24], (DEPTH, 2 * D_FF), 0.02),
        "w_down": nrm(ks[25], (DEPTH, D_FF, D_MODEL), D_FF ** -0.5),
        "norm_final_g": 1.0 + nrm(ks[26], (D_MODEL,), 0.02),
    }


def reference(x_prompt, x_sample, state_rnn_conv, state_rnn_h, cache_win_k, cache_win_v, state_ffn_conv,
              norm_mix_g, w_in, rnn_conv_w, rnn_conv_b, w_gate_a, b_gate_a, w_gate_x, b_gate_x, rnn_lambda,
              attn_sinks, rel_bias_table, gn_rnn_g, gn_attn_g, w_out, norm_ffn_g, w_up, ffn_conv_w, ffn_conv_b,
              w_down, norm_final_g):
    xp, xs = x_prompt, x_sample
    Bp = x_prompt.shape[0]
    dt = x_prompt.dtype
    p_states, s_states = [], []
    for l in range(DEPTH):
        lw = {
            "norm_mix_g": norm_mix_g[l], "w_in": w_in[l], "rnn_conv_w": rnn_conv_w[l], "rnn_conv_b": rnn_conv_b[l],
            "w_gate_a": w_gate_a[l], "b_gate_a": b_gate_a[l], "w_gate_x": w_gate_x[l], "b_gate_x": b_gate_x[l],
            "rnn_lambda": rnn_lambda[l], "attn_sinks": attn_sinks[l], "gn_rnn_g": gn_rnn_g[l],
            "gn_attn_g": gn_attn_g[l], "w_out": w_out[l], "norm_ffn_g": norm_ffn_g[l], "w_up": w_up[l],
            "ffn_conv_w": ffn_conv_w[l], "ffn_conv_b": ffn_conv_b[l], "w_down": w_down[l],
        }
        xp, st_p = _layer(xp,
                          jnp.zeros((Bp, CONV_W - 1, D_RNN), dt),
                          jnp.zeros((Bp, D_RNN), dt),
                          None, None,
                          jnp.zeros((Bp, FFN_CONV_W - 1, 2 * D_FF), dt),
                          lw, rel_bias_table, True)
        xs, st_s = _layer(xs, state_rnn_conv[l], state_rnn_h[l], cache_win_k[l], cache_win_v[l],
                          state_ffn_conv[l], lw, rel_bias_table, False)
        p_states.append(st_p)
        s_states.append(st_s)
    y_prompt = _rmsnorm(xp, norm_final_g)
    y_sample = _rmsnorm(xs, norm_final_g)
    ps = [jnp.stack(z, axis=0) for z in zip(*p_states)]
    ss = [jnp.stack(z, axis=0) for z in zip(*s_states)]
    return (y_prompt, y_sample, ps[0], ps[1], ps[2], ps[3], ps[4], ss[0], ss[1], ss[2], ss[3], ss[4])
```

```python
import functools
import math

import numpy as np
import jax
import jax.numpy as jnp
from jax import lax
from jax.experimental import pallas as pl
from jax.experimental.pallas import tpu as pltpu

F32 = jnp.float32
BF16 = jnp.bfloat16

HEAD_DIM = 64
KV_HEADS = 4
N_HEADS = 16
GQA_GROUP = N_HEADS // KV_HEADS
RNN_BLOCKS = 16
RG_C = 8.0
WINDOW = 128
N_BUCKETS = 32
MAX_EXACT = N_BUCKETS // 2
REL_MAX_DIST = 128
EPS = 1e-6
NEG_INF = -1e30
ATTN_SCALE = HEAD_DIM ** -0.5

LANES = 128
SUBLANES = 8
VMEM_LIMIT_BYTES = 56 * 1024 * 1024

GATE_TILE = 256


def _t5_bucket_np(d):
    n = np.maximum(d, 0)
    nf = np.maximum(n, 1).astype(np.float32)
    large = MAX_EXACT + (np.log(nf / MAX_EXACT) / math.log(REL_MAX_DIST / MAX_EXACT)
                         * (N_BUCKETS - MAX_EXACT)).astype(np.int32)
    large = np.minimum(large, N_BUCKETS - 1)
    return np.where(n < MAX_EXACT, n, large).astype(np.int32)


def _rms(x, g):
    ms = jnp.mean(x * x, axis=-1, keepdims=True)
    return (x * lax.rsqrt(ms + EPS)) * g


def _softplus(x):
    return jnp.maximum(x, 0.0) + jnp.log1p(jnp.exp(-jnp.abs(x)))


def _const_spec(shape):
    nd = len(shape)
    return pl.BlockSpec(shape, lambda *_: (0,) * nd, pipeline_mode=pl.Buffered(1))


def _smem_spec():
    return pl.BlockSpec(memory_space=pltpu.SMEM)


def _inproj_body(x_ref, g_ref, w_ref, xr_ref, gr_ref, q_ref, k_ref, v_ref, *, d_rnn, d_attn, kv_dim):
    xn = _rms(x_ref[...], g_ref[...])
    p = jnp.dot(xn.astype(BF16), w_ref[...], preferred_element_type=F32)
    o1, o2, o3, o4 = d_rnn, 2 * d_rnn, 2 * d_rnn + d_attn, 2 * d_rnn + d_attn + kv_dim
    xr_ref[...] = p[:, :o1]
    gr_ref[...] = p[:, o1:o2]
    q_ref[...] = p[:, o2:o3].astype(BF16)
    k_ref[...] = p[:, o3:o4]
    v_ref[...] = p[:, o4:]


def _inproj(x2d, x_spec, n_steps, tm, g, w, d_rnn, d_attn, kv_dim):
    n = n_steps * tm
    d_model = g.shape[-1]
    row = lambda i: (i, 0)
    out_shape = (
        jax.ShapeDtypeStruct((n, d_rnn), F32),
        jax.ShapeDtypeStruct((n, d_rnn), F32),
        jax.ShapeDtypeStruct((n, d_attn), BF16),
        jax.ShapeDtypeStruct((n, kv_dim), F32),
        jax.ShapeDtypeStruct((n, kv_dim), F32),
    )
    out_specs = (
        pl.BlockSpec((tm, d_rnn), row),
        pl.BlockSpec((tm, d_rnn), row),
        pl.BlockSpec((tm, d_attn), row),
        pl.BlockSpec((tm, kv_dim), row),
        pl.BlockSpec((tm, kv_dim), row),
    )
    return pl.pallas_call(
        functools.partial(_inproj_body, d_rnn=d_rnn, d_attn=d_attn, kv_dim=kv_dim),
        grid=(n_steps,),
        in_specs=[x_spec, _const_spec((1, d_model)), _const_spec(w.shape)],
        out_specs=out_specs,
        out_shape=out_shape,
        compiler_params=pltpu.CompilerParams(
            dimension_semantics=("arbitrary",), vmem_limit_bytes=VMEM_LIMIT_BYTES),
        name="inproj",
    )(x2d, g, w)


def _rglru_gates(xc, wg_ref, ba, bx, lam):
    d_rnn = xc.shape[-1]
    pre_a, pre_x = [], []
    for c in range(d_rnn // GATE_TILE):
        xb = xc[:, c * GATE_TILE:(c + 1) * GATE_TILE].astype(BF16)
        pre = jnp.dot(xb, wg_ref[c], preferred_element_type=F32)
        pre_a.append(pre[:, :GATE_TILE])
        pre_x.append(pre[:, GATE_TILE:])
    r = jax.nn.sigmoid(jnp.concatenate(pre_a, axis=1) + ba)
    i = jax.nn.sigmoid(jnp.concatenate(pre_x, axis=1) + bx)
    log_a = (-RG_C * r) * _softplus(-lam)
    a = jnp.exp(log_a)
    one_minus_a2 = -jnp.tanh(log_a) * (a * a + 1.0)
    u = jnp.sqrt(one_minus_a2) * (i * xc)
    return a, u


def _rglru_prompt_body(xr_ref, gr_ref, cw_ref, cb_ref, wg_ref, ba_ref, bx_ref, lam_ref, gn_ref,
                       out_ref, hlast_ref, xbuf, abuf, ubuf, hbuf, hc_ref, *, tt, conv_w):
    ti = pl.program_id(1)
    d_rnn = xr_ref.shape[-1]

    @pl.when(ti == 0)
    def _():
        xbuf[0:SUBLANES, :] = jnp.zeros((SUBLANES, d_rnn), F32)
        hc_ref[...] = jnp.zeros_like(hc_ref)

    x = xr_ref[...]
    xbuf[SUBLANES:SUBLANES + tt, :] = x
    xc = cw_ref[conv_w - 1:conv_w, :] * x
    for k in range(conv_w - 1):
        off = SUBLANES - (conv_w - 1) + k
        xc = xc + cw_ref[k:k + 1, :] * xbuf[off:off + tt, :]
    xc = xc + cb_ref[...]
    xbuf[0:SUBLANES, :] = xbuf[tt:tt + SUBLANES, :]

    a, u = _rglru_gates(xc, wg_ref, ba_ref[...], bx_ref[...], lam_ref[...])
    abuf[...] = a
    ubuf[...] = u

    row = lax.broadcasted_iota(jnp.int32, (SUBLANES, d_rnn), 0)

    def group(gidx, hc):
        r0 = pl.multiple_of(gidx * SUBLANES, SUBLANES)
        ag = abuf[pl.ds(r0, SUBLANES), :]
        ug = ubuf[pl.ds(r0, SUBLANES), :]
        for k in (1, 2, 4):
            a_prev = jnp.where(row >= k, pltpu.roll(ag, k, 0), 1.0)
            u_prev = jnp.where(row >= k, pltpu.roll(ug, k, 0), 0.0)
            ug = ag * u_prev + ug
            ag = ag * a_prev
        h = ag * hc + ug
        hbuf[pl.ds(r0, SUBLANES), :] = h
        return jnp.broadcast_to(h[SUBLANES - 1:SUBLANES, :], (SUBLANES, d_rnn))

    hc = lax.fori_loop(0, tt // SUBLANES, group, hc_ref[...], unroll=2)
    hc_ref[...] = hc
    hlast_ref[0] = hc

    y = jax.nn.gelu(gr_ref[...]) * hbuf[...]
    out_ref[...] = _rms(y, gn_ref[...]).astype(BF16)


def _rglru_prompt(xr, gr, cw, cb, wg, ba, bx, lam, gn, batch, seq, tt):
    d_rnn = xr.shape[-1]
    nt = seq // tt
    conv_w = cw.shape[0]
    tile = lambda b, t: (b * nt + t, 0)
    out, hlast = pl.pallas_call(
        functools.partial(_rglru_prompt_body, tt=tt, conv_w=conv_w),
        grid=(batch, nt),
        in_specs=[
            pl.BlockSpec((tt, d_rnn), tile),
            pl.BlockSpec((tt, d_rnn), tile),
            _const_spec(cw.shape), _const_spec(cb.shape), _const_spec(wg.shape),
            _const_spec(ba.shape), _const_spec(bx.shape), _const_spec(lam.shape), _const_spec(gn.shape),
        ],
        out_specs=(
            pl.BlockSpec((tt, d_rnn), tile),
            pl.BlockSpec((1, SUBLANES, d_rnn), lambda b, t: (b, 0, 0)),
        ),
        out_shape=(
            jax.ShapeDtypeStruct((batch * seq, d_rnn), BF16),
            jax.ShapeDtypeStruct((batch, SUBLANES, d_rnn), F32),
        ),
        scratch_shapes=[
            pltpu.VMEM((tt + SUBLANES, d_rnn), F32),
            pltpu.VMEM((tt, d_rnn), F32),
            pltpu.VMEM((tt, d_rnn), F32),
            pltpu.VMEM((tt, d_rnn), F32),
            pltpu.VMEM((SUBLANES, d_rnn), F32),
        ],
        compiler_params=pltpu.CompilerParams(
            dimension_semantics=("arbitrary", "arbitrary"), vmem_limit_bytes=VMEM_LIMIT_BYTES),
        name="rglru_prompt",
    )(xr, gr, cw, cb, wg, ba, bx, lam, gn)
    return out, hlast[:, 0, :]


def _rglru_sample_body(xr_ref, gr_ref, c0_ref, c1_ref, c2_ref, h0_ref, cw_ref, cb_ref, wg_ref,
                       ba_ref, bx_ref, lam_ref, gn_ref, out_ref, hnew_ref, *, nb, steps):
    x = xr_ref[...]
    hist = [c0_ref[...], c1_ref[...], c2_ref[...]]
    xs = [x[t * nb:(t + 1) * nb, :] for t in range(steps)]
    xp = hist + xs
    conv_w = len(hist) + 1
    xc = cw_ref[conv_w - 1:conv_w, :] * x
    for k in range(conv_w - 1):
        shifted = jnp.concatenate(xp[k:k + steps], axis=0)
        xc = xc + cw_ref[k:k + 1, :] * shifted
    xc = xc + cb_ref[...]
    a, u = _rglru_gates(xc, wg_ref, ba_ref[...], bx_ref[...], lam_ref[...])
    h = h0_ref[...]
    hs = []
    for t in range(steps):
        h = a[t * nb:(t + 1) * nb, :] * h + u[t * nb:(t + 1) * nb, :]
        hs.append(h)
    hnew_ref[...] = h
    y = jax.nn.gelu(gr_ref[...]) * jnp.concatenate(hs, axis=0)
    out_ref[...] = _rms(y, gn_ref[...]).astype(BF16)


def _rglru_sample(xr, gr, conv_state2d, h0, cw, cb, wg, ba, bx, lam, gn, nb, steps):
    d_rnn = xr.shape[-1]
    n = nb * steps
    assert cw.shape[0] == 4
    full = lambda shape: pl.BlockSpec(shape, lambda i: (0,) * len(shape))
    return pl.pallas_call(
        functools.partial(_rglru_sample_body, nb=nb, steps=steps),
        grid=(1,),
        in_specs=[
            full((n, d_rnn)), full((n, d_rnn)),
            pl.BlockSpec((nb, d_rnn), lambda i: (0, 0)),
            pl.BlockSpec((nb, d_rnn), lambda i: (0, 1)),
            pl.BlockSpec((nb, d_rnn), lambda i: (0, 2)),
            full((nb, d_rnn)),
            full(cw.shape), full(cb.shape), full(wg.shape), full(ba.shape), full(bx.shape),
            full(lam.shape), full(gn.shape),
        ],
        out_specs=(full((n, d_rnn)), full((nb, d_rnn))),
        out_shape=(jax.ShapeDtypeStruct((n, d_rnn), BF16), jax.ShapeDtypeStruct((nb, d_rnn), F32)),
        compiler_params=pltpu.CompilerParams(
            dimension_semantics=("arbitrary",), vmem_limit_bytes=VMEM_LIMIT_BYTES),
        name="rglru_sample",
    )(xr, gr, conv_state2d, conv_state2d, conv_state2d, h0, cw, cb, wg, ba, bx, lam, gn)


def _swa_prompt_body(code_ref, tbl_ref, sink_ref, q_ref, kp_ref, kc_ref, vp_ref, vc_ref, gn_ref,
                     out_ref, bias_ref):
    j = pl.program_id(1)
    blk = q_ref.shape[0]
    rows = GQA_GROUP * blk

    @pl.when((pl.program_id(0) == 0) & (j == 0))
    def _():
        code = code_ref[...]
        for h in range(N_HEADS):
            def pick(b, acc, h=h):
                return jnp.where(code == b, tbl_ref[b * N_HEADS + h], acc)
            bias_h = lax.fori_loop(0, N_BUCKETS, pick, jnp.zeros(code.shape, F32))
            kv, g = divmod(h, GQA_GROUP)
            bias_ref[kv, g * blk:(g + 1) * blk, :] = bias_h

    qi = lax.broadcasted_iota(jnp.int32, (rows, 2 * blk), 0) & (blk - 1)
    kj = lax.broadcasted_iota(jnp.int32, (rows, 2 * blk), 1)
    dist = blk + qi - kj
    first_key = jnp.where(j > 0, 0, blk)
    dist = jnp.where(kj >= first_key, dist, -1)
    valid = (dist & -WINDOW) == 0

    kband = jnp.concatenate([kp_ref[...], kc_ref[...]], axis=0)
    vband = jnp.concatenate([vp_ref[...], vc_ref[...]], axis=0)
    lane = lax.broadcasted_iota(jnp.int32, (2 * blk, LANES), 1)
    row_g = lax.broadcasted_iota(jnp.int32, (rows, 1), 0) // blk
    q = q_ref[...]

    outs = []
    for cg in range(KV_HEADS // 2):
        qs = jnp.concatenate(
            [q[:, (cg * GQA_GROUP + g) * LANES:(cg * GQA_GROUP + g + 1) * LANES] for g in range(GQA_GROUP)],
            axis=0)
        ka = kband[:, cg * LANES:(cg + 1) * LANES]
        va = vband[:, cg * LANES:(cg + 1) * LANES]
        o = None
        for par in range(2):
            kv = 2 * cg + par
            half = (lane < HEAD_DIM) if par == 0 else (lane >= HEAD_DIM)
            km = jnp.where(half, ka, 0.0).astype(BF16)
            vm = jnp.where(half, va, 0.0).astype(BF16)
            s = lax.dot_general(qs, km, (((1,), (1,)), ((), ())), preferred_element_type=F32)
            s = s * ATTN_SCALE + bias_ref[kv]
            s = jnp.where(valid, s, NEG_INF)
            sink = jnp.zeros((rows, 1), F32)
            for g in range(GQA_GROUP):
                sink = jnp.where(row_g == g, sink_ref[kv * GQA_GROUP + g], sink)
            m = jnp.maximum(jnp.max(s, axis=-1, keepdims=True), sink)
            p = jnp.exp(s - m)
            denom = jnp.sum(p, axis=-1, keepdims=True) + jnp.exp(sink - m)
            w = p * (1.0 / denom)
            part = jnp.dot(w.astype(BF16), vm, preferred_element_type=F32)
            o = part if o is None else o + part
        outs.extend(o[g * blk:(g + 1) * blk, :] for g in range(GQA_GROUP))
    y = jnp.concatenate(outs, axis=1)
    out_ref[...] = _rms(y, gn_ref[...]).astype(BF16)


def _swa_prompt(q, k, v, code, tbl, sinks, gn, batch, seq):
    blk = WINDOW
    nb = seq // blk
    d_attn = q.shape[-1]
    kv_dim = k.shape[-1]
    cur = lambda b, j: (b * nb + j, 0)
    prev = lambda b, j: (b * nb + jnp.maximum(j - 1, 0), 0)
    return pl.pallas_call(
        _swa_prompt_body,
        grid=(batch, nb),
        in_specs=[
            _const_spec(code.shape), _smem_spec(), _smem_spec(),
            pl.BlockSpec((blk, d_attn), cur),
            pl.BlockSpec((blk, kv_dim), prev), pl.BlockSpec((blk, kv_dim), cur),
            pl.BlockSpec((blk, kv_dim), prev), pl.BlockSpec((blk, kv_dim), cur),
            _const_spec(gn.shape),
        ],
        out_specs=pl.BlockSpec((blk, d_attn), cur),
        out_shape=jax.ShapeDtypeStruct((batch * seq, d_attn), BF16),
        scratch_shapes=[pltpu.VMEM((KV_HEADS, GQA_GROUP * blk, 2 * blk), F32)],
        compiler_params=pltpu.CompilerParams(
            dimension_semantics=("arbitrary", "arbitrary"), vmem_limit_bytes=VMEM_LIMIT_BYTES),
        name="swa_prompt",
    )(code, tbl, sinks, q, k, k, v, v, gn)


def _swa_sample_body(codec_ref, coden_ref, tbl_ref, sink_ref, q_ref, ck_ref, cv_ref, kn_ref, vn_ref,
                     out_ref, biasc_ref, biasn_ref, sinkc_ref, *, steps):
    bb = q_ref.shape[0]
    nrow = N_HEADS * steps
    win = ck_ref.shape[1]
    npad = kn_ref.shape[1]

    @pl.when(pl.program_id(0) == 0)
    def _():
        codec = codec_ref[...]
        coden = coden_ref[...]
        hrow = lax.broadcasted_iota(jnp.int32, (nrow, 1), 0) // steps

        def pick(idx, accs):
            ac, an = accs
            val = tbl_ref[idx]
            return jnp.where(codec == idx, val, ac), jnp.where(coden == idx, val, an)
        bc, bn = lax.fori_loop(0, N_BUCKETS * N_HEADS, pick,
                               (jnp.zeros(codec.shape, F32), jnp.zeros(coden.shape, F32)))
        biasc_ref[...] = bc
        biasn_ref[...] = bn

        def pick_sink(h, acc):
            return jnp.where(hrow == h, sink_ref[h], acc)
        sinkc_ref[...] = lax.fori_loop(0, N_HEADS, pick_sink, jnp.zeros((nrow, 1), F32))

    q = q_ref[...].astype(F32)
    lane = lax.broadcasted_iota(jnp.int32, (bb, GQA_GROUP * steps, LANES), 2)
    zeros = jnp.zeros((bb, GQA_GROUP * steps, LANES), F32)
    pieces = []
    for cg in range(KV_HEADS // 2):
        for par in range(2):
            half = (lane < HEAD_DIM) if par == 0 else (lane >= HEAD_DIM)
            qm = jnp.where(half, q[:, cg], zeros)
            pieces.append(jnp.concatenate([qm, zeros] if cg == 0 else [zeros, qm], axis=2))
    qm = jnp.concatenate(pieces, axis=1).astype(BF16)

    ck = ck_ref[...].astype(BF16)
    kn = kn_ref[...].astype(BF16)
    s_c = jnp.einsum("bqd,bkd->bqk", qm, ck, preferred_element_type=F32)
    s_n = jnp.einsum("bqd,bkd->bqk", qm, kn, preferred_element_type=F32)

    t_c = lax.broadcasted_iota(jnp.int32, (nrow, win), 0) % steps
    k_c = lax.broadcasted_iota(jnp.int32, (nrow, win), 1)
    valid_c = k_c > t_c
    t_n = lax.broadcasted_iota(jnp.int32, (nrow, npad), 0) % steps
    k_n = lax.broadcasted_iota(jnp.int32, (nrow, npad), 1)
    valid_n = k_n <= t_n

    s_c = jnp.where(valid_c[None], s_c * ATTN_SCALE + biasc_ref[...][None], NEG_INF)
    s_n = jnp.where(valid_n[None], s_n * ATTN_SCALE + biasn_ref[...][None], NEG_INF)
    sink = sinkc_ref[...][None]
    m = jnp.maximum(jnp.maximum(jnp.max(s_c, axis=-1, keepdims=True),
                                jnp.max(s_n, axis=-1, keepdims=True)), sink)
    p_c = jnp.exp(s_c - m)
    p_n = jnp.exp(s_n - m)
    denom = (jnp.sum(p_c, axis=-1, keepdims=True) + jnp.sum(p_n, axis=-1, keepdims=True)
             + jnp.exp(sink - m))
    r = 1.0 / denom
    w_c = (p_c * r).astype(BF16)
    w_n = (p_n * r).astype(BF16)
    o = (jnp.einsum("bqk,bkd->bqd", w_c, cv_ref[...].astype(BF16), preferred_element_type=F32)
         + jnp.einsum("bqk,bkd->bqd", w_n, vn_ref[...].astype(BF16), preferred_element_type=F32))
    gt = GQA_GROUP * steps
    lane_o = lax.broadcasted_iota(jnp.int32, (bb, gt, LANES), 2)
    for cg in range(KV_HEADS // 2):
        lo = o[:, cg * 2 * gt:cg * 2 * gt + gt, cg * LANES:(cg + 1) * LANES]
        hi = o[:, cg * 2 * gt + gt:(cg + 1) * 2 * gt, cg * LANES:(cg + 1) * LANES]
        out_ref[:, cg] = jnp.where(lane_o < HEAD_DIM, lo, hi)


def _swa_sample(qs, ck, cv, kn, vn, codec, coden, tbl, sinks, steps, bb):
    nbatch, ncg, gt, _ = qs.shape
    win, kv_dim = ck.shape[1], ck.shape[2]
    npad = kn.shape[1]
    nrow = N_HEADS * steps
    blk4 = lambda i: (i, 0, 0, 0)
    blk3 = lambda i: (i, 0, 0)
    return pl.pallas_call(
        functools.partial(_swa_sample_body, steps=steps),
        grid=(nbatch // bb,),
        in_specs=[
            _const_spec(codec.shape), _const_spec(coden.shape), _smem_spec(), _smem_spec(),
            pl.BlockSpec((bb, ncg, gt, LANES), blk4),
            pl.BlockSpec((bb, win, kv_dim), blk3), pl.BlockSpec((bb, win, kv_dim), blk3),
            pl.BlockSpec((bb, npad, kv_dim), blk3), pl.BlockSpec((bb, npad, kv_dim), blk3),
        ],
        out_specs=pl.BlockSpec((bb, ncg, gt, LANES), blk4),
        out_shape=jax.ShapeDtypeStruct((nbatch, ncg, gt, LANES), F32),
        scratch_shapes=[pltpu.VMEM((nrow, win), F32), pltpu.VMEM((nrow, npad), F32),
                        pltpu.VMEM((nrow, 1), F32)],
        compiler_params=pltpu.CompilerParams(
            dimension_semantics=("arbitrary",), vmem_limit_bytes=VMEM_LIMIT_BYTES),
        name="swa_sample",
    )(codec, coden, tbl, sinks, qs, ck, cv, kn, vn)


def _ffn_conv(up, hist, cw_ref, cb_ref, ubuf, *, tm, sample_nb):
    if sample_nb is None:
        ubuf[0:SUBLANES, :] = hist
        ubuf[SUBLANES:SUBLANES + tm, :] = up
        prev2 = ubuf[SUBLANES - 2:SUBLANES - 2 + tm, :]
        prev1 = ubuf[SUBLANES - 1:SUBLANES - 1 + tm, :]
    else:
        s0, s1 = hist
        prev2 = jnp.concatenate([s0, s1, up[:tm - 2 * sample_nb, :]], axis=0)
        prev1 = jnp.concatenate([s1, up[:tm - sample_nb, :]], axis=0)
    return cw_ref[0:1, :] * prev2 + cw_ref[1:2, :] * prev1 + cw_ref[2:3, :] * up + cb_ref[...]


def _ffn_body(*refs, tm, tiles_per_seq, sample_nb):
    if sample_nb is None:
        (x_ref, mr_ref, ma_ref, wo_ref, gf_ref, wg_ref, wv_ref, cwg_ref, cwv_ref, cbg_ref, cbv_ref,
         wd_ref, gfin_ref, y_ref, tg_ref, tv_ref, hn_ref, ubuf_g, ubuf_v, car_g, car_v) = refs
    else:
        (x_ref, mr_ref, ma_ref, gna_ref, wo_ref, gf_ref, wg_ref, wv_ref, cwg_ref, cwv_ref, cbg_ref,
         cbv_ref, wd_ref, gfin_ref, s0g_ref, s1g_ref, s0v_ref, s1v_ref, y_ref, tg_ref, tv_ref,
         hn_ref) = refs
    i = pl.program_id(0)
    j = pl.program_id(1)
    nj = pl.num_programs(1)

    @pl.when(j == 0)
    def _():
        if sample_nb is None:
            ma = ma_ref[...]
        else:
            ma = _rms(ma_ref[...], gna_ref[...]).astype(BF16)
        merged = jnp.concatenate([mr_ref[...], ma], axis=1)
        h = x_ref[...] + jnp.dot(merged, wo_ref[...], preferred_element_type=F32)
        y_ref[...] = h
        hn_ref[...] = _rms(h, gf_ref[...]).astype(BF16)

    hn = hn_ref[...]
    up_g = jnp.dot(hn, wg_ref[...], preferred_element_type=F32)
    up_v = jnp.dot(hn, wv_ref[...], preferred_element_type=F32)

    if sample_nb is None:
        @pl.when((i == 0) & (j == 0))
        def _():
            car_g[...] = jnp.zeros_like(car_g)
            car_v[...] = jnp.zeros_like(car_v)

        seq_start = i % tiles_per_seq == 0
        hist_g = jnp.where(seq_start, 0.0, car_g[j])
        hist_v = jnp.where(seq_start, 0.0, car_v[j])
        c_g = _ffn_conv(up_g, hist_g, cwg_ref, cbg_ref, ubuf_g, tm=tm, sample_nb=None)
        c_v = _ffn_conv(up_v, hist_v, cwv_ref, cbv_ref, ubuf_v, tm=tm, sample_nb=None)
        tail_g = up_g[tm - SUBLANES:, :]
        tail_v = up_v[tm - SUBLANES:, :]
        car_g[j] = tail_g
        car_v[j] = tail_v
        tg_ref[0] = tail_g
        tv_ref[0] = tail_v
    else:
        c_g = _ffn_conv(up_g, (s0g_ref[...], s1g_ref[...]), cwg_ref, cbg_ref, None, tm=tm, sample_nb=sample_nb)
        c_v = _ffn_conv(up_v, (s0v_ref[...], s1v_ref[...]), cwv_ref, cbv_ref, None, tm=tm, sample_nb=sample_nb)
        tg_ref[0] = up_g[tm - 2 * sample_nb:tm - sample_nb, :]
        tg_ref[1] = up_g[tm - sample_nb:, :]
        tv_ref[0] = up_v[tm - 2 * sample_nb:tm - sample_nb, :]
        tv_ref[1] = up_v[tm - sample_nb:, :]

    act = (jax.nn.gelu(c_g) * c_v).astype(BF16)
    y_ref[...] += jnp.dot(act, wd_ref[...], preferred_element_type=F32)

    @pl.when(j == nj - 1)
    def _():
        y_ref[...] = _rms(y_ref[...], gfin_ref[...])


def _ffn_prompt(x, mr, ma, wo, gf, wup, cw, cb, wd, gfin, batch, seq, tm, tf):
    n, d_model = x.shape
    d_half = mr.shape[-1]
    d_ff = wd.shape[0]
    ni, nj = n // tm, d_ff // tf
    tiles_per_seq = seq // tm
    row = lambda i, j: (i, 0)
    gate_col = lambda i, j: (0, j)
    val_col = lambda i, j: (0, nj + j)
    tail = lambda i, j: (i, 0, j)
    y, tg, tv = pl.pallas_call(
        functools.partial(_ffn_body, tm=tm, tiles_per_seq=tiles_per_seq, sample_nb=None),
        grid=(ni, nj),
        in_specs=[
            pl.BlockSpec((tm, d_model), row),
            pl.BlockSpec((tm, d_half), row), pl.BlockSpec((tm, d_half), row),
            _const_spec(wo.shape), _const_spec(gf.shape),
            pl.BlockSpec((d_model, tf), gate_col), pl.BlockSpec((d_model, tf), val_col),
            pl.BlockSpec((cw.shape[0], tf), gate_col), pl.BlockSpec((cw.shape[0], tf), val_col),
            pl.BlockSpec((1, tf), gate_col), pl.BlockSpec((1, tf), val_col),
            pl.BlockSpec((tf, d_model), lambda i, j: (j, 0)),
            _const_spec(gfin.shape),
        ],
        out_specs=(
            pl.BlockSpec((tm, d_model), row),
            pl.BlockSpec((1, SUBLANES, tf), tail),
            pl.BlockSpec((1, SUBLANES, tf), tail),
        ),
        out_shape=(
            jax.ShapeDtypeStruct((n, d_model), F32),
            jax.ShapeDtypeStruct((ni, SUBLANES, d_ff), F32),
            jax.ShapeDtypeStruct((ni, SUBLANES, d_ff), F32),
        ),
        scratch_shapes=[
            pltpu.VMEM((tm, d_model), BF16),
            pltpu.VMEM((tm + SUBLANES, tf), F32), pltpu.VMEM((tm + SUBLANES, tf), F32),
            pltpu.VMEM((nj, SUBLANES, tf), F32), pltpu.VMEM((nj, SUBLANES, tf), F32),
        ],
        compiler_params=pltpu.CompilerParams(
            dimension_semantics=("arbitrary", "arbitrary"), vmem_limit_bytes=VMEM_LIMIT_BYTES),
        name="ffn_prompt",
    )(x, mr, ma, wo, gf, wup, wup, cw, cw, cb, cb, wd, gfin)
    return y, tg[tiles_per_seq - 1::tiles_per_seq], tv[tiles_per_seq - 1::tiles_per_seq]


def _ffn_sample(x, mr, ma, gna, wo, gf, wup, cw, cb, wd, gfin, state2d, nb, steps, tf):
    n, d_model = x.shape
    d_half = mr.shape[-1]
    d_ff = wd.shape[0]
    nj = d_ff // tf
    full2 = lambda shape: pl.BlockSpec(shape, lambda i, j: (0, 0))
    gate_col = lambda i, j: (0, j)
    val_col = lambda i, j: (0, nj + j)
    y, tg, tv = pl.pallas_call(
        functools.partial(_ffn_body, tm=n, tiles_per_seq=1, sample_nb=nb),
        grid=(1, nj),
        in_specs=[
            full2((n, d_model)), full2((n, d_half)), full2((n, d_half)), full2(gna.shape),
            _const_spec(wo.shape), _const_spec(gf.shape),
            pl.BlockSpec((d_model, tf), gate_col), pl.BlockSpec((d_model, tf), val_col),
            pl.BlockSpec((cw.shape[0], tf), gate_col), pl.BlockSpec((cw.shape[0], tf), val_col),
            pl.BlockSpec((1, tf), gate_col), pl.BlockSpec((1, tf), val_col),
            pl.BlockSpec((tf, d_model), lambda i, j: (j, 0)),
            _const_spec(gfin.shape),
            pl.BlockSpec((nb, tf), lambda i, j: (0, j)),
            pl.BlockSpec((nb, tf), lambda i, j: (0, 2 * nj + j)),
            pl.BlockSpec((nb, tf), lambda i, j: (0, nj + j)),
            pl.BlockSpec((nb, tf), lambda i, j: (0, 3 * nj + j)),
        ],
        out_specs=(
            full2((n, d_model)),
            pl.BlockSpec((2, nb, tf), lambda i, j: (0, 0, j)),
            pl.BlockSpec((2, nb, tf), lambda i, j: (0, 0, j)),
        ),
        out_shape=(
            jax.ShapeDtypeStruct((n, d_model), F32),
            jax.ShapeDtypeStruct((2, nb, d_ff), F32),
            jax.ShapeDtypeStruct((2, nb, d_ff), F32),
        ),
        scratch_shapes=[pltpu.VMEM((n, d_model), BF16)],
        compiler_params=pltpu.CompilerParams(
            dimension_semantics=("arbitrary", "arbitrary"), vmem_limit_bytes=VMEM_LIMIT_BYTES),
        name="ffn_sample",
    )(x, mr, ma, gna, wo, gf, wup, wup, cw, cw, cb, cb, wd, gfin, state2d, state2d, state2d, state2d)
    return y, tg, tv


def _q_perm():
    idx = np.empty((KV_HEADS // 2, GQA_GROUP, 2, HEAD_DIM), np.int32)
    for cg in range(KV_HEADS // 2):
        for g in range(GQA_GROUP):
            for par in range(2):
                h = GQA_GROUP * (2 * cg + par) + g
                idx[cg, g, par] = h * HEAD_DIM + np.arange(HEAD_DIM)
    return idx.reshape(-1)


def _gate_weights(w_a, w_x):
    nblk, bs, _ = w_a.shape
    per = GATE_TILE // bs
    eye = jnp.eye(per, dtype=w_a.dtype)

    def pack(w):
        w4 = w.reshape(nblk // per, per, bs, bs)
        return jnp.einsum("cgij,gh->cgihj", w4, eye).reshape(nblk // per, GATE_TILE, GATE_TILE)
    return jnp.concatenate([pack(w_a), pack(w_x)], axis=-1).astype(BF16)


def kernel(x_prompt, x_sample, state_rnn_conv, state_rnn_h, cache_win_k, cache_win_v, state_ffn_conv,
           norm_mix_g, w_in, rnn_conv_w, rnn_conv_b, w_gate_a, b_gate_a, w_gate_x, b_gate_x, rnn_lambda,
           attn_sinks, rel_bias_table, gn_rnn_g, gn_attn_g, w_out, norm_ffn_g, w_up, ffn_conv_w,
           ffn_conv_b, w_down, norm_final_g):
    batch, seq, d_model = x_prompt.shape
    nb, steps, _ = x_sample.shape
    depth = w_in.shape[0]
    d_rnn = rnn_conv_w.shape[-1]
    d_attn = N_HEADS * HEAD_DIM
    kv_dim = KV_HEADS * HEAD_DIM
    d_ff = w_down.shape[1]
    win = cache_win_k.shape[2]
    assert depth == 1 and d_rnn + d_attn == d_model and w_in.shape[-1] == 2 * d_rnn + d_attn + 2 * kv_dim
    assert win == WINDOW and seq % WINDOW == 0 and w_gate_a.shape[1] == RNN_BLOCKS
    assert rnn_conv_w.shape[1] == 4 and ffn_conv_w.shape[1] == 3 and steps >= 3 and nb % SUBLANES == 0

    qperm = _q_perm()
    w_in0 = w_in[0]
    o_q = 2 * d_rnn
    w_in_p = jnp.concatenate(
        [w_in0[:, :o_q], w_in0[:, o_q:o_q + d_attn][:, qperm], w_in0[:, o_q + d_attn:]], axis=1).astype(BF16)
    w_out0 = w_out[0]
    w_out_p = jnp.concatenate([w_out0[:d_rnn], w_out0[d_rnn:][qperm]], axis=0).astype(BF16)
    gn_attn_p = gn_attn_g[0][qperm].reshape(1, d_attn)
    w_up_b = w_up[0].astype(BF16)
    w_down_b = w_down[0].astype(BF16)
    wg = _gate_weights(w_gate_a[0], w_gate_x[0])
    row2 = lambda a: a.reshape(1, -1)
    g_mix, g_ffn, g_fin, g_rnn = row2(norm_mix_g[0]), row2(norm_ffn_g[0]), row2(norm_final_g), row2(gn_rnn_g[0])
    cw_r, cb_r = rnn_conv_w[0], row2(rnn_conv_b[0])
    ba, bx, lam = row2(b_gate_a[0]), row2(b_gate_x[0]), row2(rnn_lambda[0])
    cw_f, cb_f = ffn_conv_w[0], row2(ffn_conv_b[0])
    tbl = rel_bias_table.reshape(-1)
    sinks = attn_sinks[0]

    blk = WINDOW
    qi = np.arange(blk)[:, None]
    kj = np.arange(2 * blk)[None, :]
    code_p = jnp.asarray(_t5_bucket_np(blk + qi - kj))
    r = np.arange(N_HEADS * steps)[:, None]
    t_r, h_r = r % steps, r // steps
    npad = 2 * SUBLANES
    code_c = jnp.asarray(_t5_bucket_np(win + t_r - np.arange(win)[None, :]) * N_HEADS + h_r)
    code_n = jnp.asarray(_t5_bucket_np(t_r - np.arange(npad)[None, :]) * N_HEADS + h_r)

    n_p = batch * seq
    tm_a = 512 if n_p % 512 == 0 else WINDOW
    xp2 = x_prompt.reshape(n_p, d_model)
    xr, gr, q, k, v = _inproj(xp2, pl.BlockSpec((tm_a, d_model), lambda i: (i, 0)), n_p // tm_a, tm_a,
                              g_mix, w_in_p, d_rnn, d_attn, kv_dim)
    tt = 512 if seq % 512 == 0 else WINDOW
    m_rnn, h_last = _rglru_prompt(xr, gr, cw_r, cb_r, wg, ba, bx, lam, g_rnn, batch, seq, tt)
    m_attn = _swa_prompt(q, k, v, code_p, tbl, sinks, gn_attn_p, batch, seq)
    tm_f = 512 if seq % 512 == 0 else WINDOW
    tf = 512
    y_p, tail_g, tail_v = _ffn_prompt(xp2, m_rnn, m_attn, w_out_p, g_ffn, w_up_b, cw_f, cb_f, w_down_b,
                                      g_fin, batch, seq, tm_f, tf)
    y_prompt = y_p.reshape(batch, seq, d_model)
    p_rnn_conv = xr.reshape(batch, seq, d_rnn)[:, seq - 3:, :][None]
    p_rnn_h = h_last[None]
    p_win_k = k.reshape(batch, seq, KV_HEADS, HEAD_DIM)[:, seq - win:][None]
    p_win_v = v.reshape(batch, seq, KV_HEADS, HEAD_DIM)[:, seq - win:][None]
    p_ffn_conv = jnp.concatenate([tail_g[:, SUBLANES - 2:, :], tail_v[:, SUBLANES - 2:, :]], axis=-1)[None]

    n_s = nb * steps
    xs2 = x_sample.reshape(nb, steps * d_model)
    xr_s, gr_s, q_s, k_s, v_s = _inproj(xs2, pl.BlockSpec((nb, d_model), lambda t: (0, t)), steps, nb,
                                        g_mix, w_in_p, d_rnn, d_attn, kv_dim)
    conv2d = state_rnn_conv[0].reshape(nb, 3 * d_rnn)
    m_rnn_s, h_new = _rglru_sample(xr_s, gr_s, conv2d, state_rnn_h[0], cw_r, cb_r, wg, ba, bx, lam, g_rnn,
                                   nb, steps)
    ncg = KV_HEADS // 2
    qs = q_s.reshape(steps, nb, ncg, GQA_GROUP, LANES).transpose(1, 2, 3, 0, 4).reshape(
        nb, ncg, GQA_GROUP * steps, LANES)
    k_new = k_s.reshape(steps, nb, kv_dim).transpose(1, 0, 2)
    v_new = v_s.reshape(steps, nb, kv_dim).transpose(1, 0, 2)
    pad = ((0, 0), (0, npad - steps), (0, 0))
    ck = cache_win_k[0].reshape(nb, win, kv_dim)
    cv = cache_win_v[0].reshape(nb, win, kv_dim)
    bb = SUBLANES
    o_s = _swa_sample(qs, ck, cv, jnp.pad(k_new, pad), jnp.pad(v_new, pad), code_c, code_n, tbl, sinks,
                      steps, bb)
    ya_s = o_s.reshape(nb, ncg, GQA_GROUP, steps, LANES).transpose(3, 0, 1, 2, 4).reshape(n_s, d_attn)
    xs_tm = x_sample.transpose(1, 0, 2).reshape(n_s, d_model)
    ffn2d = state_ffn_conv[0].reshape(nb, 2 * 2 * d_ff)
    y_s, ns_g, ns_v = _ffn_sample(xs_tm, m_rnn_s, ya_s, gn_attn_p, w_out_p, g_ffn, w_up_b, cw_f, cb_f,
                                  w_down_b, g_fin, ffn2d, nb, steps, tf)
    y_sample = y_s.reshape(steps, nb, d_model).transpose(1, 0, 2)
    s_rnn_conv = xr_s.reshape(steps, nb, d_rnn)[steps - 3:].transpose(1, 0, 2)[None]
    s_rnn_h = h_new[None]
    s_win_k = jnp.concatenate([ck[:, steps:], k_new], axis=1).reshape(1, nb, win, KV_HEADS, HEAD_DIM)
    s_win_v = jnp.concatenate([cv[:, steps:], v_new], axis=1).reshape(1, nb, win, KV_HEADS, HEAD_DIM)
    s_ffn_conv = jnp.concatenate([ns_g, ns_v], axis=-1).transpose(1, 0, 2)[None]

    return (y_prompt, y_sample, p_rnn_conv, p_rnn_h, p_win_k, p_win_v, p_ffn_conv,
            s_rnn_conv, s_rnn_h, s_win_k, s_win_v, s_ffn_conv)
```

```python
import functools
import math

import numpy as np
import jax
import jax.numpy as jnp
from jax import lax
from jax.experimental import pallas as pl
from jax.experimental.pallas import tpu as pltpu

F32 = jnp.float32
BF16 = jnp.bfloat16

HEAD_DIM = 64
KV_HEADS = 4
N_HEADS = 16
GQA_GROUP = N_HEADS // KV_HEADS
RNN_BLOCKS = 16
RG_C = 8.0
WINDOW = 128
N_BUCKETS = 32
MAX_EXACT = N_BUCKETS // 2
REL_MAX_DIST = 128
EPS = 1e-6
NEG_INF = -1e30
ATTN_SCALE = HEAD_DIM ** -0.5

LANES = 128
SUBLANES = 8
VMEM_LIMIT_BYTES = 56 * 1024 * 1024

GATE_TILE = 256


def _t5_bucket_np(d):
    n = np.maximum(d, 0)
    nf = np.maximum(n, 1).astype(np.float32)
    large = MAX_EXACT + (np.log(nf / MAX_EXACT) / math.log(REL_MAX_DIST / MAX_EXACT)
                         * (N_BUCKETS - MAX_EXACT)).astype(np.int32)
    large = np.minimum(large, N_BUCKETS - 1)
    return np.where(n < MAX_EXACT, n, large).astype(np.int32)


def _rms(x, g):
    ms = jnp.mean(x * x, axis=-1, keepdims=True)
    return (x * lax.rsqrt(ms + EPS)) * g


def _softplus(x):
    return jnp.maximum(x, 0.0) + jnp.log1p(jnp.exp(-jnp.abs(x)))


def _const_spec(shape):
    nd = len(shape)
    return pl.BlockSpec(shape, lambda *_: (0,) * nd, pipeline_mode=pl.Buffered(1))


def _smem_spec():
    return pl.BlockSpec(memory_space=pltpu.SMEM)


def _inproj_body(x_ref, g_ref, w_ref, xr_ref, gr_ref, q_ref, k_ref, v_ref, *, d_rnn, d_attn, kv_dim):
    xn = _rms(x_ref[...], g_ref[...])
    p = jnp.dot(xn.astype(BF16), w_ref[...], preferred_element_type=F32)
    o1, o2, o3, o4 = d_rnn, 2 * d_rnn, 2 * d_rnn + d_attn, 2 * d_rnn + d_attn + kv_dim
    xr_ref[...] = p[:, :o1]
    gr_ref[...] = p[:, o1:o2]
    q_ref[...] = p[:, o2:o3].astype(BF16)
    k_ref[...] = p[:, o3:o4]
    v_ref[...] = p[:, o4:]


def _inproj(x2d, x_spec, n_steps, tm, g, w, d_rnn, d_attn, kv_dim):
    n = n_steps * tm
    d_model = g.shape[-1]
    row = lambda i: (i, 0)
    out_shape = (
        jax.ShapeDtypeStruct((n, d_rnn), F32),
        jax.ShapeDtypeStruct((n, d_rnn), F32),
        jax.ShapeDtypeStruct((n, d_attn), BF16),
        jax.ShapeDtypeStruct((n, kv_dim), F32),
        jax.ShapeDtypeStruct((n, kv_dim), F32),
    )
    out_specs = (
        pl.BlockSpec((tm, d_rnn), row),
        pl.BlockSpec((tm, d_rnn), row),
        pl.BlockSpec((tm, d_attn), row),
        pl.BlockSpec((tm, kv_dim), row),
        pl.BlockSpec((tm, kv_dim), row),
    )
    return pl.pallas_call(
        functools.partial(_inproj_body, d_rnn=d_rnn, d_attn=d_attn, kv_dim=kv_dim),
        grid=(n_steps,),
        in_specs=[x_spec, _const_spec((1, d_model)), _const_spec(w.shape)],
        out_specs=out_specs,
        out_shape=out_shape,
        compiler_params=pltpu.CompilerParams(
            dimension_semantics=("arbitrary",), vmem_limit_bytes=VMEM_LIMIT_BYTES),
        name="inproj",
    )(x2d, g, w)


def _rglru_gates(xc, wg_ref, ba, bx, lam):
    d_rnn = xc.shape[-1]
    pre_a, pre_x = [], []
    for c in range(d_rnn // GATE_TILE):
        xb = xc[:, c * GATE_TILE:(c + 1) * GATE_TILE].astype(BF16)
        pre = jnp.dot(xb, wg_ref[c], preferred_element_type=F32)
        pre_a.append(pre[:, :GATE_TILE])
        pre_x.append(pre[:, GATE_TILE:])
    r = jax.nn.sigmoid(jnp.concatenate(pre_a, axis=1) + ba)
    i = jax.nn.sigmoid(jnp.concatenate(pre_x, axis=1) + bx)
    log_a = (-RG_C * r) * _softplus(-lam)
    a = jnp.exp(log_a)
    one_minus_a2 = -jnp.tanh(log_a) * (a * a + 1.0)
    u = jnp.sqrt(one_minus_a2) * (i * xc)
    return a, u


def _rglru_prompt_body(xr_ref, gr_ref, cw_ref, cb_ref, wg_ref, ba_ref, bx_ref, lam_ref, gn_ref,
                       out_ref, hlast_ref, xbuf, abuf, ubuf, hbuf, hc_ref, *, tt, conv_w):
    ti = pl.program_id(1)
    d_rnn = xr_ref.shape[-1]

    @pl.when(ti == 0)
    def _():
        xbuf[0:SUBLANES, :] = jnp.zeros((SUBLANES, d_rnn), F32)
        hc_ref[...] = jnp.zeros_like(hc_ref)

    x = xr_ref[...]
    xbuf[SUBLANES:SUBLANES + tt, :] = x
    xc = cw_ref[conv_w - 1:conv_w, :] * x
    for k in range(conv_w - 1):
        off = SUBLANES - (conv_w - 1) + k
        xc = xc + cw_ref[k:k + 1, :] * xbuf[off:off + tt, :]
    xc = xc + cb_ref[...]
    xbuf[0:SUBLANES, :] = xbuf[tt:tt + SUBLANES, :]

    a, u = _rglru_gates(xc, wg_ref, ba_ref[...], bx_ref[...], lam_ref[...])
    abuf[...] = a
    ubuf[...] = u

    row = lax.broadcasted_iota(jnp.int32, (SUBLANES, d_rnn), 0)

    def group(gidx, hc):
        r0 = pl.multiple_of(gidx * SUBLANES, SUBLANES)
        ag = abuf[pl.ds(r0, SUBLANES), :]
        ug = ubuf[pl.ds(r0, SUBLANES), :]
        for k in (1, 2, 4):
            a_prev = jnp.where(row >= k, pltpu.roll(ag, k, 0), 1.0)
            u_prev = jnp.where(row >= k, pltpu.roll(ug, k, 0), 0.0)
            ug = ag * u_prev + ug
            ag = ag * a_prev
        h = ag * hc + ug
        hbuf[pl.ds(r0, SUBLANES), :] = h
        return jnp.broadcast_to(h[SUBLANES - 1:SUBLANES, :], (SUBLANES, d_rnn))

    hc = lax.fori_loop(0, tt // SUBLANES, group, hc_ref[...], unroll=2)
    hc_ref[...] = hc
    hlast_ref[0] = hc

    y = jax.nn.gelu(gr_ref[...]) * hbuf[...]
    out_ref[...] = _rms(y, gn_ref[...]).astype(BF16)


def _rglru_prompt(xr, gr, cw, cb, wg, ba, bx, lam, gn, batch, seq, tt):
    d_rnn = xr.shape[-1]
    nt = seq // tt
    conv_w = cw.shape[0]
    tile = lambda b, t: (b * nt + t, 0)
    out, hlast = pl.pallas_call(
        functools.partial(_rglru_prompt_body, tt=tt, conv_w=conv_w),
        grid=(batch, nt),
        in_specs=[
            pl.BlockSpec((tt, d_rnn), tile),
            pl.BlockSpec((tt, d_rnn), tile),
            _const_spec(cw.shape), _const_spec(cb.shape), _const_spec(wg.shape),
            _const_spec(ba.shape), _const_spec(bx.shape), _const_spec(lam.shape), _const_spec(gn.shape),
        ],
        out_specs=(
            pl.BlockSpec((tt, d_rnn), tile),
            pl.BlockSpec((1, SUBLANES, d_rnn), lambda b, t: (b, 0, 0)),
        ),
        out_shape=(
            jax.ShapeDtypeStruct((batch * seq, d_rnn), BF16),
            jax.ShapeDtypeStruct((batch, SUBLANES, d_rnn), F32),
        ),
        scratch_shapes=[
            pltpu.VMEM((tt + SUBLANES, d_rnn), F32),
            pltpu.VMEM((tt, d_rnn), F32),
            pltpu.VMEM((tt, d_rnn), F32),
            pltpu.VMEM((tt, d_rnn), F32),
            pltpu.VMEM((SUBLANES, d_rnn), F32),
        ],
        compiler_params=pltpu.CompilerParams(
            dimension_semantics=("arbitrary", "arbitrary"), vmem_limit_bytes=VMEM_LIMIT_BYTES),
        name="rglru_prompt",
    )(xr, gr, cw, cb, wg, ba, bx, lam, gn)
    return out, hlast[:, 0, :]


def _rglru_sample_body(xr_ref, gr_ref, c0_ref, c1_ref, c2_ref, h0_ref, cw_ref, cb_ref, wg_ref,
                       ba_ref, bx_ref, lam_ref, gn_ref, out_ref, hnew_ref, *, nb, steps):
    x = xr_ref[...]
    hist = [c0_ref[...], c1_ref[...], c2_ref[...]]
    xs = [x[t * nb:(t + 1) * nb, :] for t in range(steps)]
    xp = hist + xs
    conv_w = len(hist) + 1
    xc = cw_ref[conv_w - 1:conv_w, :] * x
    for k in range(conv_w - 1):
        shifted = jnp.concatenate(xp[k:k + steps], axis=0)
        xc = xc + cw_ref[k:k + 1, :] * shifted
    xc = xc + cb_ref[...]
    a, u = _rglru_gates(xc, wg_ref, ba_ref[...], bx_ref[...], lam_ref[...])
    h = h0_ref[...]
    hs = []
    for t in range(steps):
        h = a[t * nb:(t + 1) * nb, :] * h + u[t * nb:(t + 1) * nb, :]
        hs.append(h)
    hnew_ref[...] = h
    y = jax.nn.gelu(gr_ref[...]) * jnp.concatenate(hs, axis=0)
    out_ref[...] = _rms(y, gn_ref[...]).astype(BF16)


def _rglru_sample(xr, gr, conv_state2d, h0, cw, cb, wg, ba, bx, lam, gn, nb, steps):
    d_rnn = xr.shape[-1]
    n = nb * steps
    assert cw.shape[0] == 4
    full = lambda shape: pl.BlockSpec(shape, lambda i: (0,) * len(shape))
    return pl.pallas_call(
        functools.partial(_rglru_sample_body, nb=nb, steps=steps),
        grid=(1,),
        in_specs=[
            full((n, d_rnn)), full((n, d_rnn)),
            pl.BlockSpec((nb, d_rnn), lambda i: (0, 0)),
            pl.BlockSpec((nb, d_rnn), lambda i: (0, 1)),
            pl.BlockSpec((nb, d_rnn), lambda i: (0, 2)),
            full((nb, d_rnn)),
            full(cw.shape), full(cb.shape), full(wg.shape), full(ba.shape), full(bx.shape),
            full(lam.shape), full(gn.shape),
        ],
        out_specs=(full((n, d_rnn)), full((nb, d_rnn))),
        out_shape=(jax.ShapeDtypeStruct((n, d_rnn), BF16), jax.ShapeDtypeStruct((nb, d_rnn), F32)),
        compiler_params=pltpu.CompilerParams(
            dimension_semantics=("arbitrary",), vmem_limit_bytes=VMEM_LIMIT_BYTES),
        name="rglru_sample",
    )(xr, gr, conv_state2d, conv_state2d, conv_state2d, h0, cw, cb, wg, ba, bx, lam, gn)


def _swa_prompt_body(code_ref, tbl_ref, sink_ref, q_ref, kp_ref, kc_ref, vp_ref, vc_ref, gn_ref,
                     out_ref, bias_ref):
    j = pl.program_id(1)
    blk = q_ref.shape[0]
    rows = GQA_GROUP * blk

    @pl.when((pl.program_id(0) == 0) & (j == 0))
    def _():
        code = code_ref[...]
        for h in range(N_HEADS):
            def pick(b, acc, h=h):
                return jnp.where(code == b, tbl_ref[b * N_HEADS + h], acc)
            bias_h = lax.fori_loop(0, N_BUCKETS, pick, jnp.zeros(code.shape, F32))
            kv, g = divmod(h, GQA_GROUP)
            bias_ref[kv, g * blk:(g + 1) * blk, :] = bias_h

    qi = lax.broadcasted_iota(jnp.int32, (rows, 2 * blk), 0) & (blk - 1)
    kj = lax.broadcasted_iota(jnp.int32, (rows, 2 * blk), 1)
    dist = blk + qi - kj
    first_key = jnp.where(j > 0, 0, blk)
    dist = jnp.where(kj >= first_key, dist, -1)
    valid = (dist & -WINDOW) == 0

    kband = jnp.concatenate([kp_ref[...], kc_ref[...]], axis=0)
    vband = jnp.concatenate([vp_ref[...], vc_ref[...]], axis=0)
    lane = lax.broadcasted_iota(jnp.int32, (2 * blk, LANES), 1)
    row_g = lax.broadcasted_iota(jnp.int32, (rows, 1), 0) // blk
    q = q_ref[...]

    outs = []
    for cg in range(KV_HEADS // 2):
        qs = jnp.concatenate(
            [q[:, (cg * GQA_GROUP + g) * LANES:(cg * GQA_GROUP + g + 1) * LANES] for g in range(GQA_GROUP)],
            axis=0)
        ka = kband[:, cg * LANES:(cg + 1) * LANES]
        va = vband[:, cg * LANES:(cg + 1) * LANES]
        o = None
        for par in range(2):
            kv = 2 * cg + par
            half = (lane < HEAD_DIM) if par == 0 else (lane >= HEAD_DIM)
            km = jnp.where(half, ka, 0.0).astype(BF16)
            vm = jnp.where(half, va, 0.0).astype(BF16)
            s = lax.dot_general(qs, km, (((1,), (1,)), ((), ())), preferred_element_type=F32)
            s = s * ATTN_SCALE + bias_ref[kv]
            s = jnp.where(valid, s, NEG_INF)
            sink = jnp.zeros((rows, 1), F32)
            for g in range(GQA_GROUP):
                sink = jnp.where(row_g == g, sink_ref[kv * GQA_GROUP + g], sink)
            m = jnp.maximum(jnp.max(s, axis=-1, keepdims=True), sink)
            p = jnp.exp(s - m)
            denom = jnp.sum(p, axis=-1, keepdims=True) + jnp.exp(sink - m)
            w = p * (1.0 / denom)
            part = jnp.dot(w.astype(BF16), vm, preferred_element_type=F32)
            o = part if o is None else o + part
        outs.extend(o[g * blk:(g + 1) * blk, :] for g in range(GQA_GROUP))
    y = jnp.concatenate(outs, axis=1)
    out_ref[...] = _rms(y, gn_ref[...]).astype(BF16)


def _swa_prompt(q, k, v, code, tbl, sinks, gn, batch, seq):
    blk = WINDOW
    nb = seq // blk
    d_attn = q.shape[-1]
    kv_dim = k.shape[-1]
    cur = lambda b, j: (b * nb + j, 0)
    prev = lambda b, j: (b * nb + jnp.maximum(j - 1, 0), 0)
    return pl.pallas_call(
        _swa_prompt_body,
        grid=(batch, nb),
        in_specs=[
            _const_spec(code.shape), _smem_spec(), _smem_spec(),
            pl.BlockSpec((blk, d_attn), cur),
            pl.BlockSpec((blk, kv_dim), prev), pl.BlockSpec((blk, kv_dim), cur),
            pl.BlockSpec((blk, kv_dim), prev), pl.BlockSpec((blk, kv_dim), cur),
            _const_spec(gn.shape),
        ],
        out_specs=pl.BlockSpec((blk, d_attn), cur),
        out_shape=jax.ShapeDtypeStruct((batch * seq, d_attn), BF16),
        scratch_shapes=[pltpu.VMEM((KV_HEADS, GQA_GROUP * blk, 2 * blk), F32)],
        compiler_params=pltpu.CompilerParams(
            dimension_semantics=("arbitrary", "arbitrary"), vmem_limit_bytes=VMEM_LIMIT_BYTES),
        name="swa_prompt",
    )(code, tbl, sinks, q, k, k, v, v, gn)


def _swa_sample_body(codec_ref, coden_ref, tbl_ref, sink_ref, q_ref, ck_ref, cv_ref, kn_ref, vn_ref,
                     out_ref, biasc_ref, biasn_ref, sinkc_ref, *, steps):
    bb = q_ref.shape[0]
    nrow = N_HEADS * steps
    win = ck_ref.shape[1]
    npad = kn_ref.shape[1]

    @pl.when(pl.program_id(0) == 0)
    def _():
        codec = codec_ref[...]
        coden = coden_ref[...]
        hrow = lax.broadcasted_iota(jnp.int32, (nrow, 1), 0) // steps

        def pick(idx, accs):
            ac, an = accs
            val = tbl_ref[idx]
            return jnp.where(codec == idx, val, ac), jnp.where(coden == idx, val, an)
        bc, bn = lax.fori_loop(0, N_BUCKETS * N_HEADS, pick,
                               (jnp.zeros(codec.shape, F32), jnp.zeros(coden.shape, F32)))
        biasc_ref[...] = bc
        biasn_ref[...] = bn

        def pick_sink(h, acc):
            return jnp.where(hrow == h, sink_ref[h], acc)
        sinkc_ref[...] = lax.fori_loop(0, N_HEADS, pick_sink, jnp.zeros((nrow, 1), F32))

    q = q_ref[...].astype(F32)
    lane = lax.broadcasted_iota(jnp.int32, (bb, GQA_GROUP * steps, LANES), 2)
    zeros = jnp.zeros((bb, GQA_GROUP * steps, LANES), F32)
    pieces = []
    for cg in range(KV_HEADS // 2):
        for par in range(2):
            half = (lane < HEAD_DIM) if par == 0 else (lane >= HEAD_DIM)
            qm = jnp.where(half, q[:, cg], zeros)
            pieces.append(jnp.concatenate([qm, zeros] if cg == 0 else [zeros, qm], axis=2))
    qm = jnp.concatenate(pieces, axis=1).astype(BF16)

    ck = ck_ref[...].astype(BF16)
    kn = kn_ref[...].astype(BF16)
    s_c = jnp.einsum("bqd,bkd->bqk", qm, ck, preferred_element_type=F32)
    s_n = jnp.einsum("bqd,bkd->bqk", qm, kn, preferred_element_type=F32)

    t_c = lax.broadcasted_iota(jnp.int32, (nrow, win), 0) % steps
    k_c = lax.broadcasted_iota(jnp.int32, (nrow, win), 1)
    valid_c = k_c > t_c
    t_n = lax.broadcasted_iota(jnp.int32, (nrow, npad), 0) % steps
    k_n = lax.broadcasted_iota(jnp.int32, (nrow, npad), 1)
    valid_n = k_n <= t_n

    s_c = jnp.where(valid_c[None], s_c * ATTN_SCALE + biasc_ref[...][None], NEG_INF)
    s_n = jnp.where(valid_n[None], s_n * ATTN_SCALE + biasn_ref[...][None], NEG_INF)
    sink = sinkc_ref[...][None]
    m = jnp.maximum(jnp.maximum(jnp.max(s_c, axis=-1, keepdims=True),
                                jnp.max(s_n, axis=-1, keepdims=True)), sink)
    p_c = jnp.exp(s_c - m)
    p_n = jnp.exp(s_n - m)
    denom = (jnp.sum(p_c, axis=-1, keepdims=True) + jnp.sum(p_n, axis=-1, keepdims=True)
             + jnp.exp(sink - m))
    r = 1.0 / denom
    w_c = (p_c * r).astype(BF16)
    w_n = (p_n * r).astype(BF16)
    o = (jnp.einsum("bqk,bkd->bqd", w_c, cv_ref[...].astype(BF16), preferred_element_type=F32)
         + jnp.einsum("bqk,bkd->bqd", w_n, vn_ref[...].astype(BF16), preferred_element_type=F32))
    gt = GQA_GROUP * steps
    lane_o = lax.broadcasted_iota(jnp.int32, (bb, gt, LANES), 2)
    for cg in range(KV_HEADS // 2):
        lo = o[:, cg * 2 * gt:cg * 2 * gt + gt, cg * LANES:(cg + 1) * LANES]
        hi = o[:, cg * 2 * gt + gt:(cg + 1) * 2 * gt, cg * LANES:(cg + 1) * LANES]
        out_ref[:, cg] = jnp.where(lane_o < HEAD_DIM, lo, hi)


def _swa_sample(qs, ck, cv, kn, vn, codec, coden, tbl, sinks, steps, bb):
    nbatch, ncg, gt, _ = qs.shape
    win, kv_dim = ck.shape[1], ck.shape[2]
    npad = kn.shape[1]
    nrow = N_HEADS * steps
    blk4 = lambda i: (i, 0, 0, 0)
    blk3 = lambda i: (i, 0, 0)
    return pl.pallas_call(
        functools.partial(_swa_sample_body, steps=steps),
        grid=(nbatch // bb,),
        in_specs=[
            _const_spec(codec.shape), _const_spec(coden.shape), _smem_spec(), _smem_spec(),
            pl.BlockSpec((bb, ncg, gt, LANES), blk4),
            pl.BlockSpec((bb, win, kv_dim), blk3), pl.BlockSpec((bb, win, kv_dim), blk3),
            pl.BlockSpec((bb, npad, kv_dim), blk3), pl.BlockSpec((bb, npad, kv_dim), blk3),
        ],
        out_specs=pl.BlockSpec((bb, ncg, gt, LANES), blk4),
        out_shape=jax.ShapeDtypeStruct((nbatch, ncg, gt, LANES), F32),
        scratch_shapes=[pltpu.VMEM((nrow, win), F32), pltpu.VMEM((nrow, npad), F32),
                        pltpu.VMEM((nrow, 1), F32)],
        compiler_params=pltpu.CompilerParams(
            dimension_semantics=("arbitrary",), vmem_limit_bytes=VMEM_LIMIT_BYTES),
        name="swa_sample",
    )(codec, coden, tbl, sinks, qs, ck, cv, kn, vn)


def _shift_rows(up, hist, k):
    rolled = pltpu.roll(up, k, 0)
    row = lax.broadcasted_iota(jnp.int32, hist.shape, 0)
    head = jnp.where(row < k, pltpu.roll(hist, k, 0), rolled[0:SUBLANES, :])
    return jnp.concatenate([head, rolled[SUBLANES:, :]], axis=0)


def _ffn_conv(up, hist, cw, cb, *, tm, sample_nb):
    if sample_nb is None:
        prev2 = _shift_rows(up, hist, 2)
        prev1 = _shift_rows(up, hist, 1)
    else:
        s0, s1 = hist
        prev2 = jnp.concatenate([s0, s1, up[:tm - 2 * sample_nb, :]], axis=0)
        prev1 = jnp.concatenate([s1, up[:tm - sample_nb, :]], axis=0)
    return cw[0:1, :] * prev2 + cw[1:2, :] * prev1 + cw[2:3, :] * up + cb


FFN_CHUNK = 256
FFN_SLAB = 256


def _ffn_body(*refs, tm, tiles_per_seq, sample_nb):
    if sample_nb is None:
        (x_ref, mr_ref, ma_ref, wo_ref, gf_ref, wg_ref, wv_ref, cwg_ref, cwv_ref, cbg_ref, cbv_ref,
         wd_ref, gfin_ref, y_ref, tg_ref, tv_ref, hn_ref, car_g, car_v) = refs
    else:
        (x_ref, mr_ref, ma_ref, gna_ref, wo_ref, gf_ref, wg_ref, wv_ref, cwg_ref, cwv_ref, cbg_ref,
         cbv_ref, wd_ref, gfin_ref, s0g_ref, s1g_ref, s0v_ref, s1v_ref, y_ref, tg_ref, tv_ref,
         hn_ref) = refs
    i = pl.program_id(0)
    j = pl.program_id(1)
    nj = pl.num_programs(1)

    @pl.when(j == 0)
    def _():
        if sample_nb is None:
            ma = ma_ref[...]
        else:
            ma = _rms(ma_ref[...], gna_ref[...]).astype(BF16)
        merged = jnp.concatenate([mr_ref[...], ma], axis=1)
        h = x_ref[...] + jnp.dot(merged, wo_ref[...], preferred_element_type=F32)
        y_ref[...] = h
        hn_ref[...] = _rms(h, gf_ref[...]).astype(BF16)

    if sample_nb is None:
        @pl.when((i == 0) & (j == 0))
        def _():
            car_g[...] = jnp.zeros_like(car_g)
            car_v[...] = jnp.zeros_like(car_v)
        seq_start = i % tiles_per_seq == 0

    if sample_nb is None:
        n_slab = tm // FFN_SLAB
        cwg, cwv, cbg, cbv = cwg_ref[...], cwv_ref[...], cbg_ref[...], cbv_ref[...]

        def up_proj(r):
            hn = hn_ref[r * FFN_SLAB:(r + 1) * FFN_SLAB, :]
            return (jnp.dot(hn, wg_ref[...], preferred_element_type=F32),
                    jnp.dot(hn, wv_ref[...], preferred_element_type=F32))

        hist_g = jnp.where(seq_start, 0.0, car_g[j])
        hist_v = jnp.where(seq_start, 0.0, car_v[j])
        pending = up_proj(0)
        for r in range(n_slab):
            up_g, up_v = pending
            if r + 1 < n_slab:
                pending = up_proj(r + 1)
            c_g = _ffn_conv(up_g, hist_g, cwg, cbg, tm=FFN_SLAB, sample_nb=None)
            c_v = _ffn_conv(up_v, hist_v, cwv, cbv, tm=FFN_SLAB, sample_nb=None)
            hist_g = up_g[FFN_SLAB - SUBLANES:, :]
            hist_v = up_v[FFN_SLAB - SUBLANES:, :]
            act = (jax.nn.gelu(c_g) * c_v).astype(BF16)
            rows = slice(r * FFN_SLAB, (r + 1) * FFN_SLAB)
            y_ref[rows, :] += jnp.dot(act, wd_ref[...], preferred_element_type=F32)
        car_g[j] = hist_g
        car_v[j] = hist_v
        tg_ref[0] = hist_g
        tv_ref[0] = hist_v
    else:
        hn = hn_ref[...]
        tf = wg_ref.shape[1]
        acts = []
        for c in range(tf // FFN_CHUNK):
            cs = slice(c * FFN_CHUNK, (c + 1) * FFN_CHUNK)
            up_g = jnp.dot(hn, wg_ref[:, cs], preferred_element_type=F32)
            up_v = jnp.dot(hn, wv_ref[:, cs], preferred_element_type=F32)
            hist_g = (s0g_ref[:, cs], s1g_ref[:, cs])
            hist_v = (s0v_ref[:, cs], s1v_ref[:, cs])
            tg_ref[0, :, cs] = up_g[tm - 2 * sample_nb:tm - sample_nb, :]
            tg_ref[1, :, cs] = up_g[tm - sample_nb:, :]
            tv_ref[0, :, cs] = up_v[tm - 2 * sample_nb:tm - sample_nb, :]
            tv_ref[1, :, cs] = up_v[tm - sample_nb:, :]
            c_g = _ffn_conv(up_g, hist_g, cwg_ref[:, cs], cbg_ref[:, cs], tm=tm, sample_nb=sample_nb)
            c_v = _ffn_conv(up_v, hist_v, cwv_ref[:, cs], cbv_ref[:, cs], tm=tm, sample_nb=sample_nb)
            acts.append((jax.nn.gelu(c_g) * c_v).astype(BF16))
        act = jnp.concatenate(acts, axis=1)
        y_ref[...] += jnp.dot(act, wd_ref[...], preferred_element_type=F32)

    @pl.when(j == nj - 1)
    def _():
        y_ref[...] = _rms(y_ref[...], gfin_ref[...])


def _ffn_prompt(x, mr, ma, wo, gf, wup, cw, cb, wd, gfin, batch, seq, tm, tf):
    n, d_model = x.shape
    d_half = mr.shape[-1]
    d_ff = wd.shape[0]
    ni, nj = n // tm, d_ff // tf
    tiles_per_seq = seq // tm
    row = lambda i, j: (i, 0)
    gate_col = lambda i, j: (0, j)
    val_col = lambda i, j: (0, nj + j)
    tail = lambda i, j: (i, 0, j)
    y, tg, tv = pl.pallas_call(
        functools.partial(_ffn_body, tm=tm, tiles_per_seq=tiles_per_seq, sample_nb=None),
        grid=(ni, nj),
        in_specs=[
            pl.BlockSpec((tm, d_model), row),
            pl.BlockSpec((tm, d_half), row), pl.BlockSpec((tm, d_half), row),
            _const_spec(wo.shape), _const_spec(gf.shape),
            pl.BlockSpec((d_model, tf), gate_col), pl.BlockSpec((d_model, tf), val_col),
            pl.BlockSpec((cw.shape[0], tf), gate_col), pl.BlockSpec((cw.shape[0], tf), val_col),
            pl.BlockSpec((1, tf), gate_col), pl.BlockSpec((1, tf), val_col),
            pl.BlockSpec((tf, d_model), lambda i, j: (j, 0)),
            _const_spec(gfin.shape),
        ],
        out_specs=(
            pl.BlockSpec((tm, d_model), row),
            pl.BlockSpec((1, SUBLANES, tf), tail),
            pl.BlockSpec((1, SUBLANES, tf), tail),
        ),
        out_shape=(
            jax.ShapeDtypeStruct((n, d_model), F32),
            jax.ShapeDtypeStruct((ni, SUBLANES, d_ff), F32),
            jax.ShapeDtypeStruct((ni, SUBLANES, d_ff), F32),
        ),
        scratch_shapes=[
            pltpu.VMEM((tm, d_model), BF16),
            pltpu.VMEM((nj, SUBLANES, tf), F32), pltpu.VMEM((nj, SUBLANES, tf), F32),
        ],
        compiler_params=pltpu.CompilerParams(
            dimension_semantics=("arbitrary", "arbitrary"), vmem_limit_bytes=VMEM_LIMIT_BYTES),
        name="ffn_prompt",
    )(x, mr, ma, wo, gf, wup, wup, cw, cw, cb, cb, wd, gfin)
    return y, tg[tiles_per_seq - 1::tiles_per_seq], tv[tiles_per_seq - 1::tiles_per_seq]


def _ffn_sample(x, mr, ma, gna, wo, gf, wup, cw, cb, wd, gfin, state2d, nb, steps, tf):
    n, d_model = x.shape
    d_half = mr.shape[-1]
    d_ff = wd.shape[0]
    nj = d_ff // tf
    full2 = lambda shape: pl.BlockSpec(shape, lambda i, j: (0, 0))
    gate_col = lambda i, j: (0, j)
    val_col = lambda i, j: (0, nj + j)
    y, tg, tv = pl.pallas_call(
        functools.partial(_ffn_body, tm=n, tiles_per_seq=1, sample_nb=nb),
        grid=(1, nj),
        in_specs=[
            full2((n, d_model)), full2((n, d_half)), full2((n, d_half)), full2(gna.shape),
            _const_spec(wo.shape), _const_spec(gf.shape),
            pl.BlockSpec((d_model, tf), gate_col), pl.BlockSpec((d_model, tf), val_col),
            pl.BlockSpec((cw.shape[0], tf), gate_col), pl.BlockSpec((cw.shape[0], tf), val_col),
            pl.BlockSpec((1, tf), gate_col), pl.BlockSpec((1, tf), val_col),
            pl.BlockSpec((tf, d_model), lambda i, j: (j, 0)),
            _const_spec(gfin.shape),
            pl.BlockSpec((nb, tf), lambda i, j: (0, j)),
            pl.BlockSpec((nb, tf), lambda i, j: (0, 2 * nj + j)),
            pl.BlockSpec((nb, tf), lambda i, j: (0, nj + j)),
            pl.BlockSpec((nb, tf), lambda i, j: (0, 3 * nj + j)),
        ],
        out_specs=(
            full2((n, d_model)),
            pl.BlockSpec((2, nb, tf), lambda i, j: (0, 0, j)),
            pl.BlockSpec((2, nb, tf), lambda i, j: (0, 0, j)),
        ),
        out_shape=(
            jax.ShapeDtypeStruct((n, d_model), F32),
            jax.ShapeDtypeStruct((2, nb, d_ff), F32),
            jax.ShapeDtypeStruct((2, nb, d_ff), F32),
        ),
        scratch_shapes=[pltpu.VMEM((n, d_model), BF16)],
        compiler_params=pltpu.CompilerParams(
            dimension_semantics=("arbitrary", "arbitrary"), vmem_limit_bytes=VMEM_LIMIT_BYTES),
        name="ffn_sample",
    )(x, mr, ma, gna, wo, gf, wup, wup, cw, cw, cb, cb, wd, gfin, state2d, state2d, state2d, state2d)
    return y, tg, tv


def _q_perm():
    idx = np.empty((KV_HEADS // 2, GQA_GROUP, 2, HEAD_DIM), np.int32)
    for cg in range(KV_HEADS // 2):
        for g in range(GQA_GROUP):
            for par in range(2):
                h = GQA_GROUP * (2 * cg + par) + g
                idx[cg, g, par] = h * HEAD_DIM + np.arange(HEAD_DIM)
    return idx.reshape(-1)


def _gate_weights(w_a, w_x):
    nblk, bs, _ = w_a.shape
    per = GATE_TILE // bs
    eye = jnp.eye(per, dtype=w_a.dtype)

    def pack(w):
        w4 = w.reshape(nblk // per, per, bs, bs)
        return jnp.einsum("cgij,gh->cgihj", w4, eye).reshape(nblk // per, GATE_TILE, GATE_TILE)
    return jnp.concatenate([pack(w_a), pack(w_x)], axis=-1).astype(BF16)


def kernel(x_prompt, x_sample, state_rnn_conv, state_rnn_h, cache_win_k, cache_win_v, state_ffn_conv,
           norm_mix_g, w_in, rnn_conv_w, rnn_conv_b, w_gate_a, b_gate_a, w_gate_x, b_gate_x, rnn_lambda,
           attn_sinks, rel_bias_table, gn_rnn_g, gn_attn_g, w_out, norm_ffn_g, w_up, ffn_conv_w,
           ffn_conv_b, w_down, norm_final_g):
    batch, seq, d_model = x_prompt.shape
    nb, steps, _ = x_sample.shape
    depth = w_in.shape[0]
    d_rnn = rnn_conv_w.shape[-1]
    d_attn = N_HEADS * HEAD_DIM
    kv_dim = KV_HEADS * HEAD_DIM
    d_ff = w_down.shape[1]
    win = cache_win_k.shape[2]
    assert depth == 1 and d_rnn + d_attn == d_model and w_in.shape[-1] == 2 * d_rnn + d_attn + 2 * kv_dim
    assert win == WINDOW and seq % WINDOW == 0 and w_gate_a.shape[1] == RNN_BLOCKS
    assert rnn_conv_w.shape[1] == 4 and ffn_conv_w.shape[1] == 3 and steps >= 3 and nb % SUBLANES == 0

    qperm = _q_perm()
    w_in0 = w_in[0]
    o_q = 2 * d_rnn
    w_in_p = jnp.concatenate(
        [w_in0[:, :o_q], w_in0[:, o_q:o_q + d_attn][:, qperm], w_in0[:, o_q + d_attn:]], axis=1).astype(BF16)
    w_out0 = w_out[0]
    w_out_p = jnp.concatenate([w_out0[:d_rnn], w_out0[d_rnn:][qperm]], axis=0).astype(BF16)
    gn_attn_p = gn_attn_g[0][qperm].reshape(1, d_attn)
    w_up_b = w_up[0].astype(BF16)
    w_down_b = w_down[0].astype(BF16)
    wg = _gate_weights(w_gate_a[0], w_gate_x[0])
    row2 = lambda a: a.reshape(1, -1)
    g_mix, g_ffn, g_fin, g_rnn = row2(norm_mix_g[0]), row2(norm_ffn_g[0]), row2(norm_final_g), row2(gn_rnn_g[0])
    cw_r, cb_r = rnn_conv_w[0], row2(rnn_conv_b[0])
    ba, bx, lam = row2(b_gate_a[0]), row2(b_gate_x[0]), row2(rnn_lambda[0])
    cw_f, cb_f = ffn_conv_w[0], row2(ffn_conv_b[0])
    tbl = rel_bias_table.reshape(-1)
    sinks = attn_sinks[0]

    blk = WINDOW
    qi = np.arange(blk)[:, None]
    kj = np.arange(2 * blk)[None, :]
    code_p = jnp.asarray(_t5_bucket_np(blk + qi - kj))
    r = np.arange(N_HEADS * steps)[:, None]
    t_r, h_r = r % steps, r // steps
    npad = 2 * SUBLANES
    code_c = jnp.asarray(_t5_bucket_np(win + t_r - np.arange(win)[None, :]) * N_HEADS + h_r)
    code_n = jnp.asarray(_t5_bucket_np(t_r - np.arange(npad)[None, :]) * N_HEADS + h_r)

    n_p = batch * seq
    tm_a = 512 if n_p % 512 == 0 else WINDOW
    xp2 = x_prompt.reshape(n_p, d_model)
    xr, gr, q, k, v = _inproj(xp2, pl.BlockSpec((tm_a, d_model), lambda i: (i, 0)), n_p // tm_a, tm_a,
                              g_mix, w_in_p, d_rnn, d_attn, kv_dim)
    tt = 512 if seq % 512 == 0 else WINDOW
    m_rnn, h_last = _rglru_prompt(xr, gr, cw_r, cb_r, wg, ba, bx, lam, g_rnn, batch, seq, tt)
    m_attn = _swa_prompt(q, k, v, code_p, tbl, sinks, gn_attn_p, batch, seq)
    tm_f = 512 if seq % 512 == 0 else WINDOW
    tf = 512
    y_p, tail_g, tail_v = _ffn_prompt(xp2, m_rnn, m_attn, w_out_p, g_ffn, w_up_b, cw_f, cb_f, w_down_b,
                                      g_fin, batch, seq, tm_f, tf)
    y_prompt = y_p.reshape(batch, seq, d_model)
    p_rnn_conv = xr.reshape(batch, seq, d_rnn)[:, seq - 3:, :][None]
    p_rnn_h = h_last[None]
    p_win_k = k.reshape(batch, seq, KV_HEADS, HEAD_DIM)[:, seq - win:][None]
    p_win_v = v.reshape(batch, seq, KV_HEADS, HEAD_DIM)[:, seq - win:][None]
    p_ffn_conv = jnp.concatenate([tail_g[:, SUBLANES - 2:, :], tail_v[:, SUBLANES - 2:, :]], axis=-1)[None]

    n_s = nb * steps
    xs2 = x_sample.reshape(nb, steps * d_model)
    xr_s, gr_s, q_s, k_s, v_s = _inproj(xs2, pl.BlockSpec((nb, d_model), lambda t: (0, t)), steps, nb,
                                        g_mix, w_in_p, d_rnn, d_attn, kv_dim)
    conv2d = state_rnn_conv[0].reshape(nb, 3 * d_rnn)
    m_rnn_s, h_new = _rglru_sample(xr_s, gr_s, conv2d, state_rnn_h[0], cw_r, cb_r, wg, ba, bx, lam, g_rnn,
                                   nb, steps)
    ncg = KV_HEADS // 2
    qs = q_s.reshape(steps, nb, ncg, GQA_GROUP, LANES).transpose(1, 2, 3, 0, 4).reshape(
        nb, ncg, GQA_GROUP * steps, LANES)
    k_new = k_s.reshape(steps, nb, kv_dim).transpose(1, 0, 2)
    v_new = v_s.reshape(steps, nb, kv_dim).transpose(1, 0, 2)
    pad = ((0, 0), (0, npad - steps), (0, 0))
    ck = cache_win_k[0].reshape(nb, win, kv_dim)
    cv = cache_win_v[0].reshape(nb, win, kv_dim)
    bb = SUBLANES
    o_s = _swa_sample(qs, ck, cv, jnp.pad(k_new, pad), jnp.pad(v_new, pad), code_c, code_n, tbl, sinks,
                      steps, bb)
    ya_s = o_s.reshape(nb, ncg, GQA_GROUP, steps, LANES).transpose(3, 0, 1, 2, 4).reshape(n_s, d_attn)
    xs_tm = x_sample.transpose(1, 0, 2).reshape(n_s, d_model)
    ffn2d = state_ffn_conv[0].reshape(nb, 2 * 2 * d_ff)
    y_s, ns_g, ns_v = _ffn_sample(xs_tm, m_rnn_s, ya_s, gn_attn_p, w_out_p, g_ffn, w_up_b, cw_f, cb_f,
                                  w_down_b, g_fin, ffn2d, nb, steps, tf)
    y_sample = y_s.reshape(steps, nb, d_model).transpose(1, 0, 2)
    s_rnn_conv = xr_s.reshape(steps, nb, d_rnn)[steps - 3:].transpose(1, 0, 2)[None]
    s_rnn_h = h_new[None]
    s_win_k = jnp.concatenate([ck[:, steps:], k_new], axis=1).reshape(1, nb, win, KV_HEADS, HEAD_DIM)
    s_win_v = jnp.concatenate([cv[:, steps:], v_new], axis=1).reshape(1, nb, win, KV_HEADS, HEAD_DIM)
    s_ffn_conv = jnp.concatenate([ns_g, ns_v], axis=-1).transpose(1, 0, 2)[None]

    return (y_prompt, y_sample, p_rnn_conv, p_rnn_h, p_win_k, p_win_v, p_ffn_conv,
            s_rnn_conv, s_rnn_h, s_win_k, s_win_v, s_ffn_conv)
```

```python
import functools
import math

import numpy as np
import jax
import jax.numpy as jnp
from jax import lax
from jax.experimental import pallas as pl
from jax.experimental.pallas import tpu as pltpu

F32 = jnp.float32
BF16 = jnp.bfloat16

HEAD_DIM = 64
KV_HEADS = 4
N_HEADS = 16
GQA_GROUP = N_HEADS // KV_HEADS
RNN_BLOCKS = 16
RG_C = 8.0
WINDOW = 128
N_BUCKETS = 32
MAX_EXACT = N_BUCKETS // 2
REL_MAX_DIST = 128
EPS = 1e-6
NEG_INF = -1e30
ATTN_SCALE = HEAD_DIM ** -0.5

LANES = 128
SUBLANES = 8
VMEM_LIMIT_BYTES = 56 * 1024 * 1024

GATE_TILE = 256


def _t5_bucket_np(d):
    n = np.maximum(d, 0)
    nf = np.maximum(n, 1).astype(np.float32)
    large = MAX_EXACT + (np.log(nf / MAX_EXACT) / math.log(REL_MAX_DIST / MAX_EXACT)
                         * (N_BUCKETS - MAX_EXACT)).astype(np.int32)
    large = np.minimum(large, N_BUCKETS - 1)
    return np.where(n < MAX_EXACT, n, large).astype(np.int32)


def _rms(x, g):
    ms = jnp.mean(x * x, axis=-1, keepdims=True)
    return (x * lax.rsqrt(ms + EPS)) * g


def _softplus(x):
    return jnp.maximum(x, 0.0) + jnp.log1p(jnp.exp(-jnp.abs(x)))


def _const_spec(shape):
    nd = len(shape)
    return pl.BlockSpec(shape, lambda *_: (0,) * nd, pipeline_mode=pl.Buffered(1))


def _smem_spec():
    return pl.BlockSpec(memory_space=pltpu.SMEM)


def _inproj_body(x_ref, g_ref, w_ref, xr_ref, gr_ref, q_ref, k_ref, v_ref, *, d_rnn, d_attn, kv_dim):
    xn = _rms(x_ref[...], g_ref[...])
    p = jnp.dot(xn.astype(BF16), w_ref[...], preferred_element_type=F32)
    o1, o2, o3, o4 = d_rnn, 2 * d_rnn, 2 * d_rnn + d_attn, 2 * d_rnn + d_attn + kv_dim
    xr_ref[...] = p[:, :o1]
    gr_ref[...] = p[:, o1:o2]
    q_ref[...] = (p[:, o2:o3] * ATTN_SCALE).astype(BF16)
    k_ref[...] = p[:, o3:o4]
    v_ref[...] = p[:, o4:]


def _inproj(x2d, x_spec, n_steps, tm, g, w, d_rnn, d_attn, kv_dim):
    n = n_steps * tm
    d_model = g.shape[-1]
    row = lambda i: (i, 0)
    out_shape = (
        jax.ShapeDtypeStruct((n, d_rnn), F32),
        jax.ShapeDtypeStruct((n, d_rnn), F32),
        jax.ShapeDtypeStruct((n, d_attn), BF16),
        jax.ShapeDtypeStruct((n, kv_dim), F32),
        jax.ShapeDtypeStruct((n, kv_dim), F32),
    )
    out_specs = (
        pl.BlockSpec((tm, d_rnn), row),
        pl.BlockSpec((tm, d_rnn), row),
        pl.BlockSpec((tm, d_attn), row),
        pl.BlockSpec((tm, kv_dim), row),
        pl.BlockSpec((tm, kv_dim), row),
    )
    return pl.pallas_call(
        functools.partial(_inproj_body, d_rnn=d_rnn, d_attn=d_attn, kv_dim=kv_dim),
        grid=(n_steps,),
        in_specs=[x_spec, _const_spec((1, d_model)), _const_spec(w.shape)],
        out_specs=out_specs,
        out_shape=out_shape,
        compiler_params=pltpu.CompilerParams(
            dimension_semantics=("arbitrary",), vmem_limit_bytes=VMEM_LIMIT_BYTES),
        name="inproj",
    )(x2d, g, w)


def _rglru_gates(xc, wg_ref, ba, bx, lam):
    d_rnn = xc.shape[-1]
    pre_a, pre_x = [], []
    for c in range(d_rnn // GATE_TILE):
        xb = xc[:, c * GATE_TILE:(c + 1) * GATE_TILE].astype(BF16)
        pre = jnp.dot(xb, wg_ref[c], preferred_element_type=F32)
        pre_a.append(pre[:, :GATE_TILE])
        pre_x.append(pre[:, GATE_TILE:])
    r = jax.nn.sigmoid(jnp.concatenate(pre_a, axis=1) + ba)
    i = jax.nn.sigmoid(jnp.concatenate(pre_x, axis=1) + bx)
    log_a = (-RG_C * r) * _softplus(-lam)
    a = jnp.exp(log_a)
    one_minus_a2 = -jnp.tanh(log_a) * (a * a + 1.0)
    u = jnp.sqrt(one_minus_a2) * (i * xc)
    return a, u


def _rglru_prompt_body(xr_ref, gr_ref, cw_ref, cb_ref, wg_ref, ba_ref, bx_ref, lam_ref, gn_ref,
                       out_ref, hlast_ref, xbuf, abuf, ubuf, hbuf, hc_ref, *, tt, conv_w):
    ti = pl.program_id(1)
    d_rnn = xr_ref.shape[-1]

    @pl.when(ti == 0)
    def _():
        xbuf[0:SUBLANES, :] = jnp.zeros((SUBLANES, d_rnn), F32)
        hc_ref[...] = jnp.zeros_like(hc_ref)

    x = xr_ref[...]
    xbuf[SUBLANES:SUBLANES + tt, :] = x
    xc = cw_ref[conv_w - 1:conv_w, :] * x
    for k in range(conv_w - 1):
        off = SUBLANES - (conv_w - 1) + k
        xc = xc + cw_ref[k:k + 1, :] * xbuf[off:off + tt, :]
    xc = xc + cb_ref[...]
    xbuf[0:SUBLANES, :] = xbuf[tt:tt + SUBLANES, :]

    a, u = _rglru_gates(xc, wg_ref, ba_ref[...], bx_ref[...], lam_ref[...])
    abuf[...] = a
    ubuf[...] = u

    row = lax.broadcasted_iota(jnp.int32, (SUBLANES, d_rnn), 0)

    def group(gidx, hc):
        r0 = pl.multiple_of(gidx * SUBLANES, SUBLANES)
        ag = abuf[pl.ds(r0, SUBLANES), :]
        ug = ubuf[pl.ds(r0, SUBLANES), :]
        for k in (1, 2, 4):
            a_prev = jnp.where(row >= k, pltpu.roll(ag, k, 0), 1.0)
            u_prev = jnp.where(row >= k, pltpu.roll(ug, k, 0), 0.0)
            ug = ag * u_prev + ug
            ag = ag * a_prev
        h = ag * hc + ug
        hbuf[pl.ds(r0, SUBLANES), :] = h
        return jnp.broadcast_to(h[SUBLANES - 1:SUBLANES, :], (SUBLANES, d_rnn))

    hc = lax.fori_loop(0, tt // SUBLANES, group, hc_ref[...], unroll=2)
    hc_ref[...] = hc
    hlast_ref[0] = hc

    y = jax.nn.gelu(gr_ref[...]) * hbuf[...]
    out_ref[...] = _rms(y, gn_ref[...]).astype(BF16)


def _rglru_prompt(xr, gr, cw, cb, wg, ba, bx, lam, gn, batch, seq, tt):
    d_rnn = xr.shape[-1]
    nt = seq // tt
    conv_w = cw.shape[0]
    tile = lambda b, t: (b * nt + t, 0)
    out, hlast = pl.pallas_call(
        functools.partial(_rglru_prompt_body, tt=tt, conv_w=conv_w),
        grid=(batch, nt),
        in_specs=[
            pl.BlockSpec((tt, d_rnn), tile),
            pl.BlockSpec((tt, d_rnn), tile),
            _const_spec(cw.shape), _const_spec(cb.shape), _const_spec(wg.shape),
            _const_spec(ba.shape), _const_spec(bx.shape), _const_spec(lam.shape), _const_spec(gn.shape),
        ],
        out_specs=(
            pl.BlockSpec((tt, d_rnn), tile),
            pl.BlockSpec((1, SUBLANES, d_rnn), lambda b, t: (b, 0, 0)),
        ),
        out_shape=(
            jax.ShapeDtypeStruct((batch * seq, d_rnn), BF16),
            jax.ShapeDtypeStruct((batch, SUBLANES, d_rnn), F32),
        ),
        scratch_shapes=[
            pltpu.VMEM((tt + SUBLANES, d_rnn), F32),
            pltpu.VMEM((tt, d_rnn), F32),
            pltpu.VMEM((tt, d_rnn), F32),
            pltpu.VMEM((tt, d_rnn), F32),
            pltpu.VMEM((SUBLANES, d_rnn), F32),
        ],
        compiler_params=pltpu.CompilerParams(
            dimension_semantics=("arbitrary", "arbitrary"), vmem_limit_bytes=VMEM_LIMIT_BYTES),
        name="rglru_prompt",
    )(xr, gr, cw, cb, wg, ba, bx, lam, gn)
    return out, hlast[:, 0, :]


def _rglru_sample_body(xr_ref, gr_ref, c0_ref, c1_ref, c2_ref, h0_ref, cw_ref, cb_ref, wg_ref,
                       ba_ref, bx_ref, lam_ref, gn_ref, out_ref, hnew_ref, *, nb, steps):
    x = xr_ref[...]
    hist = [c0_ref[...], c1_ref[...], c2_ref[...]]
    xs = [x[t * nb:(t + 1) * nb, :] for t in range(steps)]
    xp = hist + xs
    conv_w = len(hist) + 1
    xc = cw_ref[conv_w - 1:conv_w, :] * x
    for k in range(conv_w - 1):
        shifted = jnp.concatenate(xp[k:k + steps], axis=0)
        xc = xc + cw_ref[k:k + 1, :] * shifted
    xc = xc + cb_ref[...]
    a, u = _rglru_gates(xc, wg_ref, ba_ref[...], bx_ref[...], lam_ref[...])
    h = h0_ref[...]
    hs = []
    for t in range(steps):
        h = a[t * nb:(t + 1) * nb, :] * h + u[t * nb:(t + 1) * nb, :]
        hs.append(h)
    hnew_ref[...] = h
    y = jax.nn.gelu(gr_ref[...]) * jnp.concatenate(hs, axis=0)
    out_ref[...] = _rms(y, gn_ref[...]).astype(BF16)


def _rglru_sample(xr, gr, conv_state2d, h0, cw, cb, wg, ba, bx, lam, gn, nb, steps):
    d_rnn = xr.shape[-1]
    n = nb * steps
    assert cw.shape[0] == 4
    full = lambda shape: pl.BlockSpec(shape, lambda i: (0,) * len(shape))
    return pl.pallas_call(
        functools.partial(_rglru_sample_body, nb=nb, steps=steps),
        grid=(1,),
        in_specs=[
            full((n, d_rnn)), full((n, d_rnn)),
            pl.BlockSpec((nb, d_rnn), lambda i: (0, 0)),
            pl.BlockSpec((nb, d_rnn), lambda i: (0, 1)),
            pl.BlockSpec((nb, d_rnn), lambda i: (0, 2)),
            full((nb, d_rnn)),
            full(cw.shape), full(cb.shape), full(wg.shape), full(ba.shape), full(bx.shape),
            full(lam.shape), full(gn.shape),
        ],
        out_specs=(full((n, d_rnn)), full((nb, d_rnn))),
        out_shape=(jax.ShapeDtypeStruct((n, d_rnn), BF16), jax.ShapeDtypeStruct((nb, d_rnn), F32)),
        compiler_params=pltpu.CompilerParams(
            dimension_semantics=("arbitrary",), vmem_limit_bytes=VMEM_LIMIT_BYTES),
        name="rglru_sample",
    )(xr, gr, conv_state2d, conv_state2d, conv_state2d, h0, cw, cb, wg, ba, bx, lam, gn)


def _swa_prompt_body(code_ref, tbl_ref, sink_ref, q_ref, kp_ref, kc_ref, vp_ref, vc_ref, gn_ref,
                     out_ref, bias_ref):
    j = pl.program_id(1)
    blk = q_ref.shape[0]
    rows = GQA_GROUP * blk

    @pl.when((pl.program_id(0) == 0) & (j == 0))
    def _():
        code = code_ref[...]
        qi = lax.broadcasted_iota(jnp.int32, code.shape, 0)
        kj = lax.broadcasted_iota(jnp.int32, code.shape, 1)
        dist = blk + qi - kj
        in_window = (dist & -WINDOW) == 0
        for h in range(N_HEADS):
            def pick(b, acc, h=h):
                return jnp.where(code == b, tbl_ref[b * N_HEADS + h], acc)
            bias_h = lax.fori_loop(0, N_BUCKETS, pick, jnp.zeros(code.shape, F32))
            bias_h = jnp.where(in_window, bias_h, NEG_INF)
            kv, g = divmod(h, GQA_GROUP)
            bias_ref[1, kv, g * blk:(g + 1) * blk, :] = bias_h
            bias_ref[0, kv, g * blk:(g + 1) * blk, :] = jnp.where(kj >= blk, bias_h, NEG_INF)

    has_prev = jnp.where(j > 0, 1, 0)

    kband = jnp.concatenate([kp_ref[...], kc_ref[...]], axis=0)
    vband = jnp.concatenate([vp_ref[...], vc_ref[...]], axis=0)
    lane = lax.broadcasted_iota(jnp.int32, (2 * blk, LANES), 1)
    row_g = lax.broadcasted_iota(jnp.int32, (rows, 1), 0) // blk
    q = q_ref[...]

    outs = []
    for cg in range(KV_HEADS // 2):
        qs = jnp.concatenate(
            [q[:, (cg * GQA_GROUP + g) * LANES:(cg * GQA_GROUP + g + 1) * LANES] for g in range(GQA_GROUP)],
            axis=0)
        ka = kband[:, cg * LANES:(cg + 1) * LANES]
        va = vband[:, cg * LANES:(cg + 1) * LANES]
        o = None
        for par in range(2):
            kv = 2 * cg + par
            half = (lane < HEAD_DIM) if par == 0 else (lane >= HEAD_DIM)
            km = jnp.where(half, ka, 0.0).astype(BF16)
            vm = jnp.where(half, va, 0.0).astype(BF16)
            s = lax.dot_general(qs, km, (((1,), (1,)), ((), ())), preferred_element_type=F32)
            s = s + bias_ref[has_prev, kv]
            sink = jnp.zeros((rows, 1), F32)
            for g in range(GQA_GROUP):
                sink = jnp.where(row_g == g, sink_ref[kv * GQA_GROUP + g], sink)
            m = jnp.maximum(jnp.max(s, axis=-1, keepdims=True), sink)
            p = jnp.exp(s - m)
            denom = jnp.sum(p, axis=-1, keepdims=True) + jnp.exp(sink - m)
            w = p * (1.0 / denom)
            part = jnp.dot(w.astype(BF16), vm, preferred_element_type=F32)
            o = part if o is None else o + part
        outs.extend(o[g * blk:(g + 1) * blk, :] for g in range(GQA_GROUP))
    y = jnp.concatenate(outs, axis=1)
    out_ref[...] = _rms(y, gn_ref[...]).astype(BF16)


def _swa_prompt(q, k, v, code, tbl, sinks, gn, batch, seq):
    blk = WINDOW
    nb = seq // blk
    d_attn = q.shape[-1]
    kv_dim = k.shape[-1]
    cur = lambda b, j: (b * nb + j, 0)
    prev = lambda b, j: (b * nb + jnp.maximum(j - 1, 0), 0)
    return pl.pallas_call(
        _swa_prompt_body,
        grid=(batch, nb),
        in_specs=[
            _const_spec(code.shape), _smem_spec(), _smem_spec(),
            pl.BlockSpec((blk, d_attn), cur),
            pl.BlockSpec((blk, kv_dim), prev), pl.BlockSpec((blk, kv_dim), cur),
            pl.BlockSpec((blk, kv_dim), prev), pl.BlockSpec((blk, kv_dim), cur),
            _const_spec(gn.shape),
        ],
        out_specs=pl.BlockSpec((blk, d_attn), cur),
        out_shape=jax.ShapeDtypeStruct((batch * seq, d_attn), BF16),
        scratch_shapes=[pltpu.VMEM((2, KV_HEADS, GQA_GROUP * blk, 2 * blk), F32)],
        compiler_params=pltpu.CompilerParams(
            dimension_semantics=("arbitrary", "arbitrary"), vmem_limit_bytes=VMEM_LIMIT_BYTES),
        name="swa_prompt",
    )(code, tbl, sinks, q, k, k, v, v, gn)


def _swa_sample_body(codec_ref, coden_ref, tbl_ref, sink_ref, q_ref, ck_ref, cv_ref, kn_ref, vn_ref,
                     out_ref, biasc_ref, biasn_ref, sinkc_ref, *, steps):
    bb = q_ref.shape[0]
    nrow = N_HEADS * steps
    win = ck_ref.shape[1]
    npad = kn_ref.shape[1]

    @pl.when(pl.program_id(0) == 0)
    def _():
        codec = codec_ref[...]
        coden = coden_ref[...]
        hrow = lax.broadcasted_iota(jnp.int32, (nrow, 1), 0) // steps

        def pick(idx, accs):
            ac, an = accs
            val = tbl_ref[idx]
            return jnp.where(codec == idx, val, ac), jnp.where(coden == idx, val, an)
        bc, bn = lax.fori_loop(0, N_BUCKETS * N_HEADS, pick,
                               (jnp.zeros(codec.shape, F32), jnp.zeros(coden.shape, F32)))
        biasc_ref[...] = bc
        biasn_ref[...] = bn

        def pick_sink(h, acc):
            return jnp.where(hrow == h, sink_ref[h], acc)
        sinkc_ref[...] = lax.fori_loop(0, N_HEADS, pick_sink, jnp.zeros((nrow, 1), F32))

    q = q_ref[...].astype(F32)
    lane = lax.broadcasted_iota(jnp.int32, (bb, GQA_GROUP * steps, LANES), 2)
    zeros = jnp.zeros((bb, GQA_GROUP * steps, LANES), F32)
    pieces = []
    for cg in range(KV_HEADS // 2):
        for par in range(2):
            half = (lane < HEAD_DIM) if par == 0 else (lane >= HEAD_DIM)
            qm = jnp.where(half, q[:, cg], zeros)
            pieces.append(jnp.concatenate([qm, zeros] if cg == 0 else [zeros, qm], axis=2))
    qm = jnp.concatenate(pieces, axis=1).astype(BF16)

    ck = ck_ref[...].astype(BF16)
    kn = kn_ref[...].astype(BF16)
    s_c = jnp.einsum("bqd,bkd->bqk", qm, ck, preferred_element_type=F32)
    s_n = jnp.einsum("bqd,bkd->bqk", qm, kn, preferred_element_type=F32)

    t_c = lax.broadcasted_iota(jnp.int32, (nrow, win), 0) % steps
    k_c = lax.broadcasted_iota(jnp.int32, (nrow, win), 1)
    valid_c = k_c > t_c
    t_n = lax.broadcasted_iota(jnp.int32, (nrow, npad), 0) % steps
    k_n = lax.broadcasted_iota(jnp.int32, (nrow, npad), 1)
    valid_n = k_n <= t_n

    s_c = jnp.where(valid_c[None], s_c + biasc_ref[...][None], NEG_INF)
    s_n = jnp.where(valid_n[None], s_n + biasn_ref[...][None], NEG_INF)
    sink = sinkc_ref[...][None]
    m = jnp.maximum(jnp.maximum(jnp.max(s_c, axis=-1, keepdims=True),
                                jnp.max(s_n, axis=-1, keepdims=True)), sink)
    p_c = jnp.exp(s_c - m)
    p_n = jnp.exp(s_n - m)
    denom = (jnp.sum(p_c, axis=-1, keepdims=True) + jnp.sum(p_n, axis=-1, keepdims=True)
             + jnp.exp(sink - m))
    r = 1.0 / denom
    w_c = (p_c * r).astype(BF16)
    w_n = (p_n * r).astype(BF16)
    o = (jnp.einsum("bqk,bkd->bqd", w_c, cv_ref[...].astype(BF16), preferred_element_type=F32)
         + jnp.einsum("bqk,bkd->bqd", w_n, vn_ref[...].astype(BF16), preferred_element_type=F32))
    gt = GQA_GROUP * steps
    lane_o = lax.broadcasted_iota(jnp.int32, (bb, gt, LANES), 2)
    for cg in range(KV_HEADS // 2):
        lo = o[:, cg * 2 * gt:cg * 2 * gt + gt, cg * LANES:(cg + 1) * LANES]
        hi = o[:, cg * 2 * gt + gt:(cg + 1) * 2 * gt, cg * LANES:(cg + 1) * LANES]
        out_ref[:, cg] = jnp.where(lane_o < HEAD_DIM, lo, hi)


def _swa_sample(qs, ck, cv, kn, vn, codec, coden, tbl, sinks, steps, bb):
    nbatch, ncg, gt, _ = qs.shape
    win, kv_dim = ck.shape[1], ck.shape[2]
    npad = kn.shape[1]
    nrow = N_HEADS * steps
    blk4 = lambda i: (i, 0, 0, 0)
    blk3 = lambda i: (i, 0, 0)
    return pl.pallas_call(
        functools.partial(_swa_sample_body, steps=steps),
        grid=(nbatch // bb,),
        in_specs=[
            _const_spec(codec.shape), _const_spec(coden.shape), _smem_spec(), _smem_spec(),
            pl.BlockSpec((bb, ncg, gt, LANES), blk4),
            pl.BlockSpec((bb, win, kv_dim), blk3), pl.BlockSpec((bb, win, kv_dim), blk3),
            pl.BlockSpec((bb, npad, kv_dim), blk3), pl.BlockSpec((bb, npad, kv_dim), blk3),
        ],
        out_specs=pl.BlockSpec((bb, ncg, gt, LANES), blk4),
        out_shape=jax.ShapeDtypeStruct((nbatch, ncg, gt, LANES), F32),
        scratch_shapes=[pltpu.VMEM((nrow, win), F32), pltpu.VMEM((nrow, npad), F32),
                        pltpu.VMEM((nrow, 1), F32)],
        compiler_params=pltpu.CompilerParams(
            dimension_semantics=("arbitrary",), vmem_limit_bytes=VMEM_LIMIT_BYTES),
        name="swa_sample",
    )(codec, coden, tbl, sinks, qs, ck, cv, kn, vn)


def _shift_rows(up, hist, k):
    rolled = pltpu.roll(up, k, 0)
    row = lax.broadcasted_iota(jnp.int32, hist.shape, 0)
    head = jnp.where(row < k, pltpu.roll(hist, k, 0), rolled[0:SUBLANES, :])
    return jnp.concatenate([head, rolled[SUBLANES:, :]], axis=0)


def _ffn_conv(up, hist, cw, cb, *, tm, sample_nb):
    if sample_nb is None:
        prev2 = _shift_rows(up, hist, 2)
        prev1 = _shift_rows(up, hist, 1)
    else:
        s0, s1 = hist
        prev2 = jnp.concatenate([s0, s1, up[:tm - 2 * sample_nb, :]], axis=0)
        prev1 = jnp.concatenate([s1, up[:tm - sample_nb, :]], axis=0)
    return cw[0:1, :] * prev2 + cw[1:2, :] * prev1 + cw[2:3, :] * up + cb


FFN_CHUNK = 256
FFN_SLAB = 256


def _ffn_body(*refs, tm, tiles_per_seq, sample_nb):
    if sample_nb is None:
        (x_ref, mr_ref, ma_ref, wo_ref, gf_ref, wg_ref, wv_ref, cwg_ref, cwv_ref, cbg_ref, cbv_ref,
         wd_ref, gfin_ref, y_ref, tg_ref, tv_ref, hn_ref, car_g, car_v) = refs
    else:
        (x_ref, mr_ref, ma_ref, gna_ref, wo_ref, gf_ref, wg_ref, wv_ref, cwg_ref, cwv_ref, cbg_ref,
         cbv_ref, wd_ref, gfin_ref, s0g_ref, s1g_ref, s0v_ref, s1v_ref, y_ref, tg_ref, tv_ref,
         hn_ref) = refs
    i = pl.program_id(0)
    j = pl.program_id(1)
    nj = pl.num_programs(1)

    @pl.when(j == 0)
    def _():
        if sample_nb is None:
            ma = ma_ref[...]
        else:
            ma = _rms(ma_ref[...], gna_ref[...]).astype(BF16)
        merged = jnp.concatenate([mr_ref[...], ma], axis=1)
        h = x_ref[...] + jnp.dot(merged, wo_ref[...], preferred_element_type=F32)
        y_ref[...] = h
        hn_ref[...] = _rms(h, gf_ref[...]).astype(BF16)

    if sample_nb is None:
        @pl.when((i == 0) & (j == 0))
        def _():
            car_g[...] = jnp.zeros_like(car_g)
            car_v[...] = jnp.zeros_like(car_v)
        seq_start = i % tiles_per_seq == 0

    if sample_nb is None:
        n_slab = tm // FFN_SLAB
        cwg, cwv, cbg, cbv = cwg_ref[...], cwv_ref[...], cbg_ref[...], cbv_ref[...]

        def up_proj(r):
            hn = hn_ref[r * FFN_SLAB:(r + 1) * FFN_SLAB, :]
            return (jnp.dot(hn, wg_ref[0], preferred_element_type=F32),
                    jnp.dot(hn, wv_ref[0], preferred_element_type=F32))

        hist_g = jnp.where(seq_start, 0.0, car_g[j])
        hist_v = jnp.where(seq_start, 0.0, car_v[j])
        pending = up_proj(0)
        for r in range(n_slab):
            up_g, up_v = pending
            if r + 1 < n_slab:
                pending = up_proj(r + 1)
            c_g = _ffn_conv(up_g, hist_g, cwg, cbg, tm=FFN_SLAB, sample_nb=None)
            c_v = _ffn_conv(up_v, hist_v, cwv, cbv, tm=FFN_SLAB, sample_nb=None)
            hist_g = up_g[FFN_SLAB - SUBLANES:, :]
            hist_v = up_v[FFN_SLAB - SUBLANES:, :]
            act = (jax.nn.gelu(c_g) * c_v).astype(BF16)
            rows = slice(r * FFN_SLAB, (r + 1) * FFN_SLAB)
            y_ref[rows, :] += jnp.dot(act, wd_ref[...], preferred_element_type=F32)
        car_g[j] = hist_g
        car_v[j] = hist_v
        tg_ref[0] = hist_g
        tv_ref[0] = hist_v
    else:
        hn = hn_ref[...]
        tf = wg_ref.shape[2]
        acts = []
        for c in range(tf // FFN_CHUNK):
            cs = slice(c * FFN_CHUNK, (c + 1) * FFN_CHUNK)
            up_g = jnp.dot(hn, wg_ref[0, :, cs], preferred_element_type=F32)
            up_v = jnp.dot(hn, wv_ref[0, :, cs], preferred_element_type=F32)
            hist_g = (s0g_ref[:, cs], s1g_ref[:, cs])
            hist_v = (s0v_ref[:, cs], s1v_ref[:, cs])
            tg_ref[0, :, cs] = up_g[tm - 2 * sample_nb:tm - sample_nb, :]
            tg_ref[1, :, cs] = up_g[tm - sample_nb:, :]
            tv_ref[0, :, cs] = up_v[tm - 2 * sample_nb:tm - sample_nb, :]
            tv_ref[1, :, cs] = up_v[tm - sample_nb:, :]
            c_g = _ffn_conv(up_g, hist_g, cwg_ref[:, cs], cbg_ref[:, cs], tm=tm, sample_nb=sample_nb)
            c_v = _ffn_conv(up_v, hist_v, cwv_ref[:, cs], cbv_ref[:, cs], tm=tm, sample_nb=sample_nb)
            acts.append((jax.nn.gelu(c_g) * c_v).astype(BF16))
        act = jnp.concatenate(acts, axis=1)
        y_ref[...] += jnp.dot(act, wd_ref[...], preferred_element_type=F32)

    @pl.when(j == nj - 1)
    def _():
        y_ref[...] = _rms(y_ref[...], gfin_ref[...])


def _ffn_prompt(x, mr, ma, wo, gf, wup, cw, cb, wd, gfin, batch, seq, tm, tf):
    n, d_model = x.shape
    d_half = mr.shape[-1]
    d_ff = wd.shape[0]
    ni, nj = n // tm, d_ff // tf
    tiles_per_seq = seq // tm
    row = lambda i, j: (i, 0)
    gate_col = lambda i, j: (0, j)
    val_col = lambda i, j: (0, nj + j)
    tail = lambda i, j: (i, 0, j)
    y, tg, tv = pl.pallas_call(
        functools.partial(_ffn_body, tm=tm, tiles_per_seq=tiles_per_seq, sample_nb=None),
        grid=(ni, nj),
        in_specs=[
            pl.BlockSpec((tm, d_model), row),
            pl.BlockSpec((tm, d_half), row), pl.BlockSpec((tm, d_half), row),
            _const_spec(wo.shape), _const_spec(gf.shape),
            pl.BlockSpec((1, d_model, tf), lambda i, j: (j, 0, 0)),
            pl.BlockSpec((1, d_model, tf), lambda i, j: (nj + j, 0, 0)),
            pl.BlockSpec((cw.shape[0], tf), gate_col), pl.BlockSpec((cw.shape[0], tf), val_col),
            pl.BlockSpec((1, tf), gate_col), pl.BlockSpec((1, tf), val_col),
            pl.BlockSpec((tf, d_model), lambda i, j: (j, 0)),
            _const_spec(gfin.shape),
        ],
        out_specs=(
            pl.BlockSpec((tm, d_model), row),
            pl.BlockSpec((1, SUBLANES, tf), tail),
            pl.BlockSpec((1, SUBLANES, tf), tail),
        ),
        out_shape=(
            jax.ShapeDtypeStruct((n, d_model), F32),
            jax.ShapeDtypeStruct((ni, SUBLANES, d_ff), F32),
            jax.ShapeDtypeStruct((ni, SUBLANES, d_ff), F32),
        ),
        scratch_shapes=[
            pltpu.VMEM((tm, d_model), BF16),
            pltpu.VMEM((nj, SUBLANES, tf), F32), pltpu.VMEM((nj, SUBLANES, tf), F32),
        ],
        compiler_params=pltpu.CompilerParams(
            dimension_semantics=("arbitrary", "arbitrary"), vmem_limit_bytes=VMEM_LIMIT_BYTES),
        name="ffn_prompt",
    )(x, mr, ma, wo, gf, wup, wup, cw, cw, cb, cb, wd, gfin)
    return y, tg[tiles_per_seq - 1::tiles_per_seq], tv[tiles_per_seq - 1::tiles_per_seq]


def _ffn_sample(x, mr, ma, gna, wo, gf, wup, cw, cb, wd, gfin, state2d, nb, steps, tf):
    n, d_model = x.shape
    d_half = mr.shape[-1]
    d_ff = wd.shape[0]
    nj = d_ff // tf
    full2 = lambda shape: pl.BlockSpec(shape, lambda i, j: (0, 0))
    gate_col = lambda i, j: (0, j)
    val_col = lambda i, j: (0, nj + j)
    y, tg, tv = pl.pallas_call(
        functools.partial(_ffn_body, tm=n, tiles_per_seq=1, sample_nb=nb),
        grid=(1, nj),
        in_specs=[
            full2((n, d_model)), full2((n, d_half)), full2((n, d_half)), full2(gna.shape),
            _const_spec(wo.shape), _const_spec(gf.shape),
            pl.BlockSpec((1, d_model, tf), lambda i, j: (j, 0, 0)),
            pl.BlockSpec((1, d_model, tf), lambda i, j: (nj + j, 0, 0)),
            pl.BlockSpec((cw.shape[0], tf), gate_col), pl.BlockSpec((cw.shape[0], tf), val_col),
            pl.BlockSpec((1, tf), gate_col), pl.BlockSpec((1, tf), val_col),
            pl.BlockSpec((tf, d_model), lambda i, j: (j, 0)),
            _const_spec(gfin.shape),
            pl.BlockSpec((nb, tf), lambda i, j: (0, j)),
            pl.BlockSpec((nb, tf), lambda i, j: (0, 2 * nj + j)),
            pl.BlockSpec((nb, tf), lambda i, j: (0, nj + j)),
            pl.BlockSpec((nb, tf), lambda i, j: (0, 3 * nj + j)),
        ],
        out_specs=(
            full2((n, d_model)),
            pl.BlockSpec((2, nb, tf), lambda i, j: (0, 0, j)),
            pl.BlockSpec((2, nb, tf), lambda i, j: (0, 0, j)),
        ),
        out_shape=(
            jax.ShapeDtypeStruct((n, d_model), F32),
            jax.ShapeDtypeStruct((2, nb, d_ff), F32),
            jax.ShapeDtypeStruct((2, nb, d_ff), F32),
        ),
        scratch_shapes=[pltpu.VMEM((n, d_model), BF16)],
        compiler_params=pltpu.CompilerParams(
            dimension_semantics=("arbitrary", "arbitrary"), vmem_limit_bytes=VMEM_LIMIT_BYTES),
        name="ffn_sample",
    )(x, mr, ma, gna, wo, gf, wup, wup, cw, cw, cb, cb, wd, gfin, state2d, state2d, state2d, state2d)
    return y, tg, tv


def _q_perm():
    idx = np.empty((KV_HEADS // 2, GQA_GROUP, 2, HEAD_DIM), np.int32)
    for cg in range(KV_HEADS // 2):
        for g in range(GQA_GROUP):
            for par in range(2):
                h = GQA_GROUP * (2 * cg + par) + g
                idx[cg, g, par] = h * HEAD_DIM + np.arange(HEAD_DIM)
    return idx.reshape(-1)


def _gate_weights(w_a, w_x):
    nblk, bs, _ = w_a.shape
    per = GATE_TILE // bs
    eye = jnp.eye(per, dtype=w_a.dtype)

    def pack(w):
        w4 = w.reshape(nblk // per, per, bs, bs)
        return jnp.einsum("cgij,gh->cgihj", w4, eye).reshape(nblk // per, GATE_TILE, GATE_TILE)
    return jnp.concatenate([pack(w_a), pack(w_x)], axis=-1).astype(BF16)


def kernel(x_prompt, x_sample, state_rnn_conv, state_rnn_h, cache_win_k, cache_win_v, state_ffn_conv,
           norm_mix_g, w_in, rnn_conv_w, rnn_conv_b, w_gate_a, b_gate_a, w_gate_x, b_gate_x, rnn_lambda,
           attn_sinks, rel_bias_table, gn_rnn_g, gn_attn_g, w_out, norm_ffn_g, w_up, ffn_conv_w,
           ffn_conv_b, w_down, norm_final_g):
    batch, seq, d_model = x_prompt.shape
    nb, steps, _ = x_sample.shape
    depth = w_in.shape[0]
    d_rnn = rnn_conv_w.shape[-1]
    d_attn = N_HEADS * HEAD_DIM
    kv_dim = KV_HEADS * HEAD_DIM
    d_ff = w_down.shape[1]
    win = cache_win_k.shape[2]
    assert depth == 1 and d_rnn + d_attn == d_model and w_in.shape[-1] == 2 * d_rnn + d_attn + 2 * kv_dim
    assert win == WINDOW and seq % WINDOW == 0 and w_gate_a.shape[1] == RNN_BLOCKS
    assert rnn_conv_w.shape[1] == 4 and ffn_conv_w.shape[1] == 3 and steps >= 3 and nb % SUBLANES == 0

    qperm = _q_perm()
    w_in0 = w_in[0]
    o_q = 2 * d_rnn
    w_in_p = jnp.concatenate(
        [w_in0[:, :o_q], w_in0[:, o_q:o_q + d_attn][:, qperm], w_in0[:, o_q + d_attn:]], axis=1).astype(BF16)
    w_out0 = w_out[0]
    w_out_p = jnp.concatenate([w_out0[:d_rnn], w_out0[d_rnn:][qperm]], axis=0).astype(BF16)
    gn_attn_p = gn_attn_g[0][qperm].reshape(1, d_attn)
    tf = 512
    w_up_b = w_up[0].astype(BF16).reshape(d_model, 2 * d_ff // tf, tf).transpose(1, 0, 2)
    w_down_b = w_down[0].astype(BF16)
    wg = _gate_weights(w_gate_a[0], w_gate_x[0])
    row2 = lambda a: a.reshape(1, -1)
    g_mix, g_ffn, g_fin, g_rnn = row2(norm_mix_g[0]), row2(norm_ffn_g[0]), row2(norm_final_g), row2(gn_rnn_g[0])
    cw_r, cb_r = rnn_conv_w[0], row2(rnn_conv_b[0])
    ba, bx, lam = row2(b_gate_a[0]), row2(b_gate_x[0]), row2(rnn_lambda[0])
    cw_f, cb_f = ffn_conv_w[0], row2(ffn_conv_b[0])
    tbl = rel_bias_table.reshape(-1)
    sinks = attn_sinks[0]

    blk = WINDOW
    qi = np.arange(blk)[:, None]
    kj = np.arange(2 * blk)[None, :]
    code_p = jnp.asarray(_t5_bucket_np(blk + qi - kj))
    r = np.arange(N_HEADS * steps)[:, None]
    t_r, h_r = r % steps, r // steps
    npad = 2 * SUBLANES
    code_c = jnp.asarray(_t5_bucket_np(win + t_r - np.arange(win)[None, :]) * N_HEADS + h_r)
    code_n = jnp.asarray(_t5_bucket_np(t_r - np.arange(npad)[None, :]) * N_HEADS + h_r)

    n_p = batch * seq
    tm_a = 512 if n_p % 512 == 0 else WINDOW
    xp2 = x_prompt.reshape(n_p, d_model)
    xr, gr, q, k, v = _inproj(xp2, pl.BlockSpec((tm_a, d_model), lambda i: (i, 0)), n_p // tm_a, tm_a,
                              g_mix, w_in_p, d_rnn, d_attn, kv_dim)
    tt = 512 if seq % 512 == 0 else WINDOW
    m_rnn, h_last = _rglru_prompt(xr, gr, cw_r, cb_r, wg, ba, bx, lam, g_rnn, batch, seq, tt)
    m_attn = _swa_prompt(q, k, v, code_p, tbl, sinks, gn_attn_p, batch, seq)
    tm_f = 512 if seq % 512 == 0 else WINDOW
    y_p, tail_g, tail_v = _ffn_prompt(xp2, m_rnn, m_attn, w_out_p, g_ffn, w_up_b, cw_f, cb_f, w_down_b,
                                      g_fin, batch, seq, tm_f, tf)
    y_prompt = y_p.reshape(batch, seq, d_model)
    p_rnn_conv = xr.reshape(batch, seq, d_rnn)[:, seq - 3:, :][None]
    p_rnn_h = h_last[None]
    p_win_k = k.reshape(batch, seq, KV_HEADS, HEAD_DIM)[:, seq - win:][None]
    p_win_v = v.reshape(batch, seq, KV_HEADS, HEAD_DIM)[:, seq - win:][None]
    p_ffn_conv = jnp.concatenate([tail_g[:, SUBLANES - 2:, :], tail_v[:, SUBLANES - 2:, :]], axis=-1)[None]

    n_s = nb * steps
    xs2 = x_sample.reshape(nb, steps * d_model)
    xr_s, gr_s, q_s, k_s, v_s = _inproj(xs2, pl.BlockSpec((nb, d_model), lambda t: (0, t)), steps, nb,
                                        g_mix, w_in_p, d_rnn, d_attn, kv_dim)
    conv2d = state_rnn_conv[0].reshape(nb, 3 * d_rnn)
    m_rnn_s, h_new = _rglru_sample(xr_s, gr_s, conv2d, state_rnn_h[0], cw_r, cb_r, wg, ba, bx, lam, g_rnn,
                                   nb, steps)
    ncg = KV_HEADS // 2
    qs = q_s.reshape(steps, nb, ncg, GQA_GROUP, LANES).transpose(1, 2, 3, 0, 4).reshape(
        nb, ncg, GQA_GROUP * steps, LANES)
    k_new = k_s.reshape(steps, nb, kv_dim).transpose(1, 0, 2)
    v_new = v_s.reshape(steps, nb, kv_dim).transpose(1, 0, 2)
    pad = ((0, 0), (0, npad - steps), (0, 0))
    ck = cache_win_k[0].reshape(nb, win, kv_dim)
    cv = cache_win_v[0].reshape(nb, win, kv_dim)
    bb = SUBLANES
    o_s = _swa_sample(qs, ck, cv, jnp.pad(k_new, pad), jnp.pad(v_new, pad), code_c, code_n, tbl, sinks,
                      steps, bb)
    ya_s = o_s.reshape(nb, ncg, GQA_GROUP, steps, LANES).transpose(3, 0, 1, 2, 4).reshape(n_s, d_attn)
    xs_tm = x_sample.transpose(1, 0, 2).reshape(n_s, d_model)
    ffn2d = state_ffn_conv[0].reshape(nb, 2 * 2 * d_ff)
    y_s, ns_g, ns_v = _ffn_sample(xs_tm, m_rnn_s, ya_s, gn_attn_p, w_out_p, g_ffn, w_up_b, cw_f, cb_f,
                                  w_down_b, g_fin, ffn2d, nb, steps, tf)
    y_sample = y_s.reshape(steps, nb, d_model).transpose(1, 0, 2)
    s_rnn_conv = xr_s.reshape(steps, nb, d_rnn)[steps - 3:].transpose(1, 0, 2)[None]
    s_rnn_h = h_new[None]
    s_win_k = jnp.concatenate([ck[:, steps:], k_new], axis=1).reshape(1, nb, win, KV_HEADS, HEAD_DIM)
    s_win_v = jnp.concatenate([cv[:, steps:], v_new], axis=1).reshape(1, nb, win, KV_HEADS, HEAD_DIM)
    s_ffn_conv = jnp.concatenate([ns_g, ns_v], axis=-1).transpose(1, 0, 2)[None]

    return (y_prompt, y_sample, p_rnn_conv, p_rnn_h, p_win_k, p_win_v, p_ffn_conv,
            s_rnn_conv, s_rnn_h, s_win_k, s_win_v, s_ffn_conv)
```

```python
import functools
import math

import numpy as np
import jax
import jax.numpy as jnp
from jax import lax
from jax.experimental import pallas as pl
from jax.experimental.pallas import tpu as pltpu

F32 = jnp.float32
BF16 = jnp.bfloat16

HEAD_DIM = 64
KV_HEADS = 4
N_HEADS = 16
GQA_GROUP = N_HEADS // KV_HEADS
RNN_BLOCKS = 16
RG_C = 8.0
WINDOW = 128
N_BUCKETS = 32
MAX_EXACT = N_BUCKETS // 2
REL_MAX_DIST = 128
EPS = 1e-6
NEG_INF = -1e30
ATTN_SCALE = HEAD_DIM ** -0.5

LANES = 128
SUBLANES = 8
VMEM_LIMIT_BYTES = 56 * 1024 * 1024

GATE_TILE = 256


def _t5_bucket_np(d):
    n = np.maximum(d, 0)
    nf = np.maximum(n, 1).astype(np.float32)
    large = MAX_EXACT + (np.log(nf / MAX_EXACT) / math.log(REL_MAX_DIST / MAX_EXACT)
                         * (N_BUCKETS - MAX_EXACT)).astype(np.int32)
    large = np.minimum(large, N_BUCKETS - 1)
    return np.where(n < MAX_EXACT, n, large).astype(np.int32)


def _rms(x, g):
    ms = jnp.mean(x * x, axis=-1, keepdims=True)
    return (x * lax.rsqrt(ms + EPS)) * g


def _softplus(x):
    return jnp.maximum(x, 0.0) + jnp.log1p(jnp.exp(-jnp.abs(x)))


def _const_spec(shape):
    nd = len(shape)
    return pl.BlockSpec(shape, lambda *_: (0,) * nd, pipeline_mode=pl.Buffered(1))


def _smem_spec():
    return pl.BlockSpec(memory_space=pltpu.SMEM)


def _inproj_body(x_ref, g_ref, w_ref, xr_ref, gr_ref, q_ref, k_ref, v_ref, *, d_rnn, d_attn, kv_dim):
    xn = _rms(x_ref[...], g_ref[...])
    p = jnp.dot(xn.astype(BF16), w_ref[...], preferred_element_type=F32)
    o1, o2, o3, o4 = d_rnn, 2 * d_rnn, 2 * d_rnn + d_attn, 2 * d_rnn + d_attn + kv_dim
    xr_ref[...] = p[:, :o1]
    gr_ref[...] = p[:, o1:o2]
    q_ref[...] = (p[:, o2:o3] * ATTN_SCALE).astype(BF16)
    k_ref[...] = p[:, o3:o4]
    v_ref[...] = p[:, o4:]


def _inproj(x2d, x_spec, n_steps, tm, g, w, d_rnn, d_attn, kv_dim):
    n = n_steps * tm
    d_model = g.shape[-1]
    row = lambda i: (i, 0)
    out_shape = (
        jax.ShapeDtypeStruct((n, d_rnn), F32),
        jax.ShapeDtypeStruct((n, d_rnn), F32),
        jax.ShapeDtypeStruct((n, d_attn), BF16),
        jax.ShapeDtypeStruct((n, kv_dim), F32),
        jax.ShapeDtypeStruct((n, kv_dim), F32),
    )
    out_specs = (
        pl.BlockSpec((tm, d_rnn), row),
        pl.BlockSpec((tm, d_rnn), row),
        pl.BlockSpec((tm, d_attn), row),
        pl.BlockSpec((tm, kv_dim), row),
        pl.BlockSpec((tm, kv_dim), row),
    )
    return pl.pallas_call(
        functools.partial(_inproj_body, d_rnn=d_rnn, d_attn=d_attn, kv_dim=kv_dim),
        grid=(n_steps,),
        in_specs=[x_spec, _const_spec((1, d_model)), _const_spec(w.shape)],
        out_specs=out_specs,
        out_shape=out_shape,
        compiler_params=pltpu.CompilerParams(
            dimension_semantics=("arbitrary",), vmem_limit_bytes=VMEM_LIMIT_BYTES),
        name="inproj",
    )(x2d, g, w)


def _rglru_gates(xc, wg_ref, ba, bx, lam):
    d_rnn = xc.shape[-1]
    pre_a, pre_x = [], []
    for c in range(d_rnn // GATE_TILE):
        xb = xc[:, c * GATE_TILE:(c + 1) * GATE_TILE].astype(BF16)
        pre = jnp.dot(xb, wg_ref[c], preferred_element_type=F32)
        pre_a.append(pre[:, :GATE_TILE])
        pre_x.append(pre[:, GATE_TILE:])
    r = jax.nn.sigmoid(jnp.concatenate(pre_a, axis=1) + ba)
    i = jax.nn.sigmoid(jnp.concatenate(pre_x, axis=1) + bx)
    log_a = (-RG_C * r) * _softplus(-lam)
    a = jnp.exp(log_a)
    one_minus_a2 = -jnp.tanh(log_a) * (a * a + 1.0)
    u = jnp.sqrt(one_minus_a2) * (i * xc)
    return a, u


def _rglru_prompt_body(xr_ref, gr_ref, cw_ref, cb_ref, wg_ref, ba_ref, bx_ref, lam_ref, gn_ref,
                       out_ref, hlast_ref, xbuf, abuf, ubuf, hbuf, hc_ref, *, tt, conv_w):
    ti = pl.program_id(1)
    d_rnn = xr_ref.shape[-1]

    @pl.when(ti == 0)
    def _():
        xbuf[0:SUBLANES, :] = jnp.zeros((SUBLANES, d_rnn), F32)
        hc_ref[...] = jnp.zeros_like(hc_ref)

    x = xr_ref[...]
    xbuf[SUBLANES:SUBLANES + tt, :] = x
    xc = cw_ref[conv_w - 1:conv_w, :] * x
    for k in range(conv_w - 1):
        off = SUBLANES - (conv_w - 1) + k
        xc = xc + cw_ref[k:k + 1, :] * xbuf[off:off + tt, :]
    xc = xc + cb_ref[...]
    xbuf[0:SUBLANES, :] = xbuf[tt:tt + SUBLANES, :]

    a, u = _rglru_gates(xc, wg_ref, ba_ref[...], bx_ref[...], lam_ref[...])
    abuf[...] = a
    ubuf[...] = u

    row = lax.broadcasted_iota(jnp.int32, (SUBLANES, d_rnn), 0)

    def group(gidx, hc):
        r0 = pl.multiple_of(gidx * SUBLANES, SUBLANES)
        ag = abuf[pl.ds(r0, SUBLANES), :]
        ug = ubuf[pl.ds(r0, SUBLANES), :]
        for k in (1, 2, 4):
            a_prev = jnp.where(row >= k, pltpu.roll(ag, k, 0), 1.0)
            u_prev = jnp.where(row >= k, pltpu.roll(ug, k, 0), 0.0)
            ug = ag * u_prev + ug
            ag = ag * a_prev
        h = ag * hc + ug
        hbuf[pl.ds(r0, SUBLANES), :] = h
        return jnp.broadcast_to(h[SUBLANES - 1:SUBLANES, :], (SUBLANES, d_rnn))

    hc = lax.fori_loop(0, tt // SUBLANES, group, hc_ref[...], unroll=2)
    hc_ref[...] = hc
    hlast_ref[0] = hc

    y = jax.nn.gelu(gr_ref[...]) * hbuf[...]
    out_ref[...] = _rms(y, gn_ref[...]).astype(BF16)


def _rglru_prompt(xr, gr, cw, cb, wg, ba, bx, lam, gn, batch, seq, tt):
    d_rnn = xr.shape[-1]
    nt = seq // tt
    conv_w = cw.shape[0]
    tile = lambda b, t: (b * nt + t, 0)
    out, hlast = pl.pallas_call(
        functools.partial(_rglru_prompt_body, tt=tt, conv_w=conv_w),
        grid=(batch, nt),
        in_specs=[
            pl.BlockSpec((tt, d_rnn), tile),
            pl.BlockSpec((tt, d_rnn), tile),
            _const_spec(cw.shape), _const_spec(cb.shape), _const_spec(wg.shape),
            _const_spec(ba.shape), _const_spec(bx.shape), _const_spec(lam.shape), _const_spec(gn.shape),
        ],
        out_specs=(
            pl.BlockSpec((tt, d_rnn), tile),
            pl.BlockSpec((1, SUBLANES, d_rnn), lambda b, t: (b, 0, 0)),
        ),
        out_shape=(
            jax.ShapeDtypeStruct((batch * seq, d_rnn), BF16),
            jax.ShapeDtypeStruct((batch, SUBLANES, d_rnn), F32),
        ),
        scratch_shapes=[
            pltpu.VMEM((tt + SUBLANES, d_rnn), F32),
            pltpu.VMEM((tt, d_rnn), F32),
            pltpu.VMEM((tt, d_rnn), F32),
            pltpu.VMEM((tt, d_rnn), F32),
            pltpu.VMEM((SUBLANES, d_rnn), F32),
        ],
        compiler_params=pltpu.CompilerParams(
            dimension_semantics=("arbitrary", "arbitrary"), vmem_limit_bytes=VMEM_LIMIT_BYTES),
        name="rglru_prompt",
    )(xr, gr, cw, cb, wg, ba, bx, lam, gn)
    return out, hlast[:, 0, :]


def _rglru_sample_body(xr_ref, gr_ref, c0_ref, c1_ref, c2_ref, h0_ref, cw_ref, cb_ref, wg_ref,
                       ba_ref, bx_ref, lam_ref, gn_ref, out_ref, hnew_ref, *, nb, steps):
    x = xr_ref[...]
    hist = [c0_ref[...], c1_ref[...], c2_ref[...]]
    xs = [x[t * nb:(t + 1) * nb, :] for t in range(steps)]
    xp = hist + xs
    conv_w = len(hist) + 1
    xc = cw_ref[conv_w - 1:conv_w, :] * x
    for k in range(conv_w - 1):
        shifted = jnp.concatenate(xp[k:k + steps], axis=0)
        xc = xc + cw_ref[k:k + 1, :] * shifted
    xc = xc + cb_ref[...]
    a, u = _rglru_gates(xc, wg_ref, ba_ref[...], bx_ref[...], lam_ref[...])
    h = h0_ref[...]
    hs = []
    for t in range(steps):
        h = a[t * nb:(t + 1) * nb, :] * h + u[t * nb:(t + 1) * nb, :]
        hs.append(h)
    hnew_ref[...] = h
    y = jax.nn.gelu(gr_ref[...]) * jnp.concatenate(hs, axis=0)
    out_ref[...] = _rms(y, gn_ref[...]).astype(BF16)


def _rglru_sample(xr, gr, conv_state2d, h0, cw, cb, wg, ba, bx, lam, gn, nb, steps):
    d_rnn = xr.shape[-1]
    n = nb * steps
    assert cw.shape[0] == 4
    full = lambda shape: pl.BlockSpec(shape, lambda i: (0,) * len(shape))
    return pl.pallas_call(
        functools.partial(_rglru_sample_body, nb=nb, steps=steps),
        grid=(1,),
        in_specs=[
            full((n, d_rnn)), full((n, d_rnn)),
            pl.BlockSpec((nb, d_rnn), lambda i: (0, 0)),
            pl.BlockSpec((nb, d_rnn), lambda i: (0, 1)),
            pl.BlockSpec((nb, d_rnn), lambda i: (0, 2)),
            full((nb, d_rnn)),
            full(cw.shape), full(cb.shape), full(wg.shape), full(ba.shape), full(bx.shape),
            full(lam.shape), full(gn.shape),
        ],
        out_specs=(full((n, d_rnn)), full((nb, d_rnn))),
        out_shape=(jax.ShapeDtypeStruct((n, d_rnn), BF16), jax.ShapeDtypeStruct((nb, d_rnn), F32)),
        compiler_params=pltpu.CompilerParams(
            dimension_semantics=("arbitrary",), vmem_limit_bytes=VMEM_LIMIT_BYTES),
        name="rglru_sample",
    )(xr, gr, conv_state2d, conv_state2d, conv_state2d, h0, cw, cb, wg, ba, bx, lam, gn)


def _swa_prompt_body(code_ref, tbl_ref, sink_ref, q_ref, kp_ref, kc_ref, vp_ref, vc_ref, gn_ref,
                     out_ref, bias_ref):
    j = pl.program_id(1)
    blk = q_ref.shape[0]
    rows = GQA_GROUP * blk

    @pl.when((pl.program_id(0) == 0) & (j == 0))
    def _():
        code = code_ref[...]
        qi = lax.broadcasted_iota(jnp.int32, code.shape, 0)
        kj = lax.broadcasted_iota(jnp.int32, code.shape, 1)
        dist = blk + qi - kj
        in_window = (dist & -WINDOW) == 0
        for h in range(N_HEADS):
            def pick(b, acc, h=h):
                return jnp.where(code == b, tbl_ref[b * N_HEADS + h], acc)
            bias_h = lax.fori_loop(0, N_BUCKETS, pick, jnp.zeros(code.shape, F32))
            bias_h = jnp.where(in_window, bias_h, NEG_INF)
            kv, g = divmod(h, GQA_GROUP)
            bias_ref[1, kv, g * blk:(g + 1) * blk, :] = bias_h
            bias_ref[0, kv, g * blk:(g + 1) * blk, :] = jnp.where(kj >= blk, bias_h, NEG_INF)

    has_prev = jnp.where(j > 0, 1, 0)

    kband = jnp.concatenate([kp_ref[...], kc_ref[...]], axis=0)
    vband = jnp.concatenate([vp_ref[...], vc_ref[...]], axis=0)
    lane = lax.broadcasted_iota(jnp.int32, (2 * blk, LANES), 1)
    row_g = lax.broadcasted_iota(jnp.int32, (rows, 1), 0) // blk
    q = q_ref[...]

    outs = []
    for cg in range(KV_HEADS // 2):
        qs = jnp.concatenate(
            [q[:, (cg * GQA_GROUP + g) * LANES:(cg * GQA_GROUP + g + 1) * LANES] for g in range(GQA_GROUP)],
            axis=0)
        ka = kband[:, cg * LANES:(cg + 1) * LANES]
        va = vband[:, cg * LANES:(cg + 1) * LANES]
        o = None
        for par in range(2):
            kv = 2 * cg + par
            half = (lane < HEAD_DIM) if par == 0 else (lane >= HEAD_DIM)
            km = jnp.where(half, ka, 0.0).astype(BF16)
            vm = jnp.where(half, va, 0.0).astype(BF16)
            s = lax.dot_general(qs, km, (((1,), (1,)), ((), ())), preferred_element_type=F32)
            s = s + bias_ref[has_prev, kv]
            sink = jnp.zeros((rows, 1), F32)
            for g in range(GQA_GROUP):
                sink = jnp.where(row_g == g, sink_ref[kv * GQA_GROUP + g], sink)
            m = jnp.maximum(jnp.max(s, axis=-1, keepdims=True), sink)
            p = jnp.exp(s - m)
            denom = jnp.sum(p, axis=-1, keepdims=True) + jnp.exp(sink - m)
            w = p * (1.0 / denom)
            part = jnp.dot(w.astype(BF16), vm, preferred_element_type=F32)
            o = part if o is None else o + part
        outs.extend(o[g * blk:(g + 1) * blk, :] for g in range(GQA_GROUP))
    y = jnp.concatenate(outs, axis=1)
    out_ref[...] = _rms(y, gn_ref[...]).astype(BF16)


def _swa_prompt(q, k, v, code, tbl, sinks, gn, batch, seq):
    blk = WINDOW
    nb = seq // blk
    d_attn = q.shape[-1]
    kv_dim = k.shape[-1]
    cur = lambda b, j: (b * nb + j, 0)
    prev = lambda b, j: (b * nb + jnp.maximum(j - 1, 0), 0)
    return pl.pallas_call(
        _swa_prompt_body,
        grid=(batch, nb),
        in_specs=[
            _const_spec(code.shape), _smem_spec(), _smem_spec(),
            pl.BlockSpec((blk, d_attn), cur),
            pl.BlockSpec((blk, kv_dim), prev), pl.BlockSpec((blk, kv_dim), cur),
            pl.BlockSpec((blk, kv_dim), prev), pl.BlockSpec((blk, kv_dim), cur),
            _const_spec(gn.shape),
        ],
        out_specs=pl.BlockSpec((blk, d_attn), cur),
        out_shape=jax.ShapeDtypeStruct((batch * seq, d_attn), BF16),
        scratch_shapes=[pltpu.VMEM((2, KV_HEADS, GQA_GROUP * blk, 2 * blk), F32)],
        compiler_params=pltpu.CompilerParams(
            dimension_semantics=("arbitrary", "arbitrary"), vmem_limit_bytes=VMEM_LIMIT_BYTES),
        name="swa_prompt",
    )(code, tbl, sinks, q, k, k, v, v, gn)


def _swa_sample_body(codec_ref, coden_ref, tbl_ref, sink_ref, q_ref, ck_ref, cv_ref, kn_ref, vn_ref,
                     out_ref, biasc_ref, biasn_ref, sinkc_ref, *, steps):
    bb = q_ref.shape[0]
    nrow = N_HEADS * steps
    win = ck_ref.shape[1]
    npad = kn_ref.shape[1]

    @pl.when(pl.program_id(0) == 0)
    def _():
        codec = codec_ref[...]
        coden = coden_ref[...]
        hrow = lax.broadcasted_iota(jnp.int32, (nrow, 1), 0) // steps

        def pick(idx, accs):
            ac, an = accs
            val = tbl_ref[idx]
            return jnp.where(codec == idx, val, ac), jnp.where(coden == idx, val, an)
        bc, bn = lax.fori_loop(0, N_BUCKETS * N_HEADS, pick,
                               (jnp.zeros(codec.shape, F32), jnp.zeros(coden.shape, F32)))
        biasc_ref[...] = bc
        biasn_ref[...] = bn

        def pick_sink(h, acc):
            return jnp.where(hrow == h, sink_ref[h], acc)
        sinkc_ref[...] = lax.fori_loop(0, N_HEADS, pick_sink, jnp.zeros((nrow, 1), F32))

    q = q_ref[...].astype(F32)
    lane = lax.broadcasted_iota(jnp.int32, (bb, GQA_GROUP * steps, LANES), 2)
    zeros = jnp.zeros((bb, GQA_GROUP * steps, LANES), F32)
    pieces = []
    for cg in range(KV_HEADS // 2):
        for par in range(2):
            half = (lane < HEAD_DIM) if par == 0 else (lane >= HEAD_DIM)
            qm = jnp.where(half, q[:, cg], zeros)
            pieces.append(jnp.concatenate([qm, zeros] if cg == 0 else [zeros, qm], axis=2))
    qm = jnp.concatenate(pieces, axis=1).astype(BF16)

    ck = ck_ref[...].astype(BF16)
    kn = kn_ref[...].astype(BF16)
    s_c = jnp.einsum("bqd,bkd->bqk", qm, ck, preferred_element_type=F32)
    s_n = jnp.einsum("bqd,bkd->bqk", qm, kn, preferred_element_type=F32)

    t_c = lax.broadcasted_iota(jnp.int32, (nrow, win), 0) % steps
    k_c = lax.broadcasted_iota(jnp.int32, (nrow, win), 1)
    valid_c = k_c > t_c
    t_n = lax.broadcasted_iota(jnp.int32, (nrow, npad), 0) % steps
    k_n = lax.broadcasted_iota(jnp.int32, (nrow, npad), 1)
    valid_n = k_n <= t_n

    s_c = jnp.where(valid_c[None], s_c + biasc_ref[...][None], NEG_INF)
    s_n = jnp.where(valid_n[None], s_n + biasn_ref[...][None], NEG_INF)
    sink = sinkc_ref[...][None]
    m = jnp.maximum(jnp.maximum(jnp.max(s_c, axis=-1, keepdims=True),
                                jnp.max(s_n, axis=-1, keepdims=True)), sink)
    p_c = jnp.exp(s_c - m)
    p_n = jnp.exp(s_n - m)
    denom = (jnp.sum(p_c, axis=-1, keepdims=True) + jnp.sum(p_n, axis=-1, keepdims=True)
             + jnp.exp(sink - m))
    r = 1.0 / denom
    w_c = (p_c * r).astype(BF16)
    w_n = (p_n * r).astype(BF16)
    o = (jnp.einsum("bqk,bkd->bqd", w_c, cv_ref[...].astype(BF16), preferred_element_type=F32)
         + jnp.einsum("bqk,bkd->bqd", w_n, vn_ref[...].astype(BF16), preferred_element_type=F32))
    gt = GQA_GROUP * steps
    lane_o = lax.broadcasted_iota(jnp.int32, (bb, gt, LANES), 2)
    for cg in range(KV_HEADS // 2):
        lo = o[:, cg * 2 * gt:cg * 2 * gt + gt, cg * LANES:(cg + 1) * LANES]
        hi = o[:, cg * 2 * gt + gt:(cg + 1) * 2 * gt, cg * LANES:(cg + 1) * LANES]
        out_ref[:, cg] = jnp.where(lane_o < HEAD_DIM, lo, hi)


def _swa_sample(qs, ck, cv, kn, vn, codec, coden, tbl, sinks, steps, bb):
    nbatch, ncg, gt, _ = qs.shape
    win, kv_dim = ck.shape[1], ck.shape[2]
    npad = kn.shape[1]
    nrow = N_HEADS * steps
    blk4 = lambda i: (i, 0, 0, 0)
    blk3 = lambda i: (i, 0, 0)
    return pl.pallas_call(
        functools.partial(_swa_sample_body, steps=steps),
        grid=(nbatch // bb,),
        in_specs=[
            _const_spec(codec.shape), _const_spec(coden.shape), _smem_spec(), _smem_spec(),
            pl.BlockSpec((bb, ncg, gt, LANES), blk4),
            pl.BlockSpec((bb, win, kv_dim), blk3), pl.BlockSpec((bb, win, kv_dim), blk3),
            pl.BlockSpec((bb, npad, kv_dim), blk3), pl.BlockSpec((bb, npad, kv_dim), blk3),
        ],
        out_specs=pl.BlockSpec((bb, ncg, gt, LANES), blk4),
        out_shape=jax.ShapeDtypeStruct((nbatch, ncg, gt, LANES), F32),
        scratch_shapes=[pltpu.VMEM((nrow, win), F32), pltpu.VMEM((nrow, npad), F32),
                        pltpu.VMEM((nrow, 1), F32)],
        compiler_params=pltpu.CompilerParams(
            dimension_semantics=("arbitrary",), vmem_limit_bytes=VMEM_LIMIT_BYTES),
        name="swa_sample",
    )(codec, coden, tbl, sinks, qs, ck, cv, kn, vn)


def _shift_rows(up, hist, k):
    rolled = pltpu.roll(up, k, 0)
    row = lax.broadcasted_iota(jnp.int32, hist.shape, 0)
    head = jnp.where(row < k, pltpu.roll(hist, k, 0), rolled[0:SUBLANES, :])
    return jnp.concatenate([head, rolled[SUBLANES:, :]], axis=0)


def _ffn_conv(up, hist, cw, cb, *, tm, sample_nb):
    if sample_nb is None:
        prev2 = _shift_rows(up, hist, 2)
        prev1 = _shift_rows(up, hist, 1)
    else:
        s0, s1 = hist
        prev2 = jnp.concatenate([s0, s1, up[:tm - 2 * sample_nb, :]], axis=0)
        prev1 = jnp.concatenate([s1, up[:tm - sample_nb, :]], axis=0)
    return cw[0:1, :] * prev2 + cw[1:2, :] * prev1 + cw[2:3, :] * up + cb


FFN_CHUNK = 256
FFN_SLAB = 256


def _ffn_prompt_body(x_hbm, mr_ref, ma_ref, wo_ref, gf_ref, wg_ref, wv_ref, cwg_ref, cwv_ref, cbg_ref,
                     cbv_ref, wd_ref, gfin_ref, y_ref, tg_ref, tv_ref, hn_ref, car_g, car_v, x_sem,
                     *, tm, tiles_per_seq):
    i = pl.program_id(0)
    j = pl.program_id(1)
    nj = pl.num_programs(1)
    n_slab = tm // FFN_SLAB
    slab = lambda r: slice(r * FFN_SLAB, (r + 1) * FFN_SLAB)

    @pl.when(j == 0)
    def _():
        x_copy = pltpu.make_async_copy(x_hbm.at[pl.ds(pl.multiple_of(i * tm, tm), tm), :], y_ref, x_sem)
        x_copy.start()

        def out_proj(r):
            merged = jnp.concatenate([mr_ref[slab(r), :], ma_ref[slab(r), :]], axis=1)
            return jnp.dot(merged, wo_ref[...], preferred_element_type=F32)

        pending = out_proj(0)
        x_copy.wait()
        for r in range(n_slab):
            d = pending
            if r + 1 < n_slab:
                pending = out_proj(r + 1)
            h = y_ref[slab(r), :] + d
            y_ref[slab(r), :] = h
            hn_ref[slab(r), :] = _rms(h, gf_ref[...]).astype(BF16)

    @pl.when((i == 0) & (j == 0))
    def _():
        car_g[...] = jnp.zeros_like(car_g)
        car_v[...] = jnp.zeros_like(car_v)

    cwg, cwv, cbg, cbv = cwg_ref[...], cwv_ref[...], cbg_ref[...], cbv_ref[...]

    def up_proj(r):
        hn = hn_ref[slab(r), :]
        return (jnp.dot(hn, wg_ref[...], preferred_element_type=F32),
                jnp.dot(hn, wv_ref[...], preferred_element_type=F32))

    seq_start = i % tiles_per_seq == 0
    hist_g = jnp.where(seq_start, 0.0, car_g[j])
    hist_v = jnp.where(seq_start, 0.0, car_v[j])
    pending = up_proj(0)
    act = None
    for r in range(n_slab):
        up_g, up_v = pending
        if r + 1 < n_slab:
            pending = up_proj(r + 1)
        if act is not None:
            y_ref[slab(r - 1), :] += jnp.dot(act, wd_ref[...], preferred_element_type=F32)
        c_g = _ffn_conv(up_g, hist_g, cwg, cbg, tm=FFN_SLAB, sample_nb=None)
        c_v = _ffn_conv(up_v, hist_v, cwv, cbv, tm=FFN_SLAB, sample_nb=None)
        hist_g = up_g[FFN_SLAB - SUBLANES:, :]
        hist_v = up_v[FFN_SLAB - SUBLANES:, :]
        act = (jax.nn.gelu(c_g) * c_v).astype(BF16)
    y_ref[slab(n_slab - 1), :] += jnp.dot(act, wd_ref[...], preferred_element_type=F32)
    car_g[j] = hist_g
    car_v[j] = hist_v
    tg_ref[0] = hist_g
    tv_ref[0] = hist_v

    @pl.when(j == nj - 1)
    def _():
        for r in range(n_slab):
            y_ref[slab(r), :] = _rms(y_ref[slab(r), :], gfin_ref[...])


def _ffn_sample_body(x_ref, mr_ref, ma_ref, gna_ref, wo_ref, gf_ref, wg_ref, wv_ref, cwg_ref, cwv_ref,
                     cbg_ref, cbv_ref, wd_ref, gfin_ref, s0g_ref, s1g_ref, s0v_ref, s1v_ref,
                     y_ref, tg_ref, tv_ref, hn_ref, *, nb):
    j = pl.program_id(1)
    nj = pl.num_programs(1)
    tm = x_ref.shape[0]

    @pl.when(j == 0)
    def _():
        ma = _rms(ma_ref[...], gna_ref[...]).astype(BF16)
        merged = jnp.concatenate([mr_ref[...], ma], axis=1)
        h = x_ref[...] + jnp.dot(merged, wo_ref[...], preferred_element_type=F32)
        y_ref[...] = h
        hn_ref[...] = _rms(h, gf_ref[...]).astype(BF16)

    hn = hn_ref[...]
    tf = wg_ref.shape[1]
    acts = []
    for c in range(tf // FFN_CHUNK):
        cs = slice(c * FFN_CHUNK, (c + 1) * FFN_CHUNK)
        up_g = jnp.dot(hn, wg_ref[:, cs], preferred_element_type=F32)
        up_v = jnp.dot(hn, wv_ref[:, cs], preferred_element_type=F32)
        tg_ref[0, :, cs] = up_g[tm - 2 * nb:tm - nb, :]
        tg_ref[1, :, cs] = up_g[tm - nb:, :]
        tv_ref[0, :, cs] = up_v[tm - 2 * nb:tm - nb, :]
        tv_ref[1, :, cs] = up_v[tm - nb:, :]
        c_g = _ffn_conv(up_g, (s0g_ref[:, cs], s1g_ref[:, cs]), cwg_ref[:, cs], cbg_ref[:, cs],
                        tm=tm, sample_nb=nb)
        c_v = _ffn_conv(up_v, (s0v_ref[:, cs], s1v_ref[:, cs]), cwv_ref[:, cs], cbv_ref[:, cs],
                        tm=tm, sample_nb=nb)
        acts.append((jax.nn.gelu(c_g) * c_v).astype(BF16))
    act = jnp.concatenate(acts, axis=1)
    y_ref[...] += jnp.dot(act, wd_ref[...], preferred_element_type=F32)

    @pl.when(j == nj - 1)
    def _():
        y_ref[...] = _rms(y_ref[...], gfin_ref[...])


def _ffn_prompt(x, mr, ma, wo, gf, wup, cw, cb, wd, gfin, batch, seq, tm, tf):
    n, d_model = x.shape
    d_half = mr.shape[-1]
    d_ff = wd.shape[0]
    ni, nj = n // tm, d_ff // tf
    tiles_per_seq = seq // tm
    row = lambda i, j: (i, 0)
    gate_col = lambda i, j: (0, j)
    val_col = lambda i, j: (0, nj + j)
    tail = lambda i, j: (i, 0, j)
    y, tg, tv = pl.pallas_call(
        functools.partial(_ffn_prompt_body, tm=tm, tiles_per_seq=tiles_per_seq),
        grid=(ni, nj),
        in_specs=[
            pl.BlockSpec(memory_space=pl.ANY),
            pl.BlockSpec((tm, d_half), row), pl.BlockSpec((tm, d_half), row),
            _const_spec(wo.shape), _const_spec(gf.shape),
            pl.BlockSpec((d_model, tf), gate_col), pl.BlockSpec((d_model, tf), val_col),
            pl.BlockSpec((cw.shape[0], tf), gate_col), pl.BlockSpec((cw.shape[0], tf), val_col),
            pl.BlockSpec((1, tf), gate_col), pl.BlockSpec((1, tf), val_col),
            pl.BlockSpec((tf, d_model), lambda i, j: (j, 0)),
            _const_spec(gfin.shape),
        ],
        out_specs=(
            pl.BlockSpec((tm, d_model), row),
            pl.BlockSpec((1, SUBLANES, tf), tail),
            pl.BlockSpec((1, SUBLANES, tf), tail),
        ),
        out_shape=(
            jax.ShapeDtypeStruct((n, d_model), F32),
            jax.ShapeDtypeStruct((ni, SUBLANES, d_ff), F32),
            jax.ShapeDtypeStruct((ni, SUBLANES, d_ff), F32),
        ),
        scratch_shapes=[
            pltpu.VMEM((tm, d_model), BF16),
            pltpu.VMEM((nj, SUBLANES, tf), F32), pltpu.VMEM((nj, SUBLANES, tf), F32),
            pltpu.SemaphoreType.DMA(()),
        ],
        compiler_params=pltpu.CompilerParams(
            dimension_semantics=("arbitrary", "arbitrary"), vmem_limit_bytes=VMEM_LIMIT_BYTES),
        name="ffn_prompt",
    )(x, mr, ma, wo, gf, wup, wup, cw, cw, cb, cb, wd, gfin)
    return y, tg[tiles_per_seq - 1::tiles_per_seq], tv[tiles_per_seq - 1::tiles_per_seq]


def _ffn_sample(x, mr, ma, gna, wo, gf, wup, cw, cb, wd, gfin, state2d, nb, steps, tf):
    n, d_model = x.shape
    d_half = mr.shape[-1]
    d_ff = wd.shape[0]
    nj = d_ff // tf
    full2 = lambda shape: pl.BlockSpec(shape, lambda i, j: (0, 0))
    gate_col = lambda i, j: (0, j)
    val_col = lambda i, j: (0, nj + j)
    y, tg, tv = pl.pallas_call(
        functools.partial(_ffn_sample_body, nb=nb),
        grid=(1, nj),
        in_specs=[
            full2((n, d_model)), full2((n, d_half)), full2((n, d_half)), full2(gna.shape),
            _const_spec(wo.shape), _const_spec(gf.shape),
            pl.BlockSpec((d_model, tf), gate_col), pl.BlockSpec((d_model, tf), val_col),
            pl.BlockSpec((cw.shape[0], tf), gate_col), pl.BlockSpec((cw.shape[0], tf), val_col),
            pl.BlockSpec((1, tf), gate_col), pl.BlockSpec((1, tf), val_col),
            pl.BlockSpec((tf, d_model), lambda i, j: (j, 0)),
            _const_spec(gfin.shape),
            pl.BlockSpec((nb, tf), lambda i, j: (0, j)),
            pl.BlockSpec((nb, tf), lambda i, j: (0, 2 * nj + j)),
            pl.BlockSpec((nb, tf), lambda i, j: (0, nj + j)),
            pl.BlockSpec((nb, tf), lambda i, j: (0, 3 * nj + j)),
        ],
        out_specs=(
            full2((n, d_model)),
            pl.BlockSpec((2, nb, tf), lambda i, j: (0, 0, j)),
            pl.BlockSpec((2, nb, tf), lambda i, j: (0, 0, j)),
        ),
        out_shape=(
            jax.ShapeDtypeStruct((n, d_model), F32),
            jax.ShapeDtypeStruct((2, nb, d_ff), F32),
            jax.ShapeDtypeStruct((2, nb, d_ff), F32),
        ),
        scratch_shapes=[pltpu.VMEM((n, d_model), BF16)],
        compiler_params=pltpu.CompilerParams(
            dimension_semantics=("arbitrary", "arbitrary"), vmem_limit_bytes=VMEM_LIMIT_BYTES),
        name="ffn_sample",
    )(x, mr, ma, gna, wo, gf, wup, wup, cw, cw, cb, cb, wd, gfin, state2d, state2d, state2d, state2d)
    return y, tg, tv


def _q_perm():
    idx = np.empty((KV_HEADS // 2, GQA_GROUP, 2, HEAD_DIM), np.int32)
    for cg in range(KV_HEADS // 2):
        for g in range(GQA_GROUP):
            for par in range(2):
                h = GQA_GROUP * (2 * cg + par) + g
                idx[cg, g, par] = h * HEAD_DIM + np.arange(HEAD_DIM)
    return idx.reshape(-1)


def _gate_weights(w_a, w_x):
    nblk, bs, _ = w_a.shape
    per = GATE_TILE // bs
    eye = jnp.eye(per, dtype=w_a.dtype)

    def pack(w):
        w4 = w.reshape(nblk // per, per, bs, bs)
        return jnp.einsum("cgij,gh->cgihj", w4, eye).reshape(nblk // per, GATE_TILE, GATE_TILE)
    return jnp.concatenate([pack(w_a), pack(w_x)], axis=-1).astype(BF16)


def kernel(x_prompt, x_sample, state_rnn_conv, state_rnn_h, cache_win_k, cache_win_v, state_ffn_conv,
           norm_mix_g, w_in, rnn_conv_w, rnn_conv_b, w_gate_a, b_gate_a, w_gate_x, b_gate_x, rnn_lambda,
           attn_sinks, rel_bias_table, gn_rnn_g, gn_attn_g, w_out, norm_ffn_g, w_up, ffn_conv_w,
           ffn_conv_b, w_down, norm_final_g):
    batch, seq, d_model = x_prompt.shape
    nb, steps, _ = x_sample.shape
    depth = w_in.shape[0]
    d_rnn = rnn_conv_w.shape[-1]
    d_attn = N_HEADS * HEAD_DIM
    kv_dim = KV_HEADS * HEAD_DIM
    d_ff = w_down.shape[1]
    win = cache_win_k.shape[2]
    assert depth == 1 and d_rnn + d_attn == d_model and w_in.shape[-1] == 2 * d_rnn + d_attn + 2 * kv_dim
    assert win == WINDOW and seq % WINDOW == 0 and w_gate_a.shape[1] == RNN_BLOCKS
    assert rnn_conv_w.shape[1] == 4 and ffn_conv_w.shape[1] == 3 and steps >= 3 and nb % SUBLANES == 0

    qperm = _q_perm()
    w_in0 = w_in[0]
    o_q = 2 * d_rnn
    w_in_p = jnp.concatenate(
        [w_in0[:, :o_q], w_in0[:, o_q:o_q + d_attn][:, qperm], w_in0[:, o_q + d_attn:]], axis=1).astype(BF16)
    w_out0 = w_out[0]
    w_out_p = jnp.concatenate([w_out0[:d_rnn], w_out0[d_rnn:][qperm]], axis=0).astype(BF16)
    gn_attn_p = gn_attn_g[0][qperm].reshape(1, d_attn)
    tf = 512
    w_up_b = w_up[0].astype(BF16)
    w_down_b = w_down[0].astype(BF16)
    wg = _gate_weights(w_gate_a[0], w_gate_x[0])
    row2 = lambda a: a.reshape(1, -1)
    g_mix, g_ffn, g_fin, g_rnn = row2(norm_mix_g[0]), row2(norm_ffn_g[0]), row2(norm_final_g), row2(gn_rnn_g[0])
    cw_r, cb_r = rnn_conv_w[0], row2(rnn_conv_b[0])
    ba, bx, lam = row2(b_gate_a[0]), row2(b_gate_x[0]), row2(rnn_lambda[0])
    cw_f, cb_f = ffn_conv_w[0], row2(ffn_conv_b[0])
    tbl = rel_bias_table.reshape(-1)
    sinks = attn_sinks[0]

    blk = WINDOW
    qi = np.arange(blk)[:, None]
    kj = np.arange(2 * blk)[None, :]
    code_p = jnp.asarray(_t5_bucket_np(blk + qi - kj))
    r = np.arange(N_HEADS * steps)[:, None]
    t_r, h_r = r % steps, r // steps
    npad = 2 * SUBLANES
    code_c = jnp.asarray(_t5_bucket_np(win + t_r - np.arange(win)[None, :]) * N_HEADS + h_r)
    code_n = jnp.asarray(_t5_bucket_np(t_r - np.arange(npad)[None, :]) * N_HEADS + h_r)

    n_p = batch * seq
    tm_a = 512 if n_p % 512 == 0 else WINDOW
    xp2 = x_prompt.reshape(n_p, d_model)
    xr, gr, q, k, v = _inproj(xp2, pl.BlockSpec((tm_a, d_model), lambda i: (i, 0)), n_p // tm_a, tm_a,
                              g_mix, w_in_p, d_rnn, d_attn, kv_dim)
    tt = 512 if seq % 512 == 0 else WINDOW
    m_rnn, h_last = _rglru_prompt(xr, gr, cw_r, cb_r, wg, ba, bx, lam, g_rnn, batch, seq, tt)
    m_attn = _swa_prompt(q, k, v, code_p, tbl, sinks, gn_attn_p, batch, seq)
    tm_f = next(t for t in (1024, 512, 256) if seq % t == 0)
    y_p, tail_g, tail_v = _ffn_prompt(xp2, m_rnn, m_attn, w_out_p, g_ffn, w_up_b, cw_f, cb_f, w_down_b,
                                      g_fin, batch, seq, tm_f, tf)
    y_prompt = y_p.reshape(batch, seq, d_model)
    p_rnn_conv = xr.reshape(batch, seq, d_rnn)[:, seq - 3:, :][None]
    p_rnn_h = h_last[None]
    p_win_k = k.reshape(batch, seq, KV_HEADS, HEAD_DIM)[:, seq - win:][None]
    p_win_v = v.reshape(batch, seq, KV_HEADS, HEAD_DIM)[:, seq - win:][None]
    p_ffn_conv = jnp.concatenate([tail_g[:, SUBLANES - 2:, :], tail_v[:, SUBLANES - 2:, :]], axis=-1)[None]

    n_s = nb * steps
    xs2 = x_sample.reshape(nb, steps * d_model)
    xr_s, gr_s, q_s, k_s, v_s = _inproj(xs2, pl.BlockSpec((nb, d_model), lambda t: (0, t)), steps, nb,
                                        g_mix, w_in_p, d_rnn, d_attn, kv_dim)
    conv2d = state_rnn_conv[0].reshape(nb, 3 * d_rnn)
    m_rnn_s, h_new = _rglru_sample(xr_s, gr_s, conv2d, state_rnn_h[0], cw_r, cb_r, wg, ba, bx, lam, g_rnn,
                                   nb, steps)
    ncg = KV_HEADS // 2
    qs = q_s.reshape(steps, nb, ncg, GQA_GROUP, LANES).transpose(1, 2, 3, 0, 4).reshape(
        nb, ncg, GQA_GROUP * steps, LANES)
    k_new = k_s.reshape(steps, nb, kv_dim).transpose(1, 0, 2)
    v_new = v_s.reshape(steps, nb, kv_dim).transpose(1, 0, 2)
    pad = ((0, 0), (0, npad - steps), (0, 0))
    ck = cache_win_k[0].reshape(nb, win, kv_dim)
    cv = cache_win_v[0].reshape(nb, win, kv_dim)
    bb = SUBLANES
    o_s = _swa_sample(qs, ck, cv, jnp.pad(k_new, pad), jnp.pad(v_new, pad), code_c, code_n, tbl, sinks,
                      steps, bb)
    ya_s = o_s.reshape(nb, ncg, GQA_GROUP, steps, LANES).transpose(3, 0, 1, 2, 4).reshape(n_s, d_attn)
    xs_tm = x_sample.transpose(1, 0, 2).reshape(n_s, d_model)
    ffn2d = state_ffn_conv[0].reshape(nb, 2 * 2 * d_ff)
    y_s, ns_g, ns_v = _ffn_sample(xs_tm, m_rnn_s, ya_s, gn_attn_p, w_out_p, g_ffn, w_up_b, cw_f, cb_f,
                                  w_down_b, g_fin, ffn2d, nb, steps, tf)
    y_sample = y_s.reshape(steps, nb, d_model).transpose(1, 0, 2)
    s_rnn_conv = xr_s.reshape(steps, nb, d_rnn)[steps - 3:].transpose(1, 0, 2)[None]
    s_rnn_h = h_new[None]
    s_win_k = jnp.concatenate([ck[:, steps:], k_new], axis=1).reshape(1, nb, win, KV_HEADS, HEAD_DIM)
    s_win_v = jnp.concatenate([cv[:, steps:], v_new], axis=1).reshape(1, nb, win, KV_HEADS, HEAD_DIM)
    s_ffn_conv = jnp.concatenate([ns_g, ns_v], axis=-1).transpose(1, 0, 2)[None]

    return (y_prompt, y_sample, p_rnn_conv, p_rnn_h, p_win_k, p_win_v, p_ffn_conv,
            s_rnn_conv, s_rnn_h, s_win_k, s_win_v, s_ffn_conv)
```

```python
import functools
import math

import numpy as np
import jax
import jax.numpy as jnp
from jax import lax
from jax.experimental import pallas as pl
from jax.experimental.pallas import tpu as pltpu

F32 = jnp.float32
BF16 = jnp.bfloat16

HEAD_DIM = 64
KV_HEADS = 4
N_HEADS = 16
GQA_GROUP = N_HEADS // KV_HEADS
RNN_BLOCKS = 16
RG_C = 8.0
WINDOW = 128
N_BUCKETS = 32
MAX_EXACT = N_BUCKETS // 2
REL_MAX_DIST = 128
EPS = 1e-6
NEG_INF = -1e30
ATTN_SCALE = HEAD_DIM ** -0.5

LANES = 128
SUBLANES = 8
VMEM_LIMIT_BYTES = 56 * 1024 * 1024

GATE_TILE = 256


def _t5_bucket_np(d):
    n = np.maximum(d, 0)
    nf = np.maximum(n, 1).astype(np.float32)
    large = MAX_EXACT + (np.log(nf / MAX_EXACT) / math.log(REL_MAX_DIST / MAX_EXACT)
                         * (N_BUCKETS - MAX_EXACT)).astype(np.int32)
    large = np.minimum(large, N_BUCKETS - 1)
    return np.where(n < MAX_EXACT, n, large).astype(np.int32)


def _rms(x, g):
    ms = jnp.mean(x * x, axis=-1, keepdims=True)
    return (x * lax.rsqrt(ms + EPS)) * g


def _softplus(x):
    return jnp.maximum(x, 0.0) + jnp.log1p(jnp.exp(-jnp.abs(x)))


def _const_spec(shape):
    nd = len(shape)
    return pl.BlockSpec(shape, lambda *_: (0,) * nd, pipeline_mode=pl.Buffered(1))


def _smem_spec():
    return pl.BlockSpec(memory_space=pltpu.SMEM)


def _inproj_body(x_ref, g_ref, wrg_ref, wq_ref, wkv_ref, xr_ref, gr_ref, q_ref, k_ref, v_ref):
    xn = _rms(x_ref[...], g_ref[...]).astype(BF16)
    d_rnn = xr_ref.shape[-1]
    kv_dim = k_ref.shape[-1]
    rg = jnp.dot(xn, wrg_ref[...], preferred_element_type=F32)
    xr_ref[...] = rg[:, :d_rnn]
    gr_ref[...] = rg[:, d_rnn:]
    q_ref[...] = (jnp.dot(xn, wq_ref[...], preferred_element_type=F32) * ATTN_SCALE).astype(BF16)
    kv = jnp.dot(xn, wkv_ref[...], preferred_element_type=F32)
    k_ref[...] = kv[:, :kv_dim]
    v_ref[...] = kv[:, kv_dim:]


def _inproj(x2d, x_spec, n_steps, tm, g, w_rg, w_q, w_kv):
    n = n_steps * tm
    d_model = g.shape[-1]
    d_rnn, d_attn, kv_dim = w_rg.shape[1] // 2, w_q.shape[1], w_kv.shape[1] // 2
    row = lambda i: (i, 0)
    out_shape = (
        jax.ShapeDtypeStruct((n, d_rnn), F32),
        jax.ShapeDtypeStruct((n, d_rnn), F32),
        jax.ShapeDtypeStruct((n, d_attn), BF16),
        jax.ShapeDtypeStruct((n, kv_dim), F32),
        jax.ShapeDtypeStruct((n, kv_dim), F32),
    )
    out_specs = (
        pl.BlockSpec((tm, d_rnn), row),
        pl.BlockSpec((tm, d_rnn), row),
        pl.BlockSpec((tm, d_attn), row),
        pl.BlockSpec((tm, kv_dim), row),
        pl.BlockSpec((tm, kv_dim), row),
    )
    return pl.pallas_call(
        _inproj_body,
        grid=(n_steps,),
        in_specs=[x_spec, _const_spec((1, d_model)), _const_spec(w_rg.shape), _const_spec(w_q.shape),
                  _const_spec(w_kv.shape)],
        out_specs=out_specs,
        out_shape=out_shape,
        compiler_params=pltpu.CompilerParams(
            dimension_semantics=("arbitrary",), vmem_limit_bytes=VMEM_LIMIT_BYTES),
        name="inproj",
    )(x2d, g, w_rg, w_q, w_kv)


def _rglru_gates(xc, wg_ref, ba, bx, lam):
    d_rnn = xc.shape[-1]
    pre_a, pre_x = [], []
    for c in range(d_rnn // GATE_TILE):
        xb = xc[:, c * GATE_TILE:(c + 1) * GATE_TILE].astype(BF16)
        pre = jnp.dot(xb, wg_ref[c], preferred_element_type=F32)
        pre_a.append(pre[:, :GATE_TILE])
        pre_x.append(pre[:, GATE_TILE:])
    r = jax.nn.sigmoid(jnp.concatenate(pre_a, axis=1) + ba)
    i = jax.nn.sigmoid(jnp.concatenate(pre_x, axis=1) + bx)
    log_a = (-RG_C * r) * _softplus(-lam)
    a = jnp.exp(log_a)
    one_minus_a2 = -jnp.tanh(log_a) * (a * a + 1.0)
    u = jnp.sqrt(one_minus_a2) * (i * xc)
    return a, u


def _rglru_prompt_body(xr_ref, gr_ref, cw_ref, cb_ref, wg_ref, ba_ref, bx_ref, lam_ref, gn_ref,
                       out_ref, hlast_ref, xbuf, abuf, ubuf, hbuf, hc_ref, *, tt, conv_w):
    ti = pl.program_id(1)
    d_rnn = xr_ref.shape[-1]

    @pl.when(ti == 0)
    def _():
        xbuf[0:SUBLANES, :] = jnp.zeros((SUBLANES, d_rnn), F32)
        hc_ref[...] = jnp.zeros_like(hc_ref)

    x = xr_ref[...]
    xbuf[SUBLANES:SUBLANES + tt, :] = x
    xc = cw_ref[conv_w - 1:conv_w, :] * x
    for k in range(conv_w - 1):
        off = SUBLANES - (conv_w - 1) + k
        xc = xc + cw_ref[k:k + 1, :] * xbuf[off:off + tt, :]
    xc = xc + cb_ref[...]
    xbuf[0:SUBLANES, :] = xbuf[tt:tt + SUBLANES, :]

    a, u = _rglru_gates(xc, wg_ref, ba_ref[...], bx_ref[...], lam_ref[...])
    abuf[...] = a
    ubuf[...] = u

    row = lax.broadcasted_iota(jnp.int32, (SUBLANES, d_rnn), 0)

    def group(gidx, hc):
        r0 = pl.multiple_of(gidx * SUBLANES, SUBLANES)
        ag = abuf[pl.ds(r0, SUBLANES), :]
        ug = ubuf[pl.ds(r0, SUBLANES), :]
        for k in (1, 2, 4):
            a_prev = jnp.where(row >= k, pltpu.roll(ag, k, 0), 1.0)
            u_prev = jnp.where(row >= k, pltpu.roll(ug, k, 0), 0.0)
            ug = ag * u_prev + ug
            ag = ag * a_prev
        h = ag * hc + ug
        hbuf[pl.ds(r0, SUBLANES), :] = h
        return jnp.broadcast_to(h[SUBLANES - 1:SUBLANES, :], (SUBLANES, d_rnn))

    hc = lax.fori_loop(0, tt // SUBLANES, group, hc_ref[...], unroll=2)
    hc_ref[...] = hc
    hlast_ref[0] = hc

    y = jax.nn.gelu(gr_ref[...]) * hbuf[...]
    out_ref[...] = _rms(y, gn_ref[...]).astype(BF16)


def _rglru_prompt(xr, gr, cw, cb, wg, ba, bx, lam, gn, batch, seq, tt):
    d_rnn = xr.shape[-1]
    nt = seq // tt
    conv_w = cw.shape[0]
    tile = lambda b, t: (b * nt + t, 0)
    out, hlast = pl.pallas_call(
        functools.partial(_rglru_prompt_body, tt=tt, conv_w=conv_w),
        grid=(batch, nt),
        in_specs=[
            pl.BlockSpec((tt, d_rnn), tile),
            pl.BlockSpec((tt, d_rnn), tile),
            _const_spec(cw.shape), _const_spec(cb.shape), _const_spec(wg.shape),
            _const_spec(ba.shape), _const_spec(bx.shape), _const_spec(lam.shape), _const_spec(gn.shape),
        ],
        out_specs=(
            pl.BlockSpec((tt, d_rnn), tile),
            pl.BlockSpec((1, SUBLANES, d_rnn), lambda b, t: (b, 0, 0)),
        ),
        out_shape=(
            jax.ShapeDtypeStruct((batch * seq, d_rnn), BF16),
            jax.ShapeDtypeStruct((batch, SUBLANES, d_rnn), F32),
        ),
        scratch_shapes=[
            pltpu.VMEM((tt + SUBLANES, d_rnn), F32),
            pltpu.VMEM((tt, d_rnn), F32),
            pltpu.VMEM((tt, d_rnn), F32),
            pltpu.VMEM((tt, d_rnn), F32),
            pltpu.VMEM((SUBLANES, d_rnn), F32),
        ],
        compiler_params=pltpu.CompilerParams(
            dimension_semantics=("arbitrary", "arbitrary"), vmem_limit_bytes=VMEM_LIMIT_BYTES),
        name="rglru_prompt",
    )(xr, gr, cw, cb, wg, ba, bx, lam, gn)
    return out, hlast[:, 0, :]


def _rglru_sample_body(xr_ref, gr_ref, c0_ref, c1_ref, c2_ref, h0_ref, cw_ref, cb_ref, wg_ref,
                       ba_ref, bx_ref, lam_ref, gn_ref, out_ref, hnew_ref, *, nb, steps):
    x = xr_ref[...]
    hist = [c0_ref[...], c1_ref[...], c2_ref[...]]
    xs = [x[t * nb:(t + 1) * nb, :] for t in range(steps)]
    xp = hist + xs
    conv_w = len(hist) + 1
    xc = cw_ref[conv_w - 1:conv_w, :] * x
    for k in range(conv_w - 1):
        shifted = jnp.concatenate(xp[k:k + steps], axis=0)
        xc = xc + cw_ref[k:k + 1, :] * shifted
    xc = xc + cb_ref[...]
    a, u = _rglru_gates(xc, wg_ref, ba_ref[...], bx_ref[...], lam_ref[...])
    h = h0_ref[...]
    hs = []
    for t in range(steps):
        h = a[t * nb:(t + 1) * nb, :] * h + u[t * nb:(t + 1) * nb, :]
        hs.append(h)
    hnew_ref[...] = h
    y = jax.nn.gelu(gr_ref[...]) * jnp.concatenate(hs, axis=0)
    out_ref[...] = _rms(y, gn_ref[...]).astype(BF16)


def _rglru_sample(xr, gr, conv_state2d, h0, cw, cb, wg, ba, bx, lam, gn, nb, steps):
    d_rnn = xr.shape[-1]
    n = nb * steps
    assert cw.shape[0] == 4
    full = lambda shape: pl.BlockSpec(shape, lambda i: (0,) * len(shape))
    return pl.pallas_call(
        functools.partial(_rglru_sample_body, nb=nb, steps=steps),
        grid=(1,),
        in_specs=[
            full((n, d_rnn)), full((n, d_rnn)),
            pl.BlockSpec((nb, d_rnn), lambda i: (0, 0)),
            pl.BlockSpec((nb, d_rnn), lambda i: (1, 0)),
            pl.BlockSpec((nb, d_rnn), lambda i: (2, 0)),
            full((nb, d_rnn)),
            full(cw.shape), full(cb.shape), full(wg.shape), full(ba.shape), full(bx.shape),
            full(lam.shape), full(gn.shape),
        ],
        out_specs=(full((n, d_rnn)), full((nb, d_rnn))),
        out_shape=(jax.ShapeDtypeStruct((n, d_rnn), BF16), jax.ShapeDtypeStruct((nb, d_rnn), F32)),
        compiler_params=pltpu.CompilerParams(
            dimension_semantics=("arbitrary",), vmem_limit_bytes=VMEM_LIMIT_BYTES),
        name="rglru_sample",
    )(xr, gr, conv_state2d, conv_state2d, conv_state2d, h0, cw, cb, wg, ba, bx, lam, gn)


def _swa_prompt_body(code_ref, tbl_ref, sink_ref, q_ref, kp_ref, kc_ref, vp_ref, vc_ref, gn_ref,
                     out_ref, bias_ref):
    j = pl.program_id(1)
    blk = q_ref.shape[0]
    rows = GQA_GROUP * blk

    @pl.when((pl.program_id(0) == 0) & (j == 0))
    def _():
        code = code_ref[...]
        qi = lax.broadcasted_iota(jnp.int32, code.shape, 0)
        kj = lax.broadcasted_iota(jnp.int32, code.shape, 1)
        dist = blk + qi - kj
        in_window = (dist & -WINDOW) == 0
        for h in range(N_HEADS):
            def pick(b, acc, h=h):
                return jnp.where(code == b, tbl_ref[b * N_HEADS + h], acc)
            bias_h = lax.fori_loop(0, N_BUCKETS, pick, jnp.zeros(code.shape, F32))
            bias_h = jnp.where(in_window, bias_h, NEG_INF)
            kv, g = divmod(h, GQA_GROUP)
            bias_ref[1, kv, g * blk:(g + 1) * blk, :] = bias_h
            bias_ref[0, kv, g * blk:(g + 1) * blk, :] = jnp.where(kj >= blk, bias_h, NEG_INF)

    has_prev = jnp.where(j > 0, 1, 0)

    kband = jnp.concatenate([kp_ref[...], kc_ref[...]], axis=0)
    vband = jnp.concatenate([vp_ref[...], vc_ref[...]], axis=0)
    lane = lax.broadcasted_iota(jnp.int32, (2 * blk, LANES), 1)
    row_g = lax.broadcasted_iota(jnp.int32, (rows, 1), 0) // blk
    q = q_ref[...]

    outs = []
    for cg in range(KV_HEADS // 2):
        qs = jnp.concatenate(
            [q[:, (cg * GQA_GROUP + g) * LANES:(cg * GQA_GROUP + g + 1) * LANES] for g in range(GQA_GROUP)],
            axis=0)
        ka = kband[:, cg * LANES:(cg + 1) * LANES]
        va = vband[:, cg * LANES:(cg + 1) * LANES]
        o = None
        for par in range(2):
            kv = 2 * cg + par
            half = (lane < HEAD_DIM) if par == 0 else (lane >= HEAD_DIM)
            km = jnp.where(half, ka, 0.0).astype(BF16)
            vm = jnp.where(half, va, 0.0).astype(BF16)
            s = lax.dot_general(qs, km, (((1,), (1,)), ((), ())), preferred_element_type=F32)
            s = s + bias_ref[has_prev, kv]
            sink = jnp.zeros((rows, 1), F32)
            for g in range(GQA_GROUP):
                sink = jnp.where(row_g == g, sink_ref[kv * GQA_GROUP + g], sink)
            m = jnp.maximum(jnp.max(s, axis=-1, keepdims=True), sink)
            p = jnp.exp(s - m)
            denom = jnp.sum(p, axis=-1, keepdims=True) + jnp.exp(sink - m)
            w = p * (1.0 / denom)
            part = jnp.dot(w.astype(BF16), vm, preferred_element_type=F32)
            o = part if o is None else o + part
        outs.extend(o[g * blk:(g + 1) * blk, :] for g in range(GQA_GROUP))
    y = jnp.concatenate(outs, axis=1)
    out_ref[...] = _rms(y, gn_ref[...]).astype(BF16)


def _swa_prompt(q, k, v, code, tbl, sinks, gn, batch, seq):
    blk = WINDOW
    nb = seq // blk
    d_attn = q.shape[-1]
    kv_dim = k.shape[-1]
    cur = lambda b, j: (b * nb + j, 0)
    prev = lambda b, j: (b * nb + jnp.maximum(j - 1, 0), 0)
    return pl.pallas_call(
        _swa_prompt_body,
        grid=(batch, nb),
        in_specs=[
            _const_spec(code.shape), _smem_spec(), _smem_spec(),
            pl.BlockSpec((blk, d_attn), cur),
            pl.BlockSpec((blk, kv_dim), prev), pl.BlockSpec((blk, kv_dim), cur),
            pl.BlockSpec((blk, kv_dim), prev), pl.BlockSpec((blk, kv_dim), cur),
            _const_spec(gn.shape),
        ],
        out_specs=pl.BlockSpec((blk, d_attn), cur),
        out_shape=jax.ShapeDtypeStruct((batch * seq, d_attn), BF16),
        scratch_shapes=[pltpu.VMEM((2, KV_HEADS, GQA_GROUP * blk, 2 * blk), F32)],
        compiler_params=pltpu.CompilerParams(
            dimension_semantics=("arbitrary", "arbitrary"), vmem_limit_bytes=VMEM_LIMIT_BYTES),
        name="swa_prompt",
    )(code, tbl, sinks, q, k, k, v, v, gn)


def _swa_sample_body(codec_ref, coden_ref, tbl_ref, sink_ref, q_ref, ck_ref, cv_ref, kn_ref, vn_ref,
                     out_ref, biasc_ref, biasn_ref, sinkc_ref, *, steps):
    bb = q_ref.shape[0]
    nrow = N_HEADS * steps
    win = ck_ref.shape[2]
    npad = kn_ref.shape[1]

    @pl.when(pl.program_id(0) == 0)
    def _():
        codec = codec_ref[...]
        coden = coden_ref[...]
        hrow = lax.broadcasted_iota(jnp.int32, (nrow, 1), 0) // steps

        def pick(idx, accs):
            ac, an = accs
            val = tbl_ref[idx]
            return jnp.where(codec == idx, val, ac), jnp.where(coden == idx, val, an)
        bc, bn = lax.fori_loop(0, N_BUCKETS * N_HEADS, pick,
                               (jnp.zeros(codec.shape, F32), jnp.zeros(coden.shape, F32)))
        biasc_ref[...] = bc
        biasn_ref[...] = bn

        def pick_sink(h, acc):
            return jnp.where(hrow == h, sink_ref[h], acc)
        sinkc_ref[...] = lax.fori_loop(0, N_HEADS, pick_sink, jnp.zeros((nrow, 1), F32))

    q = q_ref[...].astype(F32)
    lane = lax.broadcasted_iota(jnp.int32, (bb, GQA_GROUP * steps, LANES), 2)
    zeros = jnp.zeros((bb, GQA_GROUP * steps, LANES), F32)
    pieces = []
    for cg in range(KV_HEADS // 2):
        for par in range(2):
            half = (lane < HEAD_DIM) if par == 0 else (lane >= HEAD_DIM)
            qm = jnp.where(half, q[:, cg], zeros)
            pieces.append(jnp.concatenate([qm, zeros] if cg == 0 else [zeros, qm], axis=2))
    qm = jnp.concatenate(pieces, axis=1).astype(BF16)

    ck = ck_ref[...].astype(BF16)
    kn = kn_ref[...].astype(BF16)
    s_c = jnp.einsum("bqd,bdk->bqk", qm, ck, preferred_element_type=F32)
    s_n = jnp.einsum("bqd,bkd->bqk", qm, kn, preferred_element_type=F32)

    t_c = lax.broadcasted_iota(jnp.int32, (nrow, win), 0) % steps
    k_c = lax.broadcasted_iota(jnp.int32, (nrow, win), 1)
    valid_c = k_c > t_c
    t_n = lax.broadcasted_iota(jnp.int32, (nrow, npad), 0) % steps
    k_n = lax.broadcasted_iota(jnp.int32, (nrow, npad), 1)
    valid_n = k_n <= t_n

    s_c = jnp.where(valid_c[None], s_c + biasc_ref[...][None], NEG_INF)
    s_n = jnp.where(valid_n[None], s_n + biasn_ref[...][None], NEG_INF)
    sink = sinkc_ref[...][None]
    m = jnp.maximum(jnp.maximum(jnp.max(s_c, axis=-1, keepdims=True),
                                jnp.max(s_n, axis=-1, keepdims=True)), sink)
    p_c = jnp.exp(s_c - m)
    p_n = jnp.exp(s_n - m)
    denom = (jnp.sum(p_c, axis=-1, keepdims=True) + jnp.sum(p_n, axis=-1, keepdims=True)
             + jnp.exp(sink - m))
    r = 1.0 / denom
    w_c = (p_c * r).astype(BF16)
    w_n = (p_n * r).astype(BF16)
    o = (jnp.einsum("bqk,bdk->bqd", w_c, cv_ref[...].astype(BF16), preferred_element_type=F32)
         + jnp.einsum("bqk,bkd->bqd", w_n, vn_ref[...].astype(BF16), preferred_element_type=F32))
    gt = GQA_GROUP * steps
    lane_o = lax.broadcasted_iota(jnp.int32, (bb, gt, LANES), 2)
    for cg in range(KV_HEADS // 2):
        lo = o[:, cg * 2 * gt:cg * 2 * gt + gt, cg * LANES:(cg + 1) * LANES]
        hi = o[:, cg * 2 * gt + gt:(cg + 1) * 2 * gt, cg * LANES:(cg + 1) * LANES]
        out_ref[:, cg] = jnp.where(lane_o < HEAD_DIM, lo, hi)


def _swa_sample(qs, ck, cv, kn, vn, codec, coden, tbl, sinks, steps, bb):
    nbatch, ncg, gt, _ = qs.shape
    kv_dim, win = ck.shape[1], ck.shape[2]
    npad = kn.shape[1]
    nrow = N_HEADS * steps
    blk4 = lambda i: (i, 0, 0, 0)
    blk3 = lambda i: (i, 0, 0)
    return pl.pallas_call(
        functools.partial(_swa_sample_body, steps=steps),
        grid=(nbatch // bb,),
        in_specs=[
            _const_spec(codec.shape), _const_spec(coden.shape), _smem_spec(), _smem_spec(),
            pl.BlockSpec((bb, ncg, gt, LANES), blk4),
            pl.BlockSpec((bb, kv_dim, win), blk3), pl.BlockSpec((bb, kv_dim, win), blk3),
            pl.BlockSpec((bb, npad, kv_dim), blk3), pl.BlockSpec((bb, npad, kv_dim), blk3),
        ],
        out_specs=pl.BlockSpec((bb, ncg, gt, LANES), blk4),
        out_shape=jax.ShapeDtypeStruct((nbatch, ncg, gt, LANES), F32),
        scratch_shapes=[pltpu.VMEM((nrow, win), F32), pltpu.VMEM((nrow, npad), F32),
                        pltpu.VMEM((nrow, 1), F32)],
        compiler_params=pltpu.CompilerParams(
            dimension_semantics=("arbitrary",), vmem_limit_bytes=VMEM_LIMIT_BYTES),
        name="swa_sample",
    )(codec, coden, tbl, sinks, qs, ck, cv, kn, vn)


def _shift_rows(up, hist, k):
    rolled = pltpu.roll(up, k, 0)
    row = lax.broadcasted_iota(jnp.int32, hist.shape, 0)
    head = jnp.where(row < k, pltpu.roll(hist, k, 0), rolled[0:SUBLANES, :])
    return jnp.concatenate([head, rolled[SUBLANES:, :]], axis=0)


def _ffn_conv(up, hist, cw, cb, *, tm, sample_nb):
    if sample_nb is None:
        prev2 = _shift_rows(up, hist, 2)
        prev1 = _shift_rows(up, hist, 1)
    else:
        s0, s1 = hist
        prev2 = jnp.concatenate([s0, s1, up[:tm - 2 * sample_nb, :]], axis=0)
        prev1 = jnp.concatenate([s1, up[:tm - sample_nb, :]], axis=0)
    return cw[0:1, :] * prev2 + cw[1:2, :] * prev1 + cw[2:3, :] * up + cb


FFN_CHUNK = 256
FFN_SLAB = 256


def _ffn_prompt_body(x_hbm, mr_ref, ma_ref, wor_ref, woa_ref, gf_ref, wg_ref, wv_ref, cwg_ref, cwv_ref, cbg_ref,
                     cbv_ref, wd_ref, gfin_ref, y_ref, tg_ref, tv_ref, hn_ref, car_g, car_v, x_sem,
                     *, tm, tiles_per_seq):
    i = pl.program_id(0)
    j = pl.program_id(1)
    nj = pl.num_programs(1)
    n_slab = tm // FFN_SLAB
    slab = lambda r: slice(r * FFN_SLAB, (r + 1) * FFN_SLAB)

    @pl.when(j == 0)
    def _():
        x_copy = pltpu.make_async_copy(x_hbm.at[pl.ds(pl.multiple_of(i * tm, tm), tm), :], y_ref, x_sem)
        x_copy.start()

        def out_proj(r):
            return (jnp.dot(mr_ref[slab(r), :], wor_ref[...], preferred_element_type=F32)
                    + jnp.dot(ma_ref[slab(r), :], woa_ref[...], preferred_element_type=F32))

        pending = out_proj(0)
        x_copy.wait()
        for r in range(n_slab):
            d = pending
            if r + 1 < n_slab:
                pending = out_proj(r + 1)
            h = y_ref[slab(r), :] + d
            y_ref[slab(r), :] = h
            hn_ref[slab(r), :] = _rms(h, gf_ref[...]).astype(BF16)

    @pl.when((i == 0) & (j == 0))
    def _():
        car_g[...] = jnp.zeros_like(car_g)
        car_v[...] = jnp.zeros_like(car_v)

    cwg, cwv, cbg, cbv = cwg_ref[...], cwv_ref[...], cbg_ref[...], cbv_ref[...]

    def up_proj(r):
        hn = hn_ref[slab(r), :]
        return (jnp.dot(hn, wg_ref[...], preferred_element_type=F32),
                jnp.dot(hn, wv_ref[...], preferred_element_type=F32))

    seq_start = i % tiles_per_seq == 0
    hist_g = jnp.where(seq_start, 0.0, car_g[j])
    hist_v = jnp.where(seq_start, 0.0, car_v[j])
    pending = up_proj(0)
    act = None
    for r in range(n_slab):
        up_g, up_v = pending
        if r + 1 < n_slab:
            pending = up_proj(r + 1)
        if act is not None:
            y_ref[slab(r - 1), :] += jnp.dot(act, wd_ref[...], preferred_element_type=F32)
        c_g = _ffn_conv(up_g, hist_g, cwg, cbg, tm=FFN_SLAB, sample_nb=None)
        c_v = _ffn_conv(up_v, hist_v, cwv, cbv, tm=FFN_SLAB, sample_nb=None)
        hist_g = up_g[FFN_SLAB - SUBLANES:, :]
        hist_v = up_v[FFN_SLAB - SUBLANES:, :]
        act = (jax.nn.gelu(c_g) * c_v).astype(BF16)
    y_ref[slab(n_slab - 1), :] += jnp.dot(act, wd_ref[...], preferred_element_type=F32)
    car_g[j] = hist_g
    car_v[j] = hist_v
    tg_ref[0] = hist_g
    tv_ref[0] = hist_v

    @pl.when(j == nj - 1)
    def _():
        for r in range(n_slab):
            y_ref[slab(r), :] = _rms(y_ref[slab(r), :], gfin_ref[...])


def _ffn_sample_body(x_ref, mr_ref, ma_ref, gna_ref, wor_ref, woa_ref, gf_ref, wg_ref, wv_ref, cwg_ref, cwv_ref,
                     cbg_ref, cbv_ref, wd_ref, gfin_ref, s0g_ref, s1g_ref, s0v_ref, s1v_ref,
                     y_ref, tg_ref, tv_ref, hn_ref, *, nb):
    j = pl.program_id(1)
    nj = pl.num_programs(1)
    tm = x_ref.shape[0]

    @pl.when(j == 0)
    def _():
        ma = _rms(ma_ref[...], gna_ref[...]).astype(BF16)
        h = (x_ref[...] + jnp.dot(mr_ref[...], wor_ref[...], preferred_element_type=F32)
             + jnp.dot(ma, woa_ref[...], preferred_element_type=F32))
        y_ref[...] = h
        hn_ref[...] = _rms(h, gf_ref[...]).astype(BF16)

    hn = hn_ref[...]
    tf = wg_ref.shape[1]
    acts = []
    for c in range(tf // FFN_CHUNK):
        cs = slice(c * FFN_CHUNK, (c + 1) * FFN_CHUNK)
        up_g = jnp.dot(hn, wg_ref[:, cs], preferred_element_type=F32)
        up_v = jnp.dot(hn, wv_ref[:, cs], preferred_element_type=F32)
        tg_ref[0, :, cs] = up_g[tm - 2 * nb:tm - nb, :]
        tg_ref[1, :, cs] = up_g[tm - nb:, :]
        tv_ref[0, :, cs] = up_v[tm - 2 * nb:tm - nb, :]
        tv_ref[1, :, cs] = up_v[tm - nb:, :]
        c_g = _ffn_conv(up_g, (s0g_ref[:, cs], s1g_ref[:, cs]), cwg_ref[:, cs], cbg_ref[:, cs],
                        tm=tm, sample_nb=nb)
        c_v = _ffn_conv(up_v, (s0v_ref[:, cs], s1v_ref[:, cs]), cwv_ref[:, cs], cbv_ref[:, cs],
                        tm=tm, sample_nb=nb)
        acts.append((jax.nn.gelu(c_g) * c_v).astype(BF16))
    act = jnp.concatenate(acts, axis=1)
    y_ref[...] += jnp.dot(act, wd_ref[...], preferred_element_type=F32)

    @pl.when(j == nj - 1)
    def _():
        y_ref[...] = _rms(y_ref[...], gfin_ref[...])


def _ffn_prompt(x, mr, ma, wor, woa, gf, wup, cw, cb, wd, gfin, batch, seq, tm, tf):
    n, d_model = x.shape
    d_half = mr.shape[-1]
    d_ff = wd.shape[0]
    ni, nj = n // tm, d_ff // tf
    tiles_per_seq = seq // tm
    row = lambda i, j: (i, 0)
    gate_col = lambda i, j: (0, j)
    val_col = lambda i, j: (0, nj + j)
    tail = lambda i, j: (i, 0, j)
    y, tg, tv = pl.pallas_call(
        functools.partial(_ffn_prompt_body, tm=tm, tiles_per_seq=tiles_per_seq),
        grid=(ni, nj),
        in_specs=[
            pl.BlockSpec(memory_space=pl.ANY),
            pl.BlockSpec((tm, d_half), row), pl.BlockSpec((tm, d_half), row),
            _const_spec(wor.shape), _const_spec(woa.shape), _const_spec(gf.shape),
            pl.BlockSpec((d_model, tf), gate_col), pl.BlockSpec((d_model, tf), val_col),
            pl.BlockSpec((cw.shape[0], tf), gate_col), pl.BlockSpec((cw.shape[0], tf), val_col),
            pl.BlockSpec((1, tf), gate_col), pl.BlockSpec((1, tf), val_col),
            pl.BlockSpec((tf, d_model), lambda i, j: (j, 0)),
            _const_spec(gfin.shape),
        ],
        out_specs=(
            pl.BlockSpec((tm, d_model), row),
            pl.BlockSpec((1, SUBLANES, tf), tail),
            pl.BlockSpec((1, SUBLANES, tf), tail),
        ),
        out_shape=(
            jax.ShapeDtypeStruct((n, d_model), F32),
            jax.ShapeDtypeStruct((ni, SUBLANES, d_ff), F32),
            jax.ShapeDtypeStruct((ni, SUBLANES, d_ff), F32),
        ),
        scratch_shapes=[
            pltpu.VMEM((tm, d_model), BF16),
            pltpu.VMEM((nj, SUBLANES, tf), F32), pltpu.VMEM((nj, SUBLANES, tf), F32),
            pltpu.SemaphoreType.DMA(()),
        ],
        compiler_params=pltpu.CompilerParams(
            dimension_semantics=("arbitrary", "arbitrary"), vmem_limit_bytes=VMEM_LIMIT_BYTES),
        name="ffn_prompt",
    )(x, mr, ma, wor, woa, gf, wup, wup, cw, cw, cb, cb, wd, gfin)
    return y, tg[tiles_per_seq - 1::tiles_per_seq], tv[tiles_per_seq - 1::tiles_per_seq]


def _ffn_sample(x, mr, ma, gna, wor, woa, gf, wup, cw, cb, wd, gfin, state2d, nb, steps, tf):
    n, d_model = x.shape
    d_half = mr.shape[-1]
    d_ff = wd.shape[0]
    nj = d_ff // tf
    full2 = lambda shape: pl.BlockSpec(shape, lambda i, j: (0, 0))
    gate_col = lambda i, j: (0, j)
    val_col = lambda i, j: (0, nj + j)
    y, tg, tv = pl.pallas_call(
        functools.partial(_ffn_sample_body, nb=nb),
        grid=(1, nj),
        in_specs=[
            full2((n, d_model)), full2((n, d_half)), full2((n, d_half)), full2(gna.shape),
            _const_spec(wor.shape), _const_spec(woa.shape), _const_spec(gf.shape),
            pl.BlockSpec((d_model, tf), gate_col), pl.BlockSpec((d_model, tf), val_col),
            pl.BlockSpec((cw.shape[0], tf), gate_col), pl.BlockSpec((cw.shape[0], tf), val_col),
            pl.BlockSpec((1, tf), gate_col), pl.BlockSpec((1, tf), val_col),
            pl.BlockSpec((tf, d_model), lambda i, j: (j, 0)),
            _const_spec(gfin.shape),
            pl.BlockSpec((nb, tf), lambda i, j: (0, j)),
            pl.BlockSpec((nb, tf), lambda i, j: (0, 2 * nj + j)),
            pl.BlockSpec((nb, tf), lambda i, j: (0, nj + j)),
            pl.BlockSpec((nb, tf), lambda i, j: (0, 3 * nj + j)),
        ],
        out_specs=(
            full2((n, d_model)),
            pl.BlockSpec((2, nb, tf), lambda i, j: (0, 0, j)),
            pl.BlockSpec((2, nb, tf), lambda i, j: (0, 0, j)),
        ),
        out_shape=(
            jax.ShapeDtypeStruct((n, d_model), F32),
            jax.ShapeDtypeStruct((2, nb, d_ff), F32),
            jax.ShapeDtypeStruct((2, nb, d_ff), F32),
        ),
        scratch_shapes=[pltpu.VMEM((n, d_model), BF16)],
        compiler_params=pltpu.CompilerParams(
            dimension_semantics=("arbitrary", "arbitrary"), vmem_limit_bytes=VMEM_LIMIT_BYTES),
        name="ffn_sample",
    )(x, mr, ma, gna, wor, woa, gf, wup, wup, cw, cw, cb, cb, wd, gfin, state2d, state2d, state2d, state2d)
    return y, tg, tv


def _perm_heads(a, axis):
    shape = a.shape
    a = a.reshape(shape[:axis] + (KV_HEADS // 2, 2, GQA_GROUP, HEAD_DIM) + shape[axis + 1:])
    return jnp.swapaxes(a, axis + 1, axis + 2).reshape(shape)


def _gate_weights(w_a, w_x):
    nblk, bs, _ = w_a.shape
    per = GATE_TILE // bs
    eye = jnp.eye(per, dtype=w_a.dtype)

    def pack(w):
        w4 = w.reshape(nblk // per, per, bs, bs)
        return jnp.einsum("cgij,gh->cgihj", w4, eye).reshape(nblk // per, GATE_TILE, GATE_TILE)
    return jnp.concatenate([pack(w_a), pack(w_x)], axis=-1).astype(BF16)


def kernel(x_prompt, x_sample, state_rnn_conv, state_rnn_h, cache_win_k, cache_win_v, state_ffn_conv,
           norm_mix_g, w_in, rnn_conv_w, rnn_conv_b, w_gate_a, b_gate_a, w_gate_x, b_gate_x, rnn_lambda,
           attn_sinks, rel_bias_table, gn_rnn_g, gn_attn_g, w_out, norm_ffn_g, w_up, ffn_conv_w,
           ffn_conv_b, w_down, norm_final_g):
    batch, seq, d_model = x_prompt.shape
    nb, steps, _ = x_sample.shape
    depth = w_in.shape[0]
    d_rnn = rnn_conv_w.shape[-1]
    d_attn = N_HEADS * HEAD_DIM
    kv_dim = KV_HEADS * HEAD_DIM
    d_ff = w_down.shape[1]
    win = cache_win_k.shape[2]
    assert depth == 1 and d_rnn + d_attn == d_model and w_in.shape[-1] == 2 * d_rnn + d_attn + 2 * kv_dim
    assert win == WINDOW and seq % WINDOW == 0 and w_gate_a.shape[1] == RNN_BLOCKS
    assert rnn_conv_w.shape[1] == 4 and ffn_conv_w.shape[1] == 3 and steps >= 3 and nb % SUBLANES == 0

    w_in0 = w_in[0]
    o_q = 2 * d_rnn
    w_rg = w_in0[:, :o_q].astype(BF16)
    w_q = _perm_heads(w_in0[:, o_q:o_q + d_attn], 1).astype(BF16)
    w_kv = w_in0[:, o_q + d_attn:].astype(BF16)
    w_out0 = w_out[0]
    w_out_r = w_out0[:d_rnn].astype(BF16)
    w_out_a = _perm_heads(w_out0[d_rnn:], 0).astype(BF16)
    gn_attn_p = _perm_heads(gn_attn_g[0], 0).reshape(1, d_attn)
    tf = 512
    w_up_b = w_up[0].astype(BF16)
    w_down_b = w_down[0].astype(BF16)
    wg = _gate_weights(w_gate_a[0], w_gate_x[0])
    row2 = lambda a: a.reshape(1, -1)
    g_mix, g_ffn, g_fin, g_rnn = row2(norm_mix_g[0]), row2(norm_ffn_g[0]), row2(norm_final_g), row2(gn_rnn_g[0])
    cw_r, cb_r = rnn_conv_w[0], row2(rnn_conv_b[0])
    ba, bx, lam = row2(b_gate_a[0]), row2(b_gate_x[0]), row2(rnn_lambda[0])
    cw_f, cb_f = ffn_conv_w[0], row2(ffn_conv_b[0])
    tbl = rel_bias_table.reshape(-1)
    sinks = attn_sinks[0]

    blk = WINDOW
    qi = np.arange(blk)[:, None]
    kj = np.arange(2 * blk)[None, :]
    code_p = jnp.asarray(_t5_bucket_np(blk + qi - kj))
    r = np.arange(N_HEADS * steps)[:, None]
    t_r, h_r = r % steps, r // steps
    npad = 2 * SUBLANES
    code_c = jnp.asarray(_t5_bucket_np(win + t_r - np.arange(win)[None, :]) * N_HEADS + h_r)
    code_n = jnp.asarray(_t5_bucket_np(t_r - np.arange(npad)[None, :]) * N_HEADS + h_r)

    n_p = batch * seq
    tm_a = 512 if n_p % 512 == 0 else WINDOW
    xp2 = x_prompt.reshape(n_p, d_model)
    xr, gr, q, k, v = _inproj(xp2, pl.BlockSpec((tm_a, d_model), lambda i: (i, 0)), n_p // tm_a, tm_a,
                              g_mix, w_rg, w_q, w_kv)
    tt = 512 if seq % 512 == 0 else WINDOW
    m_rnn, h_last = _rglru_prompt(xr, gr, cw_r, cb_r, wg, ba, bx, lam, g_rnn, batch, seq, tt)
    m_attn = _swa_prompt(q, k, v, code_p, tbl, sinks, gn_attn_p, batch, seq)
    tm_f = next(t for t in (1024, 512, 256) if seq % t == 0)
    y_p, tail_g, tail_v = _ffn_prompt(xp2, m_rnn, m_attn, w_out_r, w_out_a, g_ffn, w_up_b, cw_f, cb_f,
                                      w_down_b, g_fin, batch, seq, tm_f, tf)
    y_prompt = y_p.reshape(batch, seq, d_model)
    p_rnn_conv = xr.reshape(batch, seq, d_rnn)[:, seq - 3:, :][None]
    p_rnn_h = h_last[None]
    last_win = lambda a: a.reshape(batch, seq, kv_dim)[:, seq - win:, :].reshape(
        1, batch, win, KV_HEADS, HEAD_DIM)
    p_win_k, p_win_v = last_win(k), last_win(v)
    p_ffn_conv = jnp.concatenate([tail_g[:, SUBLANES - 2:, :], tail_v[:, SUBLANES - 2:, :]], axis=-1)[None]

    n_s = nb * steps
    xs_tm = x_sample.transpose(1, 0, 2).reshape(n_s, d_model)
    xr_s, gr_s, q_s, k_s, v_s = _inproj(xs_tm, pl.BlockSpec((nb, d_model), lambda t: (t, 0)), steps, nb,
                                        g_mix, w_rg, w_q, w_kv)
    conv_tm = state_rnn_conv[0].transpose(1, 0, 2).reshape(3 * nb, d_rnn)
    m_rnn_s, h_new = _rglru_sample(xr_s, gr_s, conv_tm, state_rnn_h[0], cw_r, cb_r, wg, ba, bx, lam, g_rnn,
                                   nb, steps)
    ncg = KV_HEADS // 2
    qs = q_s.reshape(steps, nb, ncg, GQA_GROUP, LANES).transpose(1, 2, 3, 0, 4).reshape(
        nb, ncg, GQA_GROUP * steps, LANES)
    k_tm = k_s.reshape(steps, nb, kv_dim)
    v_tm = v_s.reshape(steps, nb, kv_dim)
    pad = ((0, 0), (0, npad - steps), (0, 0))
    ck_t = cache_win_k[0].transpose(0, 2, 3, 1).reshape(nb, kv_dim, win)
    cv_t = cache_win_v[0].transpose(0, 2, 3, 1).reshape(nb, kv_dim, win)
    bb = SUBLANES
    o_s = _swa_sample(qs, ck_t, cv_t, jnp.pad(k_tm.transpose(1, 0, 2), pad), jnp.pad(v_tm.transpose(1, 0, 2), pad),
                      code_c, code_n, tbl, sinks, steps, bb)
    ya_s = o_s.reshape(nb, ncg, GQA_GROUP, steps, LANES).transpose(3, 0, 1, 2, 4).reshape(n_s, d_attn)
    ffn2d = state_ffn_conv[0].reshape(nb, 2 * 2 * d_ff)
    y_s, ns_g, ns_v = _ffn_sample(xs_tm, m_rnn_s, ya_s, gn_attn_p, w_out_r, w_out_a, g_ffn, w_up_b, cw_f, cb_f,
                                  w_down_b, g_fin, ffn2d, nb, steps, tf)
    y_sample = y_s.reshape(steps, nb, d_model).transpose(1, 0, 2)
    s_rnn_conv = xr_s.reshape(steps, nb, d_rnn)[steps - 3:].transpose(1, 0, 2)[None]
    s_rnn_h = h_new[None]

    def slide(win_t, new_tm):
        new_t = new_tm.transpose(1, 2, 0)
        out_t = jnp.concatenate([win_t[:, :, steps:], new_t], axis=2)
        return out_t.reshape(nb, KV_HEADS, HEAD_DIM, win).transpose(0, 3, 1, 2)[None]
    s_win_k, s_win_v = slide(ck_t, k_tm), slide(cv_t, v_tm)
    s_ffn_conv = jnp.concatenate([ns_g, ns_v], axis=-1).transpose(1, 0, 2)[None]

    return (y_prompt, y_sample, p_rnn_conv, p_rnn_h, p_win_k, p_win_v, p_ffn_conv,
            s_rnn_conv, s_rnn_h, s_win_k, s_win_v, s_ffn_conv)
```

```python
import functools
import math

import numpy as np
import jax
import jax.numpy as jnp
from jax import lax
from jax.experimental import pallas as pl
from jax.experimental.pallas import tpu as pltpu

F32 = jnp.float32
BF16 = jnp.bfloat16

HEAD_DIM = 64
KV_HEADS = 4
N_HEADS = 16
GQA_GROUP = N_HEADS // KV_HEADS
RNN_BLOCKS = 16
RG_C = 8.0
WINDOW = 128
N_BUCKETS = 32
MAX_EXACT = N_BUCKETS // 2
REL_MAX_DIST = 128
EPS = 1e-6
NEG_INF = -1e30
ATTN_SCALE = HEAD_DIM ** -0.5

LANES = 128
SUBLANES = 8
VMEM_LIMIT_BYTES = 56 * 1024 * 1024

GATE_TILE = 256


def _t5_bucket_np(d):
    n = np.maximum(d, 0)
    nf = np.maximum(n, 1).astype(np.float32)
    large = MAX_EXACT + (np.log(nf / MAX_EXACT) / math.log(REL_MAX_DIST / MAX_EXACT)
                         * (N_BUCKETS - MAX_EXACT)).astype(np.int32)
    large = np.minimum(large, N_BUCKETS - 1)
    return np.where(n < MAX_EXACT, n, large).astype(np.int32)


def _rms(x, g):
    ms = jnp.mean(x * x, axis=-1, keepdims=True)
    return (x * lax.rsqrt(ms + EPS)) * g


def _softplus(x):
    return jnp.maximum(x, 0.0) + jnp.log1p(jnp.exp(-jnp.abs(x)))


def _const_spec(shape):
    nd = len(shape)
    return pl.BlockSpec(shape, lambda *_: (0,) * nd, pipeline_mode=pl.Buffered(1))


def _smem_spec():
    return pl.BlockSpec(memory_space=pltpu.SMEM)


def _inproj_body(x_ref, g_ref, wrg_ref, wq_ref, wkv_ref, xr_ref, gr_ref, q_ref, k_ref, v_ref):
    xn = _rms(x_ref[...], g_ref[...]).astype(BF16)
    d_rnn = xr_ref.shape[-1]
    kv_dim = k_ref.shape[-1]
    rg = jnp.dot(xn, wrg_ref[...], preferred_element_type=F32)
    xr_ref[...] = rg[:, :d_rnn]
    gr_ref[...] = rg[:, d_rnn:]
    q_ref[...] = (jnp.dot(xn, wq_ref[...], preferred_element_type=F32) * ATTN_SCALE).astype(BF16)
    kv = jnp.dot(xn, wkv_ref[...], preferred_element_type=F32)
    k_ref[...] = kv[:, :kv_dim]
    v_ref[...] = kv[:, kv_dim:]


def _inproj(x2d, x_spec, n_steps, tm, g, w_rg, w_q, w_kv):
    n = n_steps * tm
    d_model = g.shape[-1]
    d_rnn, d_attn, kv_dim = w_rg.shape[1] // 2, w_q.shape[1], w_kv.shape[1] // 2
    row = lambda i: (i, 0)
    out_shape = (
        jax.ShapeDtypeStruct((n, d_rnn), F32),
        jax.ShapeDtypeStruct((n, d_rnn), F32),
        jax.ShapeDtypeStruct((n, d_attn), BF16),
        jax.ShapeDtypeStruct((n, kv_dim), F32),
        jax.ShapeDtypeStruct((n, kv_dim), F32),
    )
    out_specs = (
        pl.BlockSpec((tm, d_rnn), row),
        pl.BlockSpec((tm, d_rnn), row),
        pl.BlockSpec((tm, d_attn), row),
        pl.BlockSpec((tm, kv_dim), row),
        pl.BlockSpec((tm, kv_dim), row),
    )
    return pl.pallas_call(
        _inproj_body,
        grid=(n_steps,),
        in_specs=[x_spec, _const_spec((1, d_model)), _const_spec(w_rg.shape), _const_spec(w_q.shape),
                  _const_spec(w_kv.shape)],
        out_specs=out_specs,
        out_shape=out_shape,
        compiler_params=pltpu.CompilerParams(
            dimension_semantics=("arbitrary",), vmem_limit_bytes=VMEM_LIMIT_BYTES),
        name="inproj",
    )(x2d, g, w_rg, w_q, w_kv)


def _rglru_gates(xc, wg_ref, ba, bx, lam):
    d_rnn = xc.shape[-1]
    pre_a, pre_x = [], []
    for c in range(d_rnn // GATE_TILE):
        xb = xc[:, c * GATE_TILE:(c + 1) * GATE_TILE].astype(BF16)
        pre = jnp.dot(xb, wg_ref[c], preferred_element_type=F32)
        pre_a.append(pre[:, :GATE_TILE])
        pre_x.append(pre[:, GATE_TILE:])
    r = jax.nn.sigmoid(jnp.concatenate(pre_a, axis=1) + ba)
    i = jax.nn.sigmoid(jnp.concatenate(pre_x, axis=1) + bx)
    log_a = (-RG_C * r) * _softplus(-lam)
    a = jnp.exp(log_a)
    one_minus_a2 = -jnp.tanh(log_a) * (a * a + 1.0)
    u = jnp.sqrt(one_minus_a2) * (i * xc)
    return a, u


def _rglru_prompt_body(xr_ref, gr_ref, cw_ref, cb_ref, wg_ref, ba_ref, bx_ref, lam_ref, gn_ref,
                       out_ref, hlast_ref, xbuf, abuf, ubuf, hbuf, hc_ref, *, tt, conv_w):
    ti = pl.program_id(1)
    d_rnn = xr_ref.shape[-1]

    @pl.when(ti == 0)
    def _():
        xbuf[0:SUBLANES, :] = jnp.zeros((SUBLANES, d_rnn), F32)
        hc_ref[...] = jnp.zeros_like(hc_ref)

    x = xr_ref[...]
    xbuf[SUBLANES:SUBLANES + tt, :] = x
    xc = cw_ref[conv_w - 1:conv_w, :] * x
    for k in range(conv_w - 1):
        off = SUBLANES - (conv_w - 1) + k
        xc = xc + cw_ref[k:k + 1, :] * xbuf[off:off + tt, :]
    xc = xc + cb_ref[...]
    xbuf[0:SUBLANES, :] = xbuf[tt:tt + SUBLANES, :]

    a, u = _rglru_gates(xc, wg_ref, ba_ref[...], bx_ref[...], lam_ref[...])
    abuf[...] = a
    ubuf[...] = u

    row = lax.broadcasted_iota(jnp.int32, (SUBLANES, d_rnn), 0)

    def group(gidx, hc):
        r0 = pl.multiple_of(gidx * SUBLANES, SUBLANES)
        ag = abuf[pl.ds(r0, SUBLANES), :]
        ug = ubuf[pl.ds(r0, SUBLANES), :]
        for k in (1, 2, 4):
            a_prev = jnp.where(row >= k, pltpu.roll(ag, k, 0), 1.0)
            u_prev = jnp.where(row >= k, pltpu.roll(ug, k, 0), 0.0)
            ug = ag * u_prev + ug
            ag = ag * a_prev
        h = ag * hc + ug
        hbuf[pl.ds(r0, SUBLANES), :] = h
        return jnp.broadcast_to(h[SUBLANES - 1:SUBLANES, :], (SUBLANES, d_rnn))

    hc = lax.fori_loop(0, tt // SUBLANES, group, hc_ref[...], unroll=2)
    hc_ref[...] = hc
    hlast_ref[0] = hc

    y = jax.nn.gelu(gr_ref[...]) * hbuf[...]
    out_ref[...] = _rms(y, gn_ref[...]).astype(BF16)


def _rglru_prompt(xr, gr, cw, cb, wg, ba, bx, lam, gn, batch, seq, tt):
    d_rnn = xr.shape[-1]
    nt = seq // tt
    conv_w = cw.shape[0]
    tile = lambda b, t: (b * nt + t, 0)
    out, hlast = pl.pallas_call(
        functools.partial(_rglru_prompt_body, tt=tt, conv_w=conv_w),
        grid=(batch, nt),
        in_specs=[
            pl.BlockSpec((tt, d_rnn), tile),
            pl.BlockSpec((tt, d_rnn), tile),
            _const_spec(cw.shape), _const_spec(cb.shape), _const_spec(wg.shape),
            _const_spec(ba.shape), _const_spec(bx.shape), _const_spec(lam.shape), _const_spec(gn.shape),
        ],
        out_specs=(
            pl.BlockSpec((tt, d_rnn), tile),
            pl.BlockSpec((1, SUBLANES, d_rnn), lambda b, t: (b, 0, 0)),
        ),
        out_shape=(
            jax.ShapeDtypeStruct((batch * seq, d_rnn), BF16),
            jax.ShapeDtypeStruct((batch, SUBLANES, d_rnn), F32),
        ),
        scratch_shapes=[
            pltpu.VMEM((tt + SUBLANES, d_rnn), F32),
            pltpu.VMEM((tt, d_rnn), F32),
            pltpu.VMEM((tt, d_rnn), F32),
            pltpu.VMEM((tt, d_rnn), F32),
            pltpu.VMEM((SUBLANES, d_rnn), F32),
        ],
        compiler_params=pltpu.CompilerParams(
            dimension_semantics=("arbitrary", "arbitrary"), vmem_limit_bytes=VMEM_LIMIT_BYTES),
        name="rglru_prompt",
    )(xr, gr, cw, cb, wg, ba, bx, lam, gn)
    return out, hlast[:, 0, :]


def _rglru_sample_body(xr_ref, gr_ref, c0_ref, c1_ref, c2_ref, h0_ref, cw_ref, cb_ref, wg_ref,
                       ba_ref, bx_ref, lam_ref, gn_ref, out_ref, hnew_ref, *, nb, steps):
    x = xr_ref[...]
    hist = [c0_ref[...], c1_ref[...], c2_ref[...]]
    xs = [x[t * nb:(t + 1) * nb, :] for t in range(steps)]
    xp = hist + xs
    conv_w = len(hist) + 1
    xc = cw_ref[conv_w - 1:conv_w, :] * x
    for k in range(conv_w - 1):
        shifted = jnp.concatenate(xp[k:k + steps], axis=0)
        xc = xc + cw_ref[k:k + 1, :] * shifted
    xc = xc + cb_ref[...]
    a, u = _rglru_gates(xc, wg_ref, ba_ref[...], bx_ref[...], lam_ref[...])
    h = h0_ref[...]
    hs = []
    for t in range(steps):
        h = a[t * nb:(t + 1) * nb, :] * h + u[t * nb:(t + 1) * nb, :]
        hs.append(h)
    hnew_ref[...] = h
    y = jax.nn.gelu(gr_ref[...]) * jnp.concatenate(hs, axis=0)
    out_ref[...] = _rms(y, gn_ref[...]).astype(BF16)


def _rglru_sample(xr, gr, conv_state2d, h0, cw, cb, wg, ba, bx, lam, gn, nb, steps):
    d_rnn = xr.shape[-1]
    n = nb * steps
    assert cw.shape[0] == 4
    full = lambda shape: pl.BlockSpec(shape, lambda i: (0,) * len(shape))
    return pl.pallas_call(
        functools.partial(_rglru_sample_body, nb=nb, steps=steps),
        grid=(1,),
        in_specs=[
            full((n, d_rnn)), full((n, d_rnn)),
            pl.BlockSpec((nb, d_rnn), lambda i: (0, 0)),
            pl.BlockSpec((nb, d_rnn), lambda i: (1, 0)),
            pl.BlockSpec((nb, d_rnn), lambda i: (2, 0)),
            full((nb, d_rnn)),
            full(cw.shape), full(cb.shape), full(wg.shape), full(ba.shape), full(bx.shape),
            full(lam.shape), full(gn.shape),
        ],
        out_specs=(full((n, d_rnn)), full((nb, d_rnn))),
        out_shape=(jax.ShapeDtypeStruct((n, d_rnn), BF16), jax.ShapeDtypeStruct((nb, d_rnn), F32)),
        compiler_params=pltpu.CompilerParams(
            dimension_semantics=("arbitrary",), vmem_limit_bytes=VMEM_LIMIT_BYTES),
        name="rglru_sample",
    )(xr, gr, conv_state2d, conv_state2d, conv_state2d, h0, cw, cb, wg, ba, bx, lam, gn)


def _swa_prompt_body(code_ref, tbl_ref, sink_ref, q_ref, kp_ref, kc_ref, vp_ref, vc_ref, gn_ref,
                     out_ref, bias_ref):
    j = pl.program_id(1)
    blk = q_ref.shape[0]
    rows = GQA_GROUP * blk

    @pl.when((pl.program_id(0) == 0) & (j == 0))
    def _():
        code = code_ref[...]
        qi = lax.broadcasted_iota(jnp.int32, code.shape, 0)
        kj = lax.broadcasted_iota(jnp.int32, code.shape, 1)
        dist = blk + qi - kj
        in_window = (dist & -WINDOW) == 0
        for h in range(N_HEADS):
            def pick(b, acc, h=h):
                return jnp.where(code == b, tbl_ref[b * N_HEADS + h], acc)
            bias_h = lax.fori_loop(0, N_BUCKETS, pick, jnp.zeros(code.shape, F32))
            bias_h = jnp.where(in_window, bias_h, NEG_INF)
            kv, g = divmod(h, GQA_GROUP)
            bias_ref[1, kv, g * blk:(g + 1) * blk, :] = bias_h
            bias_ref[0, kv, g * blk:(g + 1) * blk, :] = jnp.where(kj >= blk, bias_h, NEG_INF)

    has_prev = jnp.where(j > 0, 1, 0)

    kband = jnp.concatenate([kp_ref[...], kc_ref[...]], axis=0)
    vband = jnp.concatenate([vp_ref[...], vc_ref[...]], axis=0)
    lane = lax.broadcasted_iota(jnp.int32, (2 * blk, LANES), 1)
    row_g = lax.broadcasted_iota(jnp.int32, (rows, 1), 0) // blk
    q = q_ref[...]

    outs = []
    for cg in range(KV_HEADS // 2):
        qs = jnp.concatenate(
            [q[:, (cg * GQA_GROUP + g) * LANES:(cg * GQA_GROUP + g + 1) * LANES] for g in range(GQA_GROUP)],
            axis=0)
        ka = kband[:, cg * LANES:(cg + 1) * LANES]
        va = vband[:, cg * LANES:(cg + 1) * LANES]
        o = None
        for par in range(2):
            kv = 2 * cg + par
            half = (lane < HEAD_DIM) if par == 0 else (lane >= HEAD_DIM)
            km = jnp.where(half, ka, 0.0).astype(BF16)
            vm = jnp.where(half, va, 0.0).astype(BF16)
            s = lax.dot_general(qs, km, (((1,), (1,)), ((), ())), preferred_element_type=F32)
            s = s + bias_ref[has_prev, kv]
            sink = jnp.zeros((rows, 1), F32)
            for g in range(GQA_GROUP):
                sink = jnp.where(row_g == g, sink_ref[kv * GQA_GROUP + g], sink)
            m = jnp.maximum(jnp.max(s, axis=-1, keepdims=True), sink)
            p = jnp.exp(s - m)
            denom = jnp.sum(p, axis=-1, keepdims=True) + jnp.exp(sink - m)
            part = jnp.dot(p.astype(BF16), vm, preferred_element_type=F32) * (1.0 / denom)
            o = part if o is None else o + part
        outs.extend(o[g * blk:(g + 1) * blk, :] for g in range(GQA_GROUP))
    y = jnp.concatenate(outs, axis=1)
    out_ref[...] = _rms(y, gn_ref[...]).astype(BF16)


def _swa_prompt(q, k, v, code, tbl, sinks, gn, batch, seq):
    blk = WINDOW
    nb = seq // blk
    d_attn = q.shape[-1]
    kv_dim = k.shape[-1]
    cur = lambda b, j: (b * nb + j, 0)
    prev = lambda b, j: (b * nb + jnp.maximum(j - 1, 0), 0)
    return pl.pallas_call(
        _swa_prompt_body,
        grid=(batch, nb),
        in_specs=[
            _const_spec(code.shape), _smem_spec(), _smem_spec(),
            pl.BlockSpec((blk, d_attn), cur),
            pl.BlockSpec((blk, kv_dim), prev), pl.BlockSpec((blk, kv_dim), cur),
            pl.BlockSpec((blk, kv_dim), prev), pl.BlockSpec((blk, kv_dim), cur),
            _const_spec(gn.shape),
        ],
        out_specs=pl.BlockSpec((blk, d_attn), cur),
        out_shape=jax.ShapeDtypeStruct((batch * seq, d_attn), BF16),
        scratch_shapes=[pltpu.VMEM((2, KV_HEADS, GQA_GROUP * blk, 2 * blk), F32)],
        compiler_params=pltpu.CompilerParams(
            dimension_semantics=("arbitrary", "arbitrary"), vmem_limit_bytes=VMEM_LIMIT_BYTES),
        name="swa_prompt",
    )(code, tbl, sinks, q, k, k, v, v, gn)


def _slide_window(win_ref, new_ref, out_ref, steps):
    bb, kv_dim, win = win_ref.shape
    npad = new_ref.shape[1]
    new_t = new_ref[...].reshape(bb * npad, kv_dim).T
    lane = lax.broadcasted_iota(jnp.int32, (kv_dim, win), 1)
    for b in range(bb):
        placed = pltpu.roll(new_t, (win - steps - b * npad) % win, 1)
        shifted = pltpu.roll(win_ref[b], win - steps, 1)
        out_ref[b] = jnp.where(lane >= win - steps, placed, shifted)


def _swa_sample_body(codec_ref, coden_ref, tbl_ref, sink_ref, q_ref, ck_ref, cv_ref, kn_ref, vn_ref,
                     out_ref, ck_out_ref, cv_out_ref, biasc_ref, biasn_ref, sinkc_ref, *, steps):
    bb = q_ref.shape[0]
    nrow = N_HEADS * steps
    win = ck_ref.shape[2]
    npad = kn_ref.shape[1]

    @pl.when(pl.program_id(0) == 0)
    def _():
        codec = codec_ref[...]
        coden = coden_ref[...]
        hrow = lax.broadcasted_iota(jnp.int32, (nrow, 1), 0) // steps

        def pick(idx, accs):
            ac, an = accs
            val = tbl_ref[idx]
            return jnp.where(codec == idx, val, ac), jnp.where(coden == idx, val, an)
        bc, bn = lax.fori_loop(0, N_BUCKETS * N_HEADS, pick,
                               (jnp.zeros(codec.shape, F32), jnp.zeros(coden.shape, F32)))
        biasc_ref[...] = bc
        biasn_ref[...] = bn

        def pick_sink(h, acc):
            return jnp.where(hrow == h, sink_ref[h], acc)
        sinkc_ref[...] = lax.fori_loop(0, N_HEADS, pick_sink, jnp.zeros((nrow, 1), F32))

    q = q_ref[...].astype(F32)
    lane = lax.broadcasted_iota(jnp.int32, (bb, GQA_GROUP * steps, LANES), 2)
    zeros = jnp.zeros((bb, GQA_GROUP * steps, LANES), F32)
    pieces = []
    for cg in range(KV_HEADS // 2):
        for par in range(2):
            half = (lane < HEAD_DIM) if par == 0 else (lane >= HEAD_DIM)
            qm = jnp.where(half, q[:, cg], zeros)
            pieces.append(jnp.concatenate([qm, zeros] if cg == 0 else [zeros, qm], axis=2))
    qm = jnp.concatenate(pieces, axis=1).astype(BF16)

    ck = ck_ref[...].astype(BF16)
    kn = kn_ref[...].astype(BF16)
    s_c = jnp.einsum("bqd,bdk->bqk", qm, ck, preferred_element_type=F32)
    s_n = jnp.einsum("bqd,bkd->bqk", qm, kn, preferred_element_type=F32)

    t_c = lax.broadcasted_iota(jnp.int32, (nrow, win), 0) % steps
    k_c = lax.broadcasted_iota(jnp.int32, (nrow, win), 1)
    valid_c = k_c > t_c
    t_n = lax.broadcasted_iota(jnp.int32, (nrow, npad), 0) % steps
    k_n = lax.broadcasted_iota(jnp.int32, (nrow, npad), 1)
    valid_n = k_n <= t_n

    s_c = jnp.where(valid_c[None], s_c + biasc_ref[...][None], NEG_INF)
    s_n = jnp.where(valid_n[None], s_n + biasn_ref[...][None], NEG_INF)
    sink = sinkc_ref[...][None]
    m = jnp.maximum(jnp.maximum(jnp.max(s_c, axis=-1, keepdims=True),
                                jnp.max(s_n, axis=-1, keepdims=True)), sink)
    p_c = jnp.exp(s_c - m)
    p_n = jnp.exp(s_n - m)
    denom = (jnp.sum(p_c, axis=-1, keepdims=True) + jnp.sum(p_n, axis=-1, keepdims=True)
             + jnp.exp(sink - m))
    r = 1.0 / denom
    w_c = (p_c * r).astype(BF16)
    w_n = (p_n * r).astype(BF16)
    o = (jnp.einsum("bqk,bdk->bqd", w_c, cv_ref[...].astype(BF16), preferred_element_type=F32)
         + jnp.einsum("bqk,bkd->bqd", w_n, vn_ref[...].astype(BF16), preferred_element_type=F32))
    gt = GQA_GROUP * steps
    lane_o = lax.broadcasted_iota(jnp.int32, (bb, gt, LANES), 2)
    for cg in range(KV_HEADS // 2):
        lo = o[:, cg * 2 * gt:cg * 2 * gt + gt, cg * LANES:(cg + 1) * LANES]
        hi = o[:, cg * 2 * gt + gt:(cg + 1) * 2 * gt, cg * LANES:(cg + 1) * LANES]
        out_ref[:, cg] = jnp.where(lane_o < HEAD_DIM, lo, hi)

    _slide_window(ck_ref, kn_ref, ck_out_ref, steps)
    _slide_window(cv_ref, vn_ref, cv_out_ref, steps)


def _swa_sample(qs, ck, cv, kn, vn, codec, coden, tbl, sinks, steps, bb):
    nbatch, ncg, gt, _ = qs.shape
    kv_dim, win = ck.shape[1], ck.shape[2]
    npad = kn.shape[1]
    nrow = N_HEADS * steps
    assert bb * npad == win
    blk4 = lambda i: (i, 0, 0, 0)
    blk3 = lambda i: (i, 0, 0)
    return pl.pallas_call(
        functools.partial(_swa_sample_body, steps=steps),
        grid=(nbatch // bb,),
        in_specs=[
            _const_spec(codec.shape), _const_spec(coden.shape), _smem_spec(), _smem_spec(),
            pl.BlockSpec((bb, ncg, gt, LANES), blk4),
            pl.BlockSpec((bb, kv_dim, win), blk3), pl.BlockSpec((bb, kv_dim, win), blk3),
            pl.BlockSpec((bb, npad, kv_dim), blk3), pl.BlockSpec((bb, npad, kv_dim), blk3),
        ],
        out_specs=(pl.BlockSpec((bb, ncg, gt, LANES), blk4),
                   pl.BlockSpec((bb, kv_dim, win), blk3), pl.BlockSpec((bb, kv_dim, win), blk3)),
        out_shape=(jax.ShapeDtypeStruct((nbatch, ncg, gt, LANES), F32),
                   jax.ShapeDtypeStruct((nbatch, kv_dim, win), F32),
                   jax.ShapeDtypeStruct((nbatch, kv_dim, win), F32)),
        scratch_shapes=[pltpu.VMEM((nrow, win), F32), pltpu.VMEM((nrow, npad), F32),
                        pltpu.VMEM((nrow, 1), F32)],
        compiler_params=pltpu.CompilerParams(
            dimension_semantics=("arbitrary",), vmem_limit_bytes=VMEM_LIMIT_BYTES),
        name="swa_sample",
    )(codec, coden, tbl, sinks, qs, ck, cv, kn, vn)


def _shift_rows(up, hist, k):
    rolled = pltpu.roll(up, k, 0)
    row = lax.broadcasted_iota(jnp.int32, hist.shape, 0)
    head = jnp.where(row < k, pltpu.roll(hist, k, 0), rolled[0:SUBLANES, :])
    return jnp.concatenate([head, rolled[SUBLANES:, :]], axis=0)


def _ffn_conv(up, hist, cw, cb, *, tm, sample_nb):
    if sample_nb is None:
        prev2 = _shift_rows(up, hist, 2)
        prev1 = _shift_rows(up, hist, 1)
    else:
        s0, s1 = hist
        prev2 = jnp.concatenate([s0, s1, up[:tm - 2 * sample_nb, :]], axis=0)
        prev1 = jnp.concatenate([s1, up[:tm - sample_nb, :]], axis=0)
    return cw[0:1, :] * prev2 + cw[1:2, :] * prev1 + cw[2:3, :] * up + cb


FFN_CHUNK = 256
FFN_SLAB = 256


def _ffn_prompt_body(x_hbm, mr_ref, ma_ref, wor_ref, woa_ref, gf_ref, wg_ref, wv_ref, cwg_ref, cwv_ref, cbg_ref,
                     cbv_ref, wd_ref, gfin_ref, y_ref, tg_ref, tv_ref, hn_ref, car_g, car_v, x_sem,
                     *, tm, tiles_per_seq):
    i = pl.program_id(0)
    j = pl.program_id(1)
    nj = pl.num_programs(1)
    n_slab = tm // FFN_SLAB
    slab = lambda r: slice(r * FFN_SLAB, (r + 1) * FFN_SLAB)

    @pl.when(j == 0)
    def _():
        x_copy = pltpu.make_async_copy(x_hbm.at[pl.ds(pl.multiple_of(i * tm, tm), tm), :], y_ref, x_sem)
        x_copy.start()

        def out_proj(r):
            return (jnp.dot(mr_ref[slab(r), :], wor_ref[...], preferred_element_type=F32)
                    + jnp.dot(ma_ref[slab(r), :], woa_ref[...], preferred_element_type=F32))

        pending = out_proj(0)
        x_copy.wait()
        for r in range(n_slab):
            d = pending
            if r + 1 < n_slab:
                pending = out_proj(r + 1)
            h = y_ref[slab(r), :] + d
            y_ref[slab(r), :] = h
            hn_ref[slab(r), :] = _rms(h, gf_ref[...]).astype(BF16)

    @pl.when((i == 0) & (j == 0))
    def _():
        car_g[...] = jnp.zeros_like(car_g)
        car_v[...] = jnp.zeros_like(car_v)

    cwg, cwv, cbg, cbv = cwg_ref[...], cwv_ref[...], cbg_ref[...], cbv_ref[...]

    def up_proj(r):
        hn = hn_ref[slab(r), :]
        return (jnp.dot(hn, wg_ref[...], preferred_element_type=F32),
                jnp.dot(hn, wv_ref[...], preferred_element_type=F32))

    seq_start = i % tiles_per_seq == 0
    hist_g = jnp.where(seq_start, 0.0, car_g[j])
    hist_v = jnp.where(seq_start, 0.0, car_v[j])
    pending = up_proj(0)
    act = None
    for r in range(n_slab):
        up_g, up_v = pending
        if r + 1 < n_slab:
            pending = up_proj(r + 1)
        if act is not None:
            y_ref[slab(r - 1), :] += jnp.dot(act, wd_ref[...], preferred_element_type=F32)
        c_g = _ffn_conv(up_g, hist_g, cwg, cbg, tm=FFN_SLAB, sample_nb=None)
        c_v = _ffn_conv(up_v, hist_v, cwv, cbv, tm=FFN_SLAB, sample_nb=None)
        hist_g = up_g[FFN_SLAB - SUBLANES:, :]
        hist_v = up_v[FFN_SLAB - SUBLANES:, :]
        act = (jax.nn.gelu(c_g) * c_v).astype(BF16)
    y_ref[slab(n_slab - 1), :] += jnp.dot(act, wd_ref[...], preferred_element_type=F32)
    car_g[j] = hist_g
    car_v[j] = hist_v
    tg_ref[0] = hist_g
    tv_ref[0] = hist_v

    @pl.when(j == nj - 1)
    def _():
        for r in range(n_slab):
            y_ref[slab(r), :] = _rms(y_ref[slab(r), :], gfin_ref[...])


def _ffn_sample_body(x_ref, mr_ref, ma_ref, gna_ref, wor_ref, woa_ref, gf_ref, wg_ref, wv_ref, cwg_ref, cwv_ref,
                     cbg_ref, cbv_ref, wd_ref, gfin_ref, s0g_ref, s1g_ref, s0v_ref, s1v_ref,
                     y_ref, tg_ref, tv_ref, hn_ref, *, nb):
    j = pl.program_id(1)
    nj = pl.num_programs(1)
    tm = x_ref.shape[0]

    @pl.when(j == 0)
    def _():
        ma = _rms(ma_ref[...], gna_ref[...]).astype(BF16)
        h = (x_ref[...] + jnp.dot(mr_ref[...], wor_ref[...], preferred_element_type=F32)
             + jnp.dot(ma, woa_ref[...], preferred_element_type=F32))
        y_ref[...] = h
        hn_ref[...] = _rms(h, gf_ref[...]).astype(BF16)

    hn = hn_ref[...]
    tf = wg_ref.shape[1]
    acts = []
    for c in range(tf // FFN_CHUNK):
        cs = slice(c * FFN_CHUNK, (c + 1) * FFN_CHUNK)
        up_g = jnp.dot(hn, wg_ref[:, cs], preferred_element_type=F32)
        up_v = jnp.dot(hn, wv_ref[:, cs], preferred_element_type=F32)
        tg_ref[0, :, cs] = up_g[tm - 2 * nb:tm - nb, :]
        tg_ref[1, :, cs] = up_g[tm - nb:, :]
        tv_ref[0, :, cs] = up_v[tm - 2 * nb:tm - nb, :]
        tv_ref[1, :, cs] = up_v[tm - nb:, :]
        c_g = _ffn_conv(up_g, (s0g_ref[:, cs], s1g_ref[:, cs]), cwg_ref[:, cs], cbg_ref[:, cs],
                        tm=tm, sample_nb=nb)
        c_v = _ffn_conv(up_v, (s0v_ref[:, cs], s1v_ref[:, cs]), cwv_ref[:, cs], cbv_ref[:, cs],
                        tm=tm, sample_nb=nb)
        acts.append((jax.nn.gelu(c_g) * c_v).astype(BF16))
    act = jnp.concatenate(acts, axis=1)
    y_ref[...] += jnp.dot(act, wd_ref[...], preferred_element_type=F32)

    @pl.when(j == nj - 1)
    def _():
        y_ref[...] = _rms(y_ref[...], gfin_ref[...])


def _ffn_prompt(x, mr, ma, wor, woa, gf, wup, cw, cb, wd, gfin, batch, seq, tm, tf):
    n, d_model = x.shape
    d_half = mr.shape[-1]
    d_ff = wd.shape[0]
    ni, nj = n // tm, d_ff // tf
    tiles_per_seq = seq // tm
    row = lambda i, j: (i, 0)
    gate_col = lambda i, j: (0, j)
    val_col = lambda i, j: (0, nj + j)
    tail = lambda i, j: (i, 0, j)
    y, tg, tv = pl.pallas_call(
        functools.partial(_ffn_prompt_body, tm=tm, tiles_per_seq=tiles_per_seq),
        grid=(ni, nj),
        in_specs=[
            pl.BlockSpec(memory_space=pl.ANY),
            pl.BlockSpec((tm, d_half), row), pl.BlockSpec((tm, d_half), row),
            _const_spec(wor.shape), _const_spec(woa.shape), _const_spec(gf.shape),
            pl.BlockSpec((d_model, tf), gate_col), pl.BlockSpec((d_model, tf), val_col),
            pl.BlockSpec((cw.shape[0], tf), gate_col), pl.BlockSpec((cw.shape[0], tf), val_col),
            pl.BlockSpec((1, tf), gate_col), pl.BlockSpec((1, tf), val_col),
            pl.BlockSpec((tf, d_model), lambda i, j: (j, 0)),
            _const_spec(gfin.shape),
        ],
        out_specs=(
            pl.BlockSpec((tm, d_model), row),
            pl.BlockSpec((1, SUBLANES, tf), tail),
            pl.BlockSpec((1, SUBLANES, tf), tail),
        ),
        out_shape=(
            jax.ShapeDtypeStruct((n, d_model), F32),
            jax.ShapeDtypeStruct((ni, SUBLANES, d_ff), F32),
            jax.ShapeDtypeStruct((ni, SUBLANES, d_ff), F32),
        ),
        scratch_shapes=[
            pltpu.VMEM((tm, d_model), BF16),
            pltpu.VMEM((nj, SUBLANES, tf), F32), pltpu.VMEM((nj, SUBLANES, tf), F32),
            pltpu.SemaphoreType.DMA(()),
        ],
        compiler_params=pltpu.CompilerParams(
            dimension_semantics=("arbitrary", "arbitrary"), vmem_limit_bytes=VMEM_LIMIT_BYTES),
        name="ffn_prompt",
    )(x, mr, ma, wor, woa, gf, wup, wup, cw, cw, cb, cb, wd, gfin)
    return y, tg[tiles_per_seq - 1::tiles_per_seq], tv[tiles_per_seq - 1::tiles_per_seq]


def _ffn_sample(x, mr, ma, gna, wor, woa, gf, wup, cw, cb, wd, gfin, state2d, nb, steps, tf):
    n, d_model = x.shape
    d_half = mr.shape[-1]
    d_ff = wd.shape[0]
    nj = d_ff // tf
    full2 = lambda shape: pl.BlockSpec(shape, lambda i, j: (0, 0))
    gate_col = lambda i, j: (0, j)
    val_col = lambda i, j: (0, nj + j)
    y, tg, tv = pl.pallas_call(
        functools.partial(_ffn_sample_body, nb=nb),
        grid=(1, nj),
        in_specs=[
            full2((n, d_model)), full2((n, d_half)), full2((n, d_half)), full2(gna.shape),
            _const_spec(wor.shape), _const_spec(woa.shape), _const_spec(gf.shape),
            pl.BlockSpec((d_model, tf), gate_col), pl.BlockSpec((d_model, tf), val_col),
            pl.BlockSpec((cw.shape[0], tf), gate_col), pl.BlockSpec((cw.shape[0], tf), val_col),
            pl.BlockSpec((1, tf), gate_col), pl.BlockSpec((1, tf), val_col),
            pl.BlockSpec((tf, d_model), lambda i, j: (j, 0)),
            _const_spec(gfin.shape),
            pl.BlockSpec((nb, tf), lambda i, j: (0, j)),
            pl.BlockSpec((nb, tf), lambda i, j: (0, 2 * nj + j)),
            pl.BlockSpec((nb, tf), lambda i, j: (0, nj + j)),
            pl.BlockSpec((nb, tf), lambda i, j: (0, 3 * nj + j)),
        ],
        out_specs=(
            full2((n, d_model)),
            pl.BlockSpec((2, nb, tf), lambda i, j: (0, 0, j)),
            pl.BlockSpec((2, nb, tf), lambda i, j: (0, 0, j)),
        ),
        out_shape=(
            jax.ShapeDtypeStruct((n, d_model), F32),
            jax.ShapeDtypeStruct((2, nb, d_ff), F32),
            jax.ShapeDtypeStruct((2, nb, d_ff), F32),
        ),
        scratch_shapes=[pltpu.VMEM((n, d_model), BF16)],
        compiler_params=pltpu.CompilerParams(
            dimension_semantics=("arbitrary", "arbitrary"), vmem_limit_bytes=VMEM_LIMIT_BYTES),
        name="ffn_sample",
    )(x, mr, ma, gna, wor, woa, gf, wup, wup, cw, cw, cb, cb, wd, gfin, state2d, state2d, state2d, state2d)
    return y, tg, tv


def _perm_heads(a, axis):
    shape = a.shape
    a = a.reshape(shape[:axis] + (KV_HEADS // 2, 2, GQA_GROUP, HEAD_DIM) + shape[axis + 1:])
    return jnp.swapaxes(a, axis + 1, axis + 2).reshape(shape)


def _gate_weights(w_a, w_x):
    nblk, bs, _ = w_a.shape
    per = GATE_TILE // bs
    eye = jnp.eye(per, dtype=w_a.dtype)

    def pack(w):
        w4 = w.reshape(nblk // per, per, bs, bs)
        return jnp.einsum("cgij,gh->cgihj", w4, eye).reshape(nblk // per, GATE_TILE, GATE_TILE)
    return jnp.concatenate([pack(w_a), pack(w_x)], axis=-1).astype(BF16)


def kernel(x_prompt, x_sample, state_rnn_conv, state_rnn_h, cache_win_k, cache_win_v, state_ffn_conv,
           norm_mix_g, w_in, rnn_conv_w, rnn_conv_b, w_gate_a, b_gate_a, w_gate_x, b_gate_x, rnn_lambda,
           attn_sinks, rel_bias_table, gn_rnn_g, gn_attn_g, w_out, norm_ffn_g, w_up, ffn_conv_w,
           ffn_conv_b, w_down, norm_final_g):
    batch, seq, d_model = x_prompt.shape
    nb, steps, _ = x_sample.shape
    depth = w_in.shape[0]
    d_rnn = rnn_conv_w.shape[-1]
    d_attn = N_HEADS * HEAD_DIM
    kv_dim = KV_HEADS * HEAD_DIM
    d_ff = w_down.shape[1]
    win = cache_win_k.shape[2]
    assert depth == 1 and d_rnn + d_attn == d_model and w_in.shape[-1] == 2 * d_rnn + d_attn + 2 * kv_dim
    assert win == WINDOW and seq % WINDOW == 0 and w_gate_a.shape[1] == RNN_BLOCKS
    assert rnn_conv_w.shape[1] == 4 and ffn_conv_w.shape[1] == 3 and steps >= 3 and nb % SUBLANES == 0

    w_in0 = w_in[0]
    o_q = 2 * d_rnn
    w_rg = w_in0[:, :o_q].astype(BF16)
    w_q = _perm_heads(w_in0[:, o_q:o_q + d_attn], 1).astype(BF16)
    w_kv = w_in0[:, o_q + d_attn:].astype(BF16)
    w_out0 = w_out[0]
    w_out_r = w_out0[:d_rnn].astype(BF16)
    w_out_a = _perm_heads(w_out0[d_rnn:], 0).astype(BF16)
    gn_attn_p = _perm_heads(gn_attn_g[0], 0).reshape(1, d_attn)
    tf = 512
    w_up_b = w_up[0].astype(BF16)
    w_down_b = w_down[0].astype(BF16)
    wg = _gate_weights(w_gate_a[0], w_gate_x[0])
    row2 = lambda a: a.reshape(1, -1)
    g_mix, g_ffn, g_fin, g_rnn = row2(norm_mix_g[0]), row2(norm_ffn_g[0]), row2(norm_final_g), row2(gn_rnn_g[0])
    cw_r, cb_r = rnn_conv_w[0], row2(rnn_conv_b[0])
    ba, bx, lam = row2(b_gate_a[0]), row2(b_gate_x[0]), row2(rnn_lambda[0])
    cw_f, cb_f = ffn_conv_w[0], row2(ffn_conv_b[0])
    tbl = rel_bias_table.reshape(-1)
    sinks = attn_sinks[0]

    blk = WINDOW
    qi = np.arange(blk)[:, None]
    kj = np.arange(2 * blk)[None, :]
    code_p = jnp.asarray(_t5_bucket_np(blk + qi - kj))
    r = np.arange(N_HEADS * steps)[:, None]
    t_r, h_r = r % steps, r // steps
    npad = 2 * SUBLANES
    code_c = jnp.asarray(_t5_bucket_np(win + t_r - np.arange(win)[None, :]) * N_HEADS + h_r)
    code_n = jnp.asarray(_t5_bucket_np(t_r - np.arange(npad)[None, :]) * N_HEADS + h_r)

    n_p = batch * seq
    tm_a = 512 if n_p % 512 == 0 else WINDOW
    xp2 = x_prompt.reshape(n_p, d_model)
    xr, gr, q, k, v = _inproj(xp2, pl.BlockSpec((tm_a, d_model), lambda i: (i, 0)), n_p // tm_a, tm_a,
                              g_mix, w_rg, w_q, w_kv)
    tt = 512 if seq % 512 == 0 else WINDOW
    m_rnn, h_last = _rglru_prompt(xr, gr, cw_r, cb_r, wg, ba, bx, lam, g_rnn, batch, seq, tt)
    m_attn = _swa_prompt(q, k, v, code_p, tbl, sinks, gn_attn_p, batch, seq)
    tm_f = next(t for t in (1024, 512, 256) if seq % t == 0)
    y_p, tail_g, tail_v = _ffn_prompt(xp2, m_rnn, m_attn, w_out_r, w_out_a, g_ffn, w_up_b, cw_f, cb_f,
                                      w_down_b, g_fin, batch, seq, tm_f, tf)
    y_prompt = y_p.reshape(batch, seq, d_model)
    p_rnn_conv = xr.reshape(batch, seq, d_rnn)[:, seq - 3:, :][None]
    p_rnn_h = h_last[None]
    last_win = lambda a: a.reshape(batch, seq, kv_dim)[:, seq - win:, :].reshape(
        1, batch, win, KV_HEADS, HEAD_DIM)
    p_win_k, p_win_v = last_win(k), last_win(v)
    p_ffn_conv = jnp.concatenate([tail_g[:, SUBLANES - 2:, :], tail_v[:, SUBLANES - 2:, :]], axis=-1)[None]

    n_s = nb * steps
    xs_tm = x_sample.transpose(1, 0, 2).reshape(n_s, d_model)
    xr_s, gr_s, q_s, k_s, v_s = _inproj(xs_tm, pl.BlockSpec((nb, d_model), lambda t: (t, 0)), steps, nb,
                                        g_mix, w_rg, w_q, w_kv)
    conv_tm = state_rnn_conv[0].transpose(1, 0, 2).reshape(3 * nb, d_rnn)
    m_rnn_s, h_new = _rglru_sample(xr_s, gr_s, conv_tm, state_rnn_h[0], cw_r, cb_r, wg, ba, bx, lam, g_rnn,
                                   nb, steps)
    ncg = KV_HEADS // 2
    qs = q_s.reshape(steps, nb, ncg, GQA_GROUP, LANES).transpose(1, 2, 3, 0, 4).reshape(
        nb, ncg, GQA_GROUP * steps, LANES)
    k_tm = k_s.reshape(steps, nb, kv_dim)
    v_tm = v_s.reshape(steps, nb, kv_dim)
    pad = ((0, 0), (0, npad - steps), (0, 0))
    ck_t = cache_win_k[0].transpose(0, 2, 3, 1).reshape(nb, kv_dim, win)
    cv_t = cache_win_v[0].transpose(0, 2, 3, 1).reshape(nb, kv_dim, win)
    bb = SUBLANES
    o_s, ck_new, cv_new = _swa_sample(qs, ck_t, cv_t, jnp.pad(k_tm.transpose(1, 0, 2), pad),
                                      jnp.pad(v_tm.transpose(1, 0, 2), pad), code_c, code_n, tbl, sinks, steps, bb)
    ya_s = o_s.reshape(nb, ncg, GQA_GROUP, steps, LANES).transpose(3, 0, 1, 2, 4).reshape(n_s, d_attn)
    ffn2d = state_ffn_conv[0].reshape(nb, 2 * 2 * d_ff)
    y_s, ns_g, ns_v = _ffn_sample(xs_tm, m_rnn_s, ya_s, gn_attn_p, w_out_r, w_out_a, g_ffn, w_up_b, cw_f, cb_f,
                                  w_down_b, g_fin, ffn2d, nb, steps, tf)
    y_sample = y_s.reshape(steps, nb, d_model).transpose(1, 0, 2)
    s_rnn_conv = xr_s.reshape(steps, nb, d_rnn)[steps - 3:].transpose(1, 0, 2)[None]
    s_rnn_h = h_new[None]

    to_cache = lambda w_t: w_t.reshape(nb, KV_HEADS, HEAD_DIM, win).transpose(0, 3, 1, 2)[None]
    s_win_k, s_win_v = to_cache(ck_new), to_cache(cv_new)
    s_ffn_conv = jnp.concatenate([ns_g, ns_v], axis=-1).transpose(1, 0, 2)[None]

    return (y_prompt, y_sample, p_rnn_conv, p_rnn_h, p_win_k, p_win_v, p_ffn_conv,
            s_rnn_conv, s_rnn_h, s_win_k, s_win_v, s_ffn_conv)
```

```python
import functools
import math

import numpy as np
import jax
import jax.numpy as jnp
from jax import lax
from jax.experimental import pallas as pl
from jax.experimental.pallas import tpu as pltpu

F32 = jnp.float32
BF16 = jnp.bfloat16

HEAD_DIM = 64
KV_HEADS = 4
N_HEADS = 16
GQA_GROUP = N_HEADS // KV_HEADS
RNN_BLOCKS = 16
RG_C = 8.0
WINDOW = 128
N_BUCKETS = 32
MAX_EXACT = N_BUCKETS // 2
REL_MAX_DIST = 128
EPS = 1e-6
NEG_INF = -1e30
ATTN_SCALE = HEAD_DIM ** -0.5

LANES = 128
SUBLANES = 8
VMEM_LIMIT_BYTES = 56 * 1024 * 1024

GATE_TILE = 256


def _t5_bucket_np(d):
    n = np.maximum(d, 0)
    nf = np.maximum(n, 1).astype(np.float32)
    large = MAX_EXACT + (np.log(nf / MAX_EXACT) / math.log(REL_MAX_DIST / MAX_EXACT)
                         * (N_BUCKETS - MAX_EXACT)).astype(np.int32)
    large = np.minimum(large, N_BUCKETS - 1)
    return np.where(n < MAX_EXACT, n, large).astype(np.int32)


def _rms(x, g):
    ms = jnp.mean(x * x, axis=-1, keepdims=True)
    return (x * lax.rsqrt(ms + EPS)) * g


def _softplus(x):
    return jnp.maximum(x, 0.0) + jnp.log1p(jnp.exp(-jnp.abs(x)))


def _const_spec(shape):
    nd = len(shape)
    return pl.BlockSpec(shape, lambda *_: (0,) * nd, pipeline_mode=pl.Buffered(1))


def _smem_spec():
    return pl.BlockSpec(memory_space=pltpu.SMEM)


def _inproj_body(x_ref, g_ref, wrg_ref, wq_ref, wkv_ref, xr_ref, gr_ref, q_ref, k_ref, v_ref):
    xn = _rms(x_ref[...], g_ref[...]).astype(BF16)
    d_rnn = xr_ref.shape[-1]
    kv_dim = k_ref.shape[-1]
    rg = jnp.dot(xn, wrg_ref[...], preferred_element_type=F32)
    xr_ref[...] = rg[:, :d_rnn]
    gr_ref[...] = rg[:, d_rnn:]
    q_ref[...] = (jnp.dot(xn, wq_ref[...], preferred_element_type=F32) * ATTN_SCALE).astype(BF16)
    kv = jnp.dot(xn, wkv_ref[...], preferred_element_type=F32)
    k_ref[...] = kv[:, :kv_dim]
    v_ref[...] = kv[:, kv_dim:]


def _inproj(x2d, x_spec, n_steps, tm, g, w_rg, w_q, w_kv):
    n = n_steps * tm
    d_model = g.shape[-1]
    d_rnn, d_attn, kv_dim = w_rg.shape[1] // 2, w_q.shape[1], w_kv.shape[1] // 2
    row = lambda i: (i, 0)
    out_shape = (
        jax.ShapeDtypeStruct((n, d_rnn), F32),
        jax.ShapeDtypeStruct((n, d_rnn), F32),
        jax.ShapeDtypeStruct((n, d_attn), BF16),
        jax.ShapeDtypeStruct((n, kv_dim), F32),
        jax.ShapeDtypeStruct((n, kv_dim), F32),
    )
    out_specs = (
        pl.BlockSpec((tm, d_rnn), row),
        pl.BlockSpec((tm, d_rnn), row),
        pl.BlockSpec((tm, d_attn), row),
        pl.BlockSpec((tm, kv_dim), row),
        pl.BlockSpec((tm, kv_dim), row),
    )
    return pl.pallas_call(
        _inproj_body,
        grid=(n_steps,),
        in_specs=[x_spec, _const_spec((1, d_model)), _const_spec(w_rg.shape), _const_spec(w_q.shape),
                  _const_spec(w_kv.shape)],
        out_specs=out_specs,
        out_shape=out_shape,
        compiler_params=pltpu.CompilerParams(
            dimension_semantics=("arbitrary",), vmem_limit_bytes=VMEM_LIMIT_BYTES),
        name="inproj",
    )(x2d, g, w_rg, w_q, w_kv)


def _rglru_gates(xc, wg_ref, ba, bx, lam):
    d_rnn = xc.shape[-1]
    pre_a, pre_x = [], []
    for c in range(d_rnn // GATE_TILE):
        xb = xc[:, c * GATE_TILE:(c + 1) * GATE_TILE].astype(BF16)
        pre = jnp.dot(xb, wg_ref[c], preferred_element_type=F32)
        pre_a.append(pre[:, :GATE_TILE])
        pre_x.append(pre[:, GATE_TILE:])
    r = jax.nn.sigmoid(jnp.concatenate(pre_a, axis=1) + ba)
    i = jax.nn.sigmoid(jnp.concatenate(pre_x, axis=1) + bx)
    log_a = (-RG_C * r) * _softplus(-lam)
    a = jnp.exp(log_a)
    one_minus_a2 = -jnp.tanh(log_a) * (a * a + 1.0)
    u = jnp.sqrt(one_minus_a2) * (i * xc)
    return a, u


def _rglru_prompt_body(xr_ref, gr_ref, cw_ref, cb_ref, wg_ref, ba_ref, bx_ref, lam_ref, gn_ref,
                       out_ref, hlast_ref, xbuf, abuf, ubuf, hbuf, hc_ref, *, tt, conv_w):
    ti = pl.program_id(1)
    d_rnn = xr_ref.shape[-1]

    @pl.when(ti == 0)
    def _():
        xbuf[0:SUBLANES, :] = jnp.zeros((SUBLANES, d_rnn), F32)
        hc_ref[...] = jnp.zeros_like(hc_ref)

    x = xr_ref[...]
    xbuf[SUBLANES:SUBLANES + tt, :] = x
    xc = cw_ref[conv_w - 1:conv_w, :] * x
    for k in range(conv_w - 1):
        off = SUBLANES - (conv_w - 1) + k
        xc = xc + cw_ref[k:k + 1, :] * xbuf[off:off + tt, :]
    xc = xc + cb_ref[...]
    xbuf[0:SUBLANES, :] = xbuf[tt:tt + SUBLANES, :]

    a, u = _rglru_gates(xc, wg_ref, ba_ref[...], bx_ref[...], lam_ref[...])
    abuf[...] = a
    ubuf[...] = u

    row = lax.broadcasted_iota(jnp.int32, (SUBLANES, d_rnn), 0)

    def group(gidx, hc):
        r0 = pl.multiple_of(gidx * SUBLANES, SUBLANES)
        ag = abuf[pl.ds(r0, SUBLANES), :]
        ug = ubuf[pl.ds(r0, SUBLANES), :]
        for k in (1, 2, 4):
            a_prev = jnp.where(row >= k, pltpu.roll(ag, k, 0), 1.0)
            u_prev = jnp.where(row >= k, pltpu.roll(ug, k, 0), 0.0)
            ug = ag * u_prev + ug
            ag = ag * a_prev
        h = ag * hc + ug
        hbuf[pl.ds(r0, SUBLANES), :] = h
        return jnp.broadcast_to(h[SUBLANES - 1:SUBLANES, :], (SUBLANES, d_rnn))

    hc = lax.fori_loop(0, tt // SUBLANES, group, hc_ref[...], unroll=2)
    hc_ref[...] = hc
    hlast_ref[0] = hc

    y = jax.nn.gelu(gr_ref[...]) * hbuf[...]
    out_ref[...] = _rms(y, gn_ref[...]).astype(BF16)


def _rglru_prompt(xr, gr, cw, cb, wg, ba, bx, lam, gn, batch, seq, tt):
    d_rnn = xr.shape[-1]
    nt = seq // tt
    conv_w = cw.shape[0]
    tile = lambda b, t: (b * nt + t, 0)
    out, hlast = pl.pallas_call(
        functools.partial(_rglru_prompt_body, tt=tt, conv_w=conv_w),
        grid=(batch, nt),
        in_specs=[
            pl.BlockSpec((tt, d_rnn), tile),
            pl.BlockSpec((tt, d_rnn), tile),
            _const_spec(cw.shape), _const_spec(cb.shape), _const_spec(wg.shape),
            _const_spec(ba.shape), _const_spec(bx.shape), _const_spec(lam.shape), _const_spec(gn.shape),
        ],
        out_specs=(
            pl.BlockSpec((tt, d_rnn), tile),
            pl.BlockSpec((1, SUBLANES, d_rnn), lambda b, t: (b, 0, 0)),
        ),
        out_shape=(
            jax.ShapeDtypeStruct((batch * seq, d_rnn), BF16),
            jax.ShapeDtypeStruct((batch, SUBLANES, d_rnn), F32),
        ),
        scratch_shapes=[
            pltpu.VMEM((tt + SUBLANES, d_rnn), F32),
            pltpu.VMEM((tt, d_rnn), F32),
            pltpu.VMEM((tt, d_rnn), F32),
            pltpu.VMEM((tt, d_rnn), F32),
            pltpu.VMEM((SUBLANES, d_rnn), F32),
        ],
        compiler_params=pltpu.CompilerParams(
            dimension_semantics=("arbitrary", "arbitrary"), vmem_limit_bytes=VMEM_LIMIT_BYTES),
        name="rglru_prompt",
    )(xr, gr, cw, cb, wg, ba, bx, lam, gn)
    return out, hlast[:, 0, :]


def _rglru_sample_body(xr_ref, gr_ref, c0_ref, c1_ref, c2_ref, h0_ref, cw_ref, cb_ref, wg_ref,
                       ba_ref, bx_ref, lam_ref, gn_ref, out_ref, hnew_ref, *, nb, steps):
    x = xr_ref[...]
    hist = [c0_ref[...], c1_ref[...], c2_ref[...]]
    xs = [x[t * nb:(t + 1) * nb, :] for t in range(steps)]
    xp = hist + xs
    conv_w = len(hist) + 1
    xc = cw_ref[conv_w - 1:conv_w, :] * x
    for k in range(conv_w - 1):
        shifted = jnp.concatenate(xp[k:k + steps], axis=0)
        xc = xc + cw_ref[k:k + 1, :] * shifted
    xc = xc + cb_ref[...]
    a, u = _rglru_gates(xc, wg_ref, ba_ref[...], bx_ref[...], lam_ref[...])
    h = h0_ref[...]
    hs = []
    for t in range(steps):
        h = a[t * nb:(t + 1) * nb, :] * h + u[t * nb:(t + 1) * nb, :]
        hs.append(h)
    hnew_ref[...] = h
    y = jax.nn.gelu(gr_ref[...]) * jnp.concatenate(hs, axis=0)
    out_ref[...] = _rms(y, gn_ref[...]).astype(BF16)


def _rglru_sample(xr, gr, conv_state2d, h0, cw, cb, wg, ba, bx, lam, gn, nb, steps):
    d_rnn = xr.shape[-1]
    n = nb * steps
    assert cw.shape[0] == 4
    full = lambda shape: pl.BlockSpec(shape, lambda i: (0,) * len(shape))
    return pl.pallas_call(
        functools.partial(_rglru_sample_body, nb=nb, steps=steps),
        grid=(1,),
        in_specs=[
            full((n, d_rnn)), full((n, d_rnn)),
            pl.BlockSpec((nb, d_rnn), lambda i: (0, 0)),
            pl.BlockSpec((nb, d_rnn), lambda i: (1, 0)),
            pl.BlockSpec((nb, d_rnn), lambda i: (2, 0)),
            full((nb, d_rnn)),
            full(cw.shape), full(cb.shape), full(wg.shape), full(ba.shape), full(bx.shape),
            full(lam.shape), full(gn.shape),
        ],
        out_specs=(full((n, d_rnn)), full((nb, d_rnn))),
        out_shape=(jax.ShapeDtypeStruct((n, d_rnn), BF16), jax.ShapeDtypeStruct((nb, d_rnn), F32)),
        compiler_params=pltpu.CompilerParams(
            dimension_semantics=("arbitrary",), vmem_limit_bytes=VMEM_LIMIT_BYTES),
        name="rglru_sample",
    )(xr, gr, conv_state2d, conv_state2d, conv_state2d, h0, cw, cb, wg, ba, bx, lam, gn)


def _swa_prompt_body(code_ref, tbl_ref, sink_ref, q_ref, kp_ref, kc_ref, vp_ref, vc_ref, gn_ref,
                     out_ref, bias_ref):
    j = pl.program_id(1)
    blk = WINDOW
    rows = GQA_GROUP * blk

    @pl.when((pl.program_id(0) == 0) & (j == 0))
    def _():
        code = code_ref[...]
        qi = lax.broadcasted_iota(jnp.int32, code.shape, 0)
        kj = lax.broadcasted_iota(jnp.int32, code.shape, 1)
        dist = blk + qi - kj
        in_window = (dist & -WINDOW) == 0
        for h in range(N_HEADS):
            def pick(b, acc, h=h):
                return jnp.where(code == b, tbl_ref[b * N_HEADS + h], acc)
            bias_h = lax.fori_loop(0, N_BUCKETS, pick, jnp.zeros(code.shape, F32))
            bias_h = jnp.where(in_window, bias_h, NEG_INF)
            kv, g = divmod(h, GQA_GROUP)
            bias_ref[1, kv, g * blk:(g + 1) * blk, :] = bias_h
            bias_ref[0, kv, g * blk:(g + 1) * blk, :] = jnp.where(kj >= blk, bias_h, NEG_INF)

    lane = lax.broadcasted_iota(jnp.int32, (2 * blk, LANES), 1)
    row_g = lax.broadcasted_iota(jnp.int32, (rows, 1), 0) // blk
    for sub in range(q_ref.shape[0] // blk):
        rows_q = slice(sub * blk, (sub + 1) * blk)
        if sub == 0:
            has_prev = jnp.where(j > 0, 1, 0)
            k_prev, v_prev = kp_ref[...], vp_ref[...]
        else:
            has_prev = 1
            k_prev, v_prev = kc_ref[(sub - 1) * blk:sub * blk, :], vc_ref[(sub - 1) * blk:sub * blk, :]
        kband = jnp.concatenate([k_prev, kc_ref[rows_q, :]], axis=0)
        vband = jnp.concatenate([v_prev, vc_ref[rows_q, :]], axis=0)
        y = _swa_block(q_ref[rows_q, :], kband, vband, bias_ref, has_prev, sink_ref, lane, row_g)
        out_ref[rows_q, :] = _rms(y, gn_ref[...]).astype(BF16)


def _swa_block(q, kband, vband, bias_ref, has_prev, sink_ref, lane, row_g):
    blk = q.shape[0]
    rows = GQA_GROUP * blk
    outs = []
    for cg in range(KV_HEADS // 2):
        qs = jnp.concatenate(
            [q[:, (cg * GQA_GROUP + g) * LANES:(cg * GQA_GROUP + g + 1) * LANES] for g in range(GQA_GROUP)],
            axis=0)
        ka = kband[:, cg * LANES:(cg + 1) * LANES]
        va = vband[:, cg * LANES:(cg + 1) * LANES]
        o = None
        for par in range(2):
            kv = 2 * cg + par
            half = (lane < HEAD_DIM) if par == 0 else (lane >= HEAD_DIM)
            km = jnp.where(half, ka, 0.0).astype(BF16)
            vm = jnp.where(half, va, 0.0).astype(BF16)
            s = lax.dot_general(qs, km, (((1,), (1,)), ((), ())), preferred_element_type=F32)
            s = s + bias_ref[has_prev, kv]
            sink = jnp.zeros((rows, 1), F32)
            for g in range(GQA_GROUP):
                sink = jnp.where(row_g == g, sink_ref[kv * GQA_GROUP + g], sink)
            m = jnp.maximum(jnp.max(s, axis=-1, keepdims=True), sink)
            p = jnp.exp(s - m)
            denom = jnp.sum(p, axis=-1, keepdims=True) + jnp.exp(sink - m)
            part = jnp.dot(p.astype(BF16), vm, preferred_element_type=F32) * (1.0 / denom)
            o = part if o is None else o + part
        outs.extend(o[g * blk:(g + 1) * blk, :] for g in range(GQA_GROUP))
    return jnp.concatenate(outs, axis=1)


def _swa_prompt(q, k, v, code, tbl, sinks, gn, batch, seq):
    blk = WINDOW
    per_step = next(n for n in (4, 2, 1) if seq % (n * blk) == 0)
    nb = seq // (per_step * blk)
    d_attn = q.shape[-1]
    kv_dim = k.shape[-1]
    cur = lambda b, j: (b * nb + j, 0)
    prev = lambda b, j: (per_step * (b * nb + j) - jnp.where(j > 0, 1, 0), 0)
    return pl.pallas_call(
        _swa_prompt_body,
        grid=(batch, nb),
        in_specs=[
            _const_spec(code.shape), _smem_spec(), _smem_spec(),
            pl.BlockSpec((per_step * blk, d_attn), cur),
            pl.BlockSpec((blk, kv_dim), prev), pl.BlockSpec((per_step * blk, kv_dim), cur),
            pl.BlockSpec((blk, kv_dim), prev), pl.BlockSpec((per_step * blk, kv_dim), cur),
            _const_spec(gn.shape),
        ],
        out_specs=pl.BlockSpec((per_step * blk, d_attn), cur),
        out_shape=jax.ShapeDtypeStruct((batch * seq, d_attn), BF16),
        scratch_shapes=[pltpu.VMEM((2, KV_HEADS, GQA_GROUP * blk, 2 * blk), F32)],
        compiler_params=pltpu.CompilerParams(
            dimension_semantics=("arbitrary", "arbitrary"), vmem_limit_bytes=VMEM_LIMIT_BYTES),
        name="swa_prompt",
    )(code, tbl, sinks, q, k, k, v, v, gn)


def _slide_window(win_ref, new_ref, out_ref, steps):
    bb, kv_dim, win = win_ref.shape
    npad = new_ref.shape[1]
    new_t = new_ref[...].reshape(bb * npad, kv_dim).T
    lane = lax.broadcasted_iota(jnp.int32, (kv_dim, win), 1)
    for b in range(bb):
        placed = pltpu.roll(new_t, (win - steps - b * npad) % win, 1)
        shifted = pltpu.roll(win_ref[b], win - steps, 1)
        out_ref[b] = jnp.where(lane >= win - steps, placed, shifted)


def _swa_sample_body(codec_ref, coden_ref, tbl_ref, sink_ref, q_ref, ck_ref, cv_ref, kn_ref, vn_ref,
                     out_ref, ck_out_ref, cv_out_ref, biasc_ref, biasn_ref, sinkc_ref, *, steps):
    bb = q_ref.shape[0]
    nrow = N_HEADS * steps
    win = ck_ref.shape[2]
    npad = kn_ref.shape[1]

    @pl.when(pl.program_id(0) == 0)
    def _():
        codec = codec_ref[...]
        coden = coden_ref[...]
        hrow = lax.broadcasted_iota(jnp.int32, (nrow, 1), 0) // steps

        def pick(idx, accs):
            ac, an = accs
            val = tbl_ref[idx]
            return jnp.where(codec == idx, val, ac), jnp.where(coden == idx, val, an)
        bc, bn = lax.fori_loop(0, N_BUCKETS * N_HEADS, pick,
                               (jnp.zeros(codec.shape, F32), jnp.zeros(coden.shape, F32)))
        biasc_ref[...] = bc
        biasn_ref[...] = bn

        def pick_sink(h, acc):
            return jnp.where(hrow == h, sink_ref[h], acc)
        sinkc_ref[...] = lax.fori_loop(0, N_HEADS, pick_sink, jnp.zeros((nrow, 1), F32))

    q = q_ref[...].astype(F32)
    lane = lax.broadcasted_iota(jnp.int32, (bb, GQA_GROUP * steps, LANES), 2)
    zeros = jnp.zeros((bb, GQA_GROUP * steps, LANES), F32)
    pieces = []
    for cg in range(KV_HEADS // 2):
        for par in range(2):
            half = (lane < HEAD_DIM) if par == 0 else (lane >= HEAD_DIM)
            qm = jnp.where(half, q[:, cg], zeros)
            pieces.append(jnp.concatenate([qm, zeros] if cg == 0 else [zeros, qm], axis=2))
    qm = jnp.concatenate(pieces, axis=1).astype(BF16)

    ck = ck_ref[...].astype(BF16)
    kn = kn_ref[...].astype(BF16)
    s_c = jnp.einsum("bqd,bdk->bqk", qm, ck, preferred_element_type=F32)
    s_n = jnp.einsum("bqd,bkd->bqk", qm, kn, preferred_element_type=F32)

    t_c = lax.broadcasted_iota(jnp.int32, (nrow, win), 0) % steps
    k_c = lax.broadcasted_iota(jnp.int32, (nrow, win), 1)
    valid_c = k_c > t_c
    t_n = lax.broadcasted_iota(jnp.int32, (nrow, npad), 0) % steps
    k_n = lax.broadcasted_iota(jnp.int32, (nrow, npad), 1)
    valid_n = k_n <= t_n

    s_c = jnp.where(valid_c[None], s_c + biasc_ref[...][None], NEG_INF)
    s_n = jnp.where(valid_n[None], s_n + biasn_ref[...][None], NEG_INF)
    sink = sinkc_ref[...][None]
    m = jnp.maximum(jnp.maximum(jnp.max(s_c, axis=-1, keepdims=True),
                                jnp.max(s_n, axis=-1, keepdims=True)), sink)
    p_c = jnp.exp(s_c - m)
    p_n = jnp.exp(s_n - m)
    denom = (jnp.sum(p_c, axis=-1, keepdims=True) + jnp.sum(p_n, axis=-1, keepdims=True)
             + jnp.exp(sink - m))
    r = 1.0 / denom
    w_c = (p_c * r).astype(BF16)
    w_n = (p_n * r).astype(BF16)
    o = (jnp.einsum("bqk,bdk->bqd", w_c, cv_ref[...].astype(BF16), preferred_element_type=F32)
         + jnp.einsum("bqk,bkd->bqd", w_n, vn_ref[...].astype(BF16), preferred_element_type=F32))
    gt = GQA_GROUP * steps
    lane_o = lax.broadcasted_iota(jnp.int32, (bb, gt, LANES), 2)
    for cg in range(KV_HEADS // 2):
        lo = o[:, cg * 2 * gt:cg * 2 * gt + gt, cg * LANES:(cg + 1) * LANES]
        hi = o[:, cg * 2 * gt + gt:(cg + 1) * 2 * gt, cg * LANES:(cg + 1) * LANES]
        out_ref[:, cg] = jnp.where(lane_o < HEAD_DIM, lo, hi)

    _slide_window(ck_ref, kn_ref, ck_out_ref, steps)
    _slide_window(cv_ref, vn_ref, cv_out_ref, steps)


def _swa_sample(qs, ck, cv, kn, vn, codec, coden, tbl, sinks, steps, bb):
    nbatch, ncg, gt, _ = qs.shape
    kv_dim, win = ck.shape[1], ck.shape[2]
    npad = kn.shape[1]
    nrow = N_HEADS * steps
    assert bb * npad == win
    blk4 = lambda i: (i, 0, 0, 0)
    blk3 = lambda i: (i, 0, 0)
    return pl.pallas_call(
        functools.partial(_swa_sample_body, steps=steps),
        grid=(nbatch // bb,),
        in_specs=[
            _const_spec(codec.shape), _const_spec(coden.shape), _smem_spec(), _smem_spec(),
            pl.BlockSpec((bb, ncg, gt, LANES), blk4),
            pl.BlockSpec((bb, kv_dim, win), blk3), pl.BlockSpec((bb, kv_dim, win), blk3),
            pl.BlockSpec((bb, npad, kv_dim), blk3), pl.BlockSpec((bb, npad, kv_dim), blk3),
        ],
        out_specs=(pl.BlockSpec((bb, ncg, gt, LANES), blk4),
                   pl.BlockSpec((bb, kv_dim, win), blk3), pl.BlockSpec((bb, kv_dim, win), blk3)),
        out_shape=(jax.ShapeDtypeStruct((nbatch, ncg, gt, LANES), F32),
                   jax.ShapeDtypeStruct((nbatch, kv_dim, win), F32),
                   jax.ShapeDtypeStruct((nbatch, kv_dim, win), F32)),
        scratch_shapes=[pltpu.VMEM((nrow, win), F32), pltpu.VMEM((nrow, npad), F32),
                        pltpu.VMEM((nrow, 1), F32)],
        compiler_params=pltpu.CompilerParams(
            dimension_semantics=("arbitrary",), vmem_limit_bytes=VMEM_LIMIT_BYTES),
        name="swa_sample",
    )(codec, coden, tbl, sinks, qs, ck, cv, kn, vn)


def _shift_rows(up, hist, k):
    rolled = pltpu.roll(up, k, 0)
    row = lax.broadcasted_iota(jnp.int32, hist.shape, 0)
    head = jnp.where(row < k, pltpu.roll(hist, k, 0), rolled[0:SUBLANES, :])
    return jnp.concatenate([head, rolled[SUBLANES:, :]], axis=0)


def _ffn_conv(up, hist, cw, cb, *, tm, sample_nb):
    if sample_nb is None:
        prev2 = _shift_rows(up, hist, 2)
        prev1 = _shift_rows(up, hist, 1)
    else:
        s0, s1 = hist
        prev2 = jnp.concatenate([s0, s1, up[:tm - 2 * sample_nb, :]], axis=0)
        prev1 = jnp.concatenate([s1, up[:tm - sample_nb, :]], axis=0)
    return cw[0:1, :] * prev2 + cw[1:2, :] * prev1 + cw[2:3, :] * up + cb


FFN_CHUNK = 256
FFN_SLAB = 256


def _ffn_prompt_body(x_hbm, mr_ref, ma_ref, wor_ref, woa_ref, gf_ref, wg_ref, wv_ref, cwg_ref, cwv_ref, cbg_ref,
                     cbv_ref, wd_ref, gfin_ref, y_ref, tg_ref, tv_ref, hn_ref, car_g, car_v, x_sem,
                     *, tm, tiles_per_seq):
    i = pl.program_id(0)
    j = pl.program_id(1)
    nj = pl.num_programs(1)
    n_slab = tm // FFN_SLAB
    slab = lambda r: slice(r * FFN_SLAB, (r + 1) * FFN_SLAB)

    @pl.when(j == 0)
    def _():
        x_copy = pltpu.make_async_copy(x_hbm.at[pl.ds(pl.multiple_of(i * tm, tm), tm), :], y_ref, x_sem)
        x_copy.start()

        def out_proj(r):
            return (jnp.dot(mr_ref[slab(r), :], wor_ref[...], preferred_element_type=F32)
                    + jnp.dot(ma_ref[slab(r), :], woa_ref[...], preferred_element_type=F32))

        pending = out_proj(0)
        x_copy.wait()
        for r in range(n_slab):
            d = pending
            if r + 1 < n_slab:
                pending = out_proj(r + 1)
            h = y_ref[slab(r), :] + d
            y_ref[slab(r), :] = h
            hn_ref[slab(r), :] = _rms(h, gf_ref[...]).astype(BF16)

    @pl.when((i == 0) & (j == 0))
    def _():
        car_g[...] = jnp.zeros_like(car_g)
        car_v[...] = jnp.zeros_like(car_v)

    cwg, cwv, cbg, cbv = cwg_ref[...], cwv_ref[...], cbg_ref[...], cbv_ref[...]

    def up_proj(r):
        hn = hn_ref[slab(r), :]
        return (jnp.dot(hn, wg_ref[...], preferred_element_type=F32),
                jnp.dot(hn, wv_ref[...], preferred_element_type=F32))

    seq_start = i % tiles_per_seq == 0
    hist_g = jnp.where(seq_start, 0.0, car_g[j])
    hist_v = jnp.where(seq_start, 0.0, car_v[j])
    pending = up_proj(0)
    act = None
    for r in range(n_slab):
        up_g, up_v = pending
        if r + 1 < n_slab:
            pending = up_proj(r + 1)
        if act is not None:
            y_ref[slab(r - 1), :] += jnp.dot(act, wd_ref[...], preferred_element_type=F32)
        c_g = _ffn_conv(up_g, hist_g, cwg, cbg, tm=FFN_SLAB, sample_nb=None)
        c_v = _ffn_conv(up_v, hist_v, cwv, cbv, tm=FFN_SLAB, sample_nb=None)
        hist_g = up_g[FFN_SLAB - SUBLANES:, :]
        hist_v = up_v[FFN_SLAB - SUBLANES:, :]
        act = (jax.nn.gelu(c_g) * c_v).astype(BF16)
    y_ref[slab(n_slab - 1), :] += jnp.dot(act, wd_ref[...], preferred_element_type=F32)
    car_g[j] = hist_g
    car_v[j] = hist_v
    tg_ref[0] = hist_g
    tv_ref[0] = hist_v

    @pl.when(j == nj - 1)
    def _():
        for r in range(n_slab):
            y_ref[slab(r), :] = _rms(y_ref[slab(r), :], gfin_ref[...])


def _ffn_sample_body(x_ref, mr_ref, ma_ref, gna_ref, wor_ref, woa_ref, gf_ref, wg_ref, wv_ref, cwg_ref, cwv_ref,
                     cbg_ref, cbv_ref, wd_ref, gfin_ref, s0g_ref, s1g_ref, s0v_ref, s1v_ref,
                     y_ref, tg_ref, tv_ref, hn_ref, *, nb):
    j = pl.program_id(1)
    nj = pl.num_programs(1)
    tm = x_ref.shape[0]

    @pl.when(j == 0)
    def _():
        ma = _rms(ma_ref[...], gna_ref[...]).astype(BF16)
        h = (x_ref[...] + jnp.dot(mr_ref[...], wor_ref[...], preferred_element_type=F32)
             + jnp.dot(ma, woa_ref[...], preferred_element_type=F32))
        y_ref[...] = h
        hn_ref[...] = _rms(h, gf_ref[...]).astype(BF16)

    hn = hn_ref[...]
    tf = wg_ref.shape[1]
    acts = []
    for c in range(tf // FFN_CHUNK):
        cs = slice(c * FFN_CHUNK, (c + 1) * FFN_CHUNK)
        up_g = jnp.dot(hn, wg_ref[:, cs], preferred_element_type=F32)
        up_v = jnp.dot(hn, wv_ref[:, cs], preferred_element_type=F32)
        tg_ref[0, :, cs] = up_g[tm - 2 * nb:tm - nb, :]
        tg_ref[1, :, cs] = up_g[tm - nb:, :]
        tv_ref[0, :, cs] = up_v[tm - 2 * nb:tm - nb, :]
        tv_ref[1, :, cs] = up_v[tm - nb:, :]
        c_g = _ffn_conv(up_g, (s0g_ref[:, cs], s1g_ref[:, cs]), cwg_ref[:, cs], cbg_ref[:, cs],
                        tm=tm, sample_nb=nb)
        c_v = _ffn_conv(up_v, (s0v_ref[:, cs], s1v_ref[:, cs]), cwv_ref[:, cs], cbv_ref[:, cs],
                        tm=tm, sample_nb=nb)
        acts.append((jax.nn.gelu(c_g) * c_v).astype(BF16))
    act = jnp.concatenate(acts, axis=1)
    y_ref[...] += jnp.dot(act, wd_ref[...], preferred_element_type=F32)

    @pl.when(j == nj - 1)
    def _():
        y_ref[...] = _rms(y_ref[...], gfin_ref[...])


def _ffn_prompt(x, mr, ma, wor, woa, gf, wup, cw, cb, wd, gfin, batch, seq, tm, tf):
    n, d_model = x.shape
    d_half = mr.shape[-1]
    d_ff = wd.shape[0]
    ni, nj = n // tm, d_ff // tf
    tiles_per_seq = seq // tm
    row = lambda i, j: (i, 0)
    gate_col = lambda i, j: (0, j)
    val_col = lambda i, j: (0, nj + j)
    tail = lambda i, j: (i, 0, j)
    y, tg, tv = pl.pallas_call(
        functools.partial(_ffn_prompt_body, tm=tm, tiles_per_seq=tiles_per_seq),
        grid=(ni, nj),
        in_specs=[
            pl.BlockSpec(memory_space=pl.ANY),
            pl.BlockSpec((tm, d_half), row), pl.BlockSpec((tm, d_half), row),
            _const_spec(wor.shape), _const_spec(woa.shape), _const_spec(gf.shape),
            pl.BlockSpec((d_model, tf), gate_col), pl.BlockSpec((d_model, tf), val_col),
            pl.BlockSpec((cw.shape[0], tf), gate_col), pl.BlockSpec((cw.shape[0], tf), val_col),
            pl.BlockSpec((1, tf), gate_col), pl.BlockSpec((1, tf), val_col),
            pl.BlockSpec((tf, d_model), lambda i, j: (j, 0)),
            _const_spec(gfin.shape),
        ],
        out_specs=(
            pl.BlockSpec((tm, d_model), row),
            pl.BlockSpec((1, SUBLANES, tf), tail),
            pl.BlockSpec((1, SUBLANES, tf), tail),
        ),
        out_shape=(
            jax.ShapeDtypeStruct((n, d_model), F32),
            jax.ShapeDtypeStruct((ni, SUBLANES, d_ff), F32),
            jax.ShapeDtypeStruct((ni, SUBLANES, d_ff), F32),
        ),
        scratch_shapes=[
            pltpu.VMEM((tm, d_model), BF16),
            pltpu.VMEM((nj, SUBLANES, tf), F32), pltpu.VMEM((nj, SUBLANES, tf), F32),
            pltpu.SemaphoreType.DMA(()),
        ],
        compiler_params=pltpu.CompilerParams(
            dimension_semantics=("arbitrary", "arbitrary"), vmem_limit_bytes=VMEM_LIMIT_BYTES),
        name="ffn_prompt",
    )(x, mr, ma, wor, woa, gf, wup, wup, cw, cw, cb, cb, wd, gfin)
    return y, tg[tiles_per_seq - 1::tiles_per_seq], tv[tiles_per_seq - 1::tiles_per_seq]


def _ffn_sample(x, mr, ma, gna, wor, woa, gf, wup, cw, cb, wd, gfin, state2d, nb, steps, tf):
    n, d_model = x.shape
    d_half = mr.shape[-1]
    d_ff = wd.shape[0]
    nj = d_ff // tf
    full2 = lambda shape: pl.BlockSpec(shape, lambda i, j: (0, 0))
    gate_col = lambda i, j: (0, j)
    val_col = lambda i, j: (0, nj + j)
    y, tg, tv = pl.pallas_call(
        functools.partial(_ffn_sample_body, nb=nb),
        grid=(1, nj),
        in_specs=[
            full2((n, d_model)), full2((n, d_half)), full2((n, d_half)), full2(gna.shape),
            _const_spec(wor.shape), _const_spec(woa.shape), _const_spec(gf.shape),
            pl.BlockSpec((d_model, tf), gate_col), pl.BlockSpec((d_model, tf), val_col),
            pl.BlockSpec((cw.shape[0], tf), gate_col), pl.BlockSpec((cw.shape[0], tf), val_col),
            pl.BlockSpec((1, tf), gate_col), pl.BlockSpec((1, tf), val_col),
            pl.BlockSpec((tf, d_model), lambda i, j: (j, 0)),
            _const_spec(gfin.shape),
            pl.BlockSpec((nb, tf), lambda i, j: (0, j)),
            pl.BlockSpec((nb, tf), lambda i, j: (0, 2 * nj + j)),
            pl.BlockSpec((nb, tf), lambda i, j: (0, nj + j)),
            pl.BlockSpec((nb, tf), lambda i, j: (0, 3 * nj + j)),
        ],
        out_specs=(
            full2((n, d_model)),
            pl.BlockSpec((2, nb, tf), lambda i, j: (0, 0, j)),
            pl.BlockSpec((2, nb, tf), lambda i, j: (0, 0, j)),
        ),
        out_shape=(
            jax.ShapeDtypeStruct((n, d_model), F32),
            jax.ShapeDtypeStruct((2, nb, d_ff), F32),
            jax.ShapeDtypeStruct((2, nb, d_ff), F32),
        ),
        scratch_shapes=[pltpu.VMEM((n, d_model), BF16)],
        compiler_params=pltpu.CompilerParams(
            dimension_semantics=("arbitrary", "arbitrary"), vmem_limit_bytes=VMEM_LIMIT_BYTES),
        name="ffn_sample",
    )(x, mr, ma, gna, wor, woa, gf, wup, wup, cw, cw, cb, cb, wd, gfin, state2d, state2d, state2d, state2d)
    return y, tg, tv


def _perm_heads(a, axis):
    shape = a.shape
    a = a.reshape(shape[:axis] + (KV_HEADS // 2, 2, GQA_GROUP, HEAD_DIM) + shape[axis + 1:])
    return jnp.swapaxes(a, axis + 1, axis + 2).reshape(shape)


def _gate_weights(w_a, w_x):
    nblk, bs, _ = w_a.shape
    per = GATE_TILE // bs
    eye = jnp.eye(per, dtype=w_a.dtype)

    def pack(w):
        w4 = w.reshape(nblk // per, per, bs, bs)
        return jnp.einsum("cgij,gh->cgihj", w4, eye).reshape(nblk // per, GATE_TILE, GATE_TILE)
    return jnp.concatenate([pack(w_a), pack(w_x)], axis=-1).astype(BF16)


def kernel(x_prompt, x_sample, state_rnn_conv, state_rnn_h, cache_win_k, cache_win_v, state_ffn_conv,
           norm_mix_g, w_in, rnn_conv_w, rnn_conv_b, w_gate_a, b_gate_a, w_gate_x, b_gate_x, rnn_lambda,
           attn_sinks, rel_bias_table, gn_rnn_g, gn_attn_g, w_out, norm_ffn_g, w_up, ffn_conv_w,
           ffn_conv_b, w_down, norm_final_g):
    batch, seq, d_model = x_prompt.shape
    nb, steps, _ = x_sample.shape
    depth = w_in.shape[0]
    d_rnn = rnn_conv_w.shape[-1]
    d_attn = N_HEADS * HEAD_DIM
    kv_dim = KV_HEADS * HEAD_DIM
    d_ff = w_down.shape[1]
    win = cache_win_k.shape[2]
    assert depth == 1 and d_rnn + d_attn == d_model and w_in.shape[-1] == 2 * d_rnn + d_attn + 2 * kv_dim
    assert win == WINDOW and seq % WINDOW == 0 and w_gate_a.shape[1] == RNN_BLOCKS
    assert rnn_conv_w.shape[1] == 4 and ffn_conv_w.shape[1] == 3 and steps >= 3 and nb % SUBLANES == 0

    w_in0 = w_in[0]
    o_q = 2 * d_rnn
    w_rg = w_in0[:, :o_q].astype(BF16)
    w_q = _perm_heads(w_in0[:, o_q:o_q + d_attn], 1).astype(BF16)
    w_kv = w_in0[:, o_q + d_attn:].astype(BF16)
    w_out0 = w_out[0]
    w_out_r = w_out0[:d_rnn].astype(BF16)
    w_out_a = _perm_heads(w_out0[d_rnn:], 0).astype(BF16)
    gn_attn_p = _perm_heads(gn_attn_g[0], 0).reshape(1, d_attn)
    tf = 512
    w_up_b = w_up[0].astype(BF16)
    w_down_b = w_down[0].astype(BF16)
    wg = _gate_weights(w_gate_a[0], w_gate_x[0])
    row2 = lambda a: a.reshape(1, -1)
    g_mix, g_ffn, g_fin, g_rnn = row2(norm_mix_g[0]), row2(norm_ffn_g[0]), row2(norm_final_g), row2(gn_rnn_g[0])
    cw_r, cb_r = rnn_conv_w[0], row2(rnn_conv_b[0])
    ba, bx, lam = row2(b_gate_a[0]), row2(b_gate_x[0]), row2(rnn_lambda[0])
    cw_f, cb_f = ffn_conv_w[0], row2(ffn_conv_b[0])
    tbl = rel_bias_table.reshape(-1)
    sinks = attn_sinks[0]

    blk = WINDOW
    qi = np.arange(blk)[:, None]
    kj = np.arange(2 * blk)[None, :]
    code_p = jnp.asarray(_t5_bucket_np(blk + qi - kj))
    r = np.arange(N_HEADS * steps)[:, None]
    t_r, h_r = r % steps, r // steps
    npad = 2 * SUBLANES
    code_c = jnp.asarray(_t5_bucket_np(win + t_r - np.arange(win)[None, :]) * N_HEADS + h_r)
    code_n = jnp.asarray(_t5_bucket_np(t_r - np.arange(npad)[None, :]) * N_HEADS + h_r)

    n_p = batch * seq
    tm_a = 512 if n_p % 512 == 0 else WINDOW
    xp2 = x_prompt.reshape(n_p, d_model)
    xr, gr, q, k, v = _inproj(xp2, pl.BlockSpec((tm_a, d_model), lambda i: (i, 0)), n_p // tm_a, tm_a,
                              g_mix, w_rg, w_q, w_kv)
    tt = 512 if seq % 512 == 0 else WINDOW
    m_rnn, h_last = _rglru_prompt(xr, gr, cw_r, cb_r, wg, ba, bx, lam, g_rnn, batch, seq, tt)
    m_attn = _swa_prompt(q, k, v, code_p, tbl, sinks, gn_attn_p, batch, seq)
    tm_f = next(t for t in (1024, 512, 256) if seq % t == 0)
    y_p, tail_g, tail_v = _ffn_prompt(xp2, m_rnn, m_attn, w_out_r, w_out_a, g_ffn, w_up_b, cw_f, cb_f,
                                      w_down_b, g_fin, batch, seq, tm_f, tf)
    y_prompt = y_p.reshape(batch, seq, d_model)
    p_rnn_conv = xr.reshape(batch, seq, d_rnn)[:, seq - 3:, :][None]
    p_rnn_h = h_last[None]
    last_win = lambda a: a.reshape(batch, seq, kv_dim)[:, seq - win:, :].reshape(
        1, batch, win, KV_HEADS, HEAD_DIM)
    p_win_k, p_win_v = last_win(k), last_win(v)
    p_ffn_conv = jnp.concatenate([tail_g[:, SUBLANES - 2:, :], tail_v[:, SUBLANES - 2:, :]], axis=-1)[None]

    n_s = nb * steps
    xs_tm = x_sample.transpose(1, 0, 2).reshape(n_s, d_model)
    xr_s, gr_s, q_s, k_s, v_s = _inproj(xs_tm, pl.BlockSpec((nb, d_model), lambda t: (t, 0)), steps, nb,
                                        g_mix, w_rg, w_q, w_kv)
    conv_tm = state_rnn_conv[0].transpose(1, 0, 2).reshape(3 * nb, d_rnn)
    m_rnn_s, h_new = _rglru_sample(xr_s, gr_s, conv_tm, state_rnn_h[0], cw_r, cb_r, wg, ba, bx, lam, g_rnn,
                                   nb, steps)
    ncg = KV_HEADS // 2
    qs = q_s.reshape(steps, nb, ncg, GQA_GROUP, LANES).transpose(1, 2, 3, 0, 4).reshape(
        nb, ncg, GQA_GROUP * steps, LANES)
    k_tm = k_s.reshape(steps, nb, kv_dim)
    v_tm = v_s.reshape(steps, nb, kv_dim)
    pad = ((0, 0), (0, npad - steps), (0, 0))
    ck_t = cache_win_k[0].transpose(0, 2, 3, 1).reshape(nb, kv_dim, win)
    cv_t = cache_win_v[0].transpose(0, 2, 3, 1).reshape(nb, kv_dim, win)
    bb = SUBLANES
    o_s, ck_new, cv_new = _swa_sample(qs, ck_t, cv_t, jnp.pad(k_tm.transpose(1, 0, 2), pad),
                                      jnp.pad(v_tm.transpose(1, 0, 2), pad), code_c, code_n, tbl, sinks, steps, bb)
    ya_s = o_s.reshape(nb, ncg, GQA_GROUP, steps, LANES).transpose(3, 0, 1, 2, 4).reshape(n_s, d_attn)
    ffn2d = state_ffn_conv[0].reshape(nb, 2 * 2 * d_ff)
    y_s, ns_g, ns_v = _ffn_sample(xs_tm, m_rnn_s, ya_s, gn_attn_p, w_out_r, w_out_a, g_ffn, w_up_b, cw_f, cb_f,
                                  w_down_b, g_fin, ffn2d, nb, steps, tf)
    y_sample = y_s.reshape(steps, nb, d_model).transpose(1, 0, 2)
    s_rnn_conv = xr_s.reshape(steps, nb, d_rnn)[steps - 3:].transpose(1, 0, 2)[None]
    s_rnn_h = h_new[None]

    to_cache = lambda w_t: w_t.reshape(nb, KV_HEADS, HEAD_DIM, win).transpose(0, 3, 1, 2)[None]
    s_win_k, s_win_v = to_cache(ck_new), to_cache(cv_new)
    s_ffn_conv = jnp.concatenate([ns_g, ns_v], axis=-1).transpose(1, 0, 2)[None]

    return (y_prompt, y_sample, p_rnn_conv, p_rnn_h, p_win_k, p_win_v, p_ffn_conv,
            s_rnn_conv, s_rnn_h, s_win_k, s_win_v, s_ffn_conv)
```

```python
import functools
import math

import numpy as np
import jax
import jax.numpy as jnp
from jax import lax
from jax.experimental import pallas as pl
from jax.experimental.pallas import tpu as pltpu

F32 = jnp.float32
BF16 = jnp.bfloat16

HEAD_DIM = 64
KV_HEADS = 4
N_HEADS = 16
GQA_GROUP = N_HEADS // KV_HEADS
RNN_BLOCKS = 16
RG_C = 8.0
WINDOW = 128
N_BUCKETS = 32
MAX_EXACT = N_BUCKETS // 2
REL_MAX_DIST = 128
EPS = 1e-6
NEG_INF = -1e30
ATTN_SCALE = HEAD_DIM ** -0.5

LANES = 128
SUBLANES = 8
VMEM_LIMIT_BYTES = 56 * 1024 * 1024
FFN_VMEM_LIMIT_BYTES = 61 * 1024 * 1024

GATE_TILE = 256


def _t5_bucket_np(d):
    n = np.maximum(d, 0)
    nf = np.maximum(n, 1).astype(np.float32)
    large = MAX_EXACT + (np.log(nf / MAX_EXACT) / math.log(REL_MAX_DIST / MAX_EXACT)
                         * (N_BUCKETS - MAX_EXACT)).astype(np.int32)
    large = np.minimum(large, N_BUCKETS - 1)
    return np.where(n < MAX_EXACT, n, large).astype(np.int32)


def _rms(x, g):
    ms = jnp.mean(x * x, axis=-1, keepdims=True)
    return (x * lax.rsqrt(ms + EPS)) * g


def _softplus(x):
    return jnp.maximum(x, 0.0) + jnp.log1p(jnp.exp(-jnp.abs(x)))


def _const_spec(shape):
    nd = len(shape)
    return pl.BlockSpec(shape, lambda *_: (0,) * nd, pipeline_mode=pl.Buffered(1))


def _smem_spec():
    return pl.BlockSpec(memory_space=pltpu.SMEM)


def _inproj_body(x_ref, g_ref, wrg_ref, wq_ref, wkv_ref, xr_ref, gr_ref, q_ref, k_ref, v_ref):
    xn = _rms(x_ref[...], g_ref[...]).astype(BF16)
    d_rnn = xr_ref.shape[-1]
    kv_dim = k_ref.shape[-1]
    rg = jnp.dot(xn, wrg_ref[...], preferred_element_type=F32)
    xr_ref[...] = rg[:, :d_rnn]
    gr_ref[...] = rg[:, d_rnn:]
    q_ref[...] = (jnp.dot(xn, wq_ref[...], preferred_element_type=F32) * ATTN_SCALE).astype(BF16)
    kv = jnp.dot(xn, wkv_ref[...], preferred_element_type=F32)
    k_ref[...] = kv[:, :kv_dim]
    v_ref[...] = kv[:, kv_dim:]


def _inproj(x2d, x_spec, n_steps, tm, g, w_rg, w_q, w_kv):
    n = n_steps * tm
    d_model = g.shape[-1]
    d_rnn, d_attn, kv_dim = w_rg.shape[1] // 2, w_q.shape[1], w_kv.shape[1] // 2
    row = lambda i: (i, 0)
    out_shape = (
        jax.ShapeDtypeStruct((n, d_rnn), F32),
        jax.ShapeDtypeStruct((n, d_rnn), F32),
        jax.ShapeDtypeStruct((n, d_attn), BF16),
        jax.ShapeDtypeStruct((n, kv_dim), F32),
        jax.ShapeDtypeStruct((n, kv_dim), F32),
    )
    out_specs = (
        pl.BlockSpec((tm, d_rnn), row),
        pl.BlockSpec((tm, d_rnn), row),
        pl.BlockSpec((tm, d_attn), row),
        pl.BlockSpec((tm, kv_dim), row),
        pl.BlockSpec((tm, kv_dim), row),
    )
    return pl.pallas_call(
        _inproj_body,
        grid=(n_steps,),
        in_specs=[x_spec, _const_spec((1, d_model)), _const_spec(w_rg.shape), _const_spec(w_q.shape),
                  _const_spec(w_kv.shape)],
        out_specs=out_specs,
        out_shape=out_shape,
        compiler_params=pltpu.CompilerParams(
            dimension_semantics=("arbitrary",), vmem_limit_bytes=VMEM_LIMIT_BYTES),
        name="inproj",
    )(x2d, g, w_rg, w_q, w_kv)


def _rglru_gates(xc, wg_ref, ba, bx, lam):
    d_rnn = xc.shape[-1]
    pre_a, pre_x = [], []
    for c in range(d_rnn // GATE_TILE):
        xb = xc[:, c * GATE_TILE:(c + 1) * GATE_TILE].astype(BF16)
        pre = jnp.dot(xb, wg_ref[c], preferred_element_type=F32)
        pre_a.append(pre[:, :GATE_TILE])
        pre_x.append(pre[:, GATE_TILE:])
    r = jax.nn.sigmoid(jnp.concatenate(pre_a, axis=1) + ba)
    i = jax.nn.sigmoid(jnp.concatenate(pre_x, axis=1) + bx)
    log_a = (-RG_C * r) * _softplus(-lam)
    a = jnp.exp(log_a)
    one_minus_a2 = -jnp.tanh(log_a) * (a * a + 1.0)
    u = jnp.sqrt(one_minus_a2) * (i * xc)
    return a, u


def _rglru_prompt_body(xr_ref, gr_ref, cw_ref, cb_ref, wg_ref, ba_ref, bx_ref, lam_ref, gn_ref,
                       out_ref, hlast_ref, xbuf, abuf, ubuf, hbuf, hc_ref, *, tt, conv_w):
    ti = pl.program_id(1)
    d_rnn = xr_ref.shape[-1]

    @pl.when(ti == 0)
    def _():
        xbuf[0:SUBLANES, :] = jnp.zeros((SUBLANES, d_rnn), F32)
        hc_ref[...] = jnp.zeros_like(hc_ref)

    x = xr_ref[...]
    xbuf[SUBLANES:SUBLANES + tt, :] = x
    xc = cw_ref[conv_w - 1:conv_w, :] * x
    for k in range(conv_w - 1):
        off = SUBLANES - (conv_w - 1) + k
        xc = xc + cw_ref[k:k + 1, :] * xbuf[off:off + tt, :]
    xc = xc + cb_ref[...]
    xbuf[0:SUBLANES, :] = xbuf[tt:tt + SUBLANES, :]

    a, u = _rglru_gates(xc, wg_ref, ba_ref[...], bx_ref[...], lam_ref[...])
    abuf[...] = a
    ubuf[...] = u

    row = lax.broadcasted_iota(jnp.int32, (SUBLANES, d_rnn), 0)

    def group(gidx, hc):
        r0 = pl.multiple_of(gidx * SUBLANES, SUBLANES)
        ag = abuf[pl.ds(r0, SUBLANES), :]
        ug = ubuf[pl.ds(r0, SUBLANES), :]
        for k in (1, 2, 4):
            a_prev = jnp.where(row >= k, pltpu.roll(ag, k, 0), 1.0)
            u_prev = jnp.where(row >= k, pltpu.roll(ug, k, 0), 0.0)
            ug = ag * u_prev + ug
            ag = ag * a_prev
        h = ag * hc + ug
        hbuf[pl.ds(r0, SUBLANES), :] = h
        return jnp.broadcast_to(h[SUBLANES - 1:SUBLANES, :], (SUBLANES, d_rnn))

    hc = lax.fori_loop(0, tt // SUBLANES, group, hc_ref[...], unroll=2)
    hc_ref[...] = hc
    hlast_ref[0] = hc

    y = jax.nn.gelu(gr_ref[...]) * hbuf[...]
    out_ref[...] = _rms(y, gn_ref[...]).astype(BF16)


def _rglru_prompt(xr, gr, cw, cb, wg, ba, bx, lam, gn, batch, seq, tt):
    d_rnn = xr.shape[-1]
    nt = seq // tt
    conv_w = cw.shape[0]
    tile = lambda b, t: (b * nt + t, 0)
    out, hlast = pl.pallas_call(
        functools.partial(_rglru_prompt_body, tt=tt, conv_w=conv_w),
        grid=(batch, nt),
        in_specs=[
            pl.BlockSpec((tt, d_rnn), tile),
            pl.BlockSpec((tt, d_rnn), tile),
            _const_spec(cw.shape), _const_spec(cb.shape), _const_spec(wg.shape),
            _const_spec(ba.shape), _const_spec(bx.shape), _const_spec(lam.shape), _const_spec(gn.shape),
        ],
        out_specs=(
            pl.BlockSpec((tt, d_rnn), tile),
            pl.BlockSpec((1, SUBLANES, d_rnn), lambda b, t: (b, 0, 0)),
        ),
        out_shape=(
            jax.ShapeDtypeStruct((batch * seq, d_rnn), BF16),
            jax.ShapeDtypeStruct((batch, SUBLANES, d_rnn), F32),
        ),
        scratch_shapes=[
            pltpu.VMEM((tt + SUBLANES, d_rnn), F32),
            pltpu.VMEM((tt, d_rnn), F32),
            pltpu.VMEM((tt, d_rnn), F32),
            pltpu.VMEM((tt, d_rnn), F32),
            pltpu.VMEM((SUBLANES, d_rnn), F32),
        ],
        compiler_params=pltpu.CompilerParams(
            dimension_semantics=("arbitrary", "arbitrary"), vmem_limit_bytes=VMEM_LIMIT_BYTES),
        name="rglru_prompt",
    )(xr, gr, cw, cb, wg, ba, bx, lam, gn)
    return out, hlast[:, 0, :]


def _rglru_sample_body(xr_ref, gr_ref, c0_ref, c1_ref, c2_ref, h0_ref, cw_ref, cb_ref, wg_ref,
                       ba_ref, bx_ref, lam_ref, gn_ref, out_ref, hnew_ref, *, nb, steps):
    x = xr_ref[...]
    hist = [c0_ref[...], c1_ref[...], c2_ref[...]]
    xs = [x[t * nb:(t + 1) * nb, :] for t in range(steps)]
    xp = hist + xs
    conv_w = len(hist) + 1
    xc = cw_ref[conv_w - 1:conv_w, :] * x
    for k in range(conv_w - 1):
        shifted = jnp.concatenate(xp[k:k + steps], axis=0)
        xc = xc + cw_ref[k:k + 1, :] * shifted
    xc = xc + cb_ref[...]
    a, u = _rglru_gates(xc, wg_ref, ba_ref[...], bx_ref[...], lam_ref[...])
    h = h0_ref[...]
    hs = []
    for t in range(steps):
        h = a[t * nb:(t + 1) * nb, :] * h + u[t * nb:(t + 1) * nb, :]
        hs.append(h)
    hnew_ref[...] = h
    y = jax.nn.gelu(gr_ref[...]) * jnp.concatenate(hs, axis=0)
    out_ref[...] = _rms(y, gn_ref[...]).astype(BF16)


def _rglru_sample(xr, gr, conv_state2d, h0, cw, cb, wg, ba, bx, lam, gn, nb, steps):
    d_rnn = xr.shape[-1]
    n = nb * steps
    assert cw.shape[0] == 4
    full = lambda shape: pl.BlockSpec(shape, lambda i: (0,) * len(shape))
    return pl.pallas_call(
        functools.partial(_rglru_sample_body, nb=nb, steps=steps),
        grid=(1,),
        in_specs=[
            full((n, d_rnn)), full((n, d_rnn)),
            pl.BlockSpec((nb, d_rnn), lambda i: (0, 0)),
            pl.BlockSpec((nb, d_rnn), lambda i: (1, 0)),
            pl.BlockSpec((nb, d_rnn), lambda i: (2, 0)),
            full((nb, d_rnn)),
            full(cw.shape), full(cb.shape), full(wg.shape), full(ba.shape), full(bx.shape),
            full(lam.shape), full(gn.shape),
        ],
        out_specs=(full((n, d_rnn)), full((nb, d_rnn))),
        out_shape=(jax.ShapeDtypeStruct((n, d_rnn), BF16), jax.ShapeDtypeStruct((nb, d_rnn), F32)),
        compiler_params=pltpu.CompilerParams(
            dimension_semantics=("arbitrary",), vmem_limit_bytes=VMEM_LIMIT_BYTES),
        name="rglru_sample",
    )(xr, gr, conv_state2d, conv_state2d, conv_state2d, h0, cw, cb, wg, ba, bx, lam, gn)


def _swa_prompt_body(code_ref, tbl_ref, sink_ref, q_ref, kp_ref, kc_ref, vp_ref, vc_ref, gn_ref,
                     out_ref, bias_ref):
    j = pl.program_id(1)
    blk = WINDOW
    rows = GQA_GROUP * blk

    @pl.when((pl.program_id(0) == 0) & (j == 0))
    def _():
        code = code_ref[...]
        qi = lax.broadcasted_iota(jnp.int32, code.shape, 0)
        kj = lax.broadcasted_iota(jnp.int32, code.shape, 1)
        dist = blk + qi - kj
        in_window = (dist & -WINDOW) == 0
        for h in range(N_HEADS):
            def pick(b, acc, h=h):
                return jnp.where(code == b, tbl_ref[b * N_HEADS + h], acc)
            bias_h = lax.fori_loop(0, N_BUCKETS, pick, jnp.zeros(code.shape, F32))
            bias_h = jnp.where(in_window, bias_h, NEG_INF)
            kv, g = divmod(h, GQA_GROUP)
            bias_ref[1, kv, g * blk:(g + 1) * blk, :] = bias_h
            bias_ref[0, kv, g * blk:(g + 1) * blk, :] = jnp.where(kj >= blk, bias_h, NEG_INF)

    lane = lax.broadcasted_iota(jnp.int32, (2 * blk, LANES), 1)
    row_g = lax.broadcasted_iota(jnp.int32, (rows, 1), 0) // blk
    for sub in range(q_ref.shape[0] // blk):
        rows_q = slice(sub * blk, (sub + 1) * blk)
        if sub == 0:
            has_prev = jnp.where(j > 0, 1, 0)
            k_prev, v_prev = kp_ref[...], vp_ref[...]
        else:
            has_prev = 1
            k_prev, v_prev = kc_ref[(sub - 1) * blk:sub * blk, :], vc_ref[(sub - 1) * blk:sub * blk, :]
        kband = jnp.concatenate([k_prev, kc_ref[rows_q, :]], axis=0)
        vband = jnp.concatenate([v_prev, vc_ref[rows_q, :]], axis=0)
        y = _swa_block(q_ref[rows_q, :], kband, vband, bias_ref, has_prev, sink_ref, lane, row_g)
        out_ref[rows_q, :] = _rms(y, gn_ref[...]).astype(BF16)


def _swa_block(q, kband, vband, bias_ref, has_prev, sink_ref, lane, row_g):
    blk = q.shape[0]
    rows = GQA_GROUP * blk
    outs = []
    for cg in range(KV_HEADS // 2):
        qs = jnp.concatenate(
            [q[:, (cg * GQA_GROUP + g) * LANES:(cg * GQA_GROUP + g + 1) * LANES] for g in range(GQA_GROUP)],
            axis=0)
        ka = kband[:, cg * LANES:(cg + 1) * LANES]
        va = vband[:, cg * LANES:(cg + 1) * LANES]
        o = None
        for par in range(2):
            kv = 2 * cg + par
            half = (lane < HEAD_DIM) if par == 0 else (lane >= HEAD_DIM)
            km = jnp.where(half, ka, 0.0).astype(BF16)
            vm = jnp.where(half, va, 0.0).astype(BF16)
            s = lax.dot_general(qs, km, (((1,), (1,)), ((), ())), preferred_element_type=F32)
            s = s + bias_ref[has_prev, kv]
            sink = jnp.zeros((rows, 1), F32)
            for g in range(GQA_GROUP):
                sink = jnp.where(row_g == g, sink_ref[kv * GQA_GROUP + g], sink)
            m = jnp.maximum(jnp.max(s, axis=-1, keepdims=True), sink)
            p = jnp.exp(s - m)
            denom = jnp.sum(p, axis=-1, keepdims=True) + jnp.exp(sink - m)
            part = jnp.dot(p.astype(BF16), vm, preferred_element_type=F32) * (1.0 / denom)
            o = part if o is None else o + part
        outs.extend(o[g * blk:(g + 1) * blk, :] for g in range(GQA_GROUP))
    return jnp.concatenate(outs, axis=1)


def _swa_prompt(q, k, v, code, tbl, sinks, gn, batch, seq):
    blk = WINDOW
    per_step = next(n for n in (4, 2, 1) if seq % (n * blk) == 0)
    nb = seq // (per_step * blk)
    d_attn = q.shape[-1]
    kv_dim = k.shape[-1]
    cur = lambda b, j: (b * nb + j, 0)
    prev = lambda b, j: (per_step * (b * nb + j) - jnp.where(j > 0, 1, 0), 0)
    return pl.pallas_call(
        _swa_prompt_body,
        grid=(batch, nb),
        in_specs=[
            _const_spec(code.shape), _smem_spec(), _smem_spec(),
            pl.BlockSpec((per_step * blk, d_attn), cur),
            pl.BlockSpec((blk, kv_dim), prev), pl.BlockSpec((per_step * blk, kv_dim), cur),
            pl.BlockSpec((blk, kv_dim), prev), pl.BlockSpec((per_step * blk, kv_dim), cur),
            _const_spec(gn.shape),
        ],
        out_specs=pl.BlockSpec((per_step * blk, d_attn), cur),
        out_shape=jax.ShapeDtypeStruct((batch * seq, d_attn), BF16),
        scratch_shapes=[pltpu.VMEM((2, KV_HEADS, GQA_GROUP * blk, 2 * blk), F32)],
        compiler_params=pltpu.CompilerParams(
            dimension_semantics=("arbitrary", "arbitrary"), vmem_limit_bytes=VMEM_LIMIT_BYTES),
        name="swa_prompt",
    )(code, tbl, sinks, q, k, k, v, v, gn)


def _slide_window(win_ref, new_ref, out_ref, steps):
    bb, kv_dim, win = win_ref.shape
    npad = new_ref.shape[1]
    new_t = new_ref[...].reshape(bb * npad, kv_dim).T
    lane = lax.broadcasted_iota(jnp.int32, (kv_dim, win), 1)
    for b in range(bb):
        placed = pltpu.roll(new_t, (win - steps - b * npad) % win, 1)
        shifted = pltpu.roll(win_ref[b], win - steps, 1)
        out_ref[b] = jnp.where(lane >= win - steps, placed, shifted)


def _swa_sample_body(codec_ref, coden_ref, tbl_ref, sink_ref, q_ref, ck_ref, cv_ref, kn_ref, vn_ref,
                     out_ref, ck_out_ref, cv_out_ref, biasc_ref, biasn_ref, sinkc_ref, *, steps):
    bb = q_ref.shape[0]
    nrow = N_HEADS * steps
    win = ck_ref.shape[2]
    npad = kn_ref.shape[1]

    @pl.when(pl.program_id(0) == 0)
    def _():
        codec = codec_ref[...]
        coden = coden_ref[...]
        hrow = lax.broadcasted_iota(jnp.int32, (nrow, 1), 0) // steps

        def pick(idx, accs):
            ac, an = accs
            val = tbl_ref[idx]
            return jnp.where(codec == idx, val, ac), jnp.where(coden == idx, val, an)
        bc, bn = lax.fori_loop(0, N_BUCKETS * N_HEADS, pick,
                               (jnp.zeros(codec.shape, F32), jnp.zeros(coden.shape, F32)))
        biasc_ref[...] = bc
        biasn_ref[...] = bn

        def pick_sink(h, acc):
            return jnp.where(hrow == h, sink_ref[h], acc)
        sinkc_ref[...] = lax.fori_loop(0, N_HEADS, pick_sink, jnp.zeros((nrow, 1), F32))

    q = q_ref[...].astype(F32)
    lane = lax.broadcasted_iota(jnp.int32, (bb, GQA_GROUP * steps, LANES), 2)
    zeros = jnp.zeros((bb, GQA_GROUP * steps, LANES), F32)
    pieces = []
    for cg in range(KV_HEADS // 2):
        for par in range(2):
            half = (lane < HEAD_DIM) if par == 0 else (lane >= HEAD_DIM)
            qm = jnp.where(half, q[:, cg], zeros)
            pieces.append(jnp.concatenate([qm, zeros] if cg == 0 else [zeros, qm], axis=2))
    qm = jnp.concatenate(pieces, axis=1).astype(BF16)

    ck = ck_ref[...].astype(BF16)
    kn = kn_ref[...].astype(BF16)
    s_c = jnp.einsum("bqd,bdk->bqk", qm, ck, preferred_element_type=F32)
    s_n = jnp.einsum("bqd,bkd->bqk", qm, kn, preferred_element_type=F32)

    t_c = lax.broadcasted_iota(jnp.int32, (nrow, win), 0) % steps
    k_c = lax.broadcasted_iota(jnp.int32, (nrow, win), 1)
    valid_c = k_c > t_c
    t_n = lax.broadcasted_iota(jnp.int32, (nrow, npad), 0) % steps
    k_n = lax.broadcasted_iota(jnp.int32, (nrow, npad), 1)
    valid_n = k_n <= t_n

    s_c = jnp.where(valid_c[None], s_c + biasc_ref[...][None], NEG_INF)
    s_n = jnp.where(valid_n[None], s_n + biasn_ref[...][None], NEG_INF)
    sink = sinkc_ref[...][None]
    m = jnp.maximum(jnp.maximum(jnp.max(s_c, axis=-1, keepdims=True),
                                jnp.max(s_n, axis=-1, keepdims=True)), sink)
    p_c = jnp.exp(s_c - m)
    p_n = jnp.exp(s_n - m)
    denom = (jnp.sum(p_c, axis=-1, keepdims=True) + jnp.sum(p_n, axis=-1, keepdims=True)
             + jnp.exp(sink - m))
    r = 1.0 / denom
    w_c = (p_c * r).astype(BF16)
    w_n = (p_n * r).astype(BF16)
    o = (jnp.einsum("bqk,bdk->bqd", w_c, cv_ref[...].astype(BF16), preferred_element_type=F32)
         + jnp.einsum("bqk,bkd->bqd", w_n, vn_ref[...].astype(BF16), preferred_element_type=F32))
    gt = GQA_GROUP * steps
    lane_o = lax.broadcasted_iota(jnp.int32, (bb, gt, LANES), 2)
    for cg in range(KV_HEADS // 2):
        lo = o[:, cg * 2 * gt:cg * 2 * gt + gt, cg * LANES:(cg + 1) * LANES]
        hi = o[:, cg * 2 * gt + gt:(cg + 1) * 2 * gt, cg * LANES:(cg + 1) * LANES]
        out_ref[:, cg] = jnp.where(lane_o < HEAD_DIM, lo, hi)

    _slide_window(ck_ref, kn_ref, ck_out_ref, steps)
    _slide_window(cv_ref, vn_ref, cv_out_ref, steps)


def _swa_sample(qs, ck, cv, kn, vn, codec, coden, tbl, sinks, steps, bb):
    nbatch, ncg, gt, _ = qs.shape
    kv_dim, win = ck.shape[1], ck.shape[2]
    npad = kn.shape[1]
    nrow = N_HEADS * steps
    assert bb * npad == win
    blk4 = lambda i: (i, 0, 0, 0)
    blk3 = lambda i: (i, 0, 0)
    return pl.pallas_call(
        functools.partial(_swa_sample_body, steps=steps),
        grid=(nbatch // bb,),
        in_specs=[
            _const_spec(codec.shape), _const_spec(coden.shape), _smem_spec(), _smem_spec(),
            pl.BlockSpec((bb, ncg, gt, LANES), blk4),
            pl.BlockSpec((bb, kv_dim, win), blk3), pl.BlockSpec((bb, kv_dim, win), blk3),
            pl.BlockSpec((bb, npad, kv_dim), blk3), pl.BlockSpec((bb, npad, kv_dim), blk3),
        ],
        out_specs=(pl.BlockSpec((bb, ncg, gt, LANES), blk4),
                   pl.BlockSpec((bb, kv_dim, win), blk3), pl.BlockSpec((bb, kv_dim, win), blk3)),
        out_shape=(jax.ShapeDtypeStruct((nbatch, ncg, gt, LANES), F32),
                   jax.ShapeDtypeStruct((nbatch, kv_dim, win), F32),
                   jax.ShapeDtypeStruct((nbatch, kv_dim, win), F32)),
        scratch_shapes=[pltpu.VMEM((nrow, win), F32), pltpu.VMEM((nrow, npad), F32),
                        pltpu.VMEM((nrow, 1), F32)],
        compiler_params=pltpu.CompilerParams(
            dimension_semantics=("arbitrary",), vmem_limit_bytes=VMEM_LIMIT_BYTES),
        name="swa_sample",
    )(codec, coden, tbl, sinks, qs, ck, cv, kn, vn)


def _shift_rows(up, hist, k):
    rolled = pltpu.roll(up, k, 0)
    row = lax.broadcasted_iota(jnp.int32, hist.shape, 0)
    head = jnp.where(row < k, pltpu.roll(hist, k, 0), rolled[0:SUBLANES, :])
    return jnp.concatenate([head, rolled[SUBLANES:, :]], axis=0)


def _ffn_conv(up, hist, cw, cb, *, tm, sample_nb):
    if sample_nb is None:
        prev2 = _shift_rows(up, hist, 2)
        prev1 = _shift_rows(up, hist, 1)
    else:
        s0, s1 = hist
        prev2 = jnp.concatenate([s0, s1, up[:tm - 2 * sample_nb, :]], axis=0)
        prev1 = jnp.concatenate([s1, up[:tm - sample_nb, :]], axis=0)
    return cw[0:1, :] * prev2 + cw[1:2, :] * prev1 + cw[2:3, :] * up + cb


FFN_CHUNK = 256
FFN_SLAB = 256


def _ffn_prompt_body(x_hbm, mr_ref, ma_ref, wor_ref, woa_ref, gf_ref, wg_ref, wv_ref, cwg_ref, cwv_ref, cbg_ref,
                     cbv_ref, wd_ref, gfin_ref, y_ref, tg_ref, tv_ref, hn_ref, car_g, car_v, x_sem,
                     *, tm, tiles_per_seq):
    i = pl.program_id(0)
    j = pl.program_id(1)
    nj = pl.num_programs(1)
    n_slab = tm // FFN_SLAB
    slab = lambda r: slice(r * FFN_SLAB, (r + 1) * FFN_SLAB)

    @pl.when(j == 0)
    def _():
        x_copy = pltpu.make_async_copy(x_hbm.at[pl.ds(pl.multiple_of(i * tm, tm), tm), :], y_ref, x_sem)
        x_copy.start()

        def out_proj(r):
            return (jnp.dot(mr_ref[slab(r), :], wor_ref[...], preferred_element_type=F32)
                    + jnp.dot(ma_ref[slab(r), :], woa_ref[...], preferred_element_type=F32))

        pending = out_proj(0)
        x_copy.wait()
        for r in range(n_slab):
            d = pending
            if r + 1 < n_slab:
                pending = out_proj(r + 1)
            h = y_ref[slab(r), :] + d
            y_ref[slab(r), :] = h
            hn_ref[slab(r), :] = _rms(h, gf_ref[...]).astype(BF16)

    @pl.when((i == 0) & (j == 0))
    def _():
        car_g[...] = jnp.zeros_like(car_g)
        car_v[...] = jnp.zeros_like(car_v)

    cwg, cwv, cbg, cbv = cwg_ref[...], cwv_ref[...], cbg_ref[...], cbv_ref[...]

    def up_proj(r):
        hn = hn_ref[slab(r), :]
        return (jnp.dot(hn, wg_ref[...], preferred_element_type=F32),
                jnp.dot(hn, wv_ref[...], preferred_element_type=F32))

    seq_start = i % tiles_per_seq == 0
    hist_g = jnp.where(seq_start, 0.0, car_g[j])
    hist_v = jnp.where(seq_start, 0.0, car_v[j])
    pending = up_proj(0)
    act = None
    for r in range(n_slab):
        up_g, up_v = pending
        if r + 1 < n_slab:
            pending = up_proj(r + 1)
        if act is not None:
            y_ref[slab(r - 1), :] += jnp.dot(act, wd_ref[...], preferred_element_type=F32)
        c_g = _ffn_conv(up_g, hist_g, cwg, cbg, tm=FFN_SLAB, sample_nb=None)
        c_v = _ffn_conv(up_v, hist_v, cwv, cbv, tm=FFN_SLAB, sample_nb=None)
        hist_g = up_g[FFN_SLAB - SUBLANES:, :]
        hist_v = up_v[FFN_SLAB - SUBLANES:, :]
        act = (jax.nn.gelu(c_g) * c_v).astype(BF16)
    y_ref[slab(n_slab - 1), :] += jnp.dot(act, wd_ref[...], preferred_element_type=F32)
    car_g[j] = hist_g
    car_v[j] = hist_v
    tg_ref[0] = hist_g
    tv_ref[0] = hist_v

    @pl.when(j == nj - 1)
    def _():
        for r in range(n_slab):
            y_ref[slab(r), :] = _rms(y_ref[slab(r), :], gfin_ref[...])


def _ffn_sample_body(x_ref, mr_ref, ma_ref, gna_ref, wor_ref, woa_ref, gf_ref, wg_ref, wv_ref, cwg_ref, cwv_ref,
                     cbg_ref, cbv_ref, wd_ref, gfin_ref, s0g_ref, s1g_ref, s0v_ref, s1v_ref,
                     y_ref, tg_ref, tv_ref, hn_ref, *, nb):
    j = pl.program_id(1)
    nj = pl.num_programs(1)
    tm = x_ref.shape[0]

    @pl.when(j == 0)
    def _():
        ma = _rms(ma_ref[...], gna_ref[...]).astype(BF16)
        h = (x_ref[...] + jnp.dot(mr_ref[...], wor_ref[...], preferred_element_type=F32)
             + jnp.dot(ma, woa_ref[...], preferred_element_type=F32))
        y_ref[...] = h
        hn_ref[...] = _rms(h, gf_ref[...]).astype(BF16)

    hn = hn_ref[...]
    tf = wg_ref.shape[1]
    acts = []
    for c in range(tf // FFN_CHUNK):
        cs = slice(c * FFN_CHUNK, (c + 1) * FFN_CHUNK)
        up_g = jnp.dot(hn, wg_ref[:, cs], preferred_element_type=F32)
        up_v = jnp.dot(hn, wv_ref[:, cs], preferred_element_type=F32)
        tg_ref[0, :, cs] = up_g[tm - 2 * nb:tm - nb, :]
        tg_ref[1, :, cs] = up_g[tm - nb:, :]
        tv_ref[0, :, cs] = up_v[tm - 2 * nb:tm - nb, :]
        tv_ref[1, :, cs] = up_v[tm - nb:, :]
        c_g = _ffn_conv(up_g, (s0g_ref[:, cs], s1g_ref[:, cs]), cwg_ref[:, cs], cbg_ref[:, cs],
                        tm=tm, sample_nb=nb)
        c_v = _ffn_conv(up_v, (s0v_ref[:, cs], s1v_ref[:, cs]), cwv_ref[:, cs], cbv_ref[:, cs],
                        tm=tm, sample_nb=nb)
        acts.append((jax.nn.gelu(c_g) * c_v).astype(BF16))
    act = jnp.concatenate(acts, axis=1)
    y_ref[...] += jnp.dot(act, wd_ref[...], preferred_element_type=F32)

    @pl.when(j == nj - 1)
    def _():
        y_ref[...] = _rms(y_ref[...], gfin_ref[...])


def _ffn_prompt(x, mr, ma, wor, woa, gf, wup, cw, cb, wd, gfin, batch, seq, tm, tf):
    n, d_model = x.shape
    d_half = mr.shape[-1]
    d_ff = wd.shape[0]
    ni, nj = n // tm, d_ff // tf
    tiles_per_seq = seq // tm
    row = lambda i, j: (i, 0)
    gate_col = lambda i, j: (0, j)
    val_col = lambda i, j: (0, nj + j)
    tail = lambda i, j: (i, 0, j)
    y, tg, tv = pl.pallas_call(
        functools.partial(_ffn_prompt_body, tm=tm, tiles_per_seq=tiles_per_seq),
        grid=(ni, nj),
        in_specs=[
            pl.BlockSpec(memory_space=pl.ANY),
            pl.BlockSpec((tm, d_half), row), pl.BlockSpec((tm, d_half), row),
            _const_spec(wor.shape), _const_spec(woa.shape), _const_spec(gf.shape),
            pl.BlockSpec((d_model, tf), gate_col), pl.BlockSpec((d_model, tf), val_col),
            pl.BlockSpec((cw.shape[0], tf), gate_col), pl.BlockSpec((cw.shape[0], tf), val_col),
            pl.BlockSpec((1, tf), gate_col), pl.BlockSpec((1, tf), val_col),
            pl.BlockSpec((tf, d_model), lambda i, j: (j, 0)),
            _const_spec(gfin.shape),
        ],
        out_specs=(
            pl.BlockSpec((tm, d_model), row),
            pl.BlockSpec((1, SUBLANES, tf), tail),
            pl.BlockSpec((1, SUBLANES, tf), tail),
        ),
        out_shape=(
            jax.ShapeDtypeStruct((n, d_model), F32),
            jax.ShapeDtypeStruct((ni, SUBLANES, d_ff), F32),
            jax.ShapeDtypeStruct((ni, SUBLANES, d_ff), F32),
        ),
        scratch_shapes=[
            pltpu.VMEM((tm, d_model), BF16),
            pltpu.VMEM((nj, SUBLANES, tf), F32), pltpu.VMEM((nj, SUBLANES, tf), F32),
            pltpu.SemaphoreType.DMA(()),
        ],
        compiler_params=pltpu.CompilerParams(
            dimension_semantics=("arbitrary", "arbitrary"), vmem_limit_bytes=FFN_VMEM_LIMIT_BYTES),
        name="ffn_prompt",
    )(x, mr, ma, wor, woa, gf, wup, wup, cw, cw, cb, cb, wd, gfin)
    return y, tg[tiles_per_seq - 1::tiles_per_seq], tv[tiles_per_seq - 1::tiles_per_seq]


def _ffn_sample(x, mr, ma, gna, wor, woa, gf, wup, cw, cb, wd, gfin, state2d, nb, steps, tf):
    n, d_model = x.shape
    d_half = mr.shape[-1]
    d_ff = wd.shape[0]
    nj = d_ff // tf
    full2 = lambda shape: pl.BlockSpec(shape, lambda i, j: (0, 0))
    gate_col = lambda i, j: (0, j)
    val_col = lambda i, j: (0, nj + j)
    y, tg, tv = pl.pallas_call(
        functools.partial(_ffn_sample_body, nb=nb),
        grid=(1, nj),
        in_specs=[
            full2((n, d_model)), full2((n, d_half)), full2((n, d_half)), full2(gna.shape),
            _const_spec(wor.shape), _const_spec(woa.shape), _const_spec(gf.shape),
            pl.BlockSpec((d_model, tf), gate_col), pl.BlockSpec((d_model, tf), val_col),
            pl.BlockSpec((cw.shape[0], tf), gate_col), pl.BlockSpec((cw.shape[0], tf), val_col),
            pl.BlockSpec((1, tf), gate_col), pl.BlockSpec((1, tf), val_col),
            pl.BlockSpec((tf, d_model), lambda i, j: (j, 0)),
            _const_spec(gfin.shape),
            pl.BlockSpec((nb, tf), lambda i, j: (0, j)),
            pl.BlockSpec((nb, tf), lambda i, j: (0, 2 * nj + j)),
            pl.BlockSpec((nb, tf), lambda i, j: (0, nj + j)),
            pl.BlockSpec((nb, tf), lambda i, j: (0, 3 * nj + j)),
        ],
        out_specs=(
            full2((n, d_model)),
            pl.BlockSpec((2, nb, tf), lambda i, j: (0, 0, j)),
            pl.BlockSpec((2, nb, tf), lambda i, j: (0, 0, j)),
        ),
        out_shape=(
            jax.ShapeDtypeStruct((n, d_model), F32),
            jax.ShapeDtypeStruct((2, nb, d_ff), F32),
            jax.ShapeDtypeStruct((2, nb, d_ff), F32),
        ),
        scratch_shapes=[pltpu.VMEM((n, d_model), BF16)],
        compiler_params=pltpu.CompilerParams(
            dimension_semantics=("arbitrary", "arbitrary"), vmem_limit_bytes=VMEM_LIMIT_BYTES),
        name="ffn_sample",
    )(x, mr, ma, gna, wor, woa, gf, wup, wup, cw, cw, cb, cb, wd, gfin, state2d, state2d, state2d, state2d)
    return y, tg, tv


def _perm_heads(a, axis):
    shape = a.shape
    a = a.reshape(shape[:axis] + (KV_HEADS // 2, 2, GQA_GROUP, HEAD_DIM) + shape[axis + 1:])
    return jnp.swapaxes(a, axis + 1, axis + 2).reshape(shape)


def _gate_weights(w_a, w_x):
    nblk, bs, _ = w_a.shape
    per = GATE_TILE // bs
    eye = jnp.eye(per, dtype=w_a.dtype)

    def pack(w):
        w4 = w.reshape(nblk // per, per, bs, bs)
        return jnp.einsum("cgij,gh->cgihj", w4, eye).reshape(nblk // per, GATE_TILE, GATE_TILE)
    return jnp.concatenate([pack(w_a), pack(w_x)], axis=-1).astype(BF16)


def kernel(x_prompt, x_sample, state_rnn_conv, state_rnn_h, cache_win_k, cache_win_v, state_ffn_conv,
           norm_mix_g, w_in, rnn_conv_w, rnn_conv_b, w_gate_a, b_gate_a, w_gate_x, b_gate_x, rnn_lambda,
           attn_sinks, rel_bias_table, gn_rnn_g, gn_attn_g, w_out, norm_ffn_g, w_up, ffn_conv_w,
           ffn_conv_b, w_down, norm_final_g):
    batch, seq, d_model = x_prompt.shape
    nb, steps, _ = x_sample.shape
    depth = w_in.shape[0]
    d_rnn = rnn_conv_w.shape[-1]
    d_attn = N_HEADS * HEAD_DIM
    kv_dim = KV_HEADS * HEAD_DIM
    d_ff = w_down.shape[1]
    win = cache_win_k.shape[2]
    assert depth == 1 and d_rnn + d_attn == d_model and w_in.shape[-1] == 2 * d_rnn + d_attn + 2 * kv_dim
    assert win == WINDOW and seq % WINDOW == 0 and w_gate_a.shape[1] == RNN_BLOCKS
    assert rnn_conv_w.shape[1] == 4 and ffn_conv_w.shape[1] == 3 and steps >= 3 and nb % SUBLANES == 0

    w_in0 = w_in[0]
    o_q = 2 * d_rnn
    w_rg = w_in0[:, :o_q].astype(BF16)
    w_q = _perm_heads(w_in0[:, o_q:o_q + d_attn], 1).astype(BF16)
    w_kv = w_in0[:, o_q + d_attn:].astype(BF16)
    w_out0 = w_out[0]
    w_out_r = w_out0[:d_rnn].astype(BF16)
    w_out_a = _perm_heads(w_out0[d_rnn:], 0).astype(BF16)
    gn_attn_p = _perm_heads(gn_attn_g[0], 0).reshape(1, d_attn)
    tf = 512
    w_up_b = w_up[0].astype(BF16)
    w_down_b = w_down[0].astype(BF16)
    wg = _gate_weights(w_gate_a[0], w_gate_x[0])
    row2 = lambda a: a.reshape(1, -1)
    g_mix, g_ffn, g_fin, g_rnn = row2(norm_mix_g[0]), row2(norm_ffn_g[0]), row2(norm_final_g), row2(gn_rnn_g[0])
    cw_r, cb_r = rnn_conv_w[0], row2(rnn_conv_b[0])
    ba, bx, lam = row2(b_gate_a[0]), row2(b_gate_x[0]), row2(rnn_lambda[0])
    cw_f, cb_f = ffn_conv_w[0], row2(ffn_conv_b[0])
    tbl = rel_bias_table.reshape(-1)
    sinks = attn_sinks[0]

    blk = WINDOW
    qi = np.arange(blk)[:, None]
    kj = np.arange(2 * blk)[None, :]
    code_p = jnp.asarray(_t5_bucket_np(blk + qi - kj))
    r = np.arange(N_HEADS * steps)[:, None]
    t_r, h_r = r % steps, r // steps
    npad = 2 * SUBLANES
    code_c = jnp.asarray(_t5_bucket_np(win + t_r - np.arange(win)[None, :]) * N_HEADS + h_r)
    code_n = jnp.asarray(_t5_bucket_np(t_r - np.arange(npad)[None, :]) * N_HEADS + h_r)

    n_p = batch * seq
    tm_a = 512 if n_p % 512 == 0 else WINDOW
    xp2 = x_prompt.reshape(n_p, d_model)
    xr, gr, q, k, v = _inproj(xp2, pl.BlockSpec((tm_a, d_model), lambda i: (i, 0)), n_p // tm_a, tm_a,
                              g_mix, w_rg, w_q, w_kv)
    tt = 512 if seq % 512 == 0 else WINDOW
    m_rnn, h_last = _rglru_prompt(xr, gr, cw_r, cb_r, wg, ba, bx, lam, g_rnn, batch, seq, tt)
    m_attn = _swa_prompt(q, k, v, code_p, tbl, sinks, gn_attn_p, batch, seq)
    tm_f = next(t for t in (1024, 512, 256) if seq % t == 0)
    y_p, tail_g, tail_v = _ffn_prompt(xp2, m_rnn, m_attn, w_out_r, w_out_a, g_ffn, w_up_b, cw_f, cb_f,
                                      w_down_b, g_fin, batch, seq, tm_f, 768 if d_ff % 768 == 0 else tf)
    y_prompt = y_p.reshape(batch, seq, d_model)
    p_rnn_conv = xr.reshape(batch, seq, d_rnn)[:, seq - 3:, :][None]
    p_rnn_h = h_last[None]
    last_win = lambda a: a.reshape(batch, seq, kv_dim)[:, seq - win:, :].reshape(
        1, batch, win, KV_HEADS, HEAD_DIM)
    p_win_k, p_win_v = last_win(k), last_win(v)
    p_ffn_conv = jnp.concatenate([tail_g[:, SUBLANES - 2:, :], tail_v[:, SUBLANES - 2:, :]], axis=-1)[None]

    n_s = nb * steps
    xs_tm = x_sample.transpose(1, 0, 2).reshape(n_s, d_model)
    xr_s, gr_s, q_s, k_s, v_s = _inproj(xs_tm, pl.BlockSpec((nb, d_model), lambda t: (t, 0)), steps, nb,
                                        g_mix, w_rg, w_q, w_kv)
    conv_tm = state_rnn_conv[0].transpose(1, 0, 2).reshape(3 * nb, d_rnn)
    m_rnn_s, h_new = _rglru_sample(xr_s, gr_s, conv_tm, state_rnn_h[0], cw_r, cb_r, wg, ba, bx, lam, g_rnn,
                                   nb, steps)
    ncg = KV_HEADS // 2
    qs = q_s.reshape(steps, nb, ncg, GQA_GROUP, LANES).transpose(1, 2, 3, 0, 4).reshape(
        nb, ncg, GQA_GROUP * steps, LANES)
    k_tm = k_s.reshape(steps, nb, kv_dim)
    v_tm = v_s.reshape(steps, nb, kv_dim)
    pad = ((0, 0), (0, npad - steps), (0, 0))
    ck_t = cache_win_k[0].transpose(0, 2, 3, 1).reshape(nb, kv_dim, win)
    cv_t = cache_win_v[0].transpose(0, 2, 3, 1).reshape(nb, kv_dim, win)
    bb = SUBLANES
    o_s, ck_new, cv_new = _swa_sample(qs, ck_t, cv_t, jnp.pad(k_tm.transpose(1, 0, 2), pad),
                                      jnp.pad(v_tm.transpose(1, 0, 2), pad), code_c, code_n, tbl, sinks, steps, bb)
    ya_s = o_s.reshape(nb, ncg, GQA_GROUP, steps, LANES).transpose(3, 0, 1, 2, 4).reshape(n_s, d_attn)
    ffn2d = state_ffn_conv[0].reshape(nb, 2 * 2 * d_ff)
    y_s, ns_g, ns_v = _ffn_sample(xs_tm, m_rnn_s, ya_s, gn_attn_p, w_out_r, w_out_a, g_ffn, w_up_b, cw_f, cb_f,
                                  w_down_b, g_fin, ffn2d, nb, steps, tf)
    y_sample = y_s.reshape(steps, nb, d_model).transpose(1, 0, 2)
    s_rnn_conv = xr_s.reshape(steps, nb, d_rnn)[steps - 3:].transpose(1, 0, 2)[None]
    s_rnn_h = h_new[None]

    to_cache = lambda w_t: w_t.reshape(nb, KV_HEADS, HEAD_DIM, win).transpose(0, 3, 1, 2)[None]
    s_win_k, s_win_v = to_cache(ck_new), to_cache(cv_new)
    s_ffn_conv = jnp.concatenate([ns_g, ns_v], axis=-1).transpose(1, 0, 2)[None]

    return (y_prompt, y_sample, p_rnn_conv, p_rnn_h, p_win_k, p_win_v, p_ffn_conv,
            s_rnn_conv, s_rnn_h, s_win_k, s_win_v, s_ffn_conv)
```

```python
import functools
import math

import numpy as np
import jax
import jax.numpy as jnp
from jax import lax
from jax.experimental import pallas as pl
from jax.experimental.pallas import tpu as pltpu

F32 = jnp.float32
BF16 = jnp.bfloat16

HEAD_DIM = 64
KV_HEADS = 4
N_HEADS = 16
GQA_GROUP = N_HEADS // KV_HEADS
RNN_BLOCKS = 16
RG_C = 8.0
WINDOW = 128
N_BUCKETS = 32
MAX_EXACT = N_BUCKETS // 2
REL_MAX_DIST = 128
EPS = 1e-6
NEG_INF = -1e30
ATTN_SCALE = HEAD_DIM ** -0.5

LANES = 128
SUBLANES = 8
VMEM_LIMIT_BYTES = 56 * 1024 * 1024
FFN_VMEM_LIMIT_BYTES = 61 * 1024 * 1024

GATE_TILE = 256

PROJ_ROWS = (512, WINDOW)
RGLRU_ROWS = (1024, 512, WINDOW)
SWA_BLOCKS_PER_STEP = (8, 4, 2, 1)
FFN_ROWS = (1024, 512, 256)
FFN_COLS_PROMPT = (768, 512)
FFN_COLS_SAMPLE = (512,)


def _pick(options, n):
    return next(t for t in options if n % t == 0)


def _t5_bucket_np(d):
    n = np.maximum(d, 0)
    nf = np.maximum(n, 1).astype(np.float32)
    large = MAX_EXACT + (np.log(nf / MAX_EXACT) / math.log(REL_MAX_DIST / MAX_EXACT)
                         * (N_BUCKETS - MAX_EXACT)).astype(np.int32)
    large = np.minimum(large, N_BUCKETS - 1)
    return np.where(n < MAX_EXACT, n, large).astype(np.int32)


def _rms(x, g):
    ms = jnp.mean(x * x, axis=-1, keepdims=True)
    return (x * lax.rsqrt(ms + EPS)) * g


def _softplus(x):
    return jnp.maximum(x, 0.0) + jnp.log1p(jnp.exp(-jnp.abs(x)))


def _const_spec(shape):
    nd = len(shape)
    return pl.BlockSpec(shape, lambda *_: (0,) * nd, pipeline_mode=pl.Buffered(1))


def _smem_spec():
    return pl.BlockSpec(memory_space=pltpu.SMEM)


def _inproj_body(x_ref, g_ref, wrg_ref, wq_ref, wkv_ref, xr_ref, gr_ref, q_ref, k_ref, v_ref):
    xn = _rms(x_ref[...], g_ref[...]).astype(BF16)
    d_rnn = xr_ref.shape[-1]
    kv_dim = k_ref.shape[-1]
    rg = jnp.dot(xn, wrg_ref[...], preferred_element_type=F32)
    xr_ref[...] = rg[:, :d_rnn]
    gr_ref[...] = rg[:, d_rnn:]
    q_ref[...] = (jnp.dot(xn, wq_ref[...], preferred_element_type=F32) * ATTN_SCALE).astype(BF16)
    kv = jnp.dot(xn, wkv_ref[...], preferred_element_type=F32)
    k_ref[...] = kv[:, :kv_dim]
    v_ref[...] = kv[:, kv_dim:]


def _inproj(x2d, x_spec, n_steps, tm, g, w_rg, w_q, w_kv):
    n = n_steps * tm
    d_model = g.shape[-1]
    d_rnn, d_attn, kv_dim = w_rg.shape[1] // 2, w_q.shape[1], w_kv.shape[1] // 2
    row = lambda i: (i, 0)
    out_shape = (
        jax.ShapeDtypeStruct((n, d_rnn), F32),
        jax.ShapeDtypeStruct((n, d_rnn), F32),
        jax.ShapeDtypeStruct((n, d_attn), BF16),
        jax.ShapeDtypeStruct((n, kv_dim), F32),
        jax.ShapeDtypeStruct((n, kv_dim), F32),
    )
    out_specs = (
        pl.BlockSpec((tm, d_rnn), row),
        pl.BlockSpec((tm, d_rnn), row),
        pl.BlockSpec((tm, d_attn), row),
        pl.BlockSpec((tm, kv_dim), row),
        pl.BlockSpec((tm, kv_dim), row),
    )
    return pl.pallas_call(
        _inproj_body,
        grid=(n_steps,),
        in_specs=[x_spec, _const_spec((1, d_model)), _const_spec(w_rg.shape), _const_spec(w_q.shape),
                  _const_spec(w_kv.shape)],
        out_specs=out_specs,
        out_shape=out_shape,
        compiler_params=pltpu.CompilerParams(
            dimension_semantics=("arbitrary",), vmem_limit_bytes=VMEM_LIMIT_BYTES),
        name="inproj",
    )(x2d, g, w_rg, w_q, w_kv)


def _rglru_gates(xc, wg_ref, ba, bx, lam):
    d_rnn = xc.shape[-1]
    pre_a, pre_x = [], []
    for c in range(d_rnn // GATE_TILE):
        xb = xc[:, c * GATE_TILE:(c + 1) * GATE_TILE].astype(BF16)
        pre = jnp.dot(xb, wg_ref[c], preferred_element_type=F32)
        pre_a.append(pre[:, :GATE_TILE])
        pre_x.append(pre[:, GATE_TILE:])
    r = jax.nn.sigmoid(jnp.concatenate(pre_a, axis=1) + ba)
    i = jax.nn.sigmoid(jnp.concatenate(pre_x, axis=1) + bx)
    log_a = (-RG_C * r) * _softplus(-lam)
    a = jnp.exp(log_a)
    one_minus_a2 = -jnp.tanh(log_a) * (a * a + 1.0)
    u = jnp.sqrt(one_minus_a2) * (i * xc)
    return a, u


def _rglru_prompt_body(xr_ref, gr_ref, cw_ref, cb_ref, wg_ref, ba_ref, bx_ref, lam_ref, gn_ref,
                       out_ref, hlast_ref, xbuf, abuf, ubuf, hbuf, hc_ref, *, tt, conv_w):
    ti = pl.program_id(1)
    d_rnn = xr_ref.shape[-1]

    @pl.when(ti == 0)
    def _():
        xbuf[0:SUBLANES, :] = jnp.zeros((SUBLANES, d_rnn), F32)
        hc_ref[...] = jnp.zeros_like(hc_ref)

    x = xr_ref[...]
    xbuf[SUBLANES:SUBLANES + tt, :] = x
    xc = cw_ref[conv_w - 1:conv_w, :] * x
    for k in range(conv_w - 1):
        off = SUBLANES - (conv_w - 1) + k
        xc = xc + cw_ref[k:k + 1, :] * xbuf[off:off + tt, :]
    xc = xc + cb_ref[...]
    xbuf[0:SUBLANES, :] = xbuf[tt:tt + SUBLANES, :]

    a, u = _rglru_gates(xc, wg_ref, ba_ref[...], bx_ref[...], lam_ref[...])
    abuf[...] = a
    ubuf[...] = u

    row = lax.broadcasted_iota(jnp.int32, (SUBLANES, d_rnn), 0)

    def group(gidx, hc):
        r0 = pl.multiple_of(gidx * SUBLANES, SUBLANES)
        ag = abuf[pl.ds(r0, SUBLANES), :]
        ug = ubuf[pl.ds(r0, SUBLANES), :]
        for k in (1, 2, 4):
            a_prev = jnp.where(row >= k, pltpu.roll(ag, k, 0), 1.0)
            u_prev = jnp.where(row >= k, pltpu.roll(ug, k, 0), 0.0)
            ug = ag * u_prev + ug
            ag = ag * a_prev
        h = ag * hc + ug
        hbuf[pl.ds(r0, SUBLANES), :] = h
        return jnp.broadcast_to(h[SUBLANES - 1:SUBLANES, :], (SUBLANES, d_rnn))

    hc = lax.fori_loop(0, tt // SUBLANES, group, hc_ref[...], unroll=2)
    hc_ref[...] = hc
    hlast_ref[0] = hc

    y = jax.nn.gelu(gr_ref[...]) * hbuf[...]
    out_ref[...] = _rms(y, gn_ref[...]).astype(BF16)


def _rglru_prompt(xr, gr, cw, cb, wg, ba, bx, lam, gn, batch, seq, tt):
    d_rnn = xr.shape[-1]
    nt = seq // tt
    conv_w = cw.shape[0]
    tile = lambda b, t: (b * nt + t, 0)
    out, hlast = pl.pallas_call(
        functools.partial(_rglru_prompt_body, tt=tt, conv_w=conv_w),
        grid=(batch, nt),
        in_specs=[
            pl.BlockSpec((tt, d_rnn), tile),
            pl.BlockSpec((tt, d_rnn), tile),
            _const_spec(cw.shape), _const_spec(cb.shape), _const_spec(wg.shape),
            _const_spec(ba.shape), _const_spec(bx.shape), _const_spec(lam.shape), _const_spec(gn.shape),
        ],
        out_specs=(
            pl.BlockSpec((tt, d_rnn), tile),
            pl.BlockSpec((1, SUBLANES, d_rnn), lambda b, t: (b, 0, 0)),
        ),
        out_shape=(
            jax.ShapeDtypeStruct((batch * seq, d_rnn), BF16),
            jax.ShapeDtypeStruct((batch, SUBLANES, d_rnn), F32),
        ),
        scratch_shapes=[
            pltpu.VMEM((tt + SUBLANES, d_rnn), F32),
            pltpu.VMEM((tt, d_rnn), F32),
            pltpu.VMEM((tt, d_rnn), F32),
            pltpu.VMEM((tt, d_rnn), F32),
            pltpu.VMEM((SUBLANES, d_rnn), F32),
        ],
        compiler_params=pltpu.CompilerParams(
            dimension_semantics=("arbitrary", "arbitrary"), vmem_limit_bytes=VMEM_LIMIT_BYTES),
        name="rglru_prompt",
    )(xr, gr, cw, cb, wg, ba, bx, lam, gn)
    return out, hlast[:, 0, :]


def _rglru_sample_body(xr_ref, gr_ref, c0_ref, c1_ref, c2_ref, h0_ref, cw_ref, cb_ref, wg_ref,
                       ba_ref, bx_ref, lam_ref, gn_ref, out_ref, hnew_ref, *, nb, steps):
    x = xr_ref[...]
    hist = [c0_ref[...], c1_ref[...], c2_ref[...]]
    xs = [x[t * nb:(t + 1) * nb, :] for t in range(steps)]
    xp = hist + xs
    conv_w = len(hist) + 1
    xc = cw_ref[conv_w - 1:conv_w, :] * x
    for k in range(conv_w - 1):
        shifted = jnp.concatenate(xp[k:k + steps], axis=0)
        xc = xc + cw_ref[k:k + 1, :] * shifted
    xc = xc + cb_ref[...]
    a, u = _rglru_gates(xc, wg_ref, ba_ref[...], bx_ref[...], lam_ref[...])
    h = h0_ref[...]
    hs = []
    for t in range(steps):
        h = a[t * nb:(t + 1) * nb, :] * h + u[t * nb:(t + 1) * nb, :]
        hs.append(h)
    hnew_ref[...] = h
    y = jax.nn.gelu(gr_ref[...]) * jnp.concatenate(hs, axis=0)
    out_ref[...] = _rms(y, gn_ref[...]).astype(BF16)


def _rglru_sample(xr, gr, conv_state2d, h0, cw, cb, wg, ba, bx, lam, gn, nb, steps):
    d_rnn = xr.shape[-1]
    n = nb * steps
    assert cw.shape[0] == 4
    full = lambda shape: pl.BlockSpec(shape, lambda i: (0,) * len(shape))
    return pl.pallas_call(
        functools.partial(_rglru_sample_body, nb=nb, steps=steps),
        grid=(1,),
        in_specs=[
            full((n, d_rnn)), full((n, d_rnn)),
            pl.BlockSpec((nb, d_rnn), lambda i: (0, 0)),
            pl.BlockSpec((nb, d_rnn), lambda i: (1, 0)),
            pl.BlockSpec((nb, d_rnn), lambda i: (2, 0)),
            full((nb, d_rnn)),
            full(cw.shape), full(cb.shape), full(wg.shape), full(ba.shape), full(bx.shape),
            full(lam.shape), full(gn.shape),
        ],
        out_specs=(full((n, d_rnn)), full((nb, d_rnn))),
        out_shape=(jax.ShapeDtypeStruct((n, d_rnn), BF16), jax.ShapeDtypeStruct((nb, d_rnn), F32)),
        compiler_params=pltpu.CompilerParams(
            dimension_semantics=("arbitrary",), vmem_limit_bytes=VMEM_LIMIT_BYTES),
        name="rglru_sample",
    )(xr, gr, conv_state2d, conv_state2d, conv_state2d, h0, cw, cb, wg, ba, bx, lam, gn)


def _swa_prompt_body(code_ref, tbl_ref, sink_ref, q_ref, kp_ref, kc_ref, vp_ref, vc_ref, gn_ref,
                     out_ref, bias_ref):
    j = pl.program_id(1)
    blk = WINDOW
    rows = GQA_GROUP * blk

    @pl.when((pl.program_id(0) == 0) & (j == 0))
    def _():
        code = code_ref[...]
        qi = lax.broadcasted_iota(jnp.int32, code.shape, 0)
        kj = lax.broadcasted_iota(jnp.int32, code.shape, 1)
        dist = blk + qi - kj
        in_window = (dist & -WINDOW) == 0
        for h in range(N_HEADS):
            def pick(b, acc, h=h):
                return jnp.where(code == b, tbl_ref[b * N_HEADS + h], acc)
            bias_h = lax.fori_loop(0, N_BUCKETS, pick, jnp.zeros(code.shape, F32))
            bias_h = jnp.where(in_window, bias_h, NEG_INF)
            kv, g = divmod(h, GQA_GROUP)
            bias_ref[1, kv, g * blk:(g + 1) * blk, :] = bias_h
            bias_ref[0, kv, g * blk:(g + 1) * blk, :] = jnp.where(kj >= blk, bias_h, NEG_INF)

    lane = lax.broadcasted_iota(jnp.int32, (2 * blk, LANES), 1)
    row_g = lax.broadcasted_iota(jnp.int32, (rows, 1), 0) // blk
    for sub in range(q_ref.shape[0] // blk):
        rows_q = slice(sub * blk, (sub + 1) * blk)
        if sub == 0:
            has_prev = jnp.where(j > 0, 1, 0)
            k_prev, v_prev = kp_ref[...], vp_ref[...]
        else:
            has_prev = 1
            k_prev, v_prev = kc_ref[(sub - 1) * blk:sub * blk, :], vc_ref[(sub - 1) * blk:sub * blk, :]
        kband = jnp.concatenate([k_prev, kc_ref[rows_q, :]], axis=0)
        vband = jnp.concatenate([v_prev, vc_ref[rows_q, :]], axis=0)
        y = _swa_block(q_ref[rows_q, :], kband, vband, bias_ref, has_prev, sink_ref, lane, row_g)
        out_ref[rows_q, :] = _rms(y, gn_ref[...]).astype(BF16)


def _swa_block(q, kband, vband, bias_ref, has_prev, sink_ref, lane, row_g):
    blk = q.shape[0]
    rows = GQA_GROUP * blk
    outs = []
    for cg in range(KV_HEADS // 2):
        qs = jnp.concatenate(
            [q[:, (cg * GQA_GROUP + g) * LANES:(cg * GQA_GROUP + g + 1) * LANES] for g in range(GQA_GROUP)],
            axis=0)
        ka = kband[:, cg * LANES:(cg + 1) * LANES]
        va = vband[:, cg * LANES:(cg + 1) * LANES]
        o = None
        for par in range(2):
            kv = 2 * cg + par
            half = (lane < HEAD_DIM) if par == 0 else (lane >= HEAD_DIM)
            km = jnp.where(half, ka, 0.0).astype(BF16)
            vm = jnp.where(half, va, 0.0).astype(BF16)
            s = lax.dot_general(qs, km, (((1,), (1,)), ((), ())), preferred_element_type=F32)
            s = s + bias_ref[has_prev, kv]
            sink = jnp.zeros((rows, 1), F32)
            for g in range(GQA_GROUP):
                sink = jnp.where(row_g == g, sink_ref[kv * GQA_GROUP + g], sink)
            m = jnp.maximum(jnp.max(s, axis=-1, keepdims=True), sink)
            p = jnp.exp(s - m)
            denom = jnp.sum(p, axis=-1, keepdims=True) + jnp.exp(sink - m)
            part = jnp.dot(p.astype(BF16), vm, preferred_element_type=F32) * (1.0 / denom)
            o = part if o is None else o + part
        outs.extend(o[g * blk:(g + 1) * blk, :] for g in range(GQA_GROUP))
    return jnp.concatenate(outs, axis=1)


def _swa_prompt(q, k, v, code, tbl, sinks, gn, batch, seq):
    blk = WINDOW
    per_step = _pick(SWA_BLOCKS_PER_STEP, seq // blk)
    nb = seq // (per_step * blk)
    d_attn = q.shape[-1]
    kv_dim = k.shape[-1]
    cur = lambda b, j: (b * nb + j, 0)
    prev = lambda b, j: (per_step * (b * nb + j) - jnp.where(j > 0, 1, 0), 0)
    return pl.pallas_call(
        _swa_prompt_body,
        grid=(batch, nb),
        in_specs=[
            _const_spec(code.shape), _smem_spec(), _smem_spec(),
            pl.BlockSpec((per_step * blk, d_attn), cur),
            pl.BlockSpec((blk, kv_dim), prev), pl.BlockSpec((per_step * blk, kv_dim), cur),
            pl.BlockSpec((blk, kv_dim), prev), pl.BlockSpec((per_step * blk, kv_dim), cur),
            _const_spec(gn.shape),
        ],
        out_specs=pl.BlockSpec((per_step * blk, d_attn), cur),
        out_shape=jax.ShapeDtypeStruct((batch * seq, d_attn), BF16),
        scratch_shapes=[pltpu.VMEM((2, KV_HEADS, GQA_GROUP * blk, 2 * blk), F32)],
        compiler_params=pltpu.CompilerParams(
            dimension_semantics=("arbitrary", "arbitrary"), vmem_limit_bytes=VMEM_LIMIT_BYTES),
        name="swa_prompt",
    )(code, tbl, sinks, q, k, k, v, v, gn)


def _slide_window(win_ref, new_ref, out_ref, steps):
    bb, kv_dim, win = win_ref.shape
    npad = new_ref.shape[1]
    new_t = new_ref[...].reshape(bb * npad, kv_dim).T
    lane = lax.broadcasted_iota(jnp.int32, (kv_dim, win), 1)
    for b in range(bb):
        placed = pltpu.roll(new_t, (win - steps - b * npad) % win, 1)
        shifted = pltpu.roll(win_ref[b], win - steps, 1)
        out_ref[b] = jnp.where(lane >= win - steps, placed, shifted)


def _swa_sample_body(codec_ref, coden_ref, tbl_ref, sink_ref, q_ref, ck_ref, cv_ref, kn_ref, vn_ref,
                     out_ref, ck_out_ref, cv_out_ref, biasc_ref, biasn_ref, sinkc_ref, *, steps):
    bb = q_ref.shape[0]
    nrow = N_HEADS * steps
    win = ck_ref.shape[2]
    npad = kn_ref.shape[1]

    @pl.when(pl.program_id(0) == 0)
    def _():
        codec = codec_ref[...]
        coden = coden_ref[...]
        hrow = lax.broadcasted_iota(jnp.int32, (nrow, 1), 0) // steps

        def pick(idx, accs):
            ac, an = accs
            val = tbl_ref[idx]
            return jnp.where(codec == idx, val, ac), jnp.where(coden == idx, val, an)
        bc, bn = lax.fori_loop(0, N_BUCKETS * N_HEADS, pick,
                               (jnp.zeros(codec.shape, F32), jnp.zeros(coden.shape, F32)))
        biasc_ref[...] = bc
        biasn_ref[...] = bn

        def pick_sink(h, acc):
            return jnp.where(hrow == h, sink_ref[h], acc)
        sinkc_ref[...] = lax.fori_loop(0, N_HEADS, pick_sink, jnp.zeros((nrow, 1), F32))

    q = q_ref[...].astype(F32)
    lane = lax.broadcasted_iota(jnp.int32, (bb, GQA_GROUP * steps, LANES), 2)
    zeros = jnp.zeros((bb, GQA_GROUP * steps, LANES), F32)
    pieces = []
    for cg in range(KV_HEADS // 2):
        for par in range(2):
            half = (lane < HEAD_DIM) if par == 0 else (lane >= HEAD_DIM)
            qm = jnp.where(half, q[:, cg], zeros)
            pieces.append(jnp.concatenate([qm, zeros] if cg == 0 else [zeros, qm], axis=2))
    qm = jnp.concatenate(pieces, axis=1).astype(BF16)

    ck = ck_ref[...].astype(BF16)
    kn = kn_ref[...].astype(BF16)
    s_c = jnp.einsum("bqd,bdk->bqk", qm, ck, preferred_element_type=F32)
    s_n = jnp.einsum("bqd,bkd->bqk", qm, kn, preferred_element_type=F32)

    t_c = lax.broadcasted_iota(jnp.int32, (nrow, win), 0) % steps
    k_c = lax.broadcasted_iota(jnp.int32, (nrow, win), 1)
    valid_c = k_c > t_c
    t_n = lax.broadcasted_iota(jnp.int32, (nrow, npad), 0) % steps
    k_n = lax.broadcasted_iota(jnp.int32, (nrow, npad), 1)
    valid_n = k_n <= t_n

    s_c = jnp.where(valid_c[None], s_c + biasc_ref[...][None], NEG_INF)
    s_n = jnp.where(valid_n[None], s_n + biasn_ref[...][None], NEG_INF)
    sink = sinkc_ref[...][None]
    m = jnp.maximum(jnp.maximum(jnp.max(s_c, axis=-1, keepdims=True),
                                jnp.max(s_n, axis=-1, keepdims=True)), sink)
    p_c = jnp.exp(s_c - m)
    p_n = jnp.exp(s_n - m)
    denom = (jnp.sum(p_c, axis=-1, keepdims=True) + jnp.sum(p_n, axis=-1, keepdims=True)
             + jnp.exp(sink - m))
    r = 1.0 / denom
    w_c = (p_c * r).astype(BF16)
    w_n = (p_n * r).astype(BF16)
    o = (jnp.einsum("bqk,bdk->bqd", w_c, cv_ref[...].astype(BF16), preferred_element_type=F32)
         + jnp.einsum("bqk,bkd->bqd", w_n, vn_ref[...].astype(BF16), preferred_element_type=F32))
    gt = GQA_GROUP * steps
    lane_o = lax.broadcasted_iota(jnp.int32, (bb, gt, LANES), 2)
    for cg in range(KV_HEADS // 2):
        lo = o[:, cg * 2 * gt:cg * 2 * gt + gt, cg * LANES:(cg + 1) * LANES]
        hi = o[:, cg * 2 * gt + gt:(cg + 1) * 2 * gt, cg * LANES:(cg + 1) * LANES]
        out_ref[:, cg] = jnp.where(lane_o < HEAD_DIM, lo, hi)

    _slide_window(ck_ref, kn_ref, ck_out_ref, steps)
    _slide_window(cv_ref, vn_ref, cv_out_ref, steps)


def _swa_sample(qs, ck, cv, kn, vn, codec, coden, tbl, sinks, steps, bb):
    nbatch, ncg, gt, _ = qs.shape
    kv_dim, win = ck.shape[1], ck.shape[2]
    npad = kn.shape[1]
    nrow = N_HEADS * steps
    assert bb * npad == win
    blk4 = lambda i: (i, 0, 0, 0)
    blk3 = lambda i: (i, 0, 0)
    return pl.pallas_call(
        functools.partial(_swa_sample_body, steps=steps),
        grid=(nbatch // bb,),
        in_specs=[
            _const_spec(codec.shape), _const_spec(coden.shape), _smem_spec(), _smem_spec(),
            pl.BlockSpec((bb, ncg, gt, LANES), blk4),
            pl.BlockSpec((bb, kv_dim, win), blk3), pl.BlockSpec((bb, kv_dim, win), blk3),
            pl.BlockSpec((bb, npad, kv_dim), blk3), pl.BlockSpec((bb, npad, kv_dim), blk3),
        ],
        out_specs=(pl.BlockSpec((bb, ncg, gt, LANES), blk4),
                   pl.BlockSpec((bb, kv_dim, win), blk3), pl.BlockSpec((bb, kv_dim, win), blk3)),
        out_shape=(jax.ShapeDtypeStruct((nbatch, ncg, gt, LANES), F32),
                   jax.ShapeDtypeStruct((nbatch, kv_dim, win), F32),
                   jax.ShapeDtypeStruct((nbatch, kv_dim, win), F32)),
        scratch_shapes=[pltpu.VMEM((nrow, win), F32), pltpu.VMEM((nrow, npad), F32),
                        pltpu.VMEM((nrow, 1), F32)],
        compiler_params=pltpu.CompilerParams(
            dimension_semantics=("arbitrary",), vmem_limit_bytes=VMEM_LIMIT_BYTES),
        name="swa_sample",
    )(codec, coden, tbl, sinks, qs, ck, cv, kn, vn)


def _shift_rows(up, hist, k):
    rolled = pltpu.roll(up, k, 0)
    row = lax.broadcasted_iota(jnp.int32, hist.shape, 0)
    head = jnp.where(row < k, pltpu.roll(hist, k, 0), rolled[0:SUBLANES, :])
    return jnp.concatenate([head, rolled[SUBLANES:, :]], axis=0)


def _ffn_conv(up, hist, cw, cb, *, tm, sample_nb):
    if sample_nb is None:
        prev2 = _shift_rows(up, hist, 2)
        prev1 = _shift_rows(up, hist, 1)
    else:
        s0, s1 = hist
        prev2 = jnp.concatenate([s0, s1, up[:tm - 2 * sample_nb, :]], axis=0)
        prev1 = jnp.concatenate([s1, up[:tm - sample_nb, :]], axis=0)
    return cw[0:1, :] * prev2 + cw[1:2, :] * prev1 + cw[2:3, :] * up + cb


FFN_CHUNK = 256
FFN_SLAB = 256


def _ffn_prompt_body(x_hbm, mr_ref, ma_ref, wor_ref, woa_ref, gf_ref, wg_ref, wv_ref, cwg_ref, cwv_ref, cbg_ref,
                     cbv_ref, wd_ref, gfin_ref, y_ref, tg_ref, tv_ref, hn_ref, car_g, car_v, x_sem,
                     *, tm, tiles_per_seq):
    i = pl.program_id(0)
    j = pl.program_id(1)
    nj = pl.num_programs(1)
    n_slab = tm // FFN_SLAB
    slab = lambda r: slice(r * FFN_SLAB, (r + 1) * FFN_SLAB)

    @pl.when(j == 0)
    def _():
        x_copy = pltpu.make_async_copy(x_hbm.at[pl.ds(pl.multiple_of(i * tm, tm), tm), :], y_ref, x_sem)
        x_copy.start()

        def out_proj(r):
            return (jnp.dot(mr_ref[slab(r), :], wor_ref[...], preferred_element_type=F32)
                    + jnp.dot(ma_ref[slab(r), :], woa_ref[...], preferred_element_type=F32))

        pending = out_proj(0)
        x_copy.wait()
        for r in range(n_slab):
            d = pending
            if r + 1 < n_slab:
                pending = out_proj(r + 1)
            h = y_ref[slab(r), :] + d
            y_ref[slab(r), :] = h
            hn_ref[slab(r), :] = _rms(h, gf_ref[...]).astype(BF16)

    @pl.when((i == 0) & (j == 0))
    def _():
        car_g[...] = jnp.zeros_like(car_g)
        car_v[...] = jnp.zeros_like(car_v)

    cwg, cwv, cbg, cbv = cwg_ref[...], cwv_ref[...], cbg_ref[...], cbv_ref[...]

    def up_proj(r):
        hn = hn_ref[slab(r), :]
        return (jnp.dot(hn, wg_ref[...], preferred_element_type=F32),
                jnp.dot(hn, wv_ref[...], preferred_element_type=F32))

    seq_start = i % tiles_per_seq == 0
    hist_g = jnp.where(seq_start, 0.0, car_g[j])
    hist_v = jnp.where(seq_start, 0.0, car_v[j])
    pending = up_proj(0)
    act = None
    for r in range(n_slab):
        up_g, up_v = pending
        if r + 1 < n_slab:
            pending = up_proj(r + 1)
        if act is not None:
            y_ref[slab(r - 1), :] += jnp.dot(act, wd_ref[...], preferred_element_type=F32)
        c_g = _ffn_conv(up_g, hist_g, cwg, cbg, tm=FFN_SLAB, sample_nb=None)
        c_v = _ffn_conv(up_v, hist_v, cwv, cbv, tm=FFN_SLAB, sample_nb=None)
        hist_g = up_g[FFN_SLAB - SUBLANES:, :]
        hist_v = up_v[FFN_SLAB - SUBLANES:, :]
        act = (jax.nn.gelu(c_g) * c_v).astype(BF16)
    y_ref[slab(n_slab - 1), :] += jnp.dot(act, wd_ref[...], preferred_element_type=F32)
    car_g[j] = hist_g
    car_v[j] = hist_v
    tg_ref[0] = hist_g
    tv_ref[0] = hist_v

    @pl.when(j == nj - 1)
    def _():
        for r in range(n_slab):
            y_ref[slab(r), :] = _rms(y_ref[slab(r), :], gfin_ref[...])


def _ffn_sample_body(x_ref, mr_ref, ma_ref, gna_ref, wor_ref, woa_ref, gf_ref, wg_ref, wv_ref, cwg_ref, cwv_ref,
                     cbg_ref, cbv_ref, wd_ref, gfin_ref, s0g_ref, s1g_ref, s0v_ref, s1v_ref,
                     y_ref, tg_ref, tv_ref, hn_ref, *, nb):
    j = pl.program_id(1)
    nj = pl.num_programs(1)
    tm = x_ref.shape[0]

    @pl.when(j == 0)
    def _():
        ma = _rms(ma_ref[...], gna_ref[...]).astype(BF16)
        h = (x_ref[...] + jnp.dot(mr_ref[...], wor_ref[...], preferred_element_type=F32)
             + jnp.dot(ma, woa_ref[...], preferred_element_type=F32))
        y_ref[...] = h
        hn_ref[...] = _rms(h, gf_ref[...]).astype(BF16)

    hn = hn_ref[...]
    tf = wg_ref.shape[1]
    acts = []
    for c in range(tf // FFN_CHUNK):
        cs = slice(c * FFN_CHUNK, (c + 1) * FFN_CHUNK)
        up_g = jnp.dot(hn, wg_ref[:, cs], preferred_element_type=F32)
        up_v = jnp.dot(hn, wv_ref[:, cs], preferred_element_type=F32)
        tg_ref[0, :, cs] = up_g[tm - 2 * nb:tm - nb, :]
        tg_ref[1, :, cs] = up_g[tm - nb:, :]
        tv_ref[0, :, cs] = up_v[tm - 2 * nb:tm - nb, :]
        tv_ref[1, :, cs] = up_v[tm - nb:, :]
        c_g = _ffn_conv(up_g, (s0g_ref[:, cs], s1g_ref[:, cs]), cwg_ref[:, cs], cbg_ref[:, cs],
                        tm=tm, sample_nb=nb)
        c_v = _ffn_conv(up_v, (s0v_ref[:, cs], s1v_ref[:, cs]), cwv_ref[:, cs], cbv_ref[:, cs],
                        tm=tm, sample_nb=nb)
        acts.append((jax.nn.gelu(c_g) * c_v).astype(BF16))
    act = jnp.concatenate(acts, axis=1)
    y_ref[...] += jnp.dot(act, wd_ref[...], preferred_element_type=F32)

    @pl.when(j == nj - 1)
    def _():
        y_ref[...] = _rms(y_ref[...], gfin_ref[...])


def _ffn_prompt(x, mr, ma, wor, woa, gf, wup, cw, cb, wd, gfin, batch, seq, tm, tf):
    n, d_model = x.shape
    d_half = mr.shape[-1]
    d_ff = wd.shape[0]
    ni, nj = n // tm, d_ff // tf
    tiles_per_seq = seq // tm
    row = lambda i, j: (i, 0)
    gate_col = lambda i, j: (0, j)
    val_col = lambda i, j: (0, nj + j)
    tail = lambda i, j: (i, 0, j)
    y, tg, tv = pl.pallas_call(
        functools.partial(_ffn_prompt_body, tm=tm, tiles_per_seq=tiles_per_seq),
        grid=(ni, nj),
        in_specs=[
            pl.BlockSpec(memory_space=pl.ANY),
            pl.BlockSpec((tm, d_half), row), pl.BlockSpec((tm, d_half), row),
            _const_spec(wor.shape), _const_spec(woa.shape), _const_spec(gf.shape),
            pl.BlockSpec((d_model, tf), gate_col), pl.BlockSpec((d_model, tf), val_col),
            pl.BlockSpec((cw.shape[0], tf), gate_col), pl.BlockSpec((cw.shape[0], tf), val_col),
            pl.BlockSpec((1, tf), gate_col), pl.BlockSpec((1, tf), val_col),
            pl.BlockSpec((tf, d_model), lambda i, j: (j, 0)),
            _const_spec(gfin.shape),
        ],
        out_specs=(
            pl.BlockSpec((tm, d_model), row),
            pl.BlockSpec((1, SUBLANES, tf), tail),
            pl.BlockSpec((1, SUBLANES, tf), tail),
        ),
        out_shape=(
            jax.ShapeDtypeStruct((n, d_model), F32),
            jax.ShapeDtypeStruct((ni, SUBLANES, d_ff), F32),
            jax.ShapeDtypeStruct((ni, SUBLANES, d_ff), F32),
        ),
        scratch_shapes=[
            pltpu.VMEM((tm, d_model), BF16),
            pltpu.VMEM((nj, SUBLANES, tf), F32), pltpu.VMEM((nj, SUBLANES, tf), F32),
            pltpu.SemaphoreType.DMA(()),
        ],
        compiler_params=pltpu.CompilerParams(
            dimension_semantics=("arbitrary", "arbitrary"), vmem_limit_bytes=FFN_VMEM_LIMIT_BYTES),
        name="ffn_prompt",
    )(x, mr, ma, wor, woa, gf, wup, wup, cw, cw, cb, cb, wd, gfin)
    return y, tg[tiles_per_seq - 1::tiles_per_seq], tv[tiles_per_seq - 1::tiles_per_seq]


def _ffn_sample(x, mr, ma, gna, wor, woa, gf, wup, cw, cb, wd, gfin, state2d, nb, steps, tf):
    n, d_model = x.shape
    d_half = mr.shape[-1]
    d_ff = wd.shape[0]
    nj = d_ff // tf
    full2 = lambda shape: pl.BlockSpec(shape, lambda i, j: (0, 0))
    gate_col = lambda i, j: (0, j)
    val_col = lambda i, j: (0, nj + j)
    y, tg, tv = pl.pallas_call(
        functools.partial(_ffn_sample_body, nb=nb),
        grid=(1, nj),
        in_specs=[
            full2((n, d_model)), full2((n, d_half)), full2((n, d_half)), full2(gna.shape),
            _const_spec(wor.shape), _const_spec(woa.shape), _const_spec(gf.shape),
            pl.BlockSpec((d_model, tf), gate_col), pl.BlockSpec((d_model, tf), val_col),
            pl.BlockSpec((cw.shape[0], tf), gate_col), pl.BlockSpec((cw.shape[0], tf), val_col),
            pl.BlockSpec((1, tf), gate_col), pl.BlockSpec((1, tf), val_col),
            pl.BlockSpec((tf, d_model), lambda i, j: (j, 0)),
            _const_spec(gfin.shape),
            pl.BlockSpec((nb, tf), lambda i, j: (0, j)),
            pl.BlockSpec((nb, tf), lambda i, j: (0, 2 * nj + j)),
            pl.BlockSpec((nb, tf), lambda i, j: (0, nj + j)),
            pl.BlockSpec((nb, tf), lambda i, j: (0, 3 * nj + j)),
        ],
        out_specs=(
            full2((n, d_model)),
            pl.BlockSpec((2, nb, tf), lambda i, j: (0, 0, j)),
            pl.BlockSpec((2, nb, tf), lambda i, j: (0, 0, j)),
        ),
        out_shape=(
            jax.ShapeDtypeStruct((n, d_model), F32),
            jax.ShapeDtypeStruct((2, nb, d_ff), F32),
            jax.ShapeDtypeStruct((2, nb, d_ff), F32),
        ),
        scratch_shapes=[pltpu.VMEM((n, d_model), BF16)],
        compiler_params=pltpu.CompilerParams(
            dimension_semantics=("arbitrary", "arbitrary"), vmem_limit_bytes=VMEM_LIMIT_BYTES),
        name="ffn_sample",
    )(x, mr, ma, gna, wor, woa, gf, wup, wup, cw, cw, cb, cb, wd, gfin, state2d, state2d, state2d, state2d)
    return y, tg, tv


def _perm_heads(a, axis):
    shape = a.shape
    a = a.reshape(shape[:axis] + (KV_HEADS // 2, 2, GQA_GROUP, HEAD_DIM) + shape[axis + 1:])
    return jnp.swapaxes(a, axis + 1, axis + 2).reshape(shape)


def _gate_weights(w_a, w_x):
    nblk, bs, _ = w_a.shape
    per = GATE_TILE // bs
    eye = jnp.eye(per, dtype=w_a.dtype)

    def pack(w):
        w4 = w.reshape(nblk // per, per, bs, bs)
        return jnp.einsum("cgij,gh->cgihj", w4, eye).reshape(nblk // per, GATE_TILE, GATE_TILE)
    return jnp.concatenate([pack(w_a), pack(w_x)], axis=-1).astype(BF16)


def kernel(x_prompt, x_sample, state_rnn_conv, state_rnn_h, cache_win_k, cache_win_v, state_ffn_conv,
           norm_mix_g, w_in, rnn_conv_w, rnn_conv_b, w_gate_a, b_gate_a, w_gate_x, b_gate_x, rnn_lambda,
           attn_sinks, rel_bias_table, gn_rnn_g, gn_attn_g, w_out, norm_ffn_g, w_up, ffn_conv_w,
           ffn_conv_b, w_down, norm_final_g):
    batch, seq, d_model = x_prompt.shape
    nb, steps, _ = x_sample.shape
    depth = w_in.shape[0]
    d_rnn = rnn_conv_w.shape[-1]
    d_attn = N_HEADS * HEAD_DIM
    kv_dim = KV_HEADS * HEAD_DIM
    d_ff = w_down.shape[1]
    win = cache_win_k.shape[2]
    assert depth == 1 and d_rnn + d_attn == d_model and w_in.shape[-1] == 2 * d_rnn + d_attn + 2 * kv_dim
    assert win == WINDOW and seq % WINDOW == 0 and w_gate_a.shape[1] == RNN_BLOCKS
    assert rnn_conv_w.shape[1] == 4 and ffn_conv_w.shape[1] == 3 and steps >= 3 and nb % SUBLANES == 0

    w_in0 = w_in[0]
    o_q = 2 * d_rnn
    w_rg = w_in0[:, :o_q].astype(BF16)
    w_q = _perm_heads(w_in0[:, o_q:o_q + d_attn], 1).astype(BF16)
    w_kv = w_in0[:, o_q + d_attn:].astype(BF16)
    w_out0 = w_out[0]
    w_out_r = w_out0[:d_rnn].astype(BF16)
    w_out_a = _perm_heads(w_out0[d_rnn:], 0).astype(BF16)
    gn_attn_p = _perm_heads(gn_attn_g[0], 0).reshape(1, d_attn)
    w_up_b = w_up[0].astype(BF16)
    w_down_b = w_down[0].astype(BF16)
    wg = _gate_weights(w_gate_a[0], w_gate_x[0])
    row2 = lambda a: a.reshape(1, -1)
    g_mix, g_ffn, g_fin, g_rnn = row2(norm_mix_g[0]), row2(norm_ffn_g[0]), row2(norm_final_g), row2(gn_rnn_g[0])
    cw_r, cb_r = rnn_conv_w[0], row2(rnn_conv_b[0])
    ba, bx, lam = row2(b_gate_a[0]), row2(b_gate_x[0]), row2(rnn_lambda[0])
    cw_f, cb_f = ffn_conv_w[0], row2(ffn_conv_b[0])
    tbl = rel_bias_table.reshape(-1)
    sinks = attn_sinks[0]

    blk = WINDOW
    qi = np.arange(blk)[:, None]
    kj = np.arange(2 * blk)[None, :]
    code_p = jnp.asarray(_t5_bucket_np(blk + qi - kj))
    r = np.arange(N_HEADS * steps)[:, None]
    t_r, h_r = r % steps, r // steps
    npad = 2 * SUBLANES
    code_c = jnp.asarray(_t5_bucket_np(win + t_r - np.arange(win)[None, :]) * N_HEADS + h_r)
    code_n = jnp.asarray(_t5_bucket_np(t_r - np.arange(npad)[None, :]) * N_HEADS + h_r)

    n_p = batch * seq
    tm_a = _pick(PROJ_ROWS, n_p)
    xp2 = x_prompt.reshape(n_p, d_model)
    xr, gr, q, k, v = _inproj(xp2, pl.BlockSpec((tm_a, d_model), lambda i: (i, 0)), n_p // tm_a, tm_a,
                              g_mix, w_rg, w_q, w_kv)
    m_rnn, h_last = _rglru_prompt(xr, gr, cw_r, cb_r, wg, ba, bx, lam, g_rnn, batch, seq,
                                  _pick(RGLRU_ROWS, seq))
    m_attn = _swa_prompt(q, k, v, code_p, tbl, sinks, gn_attn_p, batch, seq)
    y_p, tail_g, tail_v = _ffn_prompt(xp2, m_rnn, m_attn, w_out_r, w_out_a, g_ffn, w_up_b, cw_f, cb_f,
                                      w_down_b, g_fin, batch, seq, _pick(FFN_ROWS, seq),
                                      _pick(FFN_COLS_PROMPT, d_ff))
    y_prompt = y_p.reshape(batch, seq, d_model)
    p_rnn_conv = xr.reshape(batch, seq, d_rnn)[:, seq - 3:, :][None]
    p_rnn_h = h_last[None]
    last_win = lambda a: a.reshape(batch, seq, kv_dim)[:, seq - win:, :].reshape(
        1, batch, win, KV_HEADS, HEAD_DIM)
    p_win_k, p_win_v = last_win(k), last_win(v)
    p_ffn_conv = jnp.concatenate([tail_g[:, SUBLANES - 2:, :], tail_v[:, SUBLANES - 2:, :]], axis=-1)[None]

    n_s = nb * steps
    xs_tm = x_sample.transpose(1, 0, 2).reshape(n_s, d_model)
    xr_s, gr_s, q_s, k_s, v_s = _inproj(xs_tm, pl.BlockSpec((nb, d_model), lambda t: (t, 0)), steps, nb,
                                        g_mix, w_rg, w_q, w_kv)
    conv_tm = state_rnn_conv[0].transpose(1, 0, 2).reshape(3 * nb, d_rnn)
    m_rnn_s, h_new = _rglru_sample(xr_s, gr_s, conv_tm, state_rnn_h[0], cw_r, cb_r, wg, ba, bx, lam, g_rnn,
                                   nb, steps)
    ncg = KV_HEADS // 2
    qs = q_s.reshape(steps, nb, ncg, GQA_GROUP, LANES).transpose(1, 2, 3, 0, 4).reshape(
        nb, ncg, GQA_GROUP * steps, LANES)
    k_tm = k_s.reshape(steps, nb, kv_dim)
    v_tm = v_s.reshape(steps, nb, kv_dim)
    pad = ((0, 0), (0, npad - steps), (0, 0))
    ck_t = cache_win_k[0].transpose(0, 2, 3, 1).reshape(nb, kv_dim, win)
    cv_t = cache_win_v[0].transpose(0, 2, 3, 1).reshape(nb, kv_dim, win)
    bb = SUBLANES
    o_s, ck_new, cv_new = _swa_sample(qs, ck_t, cv_t, jnp.pad(k_tm.transpose(1, 0, 2), pad),
                                      jnp.pad(v_tm.transpose(1, 0, 2), pad), code_c, code_n, tbl, sinks, steps, bb)
    ya_s = o_s.reshape(nb, ncg, GQA_GROUP, steps, LANES).transpose(3, 0, 1, 2, 4).reshape(n_s, d_attn)
    ffn2d = state_ffn_conv[0].reshape(nb, 2 * 2 * d_ff)
    y_s, ns_g, ns_v = _ffn_sample(xs_tm, m_rnn_s, ya_s, gn_attn_p, w_out_r, w_out_a, g_ffn, w_up_b, cw_f, cb_f,
                                  w_down_b, g_fin, ffn2d, nb, steps, _pick(FFN_COLS_SAMPLE, d_ff))
    y_sample = y_s.reshape(steps, nb, d_model).transpose(1, 0, 2)
    s_rnn_conv = xr_s.reshape(steps, nb, d_rnn)[steps - 3:].transpose(1, 0, 2)[None]
    s_rnn_h = h_new[None]

    to_cache = lambda w_t: w_t.reshape(nb, KV_HEADS, HEAD_DIM, win).transpose(0, 3, 1, 2)[None]
    s_win_k, s_win_v = to_cache(ck_new), to_cache(cv_new)
    s_ffn_conv = jnp.concatenate([ns_g, ns_v], axis=-1).transpose(1, 0, 2)[None]

    return (y_prompt, y_sample, p_rnn_conv, p_rnn_h, p_win_k, p_win_v, p_ffn_conv,
            s_rnn_conv, s_rnn_h, s_win_k, s_win_v, s_ffn_conv)
```

```python
import functools
import math

import numpy as np
import jax
import jax.numpy as jnp
from jax import lax
from jax.experimental import pallas as pl
from jax.experimental.pallas import tpu as pltpu

F32 = jnp.float32
BF16 = jnp.bfloat16

HEAD_DIM = 64
KV_HEADS = 4
N_HEADS = 16
GQA_GROUP = N_HEADS // KV_HEADS
RNN_BLOCKS = 16
RG_C = 8.0
WINDOW = 128
N_BUCKETS = 32
MAX_EXACT = N_BUCKETS // 2
REL_MAX_DIST = 128
EPS = 1e-6
NEG_INF = -1e30
ATTN_SCALE = HEAD_DIM ** -0.5

LANES = 128
SUBLANES = 8
VMEM_LIMIT_BYTES = 56 * 1024 * 1024
FFN_VMEM_LIMIT_BYTES = 61 * 1024 * 1024

GATE_TILE = 256

PROJ_ROWS = (512, WINDOW)
RGLRU_ROWS = (1024, 512, WINDOW)
SWA_BLOCKS_PER_STEP = (8, 4, 2, 1)
FFN_ROWS = (1024, 512, 256)
FFN_COLS_PROMPT = (768, 512)
FFN_COLS_SAMPLE = (512,)


def _pick(options, n):
    return next(t for t in options if n % t == 0)


def _t5_bucket_np(d):
    n = np.maximum(d, 0)
    nf = np.maximum(n, 1).astype(np.float32)
    large = MAX_EXACT + (np.log(nf / MAX_EXACT) / math.log(REL_MAX_DIST / MAX_EXACT)
                         * (N_BUCKETS - MAX_EXACT)).astype(np.int32)
    large = np.minimum(large, N_BUCKETS - 1)
    return np.where(n < MAX_EXACT, n, large).astype(np.int32)


def _rms(x, g):
    ms = jnp.mean(x * x, axis=-1, keepdims=True)
    return (x * lax.rsqrt(ms + EPS)) * g


def _softplus(x):
    return jnp.maximum(x, 0.0) + jnp.log1p(jnp.exp(-jnp.abs(x)))


def _const_spec(shape):
    nd = len(shape)
    return pl.BlockSpec(shape, lambda *_: (0,) * nd, pipeline_mode=pl.Buffered(1))


def _smem_spec():
    return pl.BlockSpec(memory_space=pltpu.SMEM)


def _inproj_body(x_ref, g_ref, w_ref, xr_ref, gr_ref, q_ref, k_ref, v_ref):
    xn = _rms(x_ref[...], g_ref[...]).astype(BF16)
    d_rnn, d_attn, kv_dim = xr_ref.shape[-1], q_ref.shape[-1], k_ref.shape[-1]
    p = jnp.dot(xn, w_ref[...], preferred_element_type=F32)
    o1, o2, o3, o4 = d_rnn, 2 * d_rnn, 2 * d_rnn + d_attn, 2 * d_rnn + d_attn + kv_dim
    xr_ref[...] = p[:, :o1]
    gr_ref[...] = p[:, o1:o2]
    k_ref[...] = p[:, o3:o4]
    v_ref[...] = p[:, o4:]
    pq = p[:, o2:o3] * ATTN_SCALE
    lane = lax.broadcasted_iota(jnp.int32, (pq.shape[0], LANES), 1)
    blocks = []
    for cg in range(KV_HEADS // 2):
        for g in range(GQA_GROUP):
            ha, hb = 2 * GQA_GROUP * cg + g, 2 * GQA_GROUP * cg + GQA_GROUP + g
            a = pq[:, (ha // 2) * LANES:(ha // 2 + 1) * LANES]
            b = pq[:, (hb // 2) * LANES:(hb // 2 + 1) * LANES]
            if g % 2 == 0:
                blocks.append(jnp.where(lane < HEAD_DIM, a, pltpu.roll(b, HEAD_DIM, 1)))
            else:
                blocks.append(jnp.where(lane < HEAD_DIM, pltpu.roll(a, HEAD_DIM, 1), b))
    q_ref[...] = jnp.concatenate(blocks, axis=1).astype(BF16)


def _inproj(x2d, x_spec, n_steps, tm, g, w, d_rnn, d_attn, kv_dim):
    n = n_steps * tm
    d_model = g.shape[-1]
    row = lambda i: (i, 0)
    out_shape = (
        jax.ShapeDtypeStruct((n, d_rnn), F32),
        jax.ShapeDtypeStruct((n, d_rnn), F32),
        jax.ShapeDtypeStruct((n, d_attn), BF16),
        jax.ShapeDtypeStruct((n, kv_dim), F32),
        jax.ShapeDtypeStruct((n, kv_dim), F32),
    )
    out_specs = (
        pl.BlockSpec((tm, d_rnn), row),
        pl.BlockSpec((tm, d_rnn), row),
        pl.BlockSpec((tm, d_attn), row),
        pl.BlockSpec((tm, kv_dim), row),
        pl.BlockSpec((tm, kv_dim), row),
    )
    return pl.pallas_call(
        _inproj_body,
        grid=(n_steps,),
        in_specs=[x_spec, _const_spec((1, d_model)), _const_spec(w.shape)],
        out_specs=out_specs,
        out_shape=out_shape,
        compiler_params=pltpu.CompilerParams(
            dimension_semantics=("arbitrary",), vmem_limit_bytes=VMEM_LIMIT_BYTES),
        name="inproj",
    )(x2d, g, w)


def _rglru_gates(xc, wg_ref, ba, bx, lam):
    d_rnn = xc.shape[-1]
    pre_a, pre_x = [], []
    for c in range(d_rnn // GATE_TILE):
        xb = xc[:, c * GATE_TILE:(c + 1) * GATE_TILE].astype(BF16)
        pre = jnp.dot(xb, wg_ref[c], preferred_element_type=F32)
        pre_a.append(pre[:, :GATE_TILE])
        pre_x.append(pre[:, GATE_TILE:])
    r = jax.nn.sigmoid(jnp.concatenate(pre_a, axis=1) + ba)
    i = jax.nn.sigmoid(jnp.concatenate(pre_x, axis=1) + bx)
    log_a = (-RG_C * r) * _softplus(-lam)
    a = jnp.exp(log_a)
    one_minus_a2 = -jnp.tanh(log_a) * (a * a + 1.0)
    u = jnp.sqrt(one_minus_a2) * (i * xc)
    return a, u


def _rglru_prompt_body(xr_ref, gr_ref, cw_ref, cb_ref, wg_ref, ba_ref, bx_ref, lam_ref, gn_ref,
                       out_ref, hlast_ref, xbuf, abuf, ubuf, hbuf, hc_ref, *, tt, conv_w):
    ti = pl.program_id(1)
    d_rnn = xr_ref.shape[-1]

    @pl.when(ti == 0)
    def _():
        xbuf[0:SUBLANES, :] = jnp.zeros((SUBLANES, d_rnn), F32)
        hc_ref[...] = jnp.zeros_like(hc_ref)

    x = xr_ref[...]
    xbuf[SUBLANES:SUBLANES + tt, :] = x
    xc = cw_ref[conv_w - 1:conv_w, :] * x
    for k in range(conv_w - 1):
        off = SUBLANES - (conv_w - 1) + k
        xc = xc + cw_ref[k:k + 1, :] * xbuf[off:off + tt, :]
    xc = xc + cb_ref[...]
    xbuf[0:SUBLANES, :] = xbuf[tt:tt + SUBLANES, :]

    a, u = _rglru_gates(xc, wg_ref, ba_ref[...], bx_ref[...], lam_ref[...])
    abuf[...] = a
    ubuf[...] = u

    row = lax.broadcasted_iota(jnp.int32, (SUBLANES, d_rnn), 0)

    def group(gidx, hc):
        r0 = pl.multiple_of(gidx * SUBLANES, SUBLANES)
        ag = abuf[pl.ds(r0, SUBLANES), :]
        ug = ubuf[pl.ds(r0, SUBLANES), :]
        for k in (1, 2, 4):
            a_prev = jnp.where(row >= k, pltpu.roll(ag, k, 0), 1.0)
            u_prev = jnp.where(row >= k, pltpu.roll(ug, k, 0), 0.0)
            ug = ag * u_prev + ug
            ag = ag * a_prev
        h = ag * hc + ug
        hbuf[pl.ds(r0, SUBLANES), :] = h
        return jnp.broadcast_to(h[SUBLANES - 1:SUBLANES, :], (SUBLANES, d_rnn))

    hc = lax.fori_loop(0, tt // SUBLANES, group, hc_ref[...], unroll=2)
    hc_ref[...] = hc
    hlast_ref[0] = hc

    y = jax.nn.gelu(gr_ref[...]) * hbuf[...]
    out_ref[...] = _rms(y, gn_ref[...]).astype(BF16)


def _rglru_prompt(xr, gr, cw, cb, wg, ba, bx, lam, gn, batch, seq, tt):
    d_rnn = xr.shape[-1]
    nt = seq // tt
    conv_w = cw.shape[0]
    tile = lambda b, t: (b * nt + t, 0)
    out, hlast = pl.pallas_call(
        functools.partial(_rglru_prompt_body, tt=tt, conv_w=conv_w),
        grid=(batch, nt),
        in_specs=[
            pl.BlockSpec((tt, d_rnn), tile),
            pl.BlockSpec((tt, d_rnn), tile),
            _const_spec(cw.shape), _const_spec(cb.shape), _const_spec(wg.shape),
            _const_spec(ba.shape), _const_spec(bx.shape), _const_spec(lam.shape), _const_spec(gn.shape),
        ],
        out_specs=(
            pl.BlockSpec((tt, d_rnn), tile),
            pl.BlockSpec((1, SUBLANES, d_rnn), lambda b, t: (b, 0, 0)),
        ),
        out_shape=(
            jax.ShapeDtypeStruct((batch * seq, d_rnn), BF16),
            jax.ShapeDtypeStruct((batch, SUBLANES, d_rnn), F32),
        ),
        scratch_shapes=[
            pltpu.VMEM((tt + SUBLANES, d_rnn), F32),
            pltpu.VMEM((tt, d_rnn), F32),
            pltpu.VMEM((tt, d_rnn), F32),
            pltpu.VMEM((tt, d_rnn), F32),
            pltpu.VMEM((SUBLANES, d_rnn), F32),
        ],
        compiler_params=pltpu.CompilerParams(
            dimension_semantics=("arbitrary", "arbitrary"), vmem_limit_bytes=VMEM_LIMIT_BYTES),
        name="rglru_prompt",
    )(xr, gr, cw, cb, wg, ba, bx, lam, gn)
    return out, hlast[:, 0, :]


def _rglru_sample_body(xr_ref, gr_ref, c0_ref, c1_ref, c2_ref, h0_ref, cw_ref, cb_ref, wg_ref,
                       ba_ref, bx_ref, lam_ref, gn_ref, out_ref, hnew_ref, *, nb, steps):
    x = xr_ref[...]
    hist = [c0_ref[...], c1_ref[...], c2_ref[...]]
    xs = [x[t * nb:(t + 1) * nb, :] for t in range(steps)]
    xp = hist + xs
    conv_w = len(hist) + 1
    xc = cw_ref[conv_w - 1:conv_w, :] * x
    for k in range(conv_w - 1):
        shifted = jnp.concatenate(xp[k:k + steps], axis=0)
        xc = xc + cw_ref[k:k + 1, :] * shifted
    xc = xc + cb_ref[...]
    a, u = _rglru_gates(xc, wg_ref, ba_ref[...], bx_ref[...], lam_ref[...])
    h = h0_ref[...]
    hs = []
    for t in range(steps):
        h = a[t * nb:(t + 1) * nb, :] * h + u[t * nb:(t + 1) * nb, :]
        hs.append(h)
    hnew_ref[...] = h
    y = jax.nn.gelu(gr_ref[...]) * jnp.concatenate(hs, axis=0)
    out_ref[...] = _rms(y, gn_ref[...]).astype(BF16)


def _rglru_sample(xr, gr, conv_state2d, h0, cw, cb, wg, ba, bx, lam, gn, nb, steps):
    d_rnn = xr.shape[-1]
    n = nb * steps
    assert cw.shape[0] == 4
    full = lambda shape: pl.BlockSpec(shape, lambda i: (0,) * len(shape))
    return pl.pallas_call(
        functools.partial(_rglru_sample_body, nb=nb, steps=steps),
        grid=(1,),
        in_specs=[
            full((n, d_rnn)), full((n, d_rnn)),
            pl.BlockSpec((nb, d_rnn), lambda i: (0, 0)),
            pl.BlockSpec((nb, d_rnn), lambda i: (1, 0)),
            pl.BlockSpec((nb, d_rnn), lambda i: (2, 0)),
            full((nb, d_rnn)),
            full(cw.shape), full(cb.shape), full(wg.shape), full(ba.shape), full(bx.shape),
            full(lam.shape), full(gn.shape),
        ],
        out_specs=(full((n, d_rnn)), full((nb, d_rnn))),
        out_shape=(jax.ShapeDtypeStruct((n, d_rnn), BF16), jax.ShapeDtypeStruct((nb, d_rnn), F32)),
        compiler_params=pltpu.CompilerParams(
            dimension_semantics=("arbitrary",), vmem_limit_bytes=VMEM_LIMIT_BYTES),
        name="rglru_sample",
    )(xr, gr, conv_state2d, conv_state2d, conv_state2d, h0, cw, cb, wg, ba, bx, lam, gn)


def _swa_prompt_body(code_ref, tbl_ref, sink_ref, q_ref, kp_ref, kc_ref, vp_ref, vc_ref, gn_ref,
                     out_ref, bias_ref):
    j = pl.program_id(1)
    blk = WINDOW
    rows = GQA_GROUP * blk

    @pl.when((pl.program_id(0) == 0) & (j == 0))
    def _():
        code = code_ref[...]
        qi = lax.broadcasted_iota(jnp.int32, code.shape, 0)
        kj = lax.broadcasted_iota(jnp.int32, code.shape, 1)
        dist = blk + qi - kj
        in_window = (dist & -WINDOW) == 0
        for h in range(N_HEADS):
            def pick(b, acc, h=h):
                return jnp.where(code == b, tbl_ref[b * N_HEADS + h], acc)
            bias_h = lax.fori_loop(0, N_BUCKETS, pick, jnp.zeros(code.shape, F32))
            bias_h = jnp.where(in_window, bias_h, NEG_INF)
            kv, g = divmod(h, GQA_GROUP)
            bias_ref[1, kv, g * blk:(g + 1) * blk, :] = bias_h
            bias_ref[0, kv, g * blk:(g + 1) * blk, :] = jnp.where(kj >= blk, bias_h, NEG_INF)

    lane = lax.broadcasted_iota(jnp.int32, (2 * blk, LANES), 1)
    row_g = lax.broadcasted_iota(jnp.int32, (rows, 1), 0) // blk
    for sub in range(q_ref.shape[0] // blk):
        rows_q = slice(sub * blk, (sub + 1) * blk)
        if sub == 0:
            has_prev = jnp.where(j > 0, 1, 0)
            k_prev, v_prev = kp_ref[...], vp_ref[...]
        else:
            has_prev = 1
            k_prev, v_prev = kc_ref[(sub - 1) * blk:sub * blk, :], vc_ref[(sub - 1) * blk:sub * blk, :]
        kband = jnp.concatenate([k_prev, kc_ref[rows_q, :]], axis=0)
        vband = jnp.concatenate([v_prev, vc_ref[rows_q, :]], axis=0)
        y = _swa_block(q_ref[rows_q, :], kband, vband, bias_ref, has_prev, sink_ref, lane, row_g)
        out_ref[rows_q, :] = _rms(y, gn_ref[...]).astype(BF16)


def _swa_block(q, kband, vband, bias_ref, has_prev, sink_ref, lane, row_g):
    blk = q.shape[0]
    rows = GQA_GROUP * blk
    outs = []
    for cg in range(KV_HEADS // 2):
        qs = jnp.concatenate(
            [q[:, (cg * GQA_GROUP + g) * LANES:(cg * GQA_GROUP + g + 1) * LANES] for g in range(GQA_GROUP)],
            axis=0)
        ka = kband[:, cg * LANES:(cg + 1) * LANES]
        va = vband[:, cg * LANES:(cg + 1) * LANES]
        o = None
        for par in range(2):
            kv = 2 * cg + par
            half = (lane < HEAD_DIM) if par == 0 else (lane >= HEAD_DIM)
            km = jnp.where(half, ka, 0.0).astype(BF16)
            vm = jnp.where(half, va, 0.0).astype(BF16)
            s = lax.dot_general(qs, km, (((1,), (1,)), ((), ())), preferred_element_type=F32)
            s = s + bias_ref[has_prev, kv]
            sink = jnp.zeros((rows, 1), F32)
            for g in range(GQA_GROUP):
                sink = jnp.where(row_g == g, sink_ref[kv * GQA_GROUP + g], sink)
            m = jnp.maximum(jnp.max(s, axis=-1, keepdims=True), sink)
            p = jnp.exp(s - m)
            denom = jnp.sum(p, axis=-1, keepdims=True) + jnp.exp(sink - m)
            part = jnp.dot(p.astype(BF16), vm, preferred_element_type=F32) * (1.0 / denom)
            o = part if o is None else o + part
        outs.extend(o[g * blk:(g + 1) * blk, :] for g in range(GQA_GROUP))
    return jnp.concatenate(outs, axis=1)


def _swa_prompt(q, k, v, code, tbl, sinks, gn, batch, seq):
    blk = WINDOW
    per_step = _pick(SWA_BLOCKS_PER_STEP, seq // blk)
    nb = seq // (per_step * blk)
    d_attn = q.shape[-1]
    kv_dim = k.shape[-1]
    cur = lambda b, j: (b * nb + j, 0)
    prev = lambda b, j: (per_step * (b * nb + j) - jnp.where(j > 0, 1, 0), 0)
    return pl.pallas_call(
        _swa_prompt_body,
        grid=(batch, nb),
        in_specs=[
            _const_spec(code.shape), _smem_spec(), _smem_spec(),
            pl.BlockSpec((per_step * blk, d_attn), cur),
            pl.BlockSpec((blk, kv_dim), prev), pl.BlockSpec((per_step * blk, kv_dim), cur),
            pl.BlockSpec((blk, kv_dim), prev), pl.BlockSpec((per_step * blk, kv_dim), cur),
            _const_spec(gn.shape),
        ],
        out_specs=pl.BlockSpec((per_step * blk, d_attn), cur),
        out_shape=jax.ShapeDtypeStruct((batch * seq, d_attn), BF16),
        scratch_shapes=[pltpu.VMEM((2, KV_HEADS, GQA_GROUP * blk, 2 * blk), F32)],
        compiler_params=pltpu.CompilerParams(
            dimension_semantics=("arbitrary", "arbitrary"), vmem_limit_bytes=VMEM_LIMIT_BYTES),
        name="swa_prompt",
    )(code, tbl, sinks, q, k, k, v, v, gn)


def _slide_window(win_ref, new_ref, out_ref, steps):
    bb, kv_dim, win = win_ref.shape
    npad = new_ref.shape[1]
    new_t = new_ref[...].reshape(bb * npad, kv_dim).T
    lane = lax.broadcasted_iota(jnp.int32, (kv_dim, win), 1)
    for b in range(bb):
        placed = pltpu.roll(new_t, (win - steps - b * npad) % win, 1)
        shifted = pltpu.roll(win_ref[b], win - steps, 1)
        out_ref[b] = jnp.where(lane >= win - steps, placed, shifted)


def _swa_sample_body(codec_ref, coden_ref, tbl_ref, sink_ref, q_ref, ck_ref, cv_ref, kn_ref, vn_ref,
                     out_ref, ck_out_ref, cv_out_ref, biasc_ref, biasn_ref, sinkc_ref, *, steps):
    bb = q_ref.shape[0]
    nrow = N_HEADS * steps
    win = ck_ref.shape[2]
    npad = kn_ref.shape[1]

    @pl.when(pl.program_id(0) == 0)
    def _():
        codec = codec_ref[...]
        coden = coden_ref[...]
        hrow = lax.broadcasted_iota(jnp.int32, (nrow, 1), 0) // steps

        def pick(idx, accs):
            ac, an = accs
            val = tbl_ref[idx]
            return jnp.where(codec == idx, val, ac), jnp.where(coden == idx, val, an)
        bc, bn = lax.fori_loop(0, N_BUCKETS * N_HEADS, pick,
                               (jnp.zeros(codec.shape, F32), jnp.zeros(coden.shape, F32)))
        biasc_ref[...] = bc
        biasn_ref[...] = bn

        def pick_sink(h, acc):
            return jnp.where(hrow == h, sink_ref[h], acc)
        sinkc_ref[...] = lax.fori_loop(0, N_HEADS, pick_sink, jnp.zeros((nrow, 1), F32))

    q = q_ref[...].astype(F32)
    lane = lax.broadcasted_iota(jnp.int32, (bb, GQA_GROUP * steps, LANES), 2)
    zeros = jnp.zeros((bb, GQA_GROUP * steps, LANES), F32)
    pieces = []
    for cg in range(KV_HEADS // 2):
        for par in range(2):
            half = (lane < HEAD_DIM) if par == 0 else (lane >= HEAD_DIM)
            qm = jnp.where(half, q[:, cg], zeros)
            pieces.append(jnp.concatenate([qm, zeros] if cg == 0 else [zeros, qm], axis=2))
    qm = jnp.concatenate(pieces, axis=1).astype(BF16)

    ck = ck_ref[...].astype(BF16)
    kn = kn_ref[...].astype(BF16)
    s_c = jnp.einsum("bqd,bdk->bqk", qm, ck, preferred_element_type=F32)
    s_n = jnp.einsum("bqd,bkd->bqk", qm, kn, preferred_element_type=F32)

    t_c = lax.broadcasted_iota(jnp.int32, (nrow, win), 0) % steps
    k_c = lax.broadcasted_iota(jnp.int32, (nrow, win), 1)
    valid_c = k_c > t_c
    t_n = lax.broadcasted_iota(jnp.int32, (nrow, npad), 0) % steps
    k_n = lax.broadcasted_iota(jnp.int32, (nrow, npad), 1)
    valid_n = k_n <= t_n

    s_c = jnp.where(valid_c[None], s_c + biasc_ref[...][None], NEG_INF)
    s_n = jnp.where(valid_n[None], s_n + biasn_ref[...][None], NEG_INF)
    sink = sinkc_ref[...][None]
    m = jnp.maximum(jnp.maximum(jnp.max(s_c, axis=-1, keepdims=True),
                                jnp.max(s_n, axis=-1, keepdims=True)), sink)
    p_c = jnp.exp(s_c - m)
    p_n = jnp.exp(s_n - m)
    denom = (jnp.sum(p_c, axis=-1, keepdims=True) + jnp.sum(p_n, axis=-1, keepdims=True)
             + jnp.exp(sink - m))
    r = 1.0 / denom
    w_c = (p_c * r).astype(BF16)
    w_n = (p_n * r).astype(BF16)
    o = (jnp.einsum("bqk,bdk->bqd", w_c, cv_ref[...].astype(BF16), preferred_element_type=F32)
         + jnp.einsum("bqk,bkd->bqd", w_n, vn_ref[...].astype(BF16), preferred_element_type=F32))
    gt = GQA_GROUP * steps
    lane_o = lax.broadcasted_iota(jnp.int32, (bb, gt, LANES), 2)
    for cg in range(KV_HEADS // 2):
        lo = o[:, cg * 2 * gt:cg * 2 * gt + gt, cg * LANES:(cg + 1) * LANES]
        hi = o[:, cg * 2 * gt + gt:(cg + 1) * 2 * gt, cg * LANES:(cg + 1) * LANES]
        out_ref[:, cg] = jnp.where(lane_o < HEAD_DIM, lo, hi)

    _slide_window(ck_ref, kn_ref, ck_out_ref, steps)
    _slide_window(cv_ref, vn_ref, cv_out_ref, steps)


def _swa_sample(qs, ck, cv, kn, vn, codec, coden, tbl, sinks, steps, bb):
    nbatch, ncg, gt, _ = qs.shape
    kv_dim, win = ck.shape[1], ck.shape[2]
    npad = kn.shape[1]
    nrow = N_HEADS * steps
    assert bb * npad == win
    blk4 = lambda i: (i, 0, 0, 0)
    blk3 = lambda i: (i, 0, 0)
    return pl.pallas_call(
        functools.partial(_swa_sample_body, steps=steps),
        grid=(nbatch // bb,),
        in_specs=[
            _const_spec(codec.shape), _const_spec(coden.shape), _smem_spec(), _smem_spec(),
            pl.BlockSpec((bb, ncg, gt, LANES), blk4),
            pl.BlockSpec((bb, kv_dim, win), blk3), pl.BlockSpec((bb, kv_dim, win), blk3),
            pl.BlockSpec((bb, npad, kv_dim), blk3), pl.BlockSpec((bb, npad, kv_dim), blk3),
        ],
        out_specs=(pl.BlockSpec((bb, ncg, gt, LANES), blk4),
                   pl.BlockSpec((bb, kv_dim, win), blk3), pl.BlockSpec((bb, kv_dim, win), blk3)),
        out_shape=(jax.ShapeDtypeStruct((nbatch, ncg, gt, LANES), F32),
                   jax.ShapeDtypeStruct((nbatch, kv_dim, win), F32),
                   jax.ShapeDtypeStruct((nbatch, kv_dim, win), F32)),
        scratch_shapes=[pltpu.VMEM((nrow, win), F32), pltpu.VMEM((nrow, npad), F32),
                        pltpu.VMEM((nrow, 1), F32)],
        compiler_params=pltpu.CompilerParams(
            dimension_semantics=("arbitrary",), vmem_limit_bytes=VMEM_LIMIT_BYTES),
        name="swa_sample",
    )(codec, coden, tbl, sinks, qs, ck, cv, kn, vn)


def _shift_rows(up, hist, k):
    rolled = pltpu.roll(up, k, 0)
    row = lax.broadcasted_iota(jnp.int32, hist.shape, 0)
    head = jnp.where(row < k, pltpu.roll(hist, k, 0), rolled[0:SUBLANES, :])
    return jnp.concatenate([head, rolled[SUBLANES:, :]], axis=0)


def _ffn_conv(up, hist, cw, cb, *, tm, sample_nb):
    if sample_nb is None:
        prev2 = _shift_rows(up, hist, 2)
        prev1 = _shift_rows(up, hist, 1)
    else:
        s0, s1 = hist
        prev2 = jnp.concatenate([s0, s1, up[:tm - 2 * sample_nb, :]], axis=0)
        prev1 = jnp.concatenate([s1, up[:tm - sample_nb, :]], axis=0)
    return cw[0:1, :] * prev2 + cw[1:2, :] * prev1 + cw[2:3, :] * up + cb


FFN_CHUNK = 256
FFN_SLAB = 256


def _ffn_prompt_body(x_hbm, mr_ref, ma_ref, wor_ref, woa_ref, gf_ref, wg_ref, wv_ref, cwg_ref, cwv_ref, cbg_ref,
                     cbv_ref, wd_ref, gfin_ref, y_ref, tg_ref, tv_ref, hn_ref, car_g, car_v, x_sem,
                     *, tm, tiles_per_seq):
    i = pl.program_id(0)
    j = pl.program_id(1)
    nj = pl.num_programs(1)
    n_slab = tm // FFN_SLAB
    slab = lambda r: slice(r * FFN_SLAB, (r + 1) * FFN_SLAB)

    @pl.when(j == 0)
    def _():
        x_copy = pltpu.make_async_copy(x_hbm.at[pl.ds(pl.multiple_of(i * tm, tm), tm), :], y_ref, x_sem)
        x_copy.start()

        def out_proj(r):
            return (jnp.dot(mr_ref[slab(r), :], wor_ref[...], preferred_element_type=F32)
                    + jnp.dot(ma_ref[slab(r), :], woa_ref[...], preferred_element_type=F32))

        pending = out_proj(0)
        x_copy.wait()
        for r in range(n_slab):
            d = pending
            if r + 1 < n_slab:
                pending = out_proj(r + 1)
            h = y_ref[slab(r), :] + d
            y_ref[slab(r), :] = h
            hn_ref[slab(r), :] = _rms(h, gf_ref[...]).astype(BF16)

    @pl.when((i == 0) & (j == 0))
    def _():
        car_g[...] = jnp.zeros_like(car_g)
        car_v[...] = jnp.zeros_like(car_v)

    cwg, cwv, cbg, cbv = cwg_ref[...], cwv_ref[...], cbg_ref[...], cbv_ref[...]

    def up_proj(r):
        hn = hn_ref[slab(r), :]
        return (jnp.dot(hn, wg_ref[...], preferred_element_type=F32),
                jnp.dot(hn, wv_ref[...], preferred_element_type=F32))

    seq_start = i % tiles_per_seq == 0
    hist_g = jnp.where(seq_start, 0.0, car_g[j])
    hist_v = jnp.where(seq_start, 0.0, car_v[j])
    pending = up_proj(0)
    act = None
    for r in range(n_slab):
        up_g, up_v = pending
        if r + 1 < n_slab:
            pending = up_proj(r + 1)
        if act is not None:
            y_ref[slab(r - 1), :] += jnp.dot(act, wd_ref[...], preferred_element_type=F32)
        c_g = _ffn_conv(up_g, hist_g, cwg, cbg, tm=FFN_SLAB, sample_nb=None)
        c_v = _ffn_conv(up_v, hist_v, cwv, cbv, tm=FFN_SLAB, sample_nb=None)
        hist_g = up_g[FFN_SLAB - SUBLANES:, :]
        hist_v = up_v[FFN_SLAB - SUBLANES:, :]
        act = (jax.nn.gelu(c_g) * c_v).astype(BF16)
    y_ref[slab(n_slab - 1), :] += jnp.dot(act, wd_ref[...], preferred_element_type=F32)
    car_g[j] = hist_g
    car_v[j] = hist_v
    tg_ref[0] = hist_g
    tv_ref[0] = hist_v

    @pl.when(j == nj - 1)
    def _():
        for r in range(n_slab):
            y_ref[slab(r), :] = _rms(y_ref[slab(r), :], gfin_ref[...])


def _ffn_sample_body(x_ref, mr_ref, ma_ref, gna_ref, wor_ref, woa_ref, gf_ref, wg_ref, wv_ref, cwg_ref, cwv_ref,
                     cbg_ref, cbv_ref, wd_ref, gfin_ref, s0g_ref, s1g_ref, s0v_ref, s1v_ref,
                     y_ref, tg_ref, tv_ref, hn_ref, *, nb):
    j = pl.program_id(1)
    nj = pl.num_programs(1)
    tm = x_ref.shape[0]

    @pl.when(j == 0)
    def _():
        ma = _rms(ma_ref[...], gna_ref[...]).astype(BF16)
        h = (x_ref[...] + jnp.dot(mr_ref[...], wor_ref[...], preferred_element_type=F32)
             + jnp.dot(ma, woa_ref[...], preferred_element_type=F32))
        y_ref[...] = h
        hn_ref[...] = _rms(h, gf_ref[...]).astype(BF16)

    hn = hn_ref[...]
    tf = wg_ref.shape[1]
    acts = []
    for c in range(tf // FFN_CHUNK):
        cs = slice(c * FFN_CHUNK, (c + 1) * FFN_CHUNK)
        up_g = jnp.dot(hn, wg_ref[:, cs], preferred_element_type=F32)
        up_v = jnp.dot(hn, wv_ref[:, cs], preferred_element_type=F32)
        tg_ref[0, :, cs] = up_g[tm - 2 * nb:tm - nb, :]
        tg_ref[1, :, cs] = up_g[tm - nb:, :]
        tv_ref[0, :, cs] = up_v[tm - 2 * nb:tm - nb, :]
        tv_ref[1, :, cs] = up_v[tm - nb:, :]
        c_g = _ffn_conv(up_g, (s0g_ref[:, cs], s1g_ref[:, cs]), cwg_ref[:, cs], cbg_ref[:, cs],
                        tm=tm, sample_nb=nb)
        c_v = _ffn_conv(up_v, (s0v_ref[:, cs], s1v_ref[:, cs]), cwv_ref[:, cs], cbv_ref[:, cs],
                        tm=tm, sample_nb=nb)
        acts.append((jax.nn.gelu(c_g) * c_v).astype(BF16))
    act = jnp.concatenate(acts, axis=1)
    y_ref[...] += jnp.dot(act, wd_ref[...], preferred_element_type=F32)

    @pl.when(j == nj - 1)
    def _():
        y_ref[...] = _rms(y_ref[...], gfin_ref[...])


def _ffn_prompt(x, mr, ma, wor, woa, gf, wup, cw, cb, wd, gfin, batch, seq, tm, tf):
    n, d_model = x.shape
    d_half = mr.shape[-1]
    d_ff = wd.shape[0]
    ni, nj = n // tm, d_ff // tf
    tiles_per_seq = seq // tm
    row = lambda i, j: (i, 0)
    gate_col = lambda i, j: (0, j)
    val_col = lambda i, j: (0, nj + j)
    tail = lambda i, j: (i, 0, j)
    y, tg, tv = pl.pallas_call(
        functools.partial(_ffn_prompt_body, tm=tm, tiles_per_seq=tiles_per_seq),
        grid=(ni, nj),
        in_specs=[
            pl.BlockSpec(memory_space=pl.ANY),
            pl.BlockSpec((tm, d_half), row), pl.BlockSpec((tm, d_half), row),
            _const_spec(wor.shape), _const_spec(woa.shape), _const_spec(gf.shape),
            pl.BlockSpec((d_model, tf), gate_col), pl.BlockSpec((d_model, tf), val_col),
            pl.BlockSpec((cw.shape[0], tf), gate_col), pl.BlockSpec((cw.shape[0], tf), val_col),
            pl.BlockSpec((1, tf), gate_col), pl.BlockSpec((1, tf), val_col),
            pl.BlockSpec((tf, d_model), lambda i, j: (j, 0)),
            _const_spec(gfin.shape),
        ],
        out_specs=(
            pl.BlockSpec((tm, d_model), row),
            pl.BlockSpec((1, SUBLANES, tf), tail),
            pl.BlockSpec((1, SUBLANES, tf), tail),
        ),
        out_shape=(
            jax.ShapeDtypeStruct((n, d_model), F32),
            jax.ShapeDtypeStruct((ni, SUBLANES, d_ff), F32),
            jax.ShapeDtypeStruct((ni, SUBLANES, d_ff), F32),
        ),
        scratch_shapes=[
            pltpu.VMEM((tm, d_model), BF16),
            pltpu.VMEM((nj, SUBLANES, tf), F32), pltpu.VMEM((nj, SUBLANES, tf), F32),
            pltpu.SemaphoreType.DMA(()),
        ],
        compiler_params=pltpu.CompilerParams(
            dimension_semantics=("arbitrary", "arbitrary"), vmem_limit_bytes=FFN_VMEM_LIMIT_BYTES),
        name="ffn_prompt",
    )(x, mr, ma, wor, woa, gf, wup, wup, cw, cw, cb, cb, wd, gfin)
    return y, tg[tiles_per_seq - 1::tiles_per_seq], tv[tiles_per_seq - 1::tiles_per_seq]


def _ffn_sample(x, mr, ma, gna, wor, woa, gf, wup, cw, cb, wd, gfin, state2d, nb, steps, tf):
    n, d_model = x.shape
    d_half = mr.shape[-1]
    d_ff = wd.shape[0]
    nj = d_ff // tf
    full2 = lambda shape: pl.BlockSpec(shape, lambda i, j: (0, 0))
    gate_col = lambda i, j: (0, j)
    val_col = lambda i, j: (0, nj + j)
    y, tg, tv = pl.pallas_call(
        functools.partial(_ffn_sample_body, nb=nb),
        grid=(1, nj),
        in_specs=[
            full2((n, d_model)), full2((n, d_half)), full2((n, d_half)), full2(gna.shape),
            _const_spec(wor.shape), _const_spec(woa.shape), _const_spec(gf.shape),
            pl.BlockSpec((d_model, tf), gate_col), pl.BlockSpec((d_model, tf), val_col),
            pl.BlockSpec((cw.shape[0], tf), gate_col), pl.BlockSpec((cw.shape[0], tf), val_col),
            pl.BlockSpec((1, tf), gate_col), pl.BlockSpec((1, tf), val_col),
            pl.BlockSpec((tf, d_model), lambda i, j: (j, 0)),
            _const_spec(gfin.shape),
            pl.BlockSpec((nb, tf), lambda i, j: (0, j)),
            pl.BlockSpec((nb, tf), lambda i, j: (0, 2 * nj + j)),
            pl.BlockSpec((nb, tf), lambda i, j: (0, nj + j)),
            pl.BlockSpec((nb, tf), lambda i, j: (0, 3 * nj + j)),
        ],
        out_specs=(
            full2((n, d_model)),
            pl.BlockSpec((2, nb, tf), lambda i, j: (0, 0, j)),
            pl.BlockSpec((2, nb, tf), lambda i, j: (0, 0, j)),
        ),
        out_shape=(
            jax.ShapeDtypeStruct((n, d_model), F32),
            jax.ShapeDtypeStruct((2, nb, d_ff), F32),
            jax.ShapeDtypeStruct((2, nb, d_ff), F32),
        ),
        scratch_shapes=[pltpu.VMEM((n, d_model), BF16)],
        compiler_params=pltpu.CompilerParams(
            dimension_semantics=("arbitrary", "arbitrary"), vmem_limit_bytes=VMEM_LIMIT_BYTES),
        name="ffn_sample",
    )(x, mr, ma, gna, wor, woa, gf, wup, wup, cw, cw, cb, cb, wd, gfin, state2d, state2d, state2d, state2d)
    return y, tg, tv


def _perm_heads(a, axis):
    shape = a.shape
    a = a.reshape(shape[:axis] + (KV_HEADS // 2, 2, GQA_GROUP, HEAD_DIM) + shape[axis + 1:])
    return jnp.swapaxes(a, axis + 1, axis + 2).reshape(shape)


def _gate_weights(w_a, w_x):
    nblk, bs, _ = w_a.shape
    per = GATE_TILE // bs
    eye = jnp.eye(per, dtype=w_a.dtype)

    def pack(w):
        w4 = w.reshape(nblk // per, per, bs, bs)
        return jnp.einsum("cgij,gh->cgihj", w4, eye).reshape(nblk // per, GATE_TILE, GATE_TILE)
    return jnp.concatenate([pack(w_a), pack(w_x)], axis=-1).astype(BF16)


def kernel(x_prompt, x_sample, state_rnn_conv, state_rnn_h, cache_win_k, cache_win_v, state_ffn_conv,
           norm_mix_g, w_in, rnn_conv_w, rnn_conv_b, w_gate_a, b_gate_a, w_gate_x, b_gate_x, rnn_lambda,
           attn_sinks, rel_bias_table, gn_rnn_g, gn_attn_g, w_out, norm_ffn_g, w_up, ffn_conv_w,
           ffn_conv_b, w_down, norm_final_g):
    batch, seq, d_model = x_prompt.shape
    nb, steps, _ = x_sample.shape
    depth = w_in.shape[0]
    d_rnn = rnn_conv_w.shape[-1]
    d_attn = N_HEADS * HEAD_DIM
    kv_dim = KV_HEADS * HEAD_DIM
    d_ff = w_down.shape[1]
    win = cache_win_k.shape[2]
    assert depth == 1 and d_rnn + d_attn == d_model and w_in.shape[-1] == 2 * d_rnn + d_attn + 2 * kv_dim
    assert win == WINDOW and seq % WINDOW == 0 and w_gate_a.shape[1] == RNN_BLOCKS
    assert rnn_conv_w.shape[1] == 4 and ffn_conv_w.shape[1] == 3 and steps >= 3 and nb % SUBLANES == 0

    w_in_b = w_in[0].astype(BF16)
    w_out0 = w_out[0]
    w_out_r = w_out0[:d_rnn].astype(BF16)
    w_out_a = _perm_heads(w_out0[d_rnn:], 0).astype(BF16)
    gn_attn_p = _perm_heads(gn_attn_g[0], 0).reshape(1, d_attn)
    w_up_b = w_up[0].astype(BF16)
    w_down_b = w_down[0].astype(BF16)
    wg = _gate_weights(w_gate_a[0], w_gate_x[0])
    row2 = lambda a: a.reshape(1, -1)
    g_mix, g_ffn, g_fin, g_rnn = row2(norm_mix_g[0]), row2(norm_ffn_g[0]), row2(norm_final_g), row2(gn_rnn_g[0])
    cw_r, cb_r = rnn_conv_w[0], row2(rnn_conv_b[0])
    ba, bx, lam = row2(b_gate_a[0]), row2(b_gate_x[0]), row2(rnn_lambda[0])
    cw_f, cb_f = ffn_conv_w[0], row2(ffn_conv_b[0])
    tbl = rel_bias_table.reshape(-1)
    sinks = attn_sinks[0]

    blk = WINDOW
    qi = np.arange(blk)[:, None]
    kj = np.arange(2 * blk)[None, :]
    code_p = jnp.asarray(_t5_bucket_np(blk + qi - kj))
    r = np.arange(N_HEADS * steps)[:, None]
    t_r, h_r = r % steps, r // steps
    npad = 2 * SUBLANES
    code_c = jnp.asarray(_t5_bucket_np(win + t_r - np.arange(win)[None, :]) * N_HEADS + h_r)
    code_n = jnp.asarray(_t5_bucket_np(t_r - np.arange(npad)[None, :]) * N_HEADS + h_r)

    n_p = batch * seq
    tm_a = _pick(PROJ_ROWS, n_p)
    xp2 = x_prompt.reshape(n_p, d_model)
    xr, gr, q, k, v = _inproj(xp2, pl.BlockSpec((tm_a, d_model), lambda i: (i, 0)), n_p // tm_a, tm_a,
                              g_mix, w_in_b, d_rnn, d_attn, kv_dim)
    m_rnn, h_last = _rglru_prompt(xr, gr, cw_r, cb_r, wg, ba, bx, lam, g_rnn, batch, seq,
                                  _pick(RGLRU_ROWS, seq))
    m_attn = _swa_prompt(q, k, v, code_p, tbl, sinks, gn_attn_p, batch, seq)
    y_p, tail_g, tail_v = _ffn_prompt(xp2, m_rnn, m_attn, w_out_r, w_out_a, g_ffn, w_up_b, cw_f, cb_f,
                                      w_down_b, g_fin, batch, seq, _pick(FFN_ROWS, seq),
                                      _pick(FFN_COLS_PROMPT, d_ff))
    y_prompt = y_p.reshape(batch, seq, d_model)
    p_rnn_conv = xr.reshape(batch, seq, d_rnn)[:, seq - 3:, :][None]
    p_rnn_h = h_last[None]
    last_win = lambda a: a.reshape(batch, seq, kv_dim)[:, seq - win:, :].reshape(
        1, batch, win, KV_HEADS, HEAD_DIM)
    p_win_k, p_win_v = last_win(k), last_win(v)
    p_ffn_conv = jnp.concatenate([tail_g[:, SUBLANES - 2:, :], tail_v[:, SUBLANES - 2:, :]], axis=-1)[None]

    n_s = nb * steps
    xs_tm = x_sample.transpose(1, 0, 2).reshape(n_s, d_model)
    xr_s, gr_s, q_s, k_s, v_s = _inproj(xs_tm, pl.BlockSpec((nb, d_model), lambda t: (t, 0)), steps, nb,
                                        g_mix, w_in_b, d_rnn, d_attn, kv_dim)
    conv_tm = state_rnn_conv[0].transpose(1, 0, 2).reshape(3 * nb, d_rnn)
    m_rnn_s, h_new = _rglru_sample(xr_s, gr_s, conv_tm, state_rnn_h[0], cw_r, cb_r, wg, ba, bx, lam, g_rnn,
                                   nb, steps)
    ncg = KV_HEADS // 2
    qs = q_s.reshape(steps, nb, ncg, GQA_GROUP, LANES).transpose(1, 2, 3, 0, 4).reshape(
        nb, ncg, GQA_GROUP * steps, LANES)
    k_tm = k_s.reshape(steps, nb, kv_dim)
    v_tm = v_s.reshape(steps, nb, kv_dim)
    pad = ((0, 0), (0, npad - steps), (0, 0))
    ck_t = cache_win_k[0].transpose(0, 2, 3, 1).reshape(nb, kv_dim, win)
    cv_t = cache_win_v[0].transpose(0, 2, 3, 1).reshape(nb, kv_dim, win)
    bb = SUBLANES
    o_s, ck_new, cv_new = _swa_sample(qs, ck_t, cv_t, jnp.pad(k_tm.transpose(1, 0, 2), pad),
                                      jnp.pad(v_tm.transpose(1, 0, 2), pad), code_c, code_n, tbl, sinks, steps, bb)
    ya_s = o_s.reshape(nb, ncg, GQA_GROUP, steps, LANES).transpose(3, 0, 1, 2, 4).reshape(n_s, d_attn)
    ffn2d = state_ffn_conv[0].reshape(nb, 2 * 2 * d_ff)
    y_s, ns_g, ns_v = _ffn_sample(xs_tm, m_rnn_s, ya_s, gn_attn_p, w_out_r, w_out_a, g_ffn, w_up_b, cw_f, cb_f,
                                  w_down_b, g_fin, ffn2d, nb, steps, _pick(FFN_COLS_SAMPLE, d_ff))
    y_sample = y_s.reshape(steps, nb, d_model).transpose(1, 0, 2)
    s_rnn_conv = xr_s.reshape(steps, nb, d_rnn)[steps - 3:].transpose(1, 0, 2)[None]
    s_rnn_h = h_new[None]

    to_cache = lambda w_t: w_t.reshape(nb, KV_HEADS, HEAD_DIM, win).transpose(0, 3, 1, 2)[None]
    s_win_k, s_win_v = to_cache(ck_new), to_cache(cv_new)
    s_ffn_conv = jnp.concatenate([ns_g, ns_v], axis=-1).transpose(1, 0, 2)[None]

    return (y_prompt, y_sample, p_rnn_conv, p_rnn_h, p_win_k, p_win_v, p_ffn_conv,
            s_rnn_conv, s_rnn_h, s_win_k, s_win_v, s_ffn_conv)
```

```python
import functools
import math

import numpy as np
import jax
import jax.numpy as jnp
from jax import lax
from jax.experimental import pallas as pl
from jax.experimental.pallas import tpu as pltpu

F32 = jnp.float32
BF16 = jnp.bfloat16

HEAD_DIM = 64
KV_HEADS = 4
N_HEADS = 16
GQA_GROUP = N_HEADS // KV_HEADS
RNN_BLOCKS = 16
RG_C = 8.0
WINDOW = 128
N_BUCKETS = 32
MAX_EXACT = N_BUCKETS // 2
REL_MAX_DIST = 128
EPS = 1e-6
NEG_INF = -1e30
ATTN_SCALE = HEAD_DIM ** -0.5

LANES = 128
SUBLANES = 8
VMEM_LIMIT_BYTES = 56 * 1024 * 1024
FFN_VMEM_LIMIT_BYTES = 61 * 1024 * 1024

GATE_TILE = 256

PROJ_ROWS = (512, WINDOW)
RGLRU_ROWS = (1024, 512, WINDOW)
SWA_BLOCKS_PER_STEP = (8, 4, 2, 1)
FFN_ROWS = (1024, 512, 256)
FFN_COLS_PROMPT = (768, 512)
FFN_COLS_SAMPLE = (768, 512)


def _pick(options, n):
    return next(t for t in options if n % t == 0)


def _t5_bucket_np(d):
    n = np.maximum(d, 0)
    nf = np.maximum(n, 1).astype(np.float32)
    large = MAX_EXACT + (np.log(nf / MAX_EXACT) / math.log(REL_MAX_DIST / MAX_EXACT)
                         * (N_BUCKETS - MAX_EXACT)).astype(np.int32)
    large = np.minimum(large, N_BUCKETS - 1)
    return np.where(n < MAX_EXACT, n, large).astype(np.int32)


def _rms(x, g):
    ms = jnp.mean(x * x, axis=-1, keepdims=True)
    return (x * lax.rsqrt(ms + EPS)) * g


def _softplus(x):
    return jnp.maximum(x, 0.0) + jnp.log1p(jnp.exp(-jnp.abs(x)))


def _const_spec(shape):
    nd = len(shape)
    return pl.BlockSpec(shape, lambda *_: (0,) * nd, pipeline_mode=pl.Buffered(1))


def _smem_spec():
    return pl.BlockSpec(memory_space=pltpu.SMEM)


def _inproj_body(x_ref, g_ref, w_ref, xr_ref, gr_ref, q_ref, k_ref, v_ref):
    xn = _rms(x_ref[...], g_ref[...]).astype(BF16)
    d_rnn, d_attn, kv_dim = xr_ref.shape[-1], q_ref.shape[-1], k_ref.shape[-1]
    p = jnp.dot(xn, w_ref[...], preferred_element_type=F32)
    o1, o2, o3, o4 = d_rnn, 2 * d_rnn, 2 * d_rnn + d_attn, 2 * d_rnn + d_attn + kv_dim
    xr_ref[...] = p[:, :o1]
    gr_ref[...] = p[:, o1:o2]
    k_ref[...] = p[:, o3:o4]
    v_ref[...] = p[:, o4:]
    pq = p[:, o2:o3] * ATTN_SCALE
    lane = lax.broadcasted_iota(jnp.int32, (pq.shape[0], LANES), 1)
    blocks = []
    for cg in range(KV_HEADS // 2):
        for g in range(GQA_GROUP):
            ha, hb = 2 * GQA_GROUP * cg + g, 2 * GQA_GROUP * cg + GQA_GROUP + g
            a = pq[:, (ha // 2) * LANES:(ha // 2 + 1) * LANES]
            b = pq[:, (hb // 2) * LANES:(hb // 2 + 1) * LANES]
            if g % 2 == 0:
                blocks.append(jnp.where(lane < HEAD_DIM, a, pltpu.roll(b, HEAD_DIM, 1)))
            else:
                blocks.append(jnp.where(lane < HEAD_DIM, pltpu.roll(a, HEAD_DIM, 1), b))
    q_ref[...] = jnp.concatenate(blocks, axis=1).astype(BF16)


def _inproj(x2d, x_spec, n_steps, tm, g, w, d_rnn, d_attn, kv_dim):
    n = n_steps * tm
    d_model = g.shape[-1]
    row = lambda i: (i, 0)
    out_shape = (
        jax.ShapeDtypeStruct((n, d_rnn), F32),
        jax.ShapeDtypeStruct((n, d_rnn), F32),
        jax.ShapeDtypeStruct((n, d_attn), BF16),
        jax.ShapeDtypeStruct((n, kv_dim), F32),
        jax.ShapeDtypeStruct((n, kv_dim), F32),
    )
    out_specs = (
        pl.BlockSpec((tm, d_rnn), row),
        pl.BlockSpec((tm, d_rnn), row),
        pl.BlockSpec((tm, d_attn), row),
        pl.BlockSpec((tm, kv_dim), row),
        pl.BlockSpec((tm, kv_dim), row),
    )
    return pl.pallas_call(
        _inproj_body,
        grid=(n_steps,),
        in_specs=[x_spec, _const_spec((1, d_model)), _const_spec(w.shape)],
        out_specs=out_specs,
        out_shape=out_shape,
        compiler_params=pltpu.CompilerParams(
            dimension_semantics=("arbitrary",), vmem_limit_bytes=VMEM_LIMIT_BYTES),
        name="inproj",
    )(x2d, g, w)


def _rglru_gates(xc, wg_ref, ba, bx, lam):
    d_rnn = xc.shape[-1]
    pre_a, pre_x = [], []
    for c in range(d_rnn // GATE_TILE):
        xb = xc[:, c * GATE_TILE:(c + 1) * GATE_TILE].astype(BF16)
        pre = jnp.dot(xb, wg_ref[c], preferred_element_type=F32)
        pre_a.append(pre[:, :GATE_TILE])
        pre_x.append(pre[:, GATE_TILE:])
    r = jax.nn.sigmoid(jnp.concatenate(pre_a, axis=1) + ba)
    i = jax.nn.sigmoid(jnp.concatenate(pre_x, axis=1) + bx)
    log_a = (-RG_C * r) * _softplus(-lam)
    a = jnp.exp(log_a)
    one_minus_a2 = -jnp.tanh(log_a) * (a * a + 1.0)
    u = jnp.sqrt(one_minus_a2) * (i * xc)
    return a, u


def _rglru_prompt_body(xr_ref, gr_ref, cw_ref, cb_ref, wg_ref, ba_ref, bx_ref, lam_ref, gn_ref,
                       out_ref, hlast_ref, xbuf, abuf, ubuf, hbuf, hc_ref, *, tt, conv_w):
    ti = pl.program_id(1)
    d_rnn = xr_ref.shape[-1]

    @pl.when(ti == 0)
    def _():
        xbuf[0:SUBLANES, :] = jnp.zeros((SUBLANES, d_rnn), F32)
        hc_ref[...] = jnp.zeros_like(hc_ref)

    x = xr_ref[...]
    xbuf[SUBLANES:SUBLANES + tt, :] = x
    xc = cw_ref[conv_w - 1:conv_w, :] * x
    for k in range(conv_w - 1):
        off = SUBLANES - (conv_w - 1) + k
        xc = xc + cw_ref[k:k + 1, :] * xbuf[off:off + tt, :]
    xc = xc + cb_ref[...]
    xbuf[0:SUBLANES, :] = xbuf[tt:tt + SUBLANES, :]

    a, u = _rglru_gates(xc, wg_ref, ba_ref[...], bx_ref[...], lam_ref[...])
    abuf[...] = a
    ubuf[...] = u

    row = lax.broadcasted_iota(jnp.int32, (SUBLANES, d_rnn), 0)

    def group(gidx, hc):
        r0 = pl.multiple_of(gidx * SUBLANES, SUBLANES)
        ag = abuf[pl.ds(r0, SUBLANES), :]
        ug = ubuf[pl.ds(r0, SUBLANES), :]
        for k in (1, 2, 4):
            a_prev = jnp.where(row >= k, pltpu.roll(ag, k, 0), 1.0)
            u_prev = jnp.where(row >= k, pltpu.roll(ug, k, 0), 0.0)
            ug = ag * u_prev + ug
            ag = ag * a_prev
        h = ag * hc + ug
        hbuf[pl.ds(r0, SUBLANES), :] = h
        return jnp.broadcast_to(h[SUBLANES - 1:SUBLANES, :], (SUBLANES, d_rnn))

    hc = lax.fori_loop(0, tt // SUBLANES, group, hc_ref[...], unroll=2)
    hc_ref[...] = hc
    hlast_ref[0] = hc

    y = jax.nn.gelu(gr_ref[...]) * hbuf[...]
    out_ref[...] = _rms(y, gn_ref[...]).astype(BF16)


def _rglru_prompt(xr, gr, cw, cb, wg, ba, bx, lam, gn, batch, seq, tt):
    d_rnn = xr.shape[-1]
    nt = seq // tt
    conv_w = cw.shape[0]
    tile = lambda b, t: (b * nt + t, 0)
    out, hlast = pl.pallas_call(
        functools.partial(_rglru_prompt_body, tt=tt, conv_w=conv_w),
        grid=(batch, nt),
        in_specs=[
            pl.BlockSpec((tt, d_rnn), tile),
            pl.BlockSpec((tt, d_rnn), tile),
            _const_spec(cw.shape), _const_spec(cb.shape), _const_spec(wg.shape),
            _const_spec(ba.shape), _const_spec(bx.shape), _const_spec(lam.shape), _const_spec(gn.shape),
        ],
        out_specs=(
            pl.BlockSpec((tt, d_rnn), tile),
            pl.BlockSpec((1, SUBLANES, d_rnn), lambda b, t: (b, 0, 0)),
        ),
        out_shape=(
            jax.ShapeDtypeStruct((batch * seq, d_rnn), BF16),
            jax.ShapeDtypeStruct((batch, SUBLANES, d_rnn), F32),
        ),
        scratch_shapes=[
            pltpu.VMEM((tt + SUBLANES, d_rnn), F32),
            pltpu.VMEM((tt, d_rnn), F32),
            pltpu.VMEM((tt, d_rnn), F32),
            pltpu.VMEM((tt, d_rnn), F32),
            pltpu.VMEM((SUBLANES, d_rnn), F32),
        ],
        compiler_params=pltpu.CompilerParams(
            dimension_semantics=("arbitrary", "arbitrary"), vmem_limit_bytes=VMEM_LIMIT_BYTES),
        name="rglru_prompt",
    )(xr, gr, cw, cb, wg, ba, bx, lam, gn)
    return out, hlast[:, 0, :]


def _rglru_sample_body(xr_ref, gr_ref, c0_ref, c1_ref, c2_ref, h0_ref, cw_ref, cb_ref, wg_ref,
                       ba_ref, bx_ref, lam_ref, gn_ref, out_ref, hnew_ref, *, nb, steps):
    x = xr_ref[...]
    hist = [c0_ref[...], c1_ref[...], c2_ref[...]]
    xs = [x[t * nb:(t + 1) * nb, :] for t in range(steps)]
    xp = hist + xs
    conv_w = len(hist) + 1
    xc = cw_ref[conv_w - 1:conv_w, :] * x
    for k in range(conv_w - 1):
        shifted = jnp.concatenate(xp[k:k + steps], axis=0)
        xc = xc + cw_ref[k:k + 1, :] * shifted
    xc = xc + cb_ref[...]
    a, u = _rglru_gates(xc, wg_ref, ba_ref[...], bx_ref[...], lam_ref[...])
    h = h0_ref[...]
    hs = []
    for t in range(steps):
        h = a[t * nb:(t + 1) * nb, :] * h + u[t * nb:(t + 1) * nb, :]
        hs.append(h)
    hnew_ref[...] = h
    y = jax.nn.gelu(gr_ref[...]) * jnp.concatenate(hs, axis=0)
    out_ref[...] = _rms(y, gn_ref[...]).astype(BF16)


def _rglru_sample(xr, gr, conv_state2d, h0, cw, cb, wg, ba, bx, lam, gn, nb, steps):
    d_rnn = xr.shape[-1]
    n = nb * steps
    assert cw.shape[0] == 4
    full = lambda shape: pl.BlockSpec(shape, lambda i: (0,) * len(shape))
    return pl.pallas_call(
        functools.partial(_rglru_sample_body, nb=nb, steps=steps),
        grid=(1,),
        in_specs=[
            full((n, d_rnn)), full((n, d_rnn)),
            pl.BlockSpec((nb, d_rnn), lambda i: (0, 0)),
            pl.BlockSpec((nb, d_rnn), lambda i: (1, 0)),
            pl.BlockSpec((nb, d_rnn), lambda i: (2, 0)),
            full((nb, d_rnn)),
            full(cw.shape), full(cb.shape), full(wg.shape), full(ba.shape), full(bx.shape),
            full(lam.shape), full(gn.shape),
        ],
        out_specs=(full((n, d_rnn)), full((nb, d_rnn))),
        out_shape=(jax.ShapeDtypeStruct((n, d_rnn), BF16), jax.ShapeDtypeStruct((nb, d_rnn), F32)),
        compiler_params=pltpu.CompilerParams(
            dimension_semantics=("arbitrary",), vmem_limit_bytes=VMEM_LIMIT_BYTES),
        name="rglru_sample",
    )(xr, gr, conv_state2d, conv_state2d, conv_state2d, h0, cw, cb, wg, ba, bx, lam, gn)


def _swa_prompt_body(code_ref, tbl_ref, sink_ref, q_ref, kp_ref, kc_ref, vp_ref, vc_ref, gn_ref,
                     out_ref, bias_ref):
    j = pl.program_id(1)
    blk = WINDOW
    rows = GQA_GROUP * blk

    @pl.when((pl.program_id(0) == 0) & (j == 0))
    def _():
        code = code_ref[...]
        qi = lax.broadcasted_iota(jnp.int32, code.shape, 0)
        kj = lax.broadcasted_iota(jnp.int32, code.shape, 1)
        dist = blk + qi - kj
        in_window = (dist & -WINDOW) == 0
        for h in range(N_HEADS):
            def pick(b, acc, h=h):
                return jnp.where(code == b, tbl_ref[b * N_HEADS + h], acc)
            bias_h = lax.fori_loop(0, N_BUCKETS, pick, jnp.zeros(code.shape, F32))
            bias_h = jnp.where(in_window, bias_h, NEG_INF)
            kv, g = divmod(h, GQA_GROUP)
            bias_ref[1, kv, g * blk:(g + 1) * blk, :] = bias_h
            bias_ref[0, kv, g * blk:(g + 1) * blk, :] = jnp.where(kj >= blk, bias_h, NEG_INF)

    lane = lax.broadcasted_iota(jnp.int32, (2 * blk, LANES), 1)
    row_g = lax.broadcasted_iota(jnp.int32, (rows, 1), 0) // blk
    for sub in range(q_ref.shape[0] // blk):
        rows_q = slice(sub * blk, (sub + 1) * blk)
        if sub == 0:
            has_prev = jnp.where(j > 0, 1, 0)
            k_prev, v_prev = kp_ref[...], vp_ref[...]
        else:
            has_prev = 1
            k_prev, v_prev = kc_ref[(sub - 1) * blk:sub * blk, :], vc_ref[(sub - 1) * blk:sub * blk, :]
        kband = jnp.concatenate([k_prev, kc_ref[rows_q, :]], axis=0)
        vband = jnp.concatenate([v_prev, vc_ref[rows_q, :]], axis=0)
        y = _swa_block(q_ref[rows_q, :], kband, vband, bias_ref, has_prev, sink_ref, lane, row_g)
        out_ref[rows_q, :] = _rms(y, gn_ref[...]).astype(BF16)


def _swa_block(q, kband, vband, bias_ref, has_prev, sink_ref, lane, row_g):
    blk = q.shape[0]
    rows = GQA_GROUP * blk
    outs = []
    for cg in range(KV_HEADS // 2):
        qs = jnp.concatenate(
            [q[:, (cg * GQA_GROUP + g) * LANES:(cg * GQA_GROUP + g + 1) * LANES] for g in range(GQA_GROUP)],
            axis=0)
        ka = kband[:, cg * LANES:(cg + 1) * LANES]
        va = vband[:, cg * LANES:(cg + 1) * LANES]
        o = None
        for par in range(2):
            kv = 2 * cg + par
            half = (lane < HEAD_DIM) if par == 0 else (lane >= HEAD_DIM)
            km = jnp.where(half, ka, 0.0).astype(BF16)
            vm = jnp.where(half, va, 0.0).astype(BF16)
            s = lax.dot_general(qs, km, (((1,), (1,)), ((), ())), preferred_element_type=F32)
            s = s + bias_ref[has_prev, kv]
            sink = jnp.zeros((rows, 1), F32)
            for g in range(GQA_GROUP):
                sink = jnp.where(row_g == g, sink_ref[kv * GQA_GROUP + g], sink)
            m = jnp.maximum(jnp.max(s, axis=-1, keepdims=True), sink)
            p = jnp.exp(s - m)
            denom = jnp.sum(p, axis=-1, keepdims=True) + jnp.exp(sink - m)
            part = jnp.dot(p.astype(BF16), vm, preferred_element_type=F32) * (1.0 / denom)
            o = part if o is None else o + part
        outs.extend(o[g * blk:(g + 1) * blk, :] for g in range(GQA_GROUP))
    return jnp.concatenate(outs, axis=1)


def _swa_prompt(q, k, v, code, tbl, sinks, gn, batch, seq):
    blk = WINDOW
    per_step = _pick(SWA_BLOCKS_PER_STEP, seq // blk)
    nb = seq // (per_step * blk)
    d_attn = q.shape[-1]
    kv_dim = k.shape[-1]
    cur = lambda b, j: (b * nb + j, 0)
    prev = lambda b, j: (per_step * (b * nb + j) - jnp.where(j > 0, 1, 0), 0)
    return pl.pallas_call(
        _swa_prompt_body,
        grid=(batch, nb),
        in_specs=[
            _const_spec(code.shape), _smem_spec(), _smem_spec(),
            pl.BlockSpec((per_step * blk, d_attn), cur),
            pl.BlockSpec((blk, kv_dim), prev), pl.BlockSpec((per_step * blk, kv_dim), cur),
            pl.BlockSpec((blk, kv_dim), prev), pl.BlockSpec((per_step * blk, kv_dim), cur),
            _const_spec(gn.shape),
        ],
        out_specs=pl.BlockSpec((per_step * blk, d_attn), cur),
        out_shape=jax.ShapeDtypeStruct((batch * seq, d_attn), BF16),
        scratch_shapes=[pltpu.VMEM((2, KV_HEADS, GQA_GROUP * blk, 2 * blk), F32)],
        compiler_params=pltpu.CompilerParams(
            dimension_semantics=("arbitrary", "arbitrary"), vmem_limit_bytes=VMEM_LIMIT_BYTES),
        name="swa_prompt",
    )(code, tbl, sinks, q, k, k, v, v, gn)


def _slide_window(win_ref, new_ref, out_ref, steps):
    bb, kv_dim, win = win_ref.shape
    npad = new_ref.shape[1]
    new_t = new_ref[...].reshape(bb * npad, kv_dim).T
    lane = lax.broadcasted_iota(jnp.int32, (kv_dim, win), 1)
    for b in range(bb):
        placed = pltpu.roll(new_t, (win - steps - b * npad) % win, 1)
        shifted = pltpu.roll(win_ref[b], win - steps, 1)
        out_ref[b] = jnp.where(lane >= win - steps, placed, shifted)


def _swa_sample_body(codec_ref, coden_ref, tbl_ref, sink_ref, q_ref, ck_ref, cv_ref, kn_ref, vn_ref,
                     out_ref, ck_out_ref, cv_out_ref, biasc_ref, biasn_ref, sinkc_ref, *, steps):
    bb = q_ref.shape[0]
    nrow = N_HEADS * steps
    win = ck_ref.shape[2]
    npad = kn_ref.shape[1]

    @pl.when(pl.program_id(0) == 0)
    def _():
        codec = codec_ref[...]
        coden = coden_ref[...]
        hrow = lax.broadcasted_iota(jnp.int32, (nrow, 1), 0) // steps

        def pick(idx, accs):
            ac, an = accs
            val = tbl_ref[idx]
            return jnp.where(codec == idx, val, ac), jnp.where(coden == idx, val, an)
        bc, bn = lax.fori_loop(0, N_BUCKETS * N_HEADS, pick,
                               (jnp.zeros(codec.shape, F32), jnp.zeros(coden.shape, F32)))
        biasc_ref[...] = bc
        biasn_ref[...] = bn

        def pick_sink(h, acc):
            return jnp.where(hrow == h, sink_ref[h], acc)
        sinkc_ref[...] = lax.fori_loop(0, N_HEADS, pick_sink, jnp.zeros((nrow, 1), F32))

    q = q_ref[...].astype(F32)
    lane = lax.broadcasted_iota(jnp.int32, (bb, GQA_GROUP * steps, LANES), 2)
    zeros = jnp.zeros((bb, GQA_GROUP * steps, LANES), F32)
    pieces = []
    for cg in range(KV_HEADS // 2):
        for par in range(2):
            half = (lane < HEAD_DIM) if par == 0 else (lane >= HEAD_DIM)
            qm = jnp.where(half, q[:, cg], zeros)
            pieces.append(jnp.concatenate([qm, zeros] if cg == 0 else [zeros, qm], axis=2))
    qm = jnp.concatenate(pieces, axis=1).astype(BF16)

    ck = ck_ref[...].astype(BF16)
    kn = kn_ref[...].astype(BF16)
    s_c = jnp.einsum("bqd,bdk->bqk", qm, ck, preferred_element_type=F32)
    s_n = jnp.einsum("bqd,bkd->bqk", qm, kn, preferred_element_type=F32)

    t_c = lax.broadcasted_iota(jnp.int32, (nrow, win), 0) % steps
    k_c = lax.broadcasted_iota(jnp.int32, (nrow, win), 1)
    valid_c = k_c > t_c
    t_n = lax.broadcasted_iota(jnp.int32, (nrow, npad), 0) % steps
    k_n = lax.broadcasted_iota(jnp.int32, (nrow, npad), 1)
    valid_n = k_n <= t_n

    s_c = jnp.where(valid_c[None], s_c + biasc_ref[...][None], NEG_INF)
    s_n = jnp.where(valid_n[None], s_n + biasn_ref[...][None], NEG_INF)
    sink = sinkc_ref[...][None]
    m = jnp.maximum(jnp.maximum(jnp.max(s_c, axis=-1, keepdims=True),
                                jnp.max(s_n, axis=-1, keepdims=True)), sink)
    p_c = jnp.exp(s_c - m)
    p_n = jnp.exp(s_n - m)
    denom = (jnp.sum(p_c, axis=-1, keepdims=True) + jnp.sum(p_n, axis=-1, keepdims=True)
             + jnp.exp(sink - m))
    r = 1.0 / denom
    w_c = (p_c * r).astype(BF16)
    w_n = (p_n * r).astype(BF16)
    o = (jnp.einsum("bqk,bdk->bqd", w_c, cv_ref[...].astype(BF16), preferred_element_type=F32)
         + jnp.einsum("bqk,bkd->bqd", w_n, vn_ref[...].astype(BF16), preferred_element_type=F32))
    gt = GQA_GROUP * steps
    lane_o = lax.broadcasted_iota(jnp.int32, (bb, gt, LANES), 2)
    for cg in range(KV_HEADS // 2):
        lo = o[:, cg * 2 * gt:cg * 2 * gt + gt, cg * LANES:(cg + 1) * LANES]
        hi = o[:, cg * 2 * gt + gt:(cg + 1) * 2 * gt, cg * LANES:(cg + 1) * LANES]
        out_ref[:, cg] = jnp.where(lane_o < HEAD_DIM, lo, hi)

    _slide_window(ck_ref, kn_ref, ck_out_ref, steps)
    _slide_window(cv_ref, vn_ref, cv_out_ref, steps)


def _swa_sample(qs, ck, cv, kn, vn, codec, coden, tbl, sinks, steps, bb):
    nbatch, ncg, gt, _ = qs.shape
    kv_dim, win = ck.shape[1], ck.shape[2]
    npad = kn.shape[1]
    nrow = N_HEADS * steps
    assert bb * npad == win
    blk4 = lambda i: (i, 0, 0, 0)
    blk3 = lambda i: (i, 0, 0)
    return pl.pallas_call(
        functools.partial(_swa_sample_body, steps=steps),
        grid=(nbatch // bb,),
        in_specs=[
            _const_spec(codec.shape), _const_spec(coden.shape), _smem_spec(), _smem_spec(),
            pl.BlockSpec((bb, ncg, gt, LANES), blk4),
            pl.BlockSpec((bb, kv_dim, win), blk3), pl.BlockSpec((bb, kv_dim, win), blk3),
            pl.BlockSpec((bb, npad, kv_dim), blk3), pl.BlockSpec((bb, npad, kv_dim), blk3),
        ],
        out_specs=(pl.BlockSpec((bb, ncg, gt, LANES), blk4),
                   pl.BlockSpec((bb, kv_dim, win), blk3), pl.BlockSpec((bb, kv_dim, win), blk3)),
        out_shape=(jax.ShapeDtypeStruct((nbatch, ncg, gt, LANES), F32),
                   jax.ShapeDtypeStruct((nbatch, kv_dim, win), F32),
                   jax.ShapeDtypeStruct((nbatch, kv_dim, win), F32)),
        scratch_shapes=[pltpu.VMEM((nrow, win), F32), pltpu.VMEM((nrow, npad), F32),
                        pltpu.VMEM((nrow, 1), F32)],
        compiler_params=pltpu.CompilerParams(
            dimension_semantics=("arbitrary",), vmem_limit_bytes=VMEM_LIMIT_BYTES),
        name="swa_sample",
    )(codec, coden, tbl, sinks, qs, ck, cv, kn, vn)


def _shift_rows(up, hist, k):
    rolled = pltpu.roll(up, k, 0)
    row = lax.broadcasted_iota(jnp.int32, hist.shape, 0)
    head = jnp.where(row < k, pltpu.roll(hist, k, 0), rolled[0:SUBLANES, :])
    return jnp.concatenate([head, rolled[SUBLANES:, :]], axis=0)


def _ffn_conv(up, hist, cw, cb, *, tm, sample_nb):
    if sample_nb is None:
        prev2 = _shift_rows(up, hist, 2)
        prev1 = _shift_rows(up, hist, 1)
    else:
        s0, s1 = hist
        prev2 = jnp.concatenate([s0, s1] + ([up[:tm - 2 * sample_nb, :]] if tm > 2 * sample_nb else []), axis=0)
        prev1 = jnp.concatenate([s1, up[:tm - sample_nb, :]], axis=0)
    return cw[0:1, :] * prev2 + cw[1:2, :] * prev1 + cw[2:3, :] * up + cb


FFN_SLAB = 256


def _ffn_prompt_body(x_hbm, mr_ref, ma_ref, wor_ref, woa_ref, gf_ref, wg_ref, wv_ref, cwg_ref, cwv_ref, cbg_ref,
                     cbv_ref, wd_ref, gfin_ref, y_ref, tg_ref, tv_ref, hn_ref, car_g, car_v, x_sem,
                     *, tm, tiles_per_seq):
    i = pl.program_id(0)
    j = pl.program_id(1)
    nj = pl.num_programs(1)
    n_slab = tm // FFN_SLAB
    slab = lambda r: slice(r * FFN_SLAB, (r + 1) * FFN_SLAB)

    @pl.when(j == 0)
    def _():
        x_copy = pltpu.make_async_copy(x_hbm.at[pl.ds(pl.multiple_of(i * tm, tm), tm), :], y_ref, x_sem)
        x_copy.start()

        def out_proj(r):
            return (jnp.dot(mr_ref[slab(r), :], wor_ref[...], preferred_element_type=F32)
                    + jnp.dot(ma_ref[slab(r), :], woa_ref[...], preferred_element_type=F32))

        pending = out_proj(0)
        x_copy.wait()
        for r in range(n_slab):
            d = pending
            if r + 1 < n_slab:
                pending = out_proj(r + 1)
            h = y_ref[slab(r), :] + d
            y_ref[slab(r), :] = h
            hn_ref[slab(r), :] = _rms(h, gf_ref[...]).astype(BF16)

    @pl.when((i == 0) & (j == 0))
    def _():
        car_g[...] = jnp.zeros_like(car_g)
        car_v[...] = jnp.zeros_like(car_v)

    cwg, cwv, cbg, cbv = cwg_ref[...], cwv_ref[...], cbg_ref[...], cbv_ref[...]

    def up_proj(r):
        hn = hn_ref[slab(r), :]
        return (jnp.dot(hn, wg_ref[...], preferred_element_type=F32),
                jnp.dot(hn, wv_ref[...], preferred_element_type=F32))

    seq_start = i % tiles_per_seq == 0
    hist_g = jnp.where(seq_start, 0.0, car_g[j])
    hist_v = jnp.where(seq_start, 0.0, car_v[j])
    pending = up_proj(0)
    act = None
    for r in range(n_slab):
        up_g, up_v = pending
        if r + 1 < n_slab:
            pending = up_proj(r + 1)
        if act is not None:
            y_ref[slab(r - 1), :] += jnp.dot(act, wd_ref[...], preferred_element_type=F32)
        c_g = _ffn_conv(up_g, hist_g, cwg, cbg, tm=FFN_SLAB, sample_nb=None)
        c_v = _ffn_conv(up_v, hist_v, cwv, cbv, tm=FFN_SLAB, sample_nb=None)
        hist_g = up_g[FFN_SLAB - SUBLANES:, :]
        hist_v = up_v[FFN_SLAB - SUBLANES:, :]
        act = (jax.nn.gelu(c_g) * c_v).astype(BF16)
    y_ref[slab(n_slab - 1), :] += jnp.dot(act, wd_ref[...], preferred_element_type=F32)
    car_g[j] = hist_g
    car_v[j] = hist_v
    tg_ref[0] = hist_g
    tv_ref[0] = hist_v

    @pl.when(j == nj - 1)
    def _():
        for r in range(n_slab):
            y_ref[slab(r), :] = _rms(y_ref[slab(r), :], gfin_ref[...])


def _ffn_sample_body(x_ref, mr_ref, ma_ref, gna_ref, wor_ref, woa_ref, gf_ref, wg_ref, wv_ref, cwg_ref, cwv_ref,
                     cbg_ref, cbv_ref, wd_ref, gfin_ref, s0g_ref, s1g_ref, s0v_ref, s1v_ref,
                     y_ref, tg_ref, tv_ref, hn_ref, *, nb):
    j = pl.program_id(1)
    nj = pl.num_programs(1)
    tm = x_ref.shape[0]

    @pl.when(j == 0)
    def _():
        ma = _rms(ma_ref[...], gna_ref[...]).astype(BF16)
        h = (x_ref[...] + jnp.dot(mr_ref[...], wor_ref[...], preferred_element_type=F32)
             + jnp.dot(ma, woa_ref[...], preferred_element_type=F32))
        y_ref[...] = h
        hn_ref[...] = _rms(h, gf_ref[...]).astype(BF16)

    rows_per_slab = 2 * nb
    n_slab = tm // rows_per_slab
    slab = lambda r: slice(r * rows_per_slab, (r + 1) * rows_per_slab)
    cwg, cwv, cbg, cbv = cwg_ref[...], cwv_ref[...], cbg_ref[...], cbv_ref[...]

    def up_proj(r):
        hn = hn_ref[slab(r), :]
        return (jnp.dot(hn, wg_ref[...], preferred_element_type=F32),
                jnp.dot(hn, wv_ref[...], preferred_element_type=F32))

    hist_g = (s0g_ref[...], s1g_ref[...])
    hist_v = (s0v_ref[...], s1v_ref[...])
    pending = up_proj(0)
    act = None
    for r in range(n_slab):
        up_g, up_v = pending
        if r + 1 < n_slab:
            pending = up_proj(r + 1)
        if act is not None:
            y_ref[slab(r - 1), :] += jnp.dot(act, wd_ref[...], preferred_element_type=F32)
        c_g = _ffn_conv(up_g, hist_g, cwg, cbg, tm=rows_per_slab, sample_nb=nb)
        c_v = _ffn_conv(up_v, hist_v, cwv, cbv, tm=rows_per_slab, sample_nb=nb)
        hist_g = (up_g[:nb, :], up_g[nb:, :])
        hist_v = (up_v[:nb, :], up_v[nb:, :])
        act = (jax.nn.gelu(c_g) * c_v).astype(BF16)
    y_ref[slab(n_slab - 1), :] += jnp.dot(act, wd_ref[...], preferred_element_type=F32)
    tg_ref[0], tg_ref[1] = hist_g
    tv_ref[0], tv_ref[1] = hist_v

    @pl.when(j == nj - 1)
    def _():
        y_ref[...] = _rms(y_ref[...], gfin_ref[...])


def _ffn_prompt(x, mr, ma, wor, woa, gf, wup, cw, cb, wd, gfin, batch, seq, tm, tf):
    n, d_model = x.shape
    d_half = mr.shape[-1]
    d_ff = wd.shape[0]
    ni, nj = n // tm, d_ff // tf
    tiles_per_seq = seq // tm
    row = lambda i, j: (i, 0)
    gate_col = lambda i, j: (0, j)
    val_col = lambda i, j: (0, nj + j)
    tail = lambda i, j: (i, 0, j)
    y, tg, tv = pl.pallas_call(
        functools.partial(_ffn_prompt_body, tm=tm, tiles_per_seq=tiles_per_seq),
        grid=(ni, nj),
        in_specs=[
            pl.BlockSpec(memory_space=pl.ANY),
            pl.BlockSpec((tm, d_half), row), pl.BlockSpec((tm, d_half), row),
            _const_spec(wor.shape), _const_spec(woa.shape), _const_spec(gf.shape),
            pl.BlockSpec((d_model, tf), gate_col), pl.BlockSpec((d_model, tf), val_col),
            pl.BlockSpec((cw.shape[0], tf), gate_col), pl.BlockSpec((cw.shape[0], tf), val_col),
            pl.BlockSpec((1, tf), gate_col), pl.BlockSpec((1, tf), val_col),
            pl.BlockSpec((tf, d_model), lambda i, j: (j, 0)),
            _const_spec(gfin.shape),
        ],
        out_specs=(
            pl.BlockSpec((tm, d_model), row),
            pl.BlockSpec((1, SUBLANES, tf), tail),
            pl.BlockSpec((1, SUBLANES, tf), tail),
        ),
        out_shape=(
            jax.ShapeDtypeStruct((n, d_model), F32),
            jax.ShapeDtypeStruct((ni, SUBLANES, d_ff), F32),
            jax.ShapeDtypeStruct((ni, SUBLANES, d_ff), F32),
        ),
        scratch_shapes=[
            pltpu.VMEM((tm, d_model), BF16),
            pltpu.VMEM((nj, SUBLANES, tf), F32), pltpu.VMEM((nj, SUBLANES, tf), F32),
            pltpu.SemaphoreType.DMA(()),
        ],
        compiler_params=pltpu.CompilerParams(
            dimension_semantics=("arbitrary", "arbitrary"), vmem_limit_bytes=FFN_VMEM_LIMIT_BYTES),
        name="ffn_prompt",
    )(x, mr, ma, wor, woa, gf, wup, wup, cw, cw, cb, cb, wd, gfin)
    return y, tg[tiles_per_seq - 1::tiles_per_seq], tv[tiles_per_seq - 1::tiles_per_seq]


def _ffn_sample(x, mr, ma, gna, wor, woa, gf, wup, cw, cb, wd, gfin, state2d, nb, steps, tf):
    n, d_model = x.shape
    d_half = mr.shape[-1]
    d_ff = wd.shape[0]
    nj = d_ff // tf
    full2 = lambda shape: pl.BlockSpec(shape, lambda i, j: (0, 0))
    gate_col = lambda i, j: (0, j)
    val_col = lambda i, j: (0, nj + j)
    y, tg, tv = pl.pallas_call(
        functools.partial(_ffn_sample_body, nb=nb),
        grid=(1, nj),
        in_specs=[
            full2((n, d_model)), full2((n, d_half)), full2((n, d_half)), full2(gna.shape),
            _const_spec(wor.shape), _const_spec(woa.shape), _const_spec(gf.shape),
            pl.BlockSpec((d_model, tf), gate_col), pl.BlockSpec((d_model, tf), val_col),
            pl.BlockSpec((cw.shape[0], tf), gate_col), pl.BlockSpec((cw.shape[0], tf), val_col),
            pl.BlockSpec((1, tf), gate_col), pl.BlockSpec((1, tf), val_col),
            pl.BlockSpec((tf, d_model), lambda i, j: (j, 0)),
            _const_spec(gfin.shape),
            pl.BlockSpec((nb, tf), lambda i, j: (0, j)),
            pl.BlockSpec((nb, tf), lambda i, j: (0, 2 * nj + j)),
            pl.BlockSpec((nb, tf), lambda i, j: (0, nj + j)),
            pl.BlockSpec((nb, tf), lambda i, j: (0, 3 * nj + j)),
        ],
        out_specs=(
            full2((n, d_model)),
            pl.BlockSpec((2, nb, tf), lambda i, j: (0, 0, j)),
            pl.BlockSpec((2, nb, tf), lambda i, j: (0, 0, j)),
        ),
        out_shape=(
            jax.ShapeDtypeStruct((n, d_model), F32),
            jax.ShapeDtypeStruct((2, nb, d_ff), F32),
            jax.ShapeDtypeStruct((2, nb, d_ff), F32),
        ),
        scratch_shapes=[pltpu.VMEM((n, d_model), BF16)],
        compiler_params=pltpu.CompilerParams(
            dimension_semantics=("arbitrary", "arbitrary"), vmem_limit_bytes=VMEM_LIMIT_BYTES),
        name="ffn_sample",
    )(x, mr, ma, gna, wor, woa, gf, wup, wup, cw, cw, cb, cb, wd, gfin, state2d, state2d, state2d, state2d)
    return y, tg, tv


def _perm_heads(a, axis):
    shape = a.shape
    a = a.reshape(shape[:axis] + (KV_HEADS // 2, 2, GQA_GROUP, HEAD_DIM) + shape[axis + 1:])
    return jnp.swapaxes(a, axis + 1, axis + 2).reshape(shape)


def _gate_weights(w_a, w_x):
    nblk, bs, _ = w_a.shape
    per = GATE_TILE // bs
    eye = jnp.eye(per, dtype=w_a.dtype)

    def pack(w):
        w4 = w.reshape(nblk // per, per, bs, bs)
        return jnp.einsum("cgij,gh->cgihj", w4, eye).reshape(nblk // per, GATE_TILE, GATE_TILE)
    return jnp.concatenate([pack(w_a), pack(w_x)], axis=-1).astype(BF16)


def kernel(x_prompt, x_sample, state_rnn_conv, state_rnn_h, cache_win_k, cache_win_v, state_ffn_conv,
           norm_mix_g, w_in, rnn_conv_w, rnn_conv_b, w_gate_a, b_gate_a, w_gate_x, b_gate_x, rnn_lambda,
           attn_sinks, rel_bias_table, gn_rnn_g, gn_attn_g, w_out, norm_ffn_g, w_up, ffn_conv_w,
           ffn_conv_b, w_down, norm_final_g):
    batch, seq, d_model = x_prompt.shape
    nb, steps, _ = x_sample.shape
    depth = w_in.shape[0]
    d_rnn = rnn_conv_w.shape[-1]
    d_attn = N_HEADS * HEAD_DIM
    kv_dim = KV_HEADS * HEAD_DIM
    d_ff = w_down.shape[1]
    win = cache_win_k.shape[2]
    assert depth == 1 and d_rnn + d_attn == d_model and w_in.shape[-1] == 2 * d_rnn + d_attn + 2 * kv_dim
    assert win == WINDOW and seq % WINDOW == 0 and w_gate_a.shape[1] == RNN_BLOCKS
    assert rnn_conv_w.shape[1] == 4 and ffn_conv_w.shape[1] == 3 and nb % SUBLANES == 0
    assert steps >= 3 and steps % 2 == 0

    w_in_b = w_in[0].astype(BF16)
    w_out0 = w_out[0]
    w_out_r = w_out0[:d_rnn].astype(BF16)
    w_out_a = _perm_heads(w_out0[d_rnn:], 0).astype(BF16)
    gn_attn_p = _perm_heads(gn_attn_g[0], 0).reshape(1, d_attn)
    w_up_b = w_up[0].astype(BF16)
    w_down_b = w_down[0].astype(BF16)
    wg = _gate_weights(w_gate_a[0], w_gate_x[0])
    row2 = lambda a: a.reshape(1, -1)
    g_mix, g_ffn, g_fin, g_rnn = row2(norm_mix_g[0]), row2(norm_ffn_g[0]), row2(norm_final_g), row2(gn_rnn_g[0])
    cw_r, cb_r = rnn_conv_w[0], row2(rnn_conv_b[0])
    ba, bx, lam = row2(b_gate_a[0]), row2(b_gate_x[0]), row2(rnn_lambda[0])
    cw_f, cb_f = ffn_conv_w[0], row2(ffn_conv_b[0])
    tbl = rel_bias_table.reshape(-1)
    sinks = attn_sinks[0]

    blk = WINDOW
    qi = np.arange(blk)[:, None]
    kj = np.arange(2 * blk)[None, :]
    code_p = jnp.asarray(_t5_bucket_np(blk + qi - kj))
    r = np.arange(N_HEADS * steps)[:, None]
    t_r, h_r = r % steps, r // steps
    npad = 2 * SUBLANES
    code_c = jnp.asarray(_t5_bucket_np(win + t_r - np.arange(win)[None, :]) * N_HEADS + h_r)
    code_n = jnp.asarray(_t5_bucket_np(t_r - np.arange(npad)[None, :]) * N_HEADS + h_r)

    n_p = batch * seq
    tm_a = _pick(PROJ_ROWS, n_p)
    xp2 = x_prompt.reshape(n_p, d_model)
    xr, gr, q, k, v = _inproj(xp2, pl.BlockSpec((tm_a, d_model), lambda i: (i, 0)), n_p // tm_a, tm_a,
                              g_mix, w_in_b, d_rnn, d_attn, kv_dim)
    m_rnn, h_last = _rglru_prompt(xr, gr, cw_r, cb_r, wg, ba, bx, lam, g_rnn, batch, seq,
                                  _pick(RGLRU_ROWS, seq))
    m_attn = _swa_prompt(q, k, v, code_p, tbl, sinks, gn_attn_p, batch, seq)
    y_p, tail_g, tail_v = _ffn_prompt(xp2, m_rnn, m_attn, w_out_r, w_out_a, g_ffn, w_up_b, cw_f, cb_f,
                                      w_down_b, g_fin, batch, seq, _pick(FFN_ROWS, seq),
                                      _pick(FFN_COLS_PROMPT, d_ff))
    y_prompt = y_p.reshape(batch, seq, d_model)
    p_rnn_conv = xr.reshape(batch, seq, d_rnn)[:, seq - 3:, :][None]
    p_rnn_h = h_last[None]
    last_win = lambda a: a.reshape(batch, seq, kv_dim)[:, seq - win:, :].reshape(
        1, batch, win, KV_HEADS, HEAD_DIM)
    p_win_k, p_win_v = last_win(k), last_win(v)
    p_ffn_conv = jnp.concatenate([tail_g[:, SUBLANES - 2:, :], tail_v[:, SUBLANES - 2:, :]], axis=-1)[None]

    n_s = nb * steps
    xs_tm = x_sample.transpose(1, 0, 2).reshape(n_s, d_model)
    xr_s, gr_s, q_s, k_s, v_s = _inproj(xs_tm, pl.BlockSpec((nb, d_model), lambda t: (t, 0)), steps, nb,
                                        g_mix, w_in_b, d_rnn, d_attn, kv_dim)
    conv_tm = state_rnn_conv[0].transpose(1, 0, 2).reshape(3 * nb, d_rnn)
    m_rnn_s, h_new = _rglru_sample(xr_s, gr_s, conv_tm, state_rnn_h[0], cw_r, cb_r, wg, ba, bx, lam, g_rnn,
                                   nb, steps)
    ncg = KV_HEADS // 2
    qs = q_s.reshape(steps, nb, ncg, GQA_GROUP, LANES).transpose(1, 2, 3, 0, 4).reshape(
        nb, ncg, GQA_GROUP * steps, LANES)
    k_tm = k_s.reshape(steps, nb, kv_dim)
    v_tm = v_s.reshape(steps, nb, kv_dim)
    pad = ((0, 0), (0, npad - steps), (0, 0))
    ck_t = cache_win_k[0].transpose(0, 2, 3, 1).reshape(nb, kv_dim, win)
    cv_t = cache_win_v[0].transpose(0, 2, 3, 1).reshape(nb, kv_dim, win)
    bb = SUBLANES
    o_s, ck_new, cv_new = _swa_sample(qs, ck_t, cv_t, jnp.pad(k_tm.transpose(1, 0, 2), pad),
                                      jnp.pad(v_tm.transpose(1, 0, 2), pad), code_c, code_n, tbl, sinks, steps, bb)
    ya_s = o_s.reshape(nb, ncg, GQA_GROUP, steps, LANES).transpose(3, 0, 1, 2, 4).reshape(n_s, d_attn)
    ffn2d = state_ffn_conv[0].reshape(nb, 2 * 2 * d_ff)
    y_s, ns_g, ns_v = _ffn_sample(xs_tm, m_rnn_s, ya_s, gn_attn_p, w_out_r, w_out_a, g_ffn, w_up_b, cw_f, cb_f,
                                  w_down_b, g_fin, ffn2d, nb, steps, _pick(FFN_COLS_SAMPLE, d_ff))
    y_sample = y_s.reshape(steps, nb, d_model).transpose(1, 0, 2)
    s_rnn_conv = xr_s.reshape(steps, nb, d_rnn)[steps - 3:].transpose(1, 0, 2)[None]
    s_rnn_h = h_new[None]

    to_cache = lambda w_t: w_t.reshape(nb, KV_HEADS, HEAD_DIM, win).transpose(0, 3, 1, 2)[None]
    s_win_k, s_win_v = to_cache(ck_new), to_cache(cv_new)
    s_ffn_conv = jnp.concatenate([ns_g, ns_v], axis=-1).transpose(1, 0, 2)[None]

    return (y_prompt, y_sample, p_rnn_conv, p_rnn_h, p_win_k, p_win_v, p_ffn_conv,
            s_rnn_conv, s_rnn_h, s_win_k, s_win_v, s_ffn_conv)
```

```python
import functools
import math

import numpy as np
import jax
import jax.numpy as jnp
from jax import lax
from jax.experimental import pallas as pl
from jax.experimental.pallas import tpu as pltpu

F32 = jnp.float32
BF16 = jnp.bfloat16

HEAD_DIM = 64
KV_HEADS = 4
N_HEADS = 16
GQA_GROUP = N_HEADS // KV_HEADS
RNN_BLOCKS = 16
RG_C = 8.0
WINDOW = 128
N_BUCKETS = 32
MAX_EXACT = N_BUCKETS // 2
REL_MAX_DIST = 128
EPS = 1e-6
NEG_INF = -1e30
ATTN_SCALE = HEAD_DIM ** -0.5

LANES = 128
SUBLANES = 8
VMEM_LIMIT_BYTES = 56 * 1024 * 1024
FFN_VMEM_LIMIT_BYTES = 61 * 1024 * 1024

GATE_TILE = 256

PROJ_ROWS = (512, WINDOW)
RGLRU_ROWS = (1024, 512, WINDOW)
SWA_BLOCKS_PER_STEP = (8, 4, 2, 1)
FFN_ROWS = (1024, 512, 256)
FFN_COLS_PROMPT = (768, 512)
FFN_COLS_SAMPLE = (768, 512)


def _pick(options, n):
    return next(t for t in options if n % t == 0)


def _t5_bucket_np(d):
    n = np.maximum(d, 0)
    nf = np.maximum(n, 1).astype(np.float32)
    large = MAX_EXACT + (np.log(nf / MAX_EXACT) / math.log(REL_MAX_DIST / MAX_EXACT)
                         * (N_BUCKETS - MAX_EXACT)).astype(np.int32)
    large = np.minimum(large, N_BUCKETS - 1)
    return np.where(n < MAX_EXACT, n, large).astype(np.int32)


def _rms(x, g):
    ms = jnp.mean(x * x, axis=-1, keepdims=True)
    return (x * lax.rsqrt(ms + EPS)) * g


def _softplus(x):
    return jnp.maximum(x, 0.0) + jnp.log1p(jnp.exp(-jnp.abs(x)))


def _const_spec(shape):
    nd = len(shape)
    return pl.BlockSpec(shape, lambda *_: (0,) * nd, pipeline_mode=pl.Buffered(1))


def _smem_spec():
    return pl.BlockSpec(memory_space=pltpu.SMEM)


def _inproj_body(x_ref, g_ref, w_ref, xr_ref, gr_ref, q_ref, k_ref, v_ref):
    xn = _rms(x_ref[...], g_ref[...]).astype(BF16)
    d_rnn, d_attn, kv_dim = xr_ref.shape[-1], q_ref.shape[-1], k_ref.shape[-1]
    p = jnp.dot(xn, w_ref[...], preferred_element_type=F32)
    o1, o2, o3, o4 = d_rnn, 2 * d_rnn, 2 * d_rnn + d_attn, 2 * d_rnn + d_attn + kv_dim
    xr_ref[...] = p[:, :o1]
    gr_ref[...] = p[:, o1:o2]
    k_ref[...] = p[:, o3:o4]
    v_ref[...] = p[:, o4:]
    pq = p[:, o2:o3] * ATTN_SCALE
    lane = lax.broadcasted_iota(jnp.int32, (pq.shape[0], LANES), 1)
    blocks = []
    for cg in range(KV_HEADS // 2):
        for g in range(GQA_GROUP):
            ha, hb = 2 * GQA_GROUP * cg + g, 2 * GQA_GROUP * cg + GQA_GROUP + g
            a = pq[:, (ha // 2) * LANES:(ha // 2 + 1) * LANES]
            b = pq[:, (hb // 2) * LANES:(hb // 2 + 1) * LANES]
            if g % 2 == 0:
                blocks.append(jnp.where(lane < HEAD_DIM, a, pltpu.roll(b, HEAD_DIM, 1)))
            else:
                blocks.append(jnp.where(lane < HEAD_DIM, pltpu.roll(a, HEAD_DIM, 1), b))
    q_ref[...] = jnp.concatenate(blocks, axis=1).astype(BF16)


def _inproj(x2d, x_spec, n_steps, tm, g, w, d_rnn, d_attn, kv_dim):
    n = n_steps * tm
    d_model = g.shape[-1]
    row = lambda i: (i, 0)
    out_shape = (
        jax.ShapeDtypeStruct((n, d_rnn), F32),
        jax.ShapeDtypeStruct((n, d_rnn), F32),
        jax.ShapeDtypeStruct((n, d_attn), BF16),
        jax.ShapeDtypeStruct((n, kv_dim), F32),
        jax.ShapeDtypeStruct((n, kv_dim), F32),
    )
    out_specs = (
        pl.BlockSpec((tm, d_rnn), row),
        pl.BlockSpec((tm, d_rnn), row),
        pl.BlockSpec((tm, d_attn), row),
        pl.BlockSpec((tm, kv_dim), row),
        pl.BlockSpec((tm, kv_dim), row),
    )
    return pl.pallas_call(
        _inproj_body,
        grid=(n_steps,),
        in_specs=[x_spec, _const_spec((1, d_model)), _const_spec(w.shape)],
        out_specs=out_specs,
        out_shape=out_shape,
        compiler_params=pltpu.CompilerParams(
            dimension_semantics=("arbitrary",), vmem_limit_bytes=VMEM_LIMIT_BYTES),
        name="inproj",
    )(x2d, g, w)


def _rglru_gates(xc, wg_ref, ba, bx, lam):
    d_rnn = xc.shape[-1]
    pre_a, pre_x = [], []
    for c in range(d_rnn // GATE_TILE):
        xb = xc[:, c * GATE_TILE:(c + 1) * GATE_TILE].astype(BF16)
        pre = jnp.dot(xb, wg_ref[c], preferred_element_type=F32)
        pre_a.append(pre[:, :GATE_TILE])
        pre_x.append(pre[:, GATE_TILE:])
    r = jax.nn.sigmoid(jnp.concatenate(pre_a, axis=1) + ba)
    i = jax.nn.sigmoid(jnp.concatenate(pre_x, axis=1) + bx)
    neg_log_a = r * (RG_C * _softplus(-lam))
    a = jnp.exp(-neg_log_a)
    one_minus_a2 = jnp.tanh(neg_log_a) * (a * a + 1.0)
    root = jnp.where(one_minus_a2 > 0.0, one_minus_a2 * lax.rsqrt(one_minus_a2), 0.0)
    u = root * (i * xc)
    return a, u


def _rglru_prompt_body(xr_ref, gr_ref, cw_ref, cb_ref, wg_ref, ba_ref, bx_ref, lam_ref, gn_ref,
                       out_ref, hlast_ref, xbuf, abuf, ubuf, hbuf, hc_ref, *, tt, conv_w):
    ti = pl.program_id(1)
    d_rnn = xr_ref.shape[-1]

    @pl.when(ti == 0)
    def _():
        xbuf[0:SUBLANES, :] = jnp.zeros((SUBLANES, d_rnn), F32)
        hc_ref[...] = jnp.zeros_like(hc_ref)

    x = xr_ref[...]
    hist = xbuf[...]
    xc = cw_ref[conv_w - 1:conv_w, :] * x
    for k in range(conv_w - 1):
        xc = xc + cw_ref[k:k + 1, :] * _shift_rows(x, hist, conv_w - 1 - k)
    xc = xc + cb_ref[...]
    xbuf[...] = x[tt - SUBLANES:, :]

    a, u = _rglru_gates(xc, wg_ref, ba_ref[...], bx_ref[...], lam_ref[...])
    abuf[...] = a
    ubuf[...] = u

    row = lax.broadcasted_iota(jnp.int32, (SUBLANES, d_rnn), 0)

    def group(gidx, hc):
        r0 = pl.multiple_of(gidx * SUBLANES, SUBLANES)
        ag = abuf[pl.ds(r0, SUBLANES), :]
        ug = ubuf[pl.ds(r0, SUBLANES), :]
        for k in (1, 2, 4):
            a_prev = jnp.where(row >= k, pltpu.roll(ag, k, 0), 1.0)
            u_prev = jnp.where(row >= k, pltpu.roll(ug, k, 0), 0.0)
            ug = ag * u_prev + ug
            ag = ag * a_prev
        h = ag * hc + ug
        hbuf[pl.ds(r0, SUBLANES), :] = h
        return jnp.broadcast_to(h[SUBLANES - 1:SUBLANES, :], (SUBLANES, d_rnn))

    hc = lax.fori_loop(0, tt // SUBLANES, group, hc_ref[...], unroll=2)
    hc_ref[...] = hc
    hlast_ref[0] = hc

    y = jax.nn.gelu(gr_ref[...]) * hbuf[...]
    out_ref[...] = _rms(y, gn_ref[...]).astype(BF16)


def _rglru_prompt(xr, gr, cw, cb, wg, ba, bx, lam, gn, batch, seq, tt):
    d_rnn = xr.shape[-1]
    nt = seq // tt
    conv_w = cw.shape[0]
    tile = lambda b, t: (b * nt + t, 0)
    out, hlast = pl.pallas_call(
        functools.partial(_rglru_prompt_body, tt=tt, conv_w=conv_w),
        grid=(batch, nt),
        in_specs=[
            pl.BlockSpec((tt, d_rnn), tile),
            pl.BlockSpec((tt, d_rnn), tile),
            _const_spec(cw.shape), _const_spec(cb.shape), _const_spec(wg.shape),
            _const_spec(ba.shape), _const_spec(bx.shape), _const_spec(lam.shape), _const_spec(gn.shape),
        ],
        out_specs=(
            pl.BlockSpec((tt, d_rnn), tile),
            pl.BlockSpec((1, SUBLANES, d_rnn), lambda b, t: (b, 0, 0)),
        ),
        out_shape=(
            jax.ShapeDtypeStruct((batch * seq, d_rnn), BF16),
            jax.ShapeDtypeStruct((batch, SUBLANES, d_rnn), F32),
        ),
        scratch_shapes=[
            pltpu.VMEM((SUBLANES, d_rnn), F32),
            pltpu.VMEM((tt, d_rnn), F32),
            pltpu.VMEM((tt, d_rnn), F32),
            pltpu.VMEM((tt, d_rnn), F32),
            pltpu.VMEM((SUBLANES, d_rnn), F32),
        ],
        compiler_params=pltpu.CompilerParams(
            dimension_semantics=("arbitrary", "arbitrary"), vmem_limit_bytes=VMEM_LIMIT_BYTES),
        name="rglru_prompt",
    )(xr, gr, cw, cb, wg, ba, bx, lam, gn)
    return out, hlast[:, 0, :]


def _rglru_sample_body(xr_ref, gr_ref, c0_ref, c1_ref, c2_ref, h0_ref, cw_ref, cb_ref, wg_ref,
                       ba_ref, bx_ref, lam_ref, gn_ref, out_ref, hnew_ref, *, nb, steps):
    x = xr_ref[...]
    hist = [c0_ref[...], c1_ref[...], c2_ref[...]]
    xs = [x[t * nb:(t + 1) * nb, :] for t in range(steps)]
    xp = hist + xs
    conv_w = len(hist) + 1
    xc = cw_ref[conv_w - 1:conv_w, :] * x
    for k in range(conv_w - 1):
        shifted = jnp.concatenate(xp[k:k + steps], axis=0)
        xc = xc + cw_ref[k:k + 1, :] * shifted
    xc = xc + cb_ref[...]
    a, u = _rglru_gates(xc, wg_ref, ba_ref[...], bx_ref[...], lam_ref[...])
    h = h0_ref[...]
    hs = []
    for t in range(steps):
        h = a[t * nb:(t + 1) * nb, :] * h + u[t * nb:(t + 1) * nb, :]
        hs.append(h)
    hnew_ref[...] = h
    y = jax.nn.gelu(gr_ref[...]) * jnp.concatenate(hs, axis=0)
    out_ref[...] = _rms(y, gn_ref[...]).astype(BF16)


def _rglru_sample(xr, gr, conv_state2d, h0, cw, cb, wg, ba, bx, lam, gn, nb, steps):
    d_rnn = xr.shape[-1]
    n = nb * steps
    assert cw.shape[0] == 4
    full = lambda shape: pl.BlockSpec(shape, lambda i: (0,) * len(shape))
    return pl.pallas_call(
        functools.partial(_rglru_sample_body, nb=nb, steps=steps),
        grid=(1,),
        in_specs=[
            full((n, d_rnn)), full((n, d_rnn)),
            pl.BlockSpec((nb, d_rnn), lambda i: (0, 0)),
            pl.BlockSpec((nb, d_rnn), lambda i: (1, 0)),
            pl.BlockSpec((nb, d_rnn), lambda i: (2, 0)),
            full((nb, d_rnn)),
            full(cw.shape), full(cb.shape), full(wg.shape), full(ba.shape), full(bx.shape),
            full(lam.shape), full(gn.shape),
        ],
        out_specs=(full((n, d_rnn)), full((nb, d_rnn))),
        out_shape=(jax.ShapeDtypeStruct((n, d_rnn), BF16), jax.ShapeDtypeStruct((nb, d_rnn), F32)),
        compiler_params=pltpu.CompilerParams(
            dimension_semantics=("arbitrary",), vmem_limit_bytes=VMEM_LIMIT_BYTES),
        name="rglru_sample",
    )(xr, gr, conv_state2d, conv_state2d, conv_state2d, h0, cw, cb, wg, ba, bx, lam, gn)


def _swa_prompt_body(code_ref, tbl_ref, sink_ref, q_ref, kp_ref, kc_ref, vp_ref, vc_ref, gn_ref,
                     out_ref, bias_ref):
    j = pl.program_id(1)
    blk = WINDOW
    rows = GQA_GROUP * blk

    @pl.when((pl.program_id(0) == 0) & (j == 0))
    def _():
        code = code_ref[...]
        qi = lax.broadcasted_iota(jnp.int32, code.shape, 0)
        kj = lax.broadcasted_iota(jnp.int32, code.shape, 1)
        dist = blk + qi - kj
        in_window = (dist & -WINDOW) == 0
        for h in range(N_HEADS):
            def pick(b, acc, h=h):
                return jnp.where(code == b, tbl_ref[b * N_HEADS + h], acc)
            bias_h = lax.fori_loop(0, N_BUCKETS, pick, jnp.zeros(code.shape, F32))
            bias_h = jnp.where(in_window, bias_h, NEG_INF)
            kv, g = divmod(h, GQA_GROUP)
            bias_ref[1, kv, g * blk:(g + 1) * blk, :] = bias_h
            bias_ref[0, kv, g * blk:(g + 1) * blk, :] = jnp.where(kj >= blk, bias_h, NEG_INF)

    lane = lax.broadcasted_iota(jnp.int32, (2 * blk, LANES), 1)
    row_g = lax.broadcasted_iota(jnp.int32, (rows, 1), 0) // blk
    for sub in range(q_ref.shape[0] // blk):
        rows_q = slice(sub * blk, (sub + 1) * blk)
        if sub == 0:
            has_prev = jnp.where(j > 0, 1, 0)
            k_prev, v_prev = kp_ref[...], vp_ref[...]
        else:
            has_prev = 1
            k_prev, v_prev = kc_ref[(sub - 1) * blk:sub * blk, :], vc_ref[(sub - 1) * blk:sub * blk, :]
        kband = jnp.concatenate([k_prev, kc_ref[rows_q, :]], axis=0)
        vband = jnp.concatenate([v_prev, vc_ref[rows_q, :]], axis=0)
        y = _swa_block(q_ref[rows_q, :], kband, vband, bias_ref, has_prev, sink_ref, lane, row_g)
        out_ref[rows_q, :] = _rms(y, gn_ref[...]).astype(BF16)


def _swa_block(q, kband, vband, bias_ref, has_prev, sink_ref, lane, row_g):
    blk = q.shape[0]
    rows = GQA_GROUP * blk
    outs = []
    for cg in range(KV_HEADS // 2):
        qs = jnp.concatenate(
            [q[:, (cg * GQA_GROUP + g) * LANES:(cg * GQA_GROUP + g + 1) * LANES] for g in range(GQA_GROUP)],
            axis=0)
        ka = kband[:, cg * LANES:(cg + 1) * LANES]
        va = vband[:, cg * LANES:(cg + 1) * LANES]
        o = None
        for par in range(2):
            kv = 2 * cg + par
            half = (lane < HEAD_DIM) if par == 0 else (lane >= HEAD_DIM)
            km = jnp.where(half, ka, 0.0).astype(BF16)
            vm = jnp.where(half, va, 0.0).astype(BF16)
            s = lax.dot_general(qs, km, (((1,), (1,)), ((), ())), preferred_element_type=F32)
            s = s + bias_ref[has_prev, kv]
            sink = jnp.zeros((rows, 1), F32)
            for g in range(GQA_GROUP):
                sink = jnp.where(row_g == g, sink_ref[kv * GQA_GROUP + g], sink)
            m = jnp.maximum(jnp.max(s, axis=-1, keepdims=True), sink)
            p = jnp.exp(s - m)
            denom = jnp.sum(p, axis=-1, keepdims=True) + jnp.exp(sink - m)
            part = jnp.dot(p.astype(BF16), vm, preferred_element_type=F32) * (1.0 / denom)
            o = part if o is None else o + part
        outs.extend(o[g * blk:(g + 1) * blk, :] for g in range(GQA_GROUP))
    return jnp.concatenate(outs, axis=1)


def _swa_prompt(q, k, v, code, tbl, sinks, gn, batch, seq):
    blk = WINDOW
    per_step = _pick(SWA_BLOCKS_PER_STEP, seq // blk)
    nb = seq // (per_step * blk)
    d_attn = q.shape[-1]
    kv_dim = k.shape[-1]
    cur = lambda b, j: (b * nb + j, 0)
    prev = lambda b, j: (per_step * (b * nb + j) - jnp.where(j > 0, 1, 0), 0)
    return pl.pallas_call(
        _swa_prompt_body,
        grid=(batch, nb),
        in_specs=[
            _const_spec(code.shape), _smem_spec(), _smem_spec(),
            pl.BlockSpec((per_step * blk, d_attn), cur),
            pl.BlockSpec((blk, kv_dim), prev), pl.BlockSpec((per_step * blk, kv_dim), cur),
            pl.BlockSpec((blk, kv_dim), prev), pl.BlockSpec((per_step * blk, kv_dim), cur),
            _const_spec(gn.shape),
        ],
        out_specs=pl.BlockSpec((per_step * blk, d_attn), cur),
        out_shape=jax.ShapeDtypeStruct((batch * seq, d_attn), BF16),
        scratch_shapes=[pltpu.VMEM((2, KV_HEADS, GQA_GROUP * blk, 2 * blk), F32)],
        compiler_params=pltpu.CompilerParams(
            dimension_semantics=("arbitrary", "arbitrary"), vmem_limit_bytes=VMEM_LIMIT_BYTES),
        name="swa_prompt",
    )(code, tbl, sinks, q, k, k, v, v, gn)


def _slide_window(win_ref, new_ref, out_ref, steps):
    bb, kv_dim, win = win_ref.shape
    npad = new_ref.shape[1]
    new_t = new_ref[...].reshape(bb * npad, kv_dim).T
    lane = lax.broadcasted_iota(jnp.int32, (kv_dim, win), 1)
    for b in range(bb):
        placed = pltpu.roll(new_t, (win - steps - b * npad) % win, 1)
        shifted = pltpu.roll(win_ref[b], win - steps, 1)
        out_ref[b] = jnp.where(lane >= win - steps, placed, shifted)


def _swa_sample_body(codec_ref, coden_ref, tbl_ref, sink_ref, q_ref, ck_ref, cv_ref, kn_ref, vn_ref,
                     out_ref, ck_out_ref, cv_out_ref, biasc_ref, biasn_ref, sinkc_ref, *, steps):
    bb = q_ref.shape[0]
    nrow = N_HEADS * steps
    win = ck_ref.shape[2]
    npad = kn_ref.shape[1]

    @pl.when(pl.program_id(0) == 0)
    def _():
        codec = codec_ref[...]
        coden = coden_ref[...]
        hrow = lax.broadcasted_iota(jnp.int32, (nrow, 1), 0) // steps

        def pick(idx, accs):
            ac, an = accs
            val = tbl_ref[idx]
            return jnp.where(codec == idx, val, ac), jnp.where(coden == idx, val, an)
        bc, bn = lax.fori_loop(0, N_BUCKETS * N_HEADS, pick,
                               (jnp.zeros(codec.shape, F32), jnp.zeros(coden.shape, F32)))
        biasc_ref[...] = bc
        biasn_ref[...] = bn

        def pick_sink(h, acc):
            return jnp.where(hrow == h, sink_ref[h], acc)
        sinkc_ref[...] = lax.fori_loop(0, N_HEADS, pick_sink, jnp.zeros((nrow, 1), F32))

    q = q_ref[...].astype(F32)
    lane = lax.broadcasted_iota(jnp.int32, (bb, GQA_GROUP * steps, LANES), 2)
    zeros = jnp.zeros((bb, GQA_GROUP * steps, LANES), F32)
    pieces = []
    for cg in range(KV_HEADS // 2):
        for par in range(2):
            half = (lane < HEAD_DIM) if par == 0 else (lane >= HEAD_DIM)
            qm = jnp.where(half, q[:, cg], zeros)
            pieces.append(jnp.concatenate([qm, zeros] if cg == 0 else [zeros, qm], axis=2))
    qm = jnp.concatenate(pieces, axis=1).astype(BF16)

    ck = ck_ref[...].astype(BF16)
    kn = kn_ref[...].astype(BF16)
    s_c = jnp.einsum("bqd,bdk->bqk", qm, ck, preferred_element_type=F32)
    s_n = jnp.einsum("bqd,bkd->bqk", qm, kn, preferred_element_type=F32)

    t_c = lax.broadcasted_iota(jnp.int32, (nrow, win), 0) % steps
    k_c = lax.broadcasted_iota(jnp.int32, (nrow, win), 1)
    valid_c = k_c > t_c
    t_n = lax.broadcasted_iota(jnp.int32, (nrow, npad), 0) % steps
    k_n = lax.broadcasted_iota(jnp.int32, (nrow, npad), 1)
    valid_n = k_n <= t_n

    s_c = jnp.where(valid_c[None], s_c + biasc_ref[...][None], NEG_INF)
    s_n = jnp.where(valid_n[None], s_n + biasn_ref[...][None], NEG_INF)
    sink = sinkc_ref[...][None]
    m = jnp.maximum(jnp.maximum(jnp.max(s_c, axis=-1, keepdims=True),
                                jnp.max(s_n, axis=-1, keepdims=True)), sink)
    p_c = jnp.exp(s_c - m)
    p_n = jnp.exp(s_n - m)
    denom = (jnp.sum(p_c, axis=-1, keepdims=True) + jnp.sum(p_n, axis=-1, keepdims=True)
             + jnp.exp(sink - m))
    r = 1.0 / denom
    w_c = (p_c * r).astype(BF16)
    w_n = (p_n * r).astype(BF16)
    o = (jnp.einsum("bqk,bdk->bqd", w_c, cv_ref[...].astype(BF16), preferred_element_type=F32)
         + jnp.einsum("bqk,bkd->bqd", w_n, vn_ref[...].astype(BF16), preferred_element_type=F32))
    gt = GQA_GROUP * steps
    lane_o = lax.broadcasted_iota(jnp.int32, (bb, gt, LANES), 2)
    for cg in range(KV_HEADS // 2):
        lo = o[:, cg * 2 * gt:cg * 2 * gt + gt, cg * LANES:(cg + 1) * LANES]
        hi = o[:, cg * 2 * gt + gt:(cg + 1) * 2 * gt, cg * LANES:(cg + 1) * LANES]
        out_ref[:, cg] = jnp.where(lane_o < HEAD_DIM, lo, hi)

    _slide_window(ck_ref, kn_ref, ck_out_ref, steps)
    _slide_window(cv_ref, vn_ref, cv_out_ref, steps)


def _swa_sample(qs, ck, cv, kn, vn, codec, coden, tbl, sinks, steps, bb):
    nbatch, ncg, gt, _ = qs.shape
    kv_dim, win = ck.shape[1], ck.shape[2]
    npad = kn.shape[1]
    nrow = N_HEADS * steps
    assert bb * npad == win
    blk4 = lambda i: (i, 0, 0, 0)
    blk3 = lambda i: (i, 0, 0)
    return pl.pallas_call(
        functools.partial(_swa_sample_body, steps=steps),
        grid=(nbatch // bb,),
        in_specs=[
            _const_spec(codec.shape), _const_spec(coden.shape), _smem_spec(), _smem_spec(),
            pl.BlockSpec((bb, ncg, gt, LANES), blk4),
            pl.BlockSpec((bb, kv_dim, win), blk3), pl.BlockSpec((bb, kv_dim, win), blk3),
            pl.BlockSpec((bb, npad, kv_dim), blk3), pl.BlockSpec((bb, npad, kv_dim), blk3),
        ],
        out_specs=(pl.BlockSpec((bb, ncg, gt, LANES), blk4),
                   pl.BlockSpec((bb, kv_dim, win), blk3), pl.BlockSpec((bb, kv_dim, win), blk3)),
        out_shape=(jax.ShapeDtypeStruct((nbatch, ncg, gt, LANES), F32),
                   jax.ShapeDtypeStruct((nbatch, kv_dim, win), F32),
                   jax.ShapeDtypeStruct((nbatch, kv_dim, win), F32)),
        scratch_shapes=[pltpu.VMEM((nrow, win), F32), pltpu.VMEM((nrow, npad), F32),
                        pltpu.VMEM((nrow, 1), F32)],
        compiler_params=pltpu.CompilerParams(
            dimension_semantics=("arbitrary",), vmem_limit_bytes=VMEM_LIMIT_BYTES),
        name="swa_sample",
    )(codec, coden, tbl, sinks, qs, ck, cv, kn, vn)


def _shift_rows(up, hist, k):
    rolled = pltpu.roll(up, k, 0)
    row = lax.broadcasted_iota(jnp.int32, hist.shape, 0)
    head = jnp.where(row < k, pltpu.roll(hist, k, 0), rolled[0:SUBLANES, :])
    return jnp.concatenate([head, rolled[SUBLANES:, :]], axis=0)


def _ffn_conv(up, hist, cw, cb, *, tm, sample_nb):
    if sample_nb is None:
        prev2 = _shift_rows(up, hist, 2)
        prev1 = _shift_rows(up, hist, 1)
    else:
        s0, s1 = hist
        prev2 = jnp.concatenate([s0, s1] + ([up[:tm - 2 * sample_nb, :]] if tm > 2 * sample_nb else []), axis=0)
        prev1 = jnp.concatenate([s1, up[:tm - sample_nb, :]], axis=0)
    return cw[0:1, :] * prev2 + cw[1:2, :] * prev1 + cw[2:3, :] * up + cb


FFN_SLAB = 256


def _ffn_prompt_body(x_hbm, mr_ref, ma_ref, wor_ref, woa_ref, gf_ref, wg_ref, wv_ref, cwg_ref, cwv_ref, cbg_ref,
                     cbv_ref, wd_ref, gfin_ref, y_ref, tg_ref, tv_ref, hn_ref, car_g, car_v, x_sem,
                     *, tm, tiles_per_seq):
    i = pl.program_id(0)
    j = pl.program_id(1)
    nj = pl.num_programs(1)
    n_slab = tm // FFN_SLAB
    slab = lambda r: slice(r * FFN_SLAB, (r + 1) * FFN_SLAB)

    @pl.when(j == 0)
    def _():
        x_copy = pltpu.make_async_copy(x_hbm.at[pl.ds(pl.multiple_of(i * tm, tm), tm), :], y_ref, x_sem)
        x_copy.start()

        def out_proj(r):
            return (jnp.dot(mr_ref[slab(r), :], wor_ref[...], preferred_element_type=F32)
                    + jnp.dot(ma_ref[slab(r), :], woa_ref[...], preferred_element_type=F32))

        pending = out_proj(0)
        x_copy.wait()
        for r in range(n_slab):
            d = pending
            if r + 1 < n_slab:
                pending = out_proj(r + 1)
            h = y_ref[slab(r), :] + d
            y_ref[slab(r), :] = h
            hn_ref[slab(r), :] = _rms(h, gf_ref[...]).astype(BF16)

    @pl.when((i == 0) & (j == 0))
    def _():
        car_g[...] = jnp.zeros_like(car_g)
        car_v[...] = jnp.zeros_like(car_v)

    cwg, cwv, cbg, cbv = cwg_ref[...], cwv_ref[...], cbg_ref[...], cbv_ref[...]

    def up_proj(r):
        hn = hn_ref[slab(r), :]
        return (jnp.dot(hn, wg_ref[...], preferred_element_type=F32),
                jnp.dot(hn, wv_ref[...], preferred_element_type=F32))

    seq_start = i % tiles_per_seq == 0
    hist_g = jnp.where(seq_start, 0.0, car_g[j])
    hist_v = jnp.where(seq_start, 0.0, car_v[j])
    pending = up_proj(0)
    act = None
    for r in range(n_slab):
        up_g, up_v = pending
        if r + 1 < n_slab:
            pending = up_proj(r + 1)
        if act is not None:
            y_ref[slab(r - 1), :] += jnp.dot(act, wd_ref[...], preferred_element_type=F32)
        c_g = _ffn_conv(up_g, hist_g, cwg, cbg, tm=FFN_SLAB, sample_nb=None)
        c_v = _ffn_conv(up_v, hist_v, cwv, cbv, tm=FFN_SLAB, sample_nb=None)
        hist_g = up_g[FFN_SLAB - SUBLANES:, :]
        hist_v = up_v[FFN_SLAB - SUBLANES:, :]
        act = (jax.nn.gelu(c_g) * c_v).astype(BF16)
    y_ref[slab(n_slab - 1), :] += jnp.dot(act, wd_ref[...], preferred_element_type=F32)
    car_g[j] = hist_g
    car_v[j] = hist_v
    tg_ref[0] = hist_g
    tv_ref[0] = hist_v

    @pl.when(j == nj - 1)
    def _():
        for r in range(n_slab):
            y_ref[slab(r), :] = _rms(y_ref[slab(r), :], gfin_ref[...])


def _ffn_sample_body(x_ref, mr_ref, ma_ref, gna_ref, wor_ref, woa_ref, gf_ref, wg_ref, wv_ref, cwg_ref, cwv_ref,
                     cbg_ref, cbv_ref, wd_ref, gfin_ref, s0g_ref, s1g_ref, s0v_ref, s1v_ref,
                     y_ref, tg_ref, tv_ref, hn_ref, *, nb):
    j = pl.program_id(1)
    nj = pl.num_programs(1)
    tm = x_ref.shape[0]

    @pl.when(j == 0)
    def _():
        ma = _rms(ma_ref[...], gna_ref[...]).astype(BF16)
        h = (x_ref[...] + jnp.dot(mr_ref[...], wor_ref[...], preferred_element_type=F32)
             + jnp.dot(ma, woa_ref[...], preferred_element_type=F32))
        y_ref[...] = h
        hn_ref[...] = _rms(h, gf_ref[...]).astype(BF16)

    rows_per_slab = 2 * nb
    n_slab = tm // rows_per_slab
    slab = lambda r: slice(r * rows_per_slab, (r + 1) * rows_per_slab)
    cwg, cwv, cbg, cbv = cwg_ref[...], cwv_ref[...], cbg_ref[...], cbv_ref[...]

    def up_proj(r):
        hn = hn_ref[slab(r), :]
        return (jnp.dot(hn, wg_ref[...], preferred_element_type=F32),
                jnp.dot(hn, wv_ref[...], preferred_element_type=F32))

    hist_g = (s0g_ref[...], s1g_ref[...])
    hist_v = (s0v_ref[...], s1v_ref[...])
    pending = up_proj(0)
    act = None
    for r in range(n_slab):
        up_g, up_v = pending
        if r + 1 < n_slab:
            pending = up_proj(r + 1)
        if act is not None:
            y_ref[slab(r - 1), :] += jnp.dot(act, wd_ref[...], preferred_element_type=F32)
        c_g = _ffn_conv(up_g, hist_g, cwg, cbg, tm=rows_per_slab, sample_nb=nb)
        c_v = _ffn_conv(up_v, hist_v, cwv, cbv, tm=rows_per_slab, sample_nb=nb)
        hist_g = (up_g[:nb, :], up_g[nb:, :])
        hist_v = (up_v[:nb, :], up_v[nb:, :])
        act = (jax.nn.gelu(c_g) * c_v).astype(BF16)
    y_ref[slab(n_slab - 1), :] += jnp.dot(act, wd_ref[...], preferred_element_type=F32)
    tg_ref[0], tg_ref[1] = hist_g
    tv_ref[0], tv_ref[1] = hist_v

    @pl.when(j == nj - 1)
    def _():
        y_ref[...] = _rms(y_ref[...], gfin_ref[...])


def _ffn_prompt(x, mr, ma, wor, woa, gf, wup, cw, cb, wd, gfin, batch, seq, tm, tf):
    n, d_model = x.shape
    d_half = mr.shape[-1]
    d_ff = wd.shape[0]
    ni, nj = n // tm, d_ff // tf
    tiles_per_seq = seq // tm
    row = lambda i, j: (i, 0)
    gate_col = lambda i, j: (0, j)
    val_col = lambda i, j: (0, nj + j)
    tail = lambda i, j: (i, 0, j)
    y, tg, tv = pl.pallas_call(
        functools.partial(_ffn_prompt_body, tm=tm, tiles_per_seq=tiles_per_seq),
        grid=(ni, nj),
        in_specs=[
            pl.BlockSpec(memory_space=pl.ANY),
            pl.BlockSpec((tm, d_half), row), pl.BlockSpec((tm, d_half), row),
            _const_spec(wor.shape), _const_spec(woa.shape), _const_spec(gf.shape),
            pl.BlockSpec((d_model, tf), gate_col), pl.BlockSpec((d_model, tf), val_col),
            pl.BlockSpec((cw.shape[0], tf), gate_col), pl.BlockSpec((cw.shape[0], tf), val_col),
            pl.BlockSpec((1, tf), gate_col), pl.BlockSpec((1, tf), val_col),
            pl.BlockSpec((tf, d_model), lambda i, j: (j, 0)),
            _const_spec(gfin.shape),
        ],
        out_specs=(
            pl.BlockSpec((tm, d_model), row),
            pl.BlockSpec((1, SUBLANES, tf), tail),
            pl.BlockSpec((1, SUBLANES, tf), tail),
        ),
        out_shape=(
            jax.ShapeDtypeStruct((n, d_model), F32),
            jax.ShapeDtypeStruct((ni, SUBLANES, d_ff), F32),
            jax.ShapeDtypeStruct((ni, SUBLANES, d_ff), F32),
        ),
        scratch_shapes=[
            pltpu.VMEM((tm, d_model), BF16),
            pltpu.VMEM((nj, SUBLANES, tf), F32), pltpu.VMEM((nj, SUBLANES, tf), F32),
            pltpu.SemaphoreType.DMA(()),
        ],
        compiler_params=pltpu.CompilerParams(
            dimension_semantics=("arbitrary", "arbitrary"), vmem_limit_bytes=FFN_VMEM_LIMIT_BYTES),
        name="ffn_prompt",
    )(x, mr, ma, wor, woa, gf, wup, wup, cw, cw, cb, cb, wd, gfin)
    return y, tg[tiles_per_seq - 1::tiles_per_seq], tv[tiles_per_seq - 1::tiles_per_seq]


def _ffn_sample(x, mr, ma, gna, wor, woa, gf, wup, cw, cb, wd, gfin, state2d, nb, steps, tf):
    n, d_model = x.shape
    d_half = mr.shape[-1]
    d_ff = wd.shape[0]
    nj = d_ff // tf
    full2 = lambda shape: pl.BlockSpec(shape, lambda i, j: (0, 0))
    gate_col = lambda i, j: (0, j)
    val_col = lambda i, j: (0, nj + j)
    y, tg, tv = pl.pallas_call(
        functools.partial(_ffn_sample_body, nb=nb),
        grid=(1, nj),
        in_specs=[
            full2((n, d_model)), full2((n, d_half)), full2((n, d_half)), full2(gna.shape),
            _const_spec(wor.shape), _const_spec(woa.shape), _const_spec(gf.shape),
            pl.BlockSpec((d_model, tf), gate_col), pl.BlockSpec((d_model, tf), val_col),
            pl.BlockSpec((cw.shape[0], tf), gate_col), pl.BlockSpec((cw.shape[0], tf), val_col),
            pl.BlockSpec((1, tf), gate_col), pl.BlockSpec((1, tf), val_col),
            pl.BlockSpec((tf, d_model), lambda i, j: (j, 0)),
            _const_spec(gfin.shape),
            pl.BlockSpec((nb, tf), lambda i, j: (0, j)),
            pl.BlockSpec((nb, tf), lambda i, j: (0, 2 * nj + j)),
            pl.BlockSpec((nb, tf), lambda i, j: (0, nj + j)),
            pl.BlockSpec((nb, tf), lambda i, j: (0, 3 * nj + j)),
        ],
        out_specs=(
            full2((n, d_model)),
            pl.BlockSpec((2, nb, tf), lambda i, j: (0, 0, j)),
            pl.BlockSpec((2, nb, tf), lambda i, j: (0, 0, j)),
        ),
        out_shape=(
            jax.ShapeDtypeStruct((n, d_model), F32),
            jax.ShapeDtypeStruct((2, nb, d_ff), F32),
            jax.ShapeDtypeStruct((2, nb, d_ff), F32),
        ),
        scratch_shapes=[pltpu.VMEM((n, d_model), BF16)],
        compiler_params=pltpu.CompilerParams(
            dimension_semantics=("arbitrary", "arbitrary"), vmem_limit_bytes=VMEM_LIMIT_BYTES),
        name="ffn_sample",
    )(x, mr, ma, gna, wor, woa, gf, wup, wup, cw, cw, cb, cb, wd, gfin, state2d, state2d, state2d, state2d)
    return y, tg, tv


def _perm_heads(a, axis):
    shape = a.shape
    a = a.reshape(shape[:axis] + (KV_HEADS // 2, 2, GQA_GROUP, HEAD_DIM) + shape[axis + 1:])
    return jnp.swapaxes(a, axis + 1, axis + 2).reshape(shape)


def _gate_weights(w_a, w_x):
    nblk, bs, _ = w_a.shape
    per = GATE_TILE // bs
    eye = jnp.eye(per, dtype=w_a.dtype)

    def pack(w):
        w4 = w.reshape(nblk // per, per, bs, bs)
        return jnp.einsum("cgij,gh->cgihj", w4, eye).reshape(nblk // per, GATE_TILE, GATE_TILE)
    return jnp.concatenate([pack(w_a), pack(w_x)], axis=-1).astype(BF16)


def kernel(x_prompt, x_sample, state_rnn_conv, state_rnn_h, cache_win_k, cache_win_v, state_ffn_conv,
           norm_mix_g, w_in, rnn_conv_w, rnn_conv_b, w_gate_a, b_gate_a, w_gate_x, b_gate_x, rnn_lambda,
           attn_sinks, rel_bias_table, gn_rnn_g, gn_attn_g, w_out, norm_ffn_g, w_up, ffn_conv_w,
           ffn_conv_b, w_down, norm_final_g):
    batch, seq, d_model = x_prompt.shape
    nb, steps, _ = x_sample.shape
    depth = w_in.shape[0]
    d_rnn = rnn_conv_w.shape[-1]
    d_attn = N_HEADS * HEAD_DIM
    kv_dim = KV_HEADS * HEAD_DIM
    d_ff = w_down.shape[1]
    win = cache_win_k.shape[2]
    assert depth == 1 and d_rnn + d_attn == d_model and w_in.shape[-1] == 2 * d_rnn + d_attn + 2 * kv_dim
    assert win == WINDOW and seq % WINDOW == 0 and w_gate_a.shape[1] == RNN_BLOCKS
    assert rnn_conv_w.shape[1] == 4 and ffn_conv_w.shape[1] == 3 and nb % SUBLANES == 0
    assert steps >= 3 and steps % 2 == 0

    w_in_b = w_in[0].astype(BF16)
    w_out0 = w_out[0]
    w_out_r = w_out0[:d_rnn].astype(BF16)
    w_out_a = _perm_heads(w_out0[d_rnn:], 0).astype(BF16)
    gn_attn_p = _perm_heads(gn_attn_g[0], 0).reshape(1, d_attn)
    w_up_b = w_up[0].astype(BF16)
    w_down_b = w_down[0].astype(BF16)
    wg = _gate_weights(w_gate_a[0], w_gate_x[0])
    row2 = lambda a: a.reshape(1, -1)
    g_mix, g_ffn, g_fin, g_rnn = row2(norm_mix_g[0]), row2(norm_ffn_g[0]), row2(norm_final_g), row2(gn_rnn_g[0])
    cw_r, cb_r = rnn_conv_w[0], row2(rnn_conv_b[0])
    ba, bx, lam = row2(b_gate_a[0]), row2(b_gate_x[0]), row2(rnn_lambda[0])
    cw_f, cb_f = ffn_conv_w[0], row2(ffn_conv_b[0])
    tbl = rel_bias_table.reshape(-1)
    sinks = attn_sinks[0]

    blk = WINDOW
    qi = np.arange(blk)[:, None]
    kj = np.arange(2 * blk)[None, :]
    code_p = jnp.asarray(_t5_bucket_np(blk + qi - kj))
    r = np.arange(N_HEADS * steps)[:, None]
    t_r, h_r = r % steps, r // steps
    npad = 2 * SUBLANES
    code_c = jnp.asarray(_t5_bucket_np(win + t_r - np.arange(win)[None, :]) * N_HEADS + h_r)
    code_n = jnp.asarray(_t5_bucket_np(t_r - np.arange(npad)[None, :]) * N_HEADS + h_r)

    n_p = batch * seq
    tm_a = _pick(PROJ_ROWS, n_p)
    xp2 = x_prompt.reshape(n_p, d_model)
    xr, gr, q, k, v = _inproj(xp2, pl.BlockSpec((tm_a, d_model), lambda i: (i, 0)), n_p // tm_a, tm_a,
                              g_mix, w_in_b, d_rnn, d_attn, kv_dim)
    m_rnn, h_last = _rglru_prompt(xr, gr, cw_r, cb_r, wg, ba, bx, lam, g_rnn, batch, seq,
                                  _pick(RGLRU_ROWS, seq))
    m_attn = _swa_prompt(q, k, v, code_p, tbl, sinks, gn_attn_p, batch, seq)
    y_p, tail_g, tail_v = _ffn_prompt(xp2, m_rnn, m_attn, w_out_r, w_out_a, g_ffn, w_up_b, cw_f, cb_f,
                                      w_down_b, g_fin, batch, seq, _pick(FFN_ROWS, seq),
                                      _pick(FFN_COLS_PROMPT, d_ff))
    y_prompt = y_p.reshape(batch, seq, d_model)
    p_rnn_conv = xr.reshape(batch, seq, d_rnn)[:, seq - 3:, :][None]
    p_rnn_h = h_last[None]
    last_win = lambda a: a.reshape(batch, seq, kv_dim)[:, seq - win:, :].reshape(
        1, batch, win, KV_HEADS, HEAD_DIM)
    p_win_k, p_win_v = last_win(k), last_win(v)
    p_ffn_conv = jnp.concatenate([tail_g[:, SUBLANES - 2:, :], tail_v[:, SUBLANES - 2:, :]], axis=-1)[None]

    n_s = nb * steps
    xs_tm = x_sample.transpose(1, 0, 2).reshape(n_s, d_model)
    xr_s, gr_s, q_s, k_s, v_s = _inproj(xs_tm, pl.BlockSpec((nb, d_model), lambda t: (t, 0)), steps, nb,
                                        g_mix, w_in_b, d_rnn, d_attn, kv_dim)
    conv_tm = state_rnn_conv[0].transpose(1, 0, 2).reshape(3 * nb, d_rnn)
    m_rnn_s, h_new = _rglru_sample(xr_s, gr_s, conv_tm, state_rnn_h[0], cw_r, cb_r, wg, ba, bx, lam, g_rnn,
                                   nb, steps)
    ncg = KV_HEADS // 2
    qs = q_s.reshape(steps, nb, ncg, GQA_GROUP, LANES).transpose(1, 2, 3, 0, 4).reshape(
        nb, ncg, GQA_GROUP * steps, LANES)
    k_tm = k_s.reshape(steps, nb, kv_dim)
    v_tm = v_s.reshape(steps, nb, kv_dim)
    pad = ((0, 0), (0, npad - steps), (0, 0))
    ck_t = cache_win_k[0].transpose(0, 2, 3, 1).reshape(nb, kv_dim, win)
    cv_t = cache_win_v[0].transpose(0, 2, 3, 1).reshape(nb, kv_dim, win)
    bb = SUBLANES
    o_s, ck_new, cv_new = _swa_sample(qs, ck_t, cv_t, jnp.pad(k_tm.transpose(1, 0, 2), pad),
                                      jnp.pad(v_tm.transpose(1, 0, 2), pad), code_c, code_n, tbl, sinks, steps, bb)
    ya_s = o_s.reshape(nb, ncg, GQA_GROUP, steps, LANES).transpose(3, 0, 1, 2, 4).reshape(n_s, d_attn)
    ffn2d = state_ffn_conv[0].reshape(nb, 2 * 2 * d_ff)
    y_s, ns_g, ns_v = _ffn_sample(xs_tm, m_rnn_s, ya_s, gn_attn_p, w_out_r, w_out_a, g_ffn, w_up_b, cw_f, cb_f,
                                  w_down_b, g_fin, ffn2d, nb, steps, _pick(FFN_COLS_SAMPLE, d_ff))
    y_sample = y_s.reshape(steps, nb, d_model).transpose(1, 0, 2)
    s_rnn_conv = xr_s.reshape(steps, nb, d_rnn)[steps - 3:].transpose(1, 0, 2)[None]
    s_rnn_h = h_new[None]

    to_cache = lambda w_t: w_t.reshape(nb, KV_HEADS, HEAD_DIM, win).transpose(0, 3, 1, 2)[None]
    s_win_k, s_win_v = to_cache(ck_new), to_cache(cv_new)
    s_ffn_conv = jnp.concatenate([ns_g, ns_v], axis=-1).transpose(1, 0, 2)[None]

    return (y_prompt, y_sample, p_rnn_conv, p_rnn_h, p_win_k, p_win_v, p_ffn_conv,
            s_rnn_conv, s_rnn_h, s_win_k, s_win_v, s_ffn_conv)
```

```python
import functools
import math

import numpy as np
import jax
import jax.numpy as jnp
from jax import lax
from jax.experimental import pallas as pl
from jax.experimental.pallas import tpu as pltpu

F32 = jnp.float32
BF16 = jnp.bfloat16

HEAD_DIM = 64
KV_HEADS = 4
N_HEADS = 16
GQA_GROUP = N_HEADS // KV_HEADS
RNN_BLOCKS = 16
RG_C = 8.0
WINDOW = 128
N_BUCKETS = 32
MAX_EXACT = N_BUCKETS // 2
REL_MAX_DIST = 128
EPS = 1e-6
NEG_INF = -1e30
ATTN_SCALE = HEAD_DIM ** -0.5

LANES = 128
SUBLANES = 8
HALF_SUBLANES = SUBLANES // 2
VMEM_LIMIT_BYTES = 56 * 1024 * 1024
FFN_VMEM_LIMIT_BYTES = 61 * 1024 * 1024

GATE_TILE = 256

PROJ_ROWS = (512, WINDOW)
RGLRU_ROWS = (1024, 512, WINDOW)
SWA_BLOCKS_PER_STEP = (8, 4, 2, 1)
FFN_ROWS = (1024, 512, 256)
FFN_COLS_PROMPT = (768, 512)
FFN_COLS_SAMPLE = (768, 512)


def _pick(options, n):
    return next(t for t in options if n % t == 0)


def _t5_bucket_np(d):
    n = np.maximum(d, 0)
    nf = np.maximum(n, 1).astype(np.float32)
    large = MAX_EXACT + (np.log(nf / MAX_EXACT) / math.log(REL_MAX_DIST / MAX_EXACT)
                         * (N_BUCKETS - MAX_EXACT)).astype(np.int32)
    large = np.minimum(large, N_BUCKETS - 1)
    return np.where(n < MAX_EXACT, n, large).astype(np.int32)


def _rms(x, g):
    ms = jnp.mean(x * x, axis=-1, keepdims=True)
    return (x * lax.rsqrt(ms + EPS)) * g


GELU_C = math.sqrt(2.0 / math.pi)
GELU_C3 = GELU_C * 0.044715


def _gelu_mul(x, v):
    t = jnp.tanh(x * (GELU_C + GELU_C3 * (x * x)))
    return (0.5 * (x * v)) * (1.0 + t)


def _softplus(x):
    return jnp.maximum(x, 0.0) + jnp.log1p(jnp.exp(-jnp.abs(x)))


def _const_spec(shape):
    nd = len(shape)
    return pl.BlockSpec(shape, lambda *_: (0,) * nd, pipeline_mode=pl.Buffered(1))


def _smem_spec():
    return pl.BlockSpec(memory_space=pltpu.SMEM)


def _inproj_body(x_ref, g_ref, w_ref, xr_ref, gr_ref, q_ref, k_ref, v_ref):
    xn = _rms(x_ref[...], g_ref[...]).astype(BF16)
    d_rnn, d_attn, kv_dim = xr_ref.shape[-1], q_ref.shape[-1], k_ref.shape[-1]
    p = jnp.dot(xn, w_ref[...], preferred_element_type=F32)
    o1, o2, o3, o4 = d_rnn, 2 * d_rnn, 2 * d_rnn + d_attn, 2 * d_rnn + d_attn + kv_dim
    xr_ref[...] = p[:, :o1]
    gr_ref[...] = p[:, o1:o2]
    k_ref[...] = p[:, o3:o4]
    v_ref[...] = p[:, o4:]
    pq = p[:, o2:o3] * ATTN_SCALE
    lane = lax.broadcasted_iota(jnp.int32, (pq.shape[0], LANES), 1)
    blocks = []
    for cg in range(KV_HEADS // 2):
        for g in range(GQA_GROUP):
            ha, hb = 2 * GQA_GROUP * cg + g, 2 * GQA_GROUP * cg + GQA_GROUP + g
            a = pq[:, (ha // 2) * LANES:(ha // 2 + 1) * LANES]
            b = pq[:, (hb // 2) * LANES:(hb // 2 + 1) * LANES]
            if g % 2 == 0:
                blocks.append(jnp.where(lane < HEAD_DIM, a, pltpu.roll(b, HEAD_DIM, 1)))
            else:
                blocks.append(jnp.where(lane < HEAD_DIM, pltpu.roll(a, HEAD_DIM, 1), b))
    q_ref[...] = jnp.concatenate(blocks, axis=1).astype(BF16)


def _inproj(x2d, x_spec, n_steps, tm, g, w, d_rnn, d_attn, kv_dim):
    n = n_steps * tm
    d_model = g.shape[-1]
    row = lambda i: (i, 0)
    out_shape = (
        jax.ShapeDtypeStruct((n, d_rnn), F32),
        jax.ShapeDtypeStruct((n, d_rnn), F32),
        jax.ShapeDtypeStruct((n, d_attn), BF16),
        jax.ShapeDtypeStruct((n, kv_dim), F32),
        jax.ShapeDtypeStruct((n, kv_dim), F32),
    )
    out_specs = (
        pl.BlockSpec((tm, d_rnn), row),
        pl.BlockSpec((tm, d_rnn), row),
        pl.BlockSpec((tm, d_attn), row),
        pl.BlockSpec((tm, kv_dim), row),
        pl.BlockSpec((tm, kv_dim), row),
    )
    return pl.pallas_call(
        _inproj_body,
        grid=(n_steps,),
        in_specs=[x_spec, _const_spec((1, d_model)), _const_spec(w.shape)],
        out_specs=out_specs,
        out_shape=out_shape,
        compiler_params=pltpu.CompilerParams(
            dimension_semantics=("arbitrary",), vmem_limit_bytes=VMEM_LIMIT_BYTES),
        name="inproj",
    )(x2d, g, w)


def _rglru_gates(xc, wg_ref, ba, bx, lam):
    d_rnn = xc.shape[-1]
    pre_a, pre_x = [], []
    for c in range(d_rnn // GATE_TILE):
        xb = xc[:, c * GATE_TILE:(c + 1) * GATE_TILE].astype(BF16)
        pre = jnp.dot(xb, wg_ref[c], preferred_element_type=F32)
        pre_a.append(pre[:, :GATE_TILE])
        pre_x.append(pre[:, GATE_TILE:])
    r = jax.nn.sigmoid(jnp.concatenate(pre_a, axis=1) + ba)
    i = jax.nn.sigmoid(jnp.concatenate(pre_x, axis=1) + bx)
    neg_log_a = r * (RG_C * _softplus(-lam))
    a = jnp.exp(-neg_log_a)
    one_minus_a2 = jnp.tanh(neg_log_a) * (a * a + 1.0)
    root = jnp.where(one_minus_a2 > 0.0, one_minus_a2 * lax.rsqrt(one_minus_a2), 0.0)
    u = root * (i * xc)
    return a, u


def _rglru_prompt_body(xr_ref, gr_ref, cw_ref, cb_ref, wg_ref, ba_ref, bx_ref, lam_ref, gn_ref,
                       out_ref, hlast_ref, xbuf, abuf, ubuf, hbuf, hc_ref, *, tt, conv_w):
    ti = pl.program_id(1)
    d_rnn = xr_ref.shape[-1]

    @pl.when(ti == 0)
    def _():
        xbuf[0:SUBLANES, :] = jnp.zeros((SUBLANES, d_rnn), F32)
        hc_ref[...] = jnp.zeros_like(hc_ref)

    x = xr_ref[...]
    hist = xbuf[...]
    xc = cw_ref[conv_w - 1:conv_w, :] * x
    for k in range(conv_w - 1):
        xc = xc + cw_ref[k:k + 1, :] * _shift_rows(x, hist, conv_w - 1 - k)
    xc = xc + cb_ref[...]
    xbuf[...] = x[tt - SUBLANES:, :]

    a, u = _rglru_gates(xc, wg_ref, ba_ref[...], bx_ref[...], lam_ref[...])
    abuf[...] = a
    ubuf[...] = u

    row = lax.broadcasted_iota(jnp.int32, (SUBLANES, d_rnn), 0)
    row_in_half = row & (HALF_SUBLANES - 1)

    def group(gidx, hc):
        r0 = pl.multiple_of(gidx * SUBLANES, SUBLANES)
        ag = abuf[pl.ds(r0, SUBLANES), :]
        ug = ubuf[pl.ds(r0, SUBLANES), :]
        for k in (1, 2):
            a_prev = jnp.where(row_in_half >= k, pltpu.roll(ag, k, 0), 1.0)
            u_prev = jnp.where(row_in_half >= k, pltpu.roll(ug, k, 0), 0.0)
            ug = ag * u_prev + ug
            ag = ag * a_prev
        h_lo = ag * hc + ug
        h_mid = jnp.broadcast_to(h_lo[HALF_SUBLANES - 1:HALF_SUBLANES, :], (SUBLANES, d_rnn))
        h = jnp.where(row < HALF_SUBLANES, h_lo, ag * h_mid + ug)
        hbuf[pl.ds(r0, SUBLANES), :] = h
        return jnp.broadcast_to(h[SUBLANES - 1:SUBLANES, :], (SUBLANES, d_rnn))

    hc = lax.fori_loop(0, tt // SUBLANES, group, hc_ref[...], unroll=2)
    hc_ref[...] = hc
    hlast_ref[0] = hc

    y = _gelu_mul(gr_ref[...], hbuf[...])
    out_ref[...] = _rms(y, gn_ref[...]).astype(BF16)


def _rglru_prompt(xr, gr, cw, cb, wg, ba, bx, lam, gn, batch, seq, tt):
    d_rnn = xr.shape[-1]
    nt = seq // tt
    conv_w = cw.shape[0]
    tile = lambda b, t: (b * nt + t, 0)
    out, hlast = pl.pallas_call(
        functools.partial(_rglru_prompt_body, tt=tt, conv_w=conv_w),
        grid=(batch, nt),
        in_specs=[
            pl.BlockSpec((tt, d_rnn), tile),
            pl.BlockSpec((tt, d_rnn), tile),
            _const_spec(cw.shape), _const_spec(cb.shape), _const_spec(wg.shape),
            _const_spec(ba.shape), _const_spec(bx.shape), _const_spec(lam.shape), _const_spec(gn.shape),
        ],
        out_specs=(
            pl.BlockSpec((tt, d_rnn), tile),
            pl.BlockSpec((1, SUBLANES, d_rnn), lambda b, t: (b, 0, 0)),
        ),
        out_shape=(
            jax.ShapeDtypeStruct((batch * seq, d_rnn), BF16),
            jax.ShapeDtypeStruct((batch, SUBLANES, d_rnn), F32),
        ),
        scratch_shapes=[
            pltpu.VMEM((SUBLANES, d_rnn), F32),
            pltpu.VMEM((tt, d_rnn), F32),
            pltpu.VMEM((tt, d_rnn), F32),
            pltpu.VMEM((tt, d_rnn), F32),
            pltpu.VMEM((SUBLANES, d_rnn), F32),
        ],
        compiler_params=pltpu.CompilerParams(
            dimension_semantics=("arbitrary", "arbitrary"), vmem_limit_bytes=VMEM_LIMIT_BYTES),
        name="rglru_prompt",
    )(xr, gr, cw, cb, wg, ba, bx, lam, gn)
    return out, hlast[:, 0, :]


def _rglru_sample_body(xr_ref, gr_ref, c0_ref, c1_ref, c2_ref, h0_ref, cw_ref, cb_ref, wg_ref,
                       ba_ref, bx_ref, lam_ref, gn_ref, out_ref, hnew_ref, *, nb, steps):
    x = xr_ref[...]
    hist = [c0_ref[...], c1_ref[...], c2_ref[...]]
    xs = [x[t * nb:(t + 1) * nb, :] for t in range(steps)]
    xp = hist + xs
    conv_w = len(hist) + 1
    xc = cw_ref[conv_w - 1:conv_w, :] * x
    for k in range(conv_w - 1):
        shifted = jnp.concatenate(xp[k:k + steps], axis=0)
        xc = xc + cw_ref[k:k + 1, :] * shifted
    xc = xc + cb_ref[...]
    a, u = _rglru_gates(xc, wg_ref, ba_ref[...], bx_ref[...], lam_ref[...])
    h = h0_ref[...]
    hs = []
    for t in range(steps):
        h = a[t * nb:(t + 1) * nb, :] * h + u[t * nb:(t + 1) * nb, :]
        hs.append(h)
    hnew_ref[...] = h
    y = _gelu_mul(gr_ref[...], jnp.concatenate(hs, axis=0))
    out_ref[...] = _rms(y, gn_ref[...]).astype(BF16)


def _rglru_sample(xr, gr, conv_state2d, h0, cw, cb, wg, ba, bx, lam, gn, nb, steps):
    d_rnn = xr.shape[-1]
    n = nb * steps
    assert cw.shape[0] == 4
    full = lambda shape: pl.BlockSpec(shape, lambda i: (0,) * len(shape))
    return pl.pallas_call(
        functools.partial(_rglru_sample_body, nb=nb, steps=steps),
        grid=(1,),
        in_specs=[
            full((n, d_rnn)), full((n, d_rnn)),
            pl.BlockSpec((nb, d_rnn), lambda i: (0, 0)),
            pl.BlockSpec((nb, d_rnn), lambda i: (1, 0)),
            pl.BlockSpec((nb, d_rnn), lambda i: (2, 0)),
            full((nb, d_rnn)),
            full(cw.shape), full(cb.shape), full(wg.shape), full(ba.shape), full(bx.shape),
            full(lam.shape), full(gn.shape),
        ],
        out_specs=(full((n, d_rnn)), full((nb, d_rnn))),
        out_shape=(jax.ShapeDtypeStruct((n, d_rnn), BF16), jax.ShapeDtypeStruct((nb, d_rnn), F32)),
        compiler_params=pltpu.CompilerParams(
            dimension_semantics=("arbitrary",), vmem_limit_bytes=VMEM_LIMIT_BYTES),
        name="rglru_sample",
    )(xr, gr, conv_state2d, conv_state2d, conv_state2d, h0, cw, cb, wg, ba, bx, lam, gn)


def _swa_prompt_body(code_ref, tbl_ref, sink_ref, q_ref, kp_ref, kc_ref, vp_ref, vc_ref, gn_ref,
                     out_ref, bias_ref):
    j = pl.program_id(1)
    blk = WINDOW
    rows = GQA_GROUP * blk

    @pl.when((pl.program_id(0) == 0) & (j == 0))
    def _():
        code = code_ref[...]
        qi = lax.broadcasted_iota(jnp.int32, code.shape, 0)
        kj = lax.broadcasted_iota(jnp.int32, code.shape, 1)
        dist = blk + qi - kj
        in_window = (dist & -WINDOW) == 0
        for h in range(N_HEADS):
            def pick(b, acc, h=h):
                return jnp.where(code == b, tbl_ref[b * N_HEADS + h], acc)
            bias_h = lax.fori_loop(0, N_BUCKETS, pick, jnp.zeros(code.shape, F32))
            bias_h = jnp.where(in_window, bias_h, NEG_INF)
            kv, g = divmod(h, GQA_GROUP)
            bias_ref[1, kv, g * blk:(g + 1) * blk, :] = bias_h
            bias_ref[0, kv, g * blk:(g + 1) * blk, :] = jnp.where(kj >= blk, bias_h, NEG_INF)

    lane = lax.broadcasted_iota(jnp.int32, (2 * blk, LANES), 1)
    row_g = lax.broadcasted_iota(jnp.int32, (rows, 1), 0) // blk
    sink_cols = []
    for kv in range(KV_HEADS):
        col = jnp.zeros((rows, 1), F32)
        for g in range(GQA_GROUP):
            col = jnp.where(row_g == g, sink_ref[kv * GQA_GROUP + g], col)
        sink_cols.append(col)
    for sub in range(q_ref.shape[0] // blk):
        rows_q = slice(sub * blk, (sub + 1) * blk)
        if sub == 0:
            has_prev = jnp.where(j > 0, 1, 0)
            k_prev, v_prev = kp_ref[...], vp_ref[...]
        else:
            has_prev = 1
            k_prev, v_prev = kc_ref[(sub - 1) * blk:sub * blk, :], vc_ref[(sub - 1) * blk:sub * blk, :]
        kband = jnp.concatenate([k_prev, kc_ref[rows_q, :]], axis=0)
        vband = jnp.concatenate([v_prev, vc_ref[rows_q, :]], axis=0)
        y = _swa_block(q_ref[rows_q, :], kband, vband, bias_ref, has_prev, sink_cols, lane)
        out_ref[rows_q, :] = _rms(y, gn_ref[...]).astype(BF16)


def _swa_block(q, kband, vband, bias_ref, has_prev, sink_cols, lane):
    blk = q.shape[0]
    rows = GQA_GROUP * blk
    outs = []
    for cg in range(KV_HEADS // 2):
        qs = jnp.concatenate(
            [q[:, (cg * GQA_GROUP + g) * LANES:(cg * GQA_GROUP + g + 1) * LANES] for g in range(GQA_GROUP)],
            axis=0)
        ka = kband[:, cg * LANES:(cg + 1) * LANES]
        va = vband[:, cg * LANES:(cg + 1) * LANES]
        o = None
        for par in range(2):
            kv = 2 * cg + par
            half = (lane < HEAD_DIM) if par == 0 else (lane >= HEAD_DIM)
            km = jnp.where(half, ka, 0.0).astype(BF16)
            vm = jnp.where(half, va, 0.0).astype(BF16)
            s = lax.dot_general(qs, km, (((1,), (1,)), ((), ())), preferred_element_type=F32)
            s = s + bias_ref[has_prev, kv]
            sink = sink_cols[kv]
            m = jnp.maximum(jnp.max(s, axis=-1, keepdims=True), sink)
            p = jnp.exp(s - m)
            denom = jnp.sum(p, axis=-1, keepdims=True) + jnp.exp(sink - m)
            part = jnp.dot(p.astype(BF16), vm, preferred_element_type=F32) * (1.0 / denom)
            o = part if o is None else o + part
        outs.extend(o[g * blk:(g + 1) * blk, :] for g in range(GQA_GROUP))
    return jnp.concatenate(outs, axis=1)


def _swa_prompt(q, k, v, code, tbl, sinks, gn, batch, seq):
    blk = WINDOW
    per_step = _pick(SWA_BLOCKS_PER_STEP, seq // blk)
    nb = seq // (per_step * blk)
    d_attn = q.shape[-1]
    kv_dim = k.shape[-1]
    cur = lambda b, j: (b * nb + j, 0)
    prev = lambda b, j: (per_step * (b * nb + j) - jnp.where(j > 0, 1, 0), 0)
    return pl.pallas_call(
        _swa_prompt_body,
        grid=(batch, nb),
        in_specs=[
            _const_spec(code.shape), _smem_spec(), _smem_spec(),
            pl.BlockSpec((per_step * blk, d_attn), cur),
            pl.BlockSpec((blk, kv_dim), prev), pl.BlockSpec((per_step * blk, kv_dim), cur),
            pl.BlockSpec((blk, kv_dim), prev), pl.BlockSpec((per_step * blk, kv_dim), cur),
            _const_spec(gn.shape),
        ],
        out_specs=pl.BlockSpec((per_step * blk, d_attn), cur),
        out_shape=jax.ShapeDtypeStruct((batch * seq, d_attn), BF16),
        scratch_shapes=[pltpu.VMEM((2, KV_HEADS, GQA_GROUP * blk, 2 * blk), F32)],
        compiler_params=pltpu.CompilerParams(
            dimension_semantics=("arbitrary", "arbitrary"), vmem_limit_bytes=VMEM_LIMIT_BYTES),
        name="swa_prompt",
    )(code, tbl, sinks, q, k, k, v, v, gn)


def _slide_window(win_ref, new_ref, out_ref, steps):
    bb, kv_dim, win = win_ref.shape
    npad = new_ref.shape[1]
    new_t = new_ref[...].reshape(bb * npad, kv_dim).T
    lane = lax.broadcasted_iota(jnp.int32, (kv_dim, win), 1)
    for b in range(bb):
        placed = pltpu.roll(new_t, (win - steps - b * npad) % win, 1)
        shifted = pltpu.roll(win_ref[b], win - steps, 1)
        out_ref[b] = jnp.where(lane >= win - steps, placed, shifted)


def _swa_sample_body(codec_ref, coden_ref, tbl_ref, sink_ref, q_ref, ck_ref, cv_ref, kn_ref, vn_ref,
                     out_ref, ck_out_ref, cv_out_ref, biasc_ref, biasn_ref, sinkc_ref, *, steps):
    bb = q_ref.shape[0]
    nrow = N_HEADS * steps
    win = ck_ref.shape[2]
    npad = kn_ref.shape[1]

    @pl.when(pl.program_id(0) == 0)
    def _():
        codec = codec_ref[...]
        coden = coden_ref[...]
        hrow = lax.broadcasted_iota(jnp.int32, (nrow, 1), 0) // steps

        def pick(idx, accs):
            ac, an = accs
            val = tbl_ref[idx]
            return jnp.where(codec == idx, val, ac), jnp.where(coden == idx, val, an)
        bc, bn = lax.fori_loop(0, N_BUCKETS * N_HEADS, pick,
                               (jnp.zeros(codec.shape, F32), jnp.zeros(coden.shape, F32)))
        biasc_ref[...] = bc
        biasn_ref[...] = bn

        def pick_sink(h, acc):
            return jnp.where(hrow == h, sink_ref[h], acc)
        sinkc_ref[...] = lax.fori_loop(0, N_HEADS, pick_sink, jnp.zeros((nrow, 1), F32))

    q = q_ref[...].astype(F32)
    lane = lax.broadcasted_iota(jnp.int32, (bb, GQA_GROUP * steps, LANES), 2)
    zeros = jnp.zeros((bb, GQA_GROUP * steps, LANES), F32)
    pieces = []
    for cg in range(KV_HEADS // 2):
        for par in range(2):
            half = (lane < HEAD_DIM) if par == 0 else (lane >= HEAD_DIM)
            qm = jnp.where(half, q[:, cg], zeros)
            pieces.append(jnp.concatenate([qm, zeros] if cg == 0 else [zeros, qm], axis=2))
    qm = jnp.concatenate(pieces, axis=1).astype(BF16)

    ck = ck_ref[...].astype(BF16)
    kn = kn_ref[...].astype(BF16)
    s_c = jnp.einsum("bqd,bdk->bqk", qm, ck, preferred_element_type=F32)
    s_n = jnp.einsum("bqd,bkd->bqk", qm, kn, preferred_element_type=F32)

    t_c = lax.broadcasted_iota(jnp.int32, (nrow, win), 0) % steps
    k_c = lax.broadcasted_iota(jnp.int32, (nrow, win), 1)
    valid_c = k_c > t_c
    t_n = lax.broadcasted_iota(jnp.int32, (nrow, npad), 0) % steps
    k_n = lax.broadcasted_iota(jnp.int32, (nrow, npad), 1)
    valid_n = k_n <= t_n

    s_c = jnp.where(valid_c[None], s_c + biasc_ref[...][None], NEG_INF)
    s_n = jnp.where(valid_n[None], s_n + biasn_ref[...][None], NEG_INF)
    sink = sinkc_ref[...][None]
    m = jnp.maximum(jnp.maximum(jnp.max(s_c, axis=-1, keepdims=True),
                                jnp.max(s_n, axis=-1, keepdims=True)), sink)
    p_c = jnp.exp(s_c - m)
    p_n = jnp.exp(s_n - m)
    denom = (jnp.sum(p_c, axis=-1, keepdims=True) + jnp.sum(p_n, axis=-1, keepdims=True)
             + jnp.exp(sink - m))
    r = 1.0 / denom
    w_c = (p_c * r).astype(BF16)
    w_n = (p_n * r).astype(BF16)
    o = (jnp.einsum("bqk,bdk->bqd", w_c, cv_ref[...].astype(BF16), preferred_element_type=F32)
         + jnp.einsum("bqk,bkd->bqd", w_n, vn_ref[...].astype(BF16), preferred_element_type=F32))
    gt = GQA_GROUP * steps
    lane_o = lax.broadcasted_iota(jnp.int32, (bb, gt, LANES), 2)
    for cg in range(KV_HEADS // 2):
        lo = o[:, cg * 2 * gt:cg * 2 * gt + gt, cg * LANES:(cg + 1) * LANES]
        hi = o[:, cg * 2 * gt + gt:(cg + 1) * 2 * gt, cg * LANES:(cg + 1) * LANES]
        out_ref[:, cg] = jnp.where(lane_o < HEAD_DIM, lo, hi)

    _slide_window(ck_ref, kn_ref, ck_out_ref, steps)
    _slide_window(cv_ref, vn_ref, cv_out_ref, steps)


def _swa_sample(qs, ck, cv, kn, vn, codec, coden, tbl, sinks, steps, bb):
    nbatch, ncg, gt, _ = qs.shape
    kv_dim, win = ck.shape[1], ck.shape[2]
    npad = kn.shape[1]
    nrow = N_HEADS * steps
    assert bb * npad == win
    blk4 = lambda i: (i, 0, 0, 0)
    blk3 = lambda i: (i, 0, 0)
    return pl.pallas_call(
        functools.partial(_swa_sample_body, steps=steps),
        grid=(nbatch // bb,),
        in_specs=[
            _const_spec(codec.shape), _const_spec(coden.shape), _smem_spec(), _smem_spec(),
            pl.BlockSpec((bb, ncg, gt, LANES), blk4),
            pl.BlockSpec((bb, kv_dim, win), blk3), pl.BlockSpec((bb, kv_dim, win), blk3),
            pl.BlockSpec((bb, npad, kv_dim), blk3), pl.BlockSpec((bb, npad, kv_dim), blk3),
        ],
        out_specs=(pl.BlockSpec((bb, ncg, gt, LANES), blk4),
                   pl.BlockSpec((bb, kv_dim, win), blk3), pl.BlockSpec((bb, kv_dim, win), blk3)),
        out_shape=(jax.ShapeDtypeStruct((nbatch, ncg, gt, LANES), F32),
                   jax.ShapeDtypeStruct((nbatch, kv_dim, win), F32),
                   jax.ShapeDtypeStruct((nbatch, kv_dim, win), F32)),
        scratch_shapes=[pltpu.VMEM((nrow, win), F32), pltpu.VMEM((nrow, npad), F32),
                        pltpu.VMEM((nrow, 1), F32)],
        compiler_params=pltpu.CompilerParams(
            dimension_semantics=("arbitrary",), vmem_limit_bytes=VMEM_LIMIT_BYTES),
        name="swa_sample",
    )(codec, coden, tbl, sinks, qs, ck, cv, kn, vn)


def _shift_rows(up, hist, k):
    rolled = pltpu.roll(up, k, 0)
    row = lax.broadcasted_iota(jnp.int32, hist.shape, 0)
    head = jnp.where(row < k, pltpu.roll(hist, k, 0), rolled[0:SUBLANES, :])
    return jnp.concatenate([head, rolled[SUBLANES:, :]], axis=0)


def _ffn_conv(up, hist, cw, cb, *, tm, sample_nb):
    if sample_nb is None:
        prev2 = _shift_rows(up, hist, 2)
        prev1 = _shift_rows(up, hist, 1)
    else:
        s0, s1 = hist
        prev2 = jnp.concatenate([s0, s1] + ([up[:tm - 2 * sample_nb, :]] if tm > 2 * sample_nb else []), axis=0)
        prev1 = jnp.concatenate([s1, up[:tm - sample_nb, :]], axis=0)
    return cw[0:1, :] * prev2 + cw[1:2, :] * prev1 + cw[2:3, :] * up + cb


FFN_SLAB = 256


def _ffn_prompt_body(x_hbm, mr_ref, ma_ref, wor_ref, woa_ref, gf_ref, wg_ref, wv_ref, cwg_ref, cwv_ref, cbg_ref,
                     cbv_ref, wd_ref, gfin_ref, y_ref, tg_ref, tv_ref, hn_ref, car_g, car_v, x_sem,
                     *, tm, tiles_per_seq):
    i = pl.program_id(0)
    j = pl.program_id(1)
    nj = pl.num_programs(1)
    n_slab = tm // FFN_SLAB
    slab = lambda r: slice(r * FFN_SLAB, (r + 1) * FFN_SLAB)

    @pl.when(j == 0)
    def _():
        x_copy = pltpu.make_async_copy(x_hbm.at[pl.ds(pl.multiple_of(i * tm, tm), tm), :], y_ref, x_sem)
        x_copy.start()

        def out_proj(r):
            return (jnp.dot(mr_ref[slab(r), :], wor_ref[...], preferred_element_type=F32)
                    + jnp.dot(ma_ref[slab(r), :], woa_ref[...], preferred_element_type=F32))

        pending = out_proj(0)
        x_copy.wait()
        for r in range(n_slab):
            d = pending
            if r + 1 < n_slab:
                pending = out_proj(r + 1)
            h = y_ref[slab(r), :] + d
            y_ref[slab(r), :] = h
            hn_ref[slab(r), :] = _rms(h, gf_ref[...]).astype(BF16)

    @pl.when((i == 0) & (j == 0))
    def _():
        car_g[...] = jnp.zeros_like(car_g)
        car_v[...] = jnp.zeros_like(car_v)

    cwg, cwv, cbg, cbv = cwg_ref[...], cwv_ref[...], cbg_ref[...], cbv_ref[...]

    def up_proj(r):
        hn = hn_ref[slab(r), :]
        return (jnp.dot(hn, wg_ref[...], preferred_element_type=F32),
                jnp.dot(hn, wv_ref[...], preferred_element_type=F32))

    seq_start = i % tiles_per_seq == 0
    hist_g = jnp.where(seq_start, 0.0, car_g[j])
    hist_v = jnp.where(seq_start, 0.0, car_v[j])
    pending = up_proj(0)
    act = None
    for r in range(n_slab):
        up_g, up_v = pending
        if r + 1 < n_slab:
            pending = up_proj(r + 1)
        if act is not None:
            y_ref[slab(r - 1), :] += jnp.dot(act, wd_ref[...], preferred_element_type=F32)
        c_g = _ffn_conv(up_g, hist_g, cwg, cbg, tm=FFN_SLAB, sample_nb=None)
        c_v = _ffn_conv(up_v, hist_v, cwv, cbv, tm=FFN_SLAB, sample_nb=None)
        hist_g = up_g[FFN_SLAB - SUBLANES:, :]
        hist_v = up_v[FFN_SLAB - SUBLANES:, :]
        act = (jax.nn.gelu(c_g) * c_v).astype(BF16)
    y_ref[slab(n_slab - 1), :] += jnp.dot(act, wd_ref[...], preferred_element_type=F32)
    car_g[j] = hist_g
    car_v[j] = hist_v
    tg_ref[0] = hist_g
    tv_ref[0] = hist_v

    @pl.when(j == nj - 1)
    def _():
        for r in range(n_slab):
            y_ref[slab(r), :] = _rms(y_ref[slab(r), :], gfin_ref[...])


def _ffn_sample_body(x_ref, mr_ref, ma_ref, gna_ref, wor_ref, woa_ref, gf_ref, wg_ref, wv_ref, cwg_ref, cwv_ref,
                     cbg_ref, cbv_ref, wd_ref, gfin_ref, s0g_ref, s1g_ref, s0v_ref, s1v_ref,
                     y_ref, tg_ref, tv_ref, hn_ref, *, nb):
    j = pl.program_id(1)
    nj = pl.num_programs(1)
    tm = x_ref.shape[0]

    @pl.when(j == 0)
    def _():
        ma = _rms(ma_ref[...], gna_ref[...]).astype(BF16)
        h = (x_ref[...] + jnp.dot(mr_ref[...], wor_ref[...], preferred_element_type=F32)
             + jnp.dot(ma, woa_ref[...], preferred_element_type=F32))
        y_ref[...] = h
        hn_ref[...] = _rms(h, gf_ref[...]).astype(BF16)

    rows_per_slab = 2 * nb
    n_slab = tm // rows_per_slab
    slab = lambda r: slice(r * rows_per_slab, (r + 1) * rows_per_slab)
    cwg, cwv, cbg, cbv = cwg_ref[...], cwv_ref[...], cbg_ref[...], cbv_ref[...]

    def up_proj(r):
        hn = hn_ref[slab(r), :]
        return (jnp.dot(hn, wg_ref[...], preferred_element_type=F32),
                jnp.dot(hn, wv_ref[...], preferred_element_type=F32))

    hist_g = (s0g_ref[...], s1g_ref[...])
    hist_v = (s0v_ref[...], s1v_ref[...])
    pending = up_proj(0)
    act = None
    for r in range(n_slab):
        up_g, up_v = pending
        if r + 1 < n_slab:
            pending = up_proj(r + 1)
        if act is not None:
            y_ref[slab(r - 1), :] += jnp.dot(act, wd_ref[...], preferred_element_type=F32)
        c_g = _ffn_conv(up_g, hist_g, cwg, cbg, tm=rows_per_slab, sample_nb=nb)
        c_v = _ffn_conv(up_v, hist_v, cwv, cbv, tm=rows_per_slab, sample_nb=nb)
        hist_g = (up_g[:nb, :], up_g[nb:, :])
        hist_v = (up_v[:nb, :], up_v[nb:, :])
        act = (jax.nn.gelu(c_g) * c_v).astype(BF16)
    y_ref[slab(n_slab - 1), :] += jnp.dot(act, wd_ref[...], preferred_element_type=F32)
    tg_ref[0], tg_ref[1] = hist_g
    tv_ref[0], tv_ref[1] = hist_v

    @pl.when(j == nj - 1)
    def _():
        y_ref[...] = _rms(y_ref[...], gfin_ref[...])


def _ffn_prompt(x, mr, ma, wor, woa, gf, wup, cw, cb, wd, gfin, batch, seq, tm, tf):
    n, d_model = x.shape
    d_half = mr.shape[-1]
    d_ff = wd.shape[0]
    ni, nj = n // tm, d_ff // tf
    tiles_per_seq = seq // tm
    row = lambda i, j: (i, 0)
    gate_col = lambda i, j: (0, j)
    val_col = lambda i, j: (0, nj + j)
    tail = lambda i, j: (i, 0, j)
    y, tg, tv = pl.pallas_call(
        functools.partial(_ffn_prompt_body, tm=tm, tiles_per_seq=tiles_per_seq),
        grid=(ni, nj),
        in_specs=[
            pl.BlockSpec(memory_space=pl.ANY),
            pl.BlockSpec((tm, d_half), row), pl.BlockSpec((tm, d_half), row),
            _const_spec(wor.shape), _const_spec(woa.shape), _const_spec(gf.shape),
            pl.BlockSpec((d_model, tf), gate_col), pl.BlockSpec((d_model, tf), val_col),
            pl.BlockSpec((cw.shape[0], tf), gate_col), pl.BlockSpec((cw.shape[0], tf), val_col),
            pl.BlockSpec((1, tf), gate_col), pl.BlockSpec((1, tf), val_col),
            pl.BlockSpec((tf, d_model), lambda i, j: (j, 0)),
            _const_spec(gfin.shape),
        ],
        out_specs=(
            pl.BlockSpec((tm, d_model), row),
            pl.BlockSpec((1, SUBLANES, tf), tail),
            pl.BlockSpec((1, SUBLANES, tf), tail),
        ),
        out_shape=(
            jax.ShapeDtypeStruct((n, d_model), F32),
            jax.ShapeDtypeStruct((ni, SUBLANES, d_ff), F32),
            jax.ShapeDtypeStruct((ni, SUBLANES, d_ff), F32),
        ),
        scratch_shapes=[
            pltpu.VMEM((tm, d_model), BF16),
            pltpu.VMEM((nj, SUBLANES, tf), F32), pltpu.VMEM((nj, SUBLANES, tf), F32),
            pltpu.SemaphoreType.DMA(()),
        ],
        compiler_params=pltpu.CompilerParams(
            dimension_semantics=("arbitrary", "arbitrary"), vmem_limit_bytes=FFN_VMEM_LIMIT_BYTES),
        name="ffn_prompt",
    )(x, mr, ma, wor, woa, gf, wup, wup, cw, cw, cb, cb, wd, gfin)
    return y, tg[tiles_per_seq - 1::tiles_per_seq], tv[tiles_per_seq - 1::tiles_per_seq]


def _ffn_sample(x, mr, ma, gna, wor, woa, gf, wup, cw, cb, wd, gfin, state2d, nb, steps, tf):
    n, d_model = x.shape
    d_half = mr.shape[-1]
    d_ff = wd.shape[0]
    nj = d_ff // tf
    full2 = lambda shape: pl.BlockSpec(shape, lambda i, j: (0, 0))
    gate_col = lambda i, j: (0, j)
    val_col = lambda i, j: (0, nj + j)
    y, tg, tv = pl.pallas_call(
        functools.partial(_ffn_sample_body, nb=nb),
        grid=(1, nj),
        in_specs=[
            full2((n, d_model)), full2((n, d_half)), full2((n, d_half)), full2(gna.shape),
            _const_spec(wor.shape), _const_spec(woa.shape), _const_spec(gf.shape),
            pl.BlockSpec((d_model, tf), gate_col), pl.BlockSpec((d_model, tf), val_col),
            pl.BlockSpec((cw.shape[0], tf), gate_col), pl.BlockSpec((cw.shape[0], tf), val_col),
            pl.BlockSpec((1, tf), gate_col), pl.BlockSpec((1, tf), val_col),
            pl.BlockSpec((tf, d_model), lambda i, j: (j, 0)),
            _const_spec(gfin.shape),
            pl.BlockSpec((nb, tf), lambda i, j: (0, j)),
            pl.BlockSpec((nb, tf), lambda i, j: (0, 2 * nj + j)),
            pl.BlockSpec((nb, tf), lambda i, j: (0, nj + j)),
            pl.BlockSpec((nb, tf), lambda i, j: (0, 3 * nj + j)),
        ],
        out_specs=(
            full2((n, d_model)),
            pl.BlockSpec((2, nb, tf), lambda i, j: (0, 0, j)),
            pl.BlockSpec((2, nb, tf), lambda i, j: (0, 0, j)),
        ),
        out_shape=(
            jax.ShapeDtypeStruct((n, d_model), F32),
            jax.ShapeDtypeStruct((2, nb, d_ff), F32),
            jax.ShapeDtypeStruct((2, nb, d_ff), F32),
        ),
        scratch_shapes=[pltpu.VMEM((n, d_model), BF16)],
        compiler_params=pltpu.CompilerParams(
            dimension_semantics=("arbitrary", "arbitrary"), vmem_limit_bytes=VMEM_LIMIT_BYTES),
        name="ffn_sample",
    )(x, mr, ma, gna, wor, woa, gf, wup, wup, cw, cw, cb, cb, wd, gfin, state2d, state2d, state2d, state2d)
    return y, tg, tv


def _perm_heads(a, axis):
    shape = a.shape
    a = a.reshape(shape[:axis] + (KV_HEADS // 2, 2, GQA_GROUP, HEAD_DIM) + shape[axis + 1:])
    return jnp.swapaxes(a, axis + 1, axis + 2).reshape(shape)


def _gate_weights(w_a, w_x):
    nblk, bs, _ = w_a.shape
    per = GATE_TILE // bs
    eye = jnp.eye(per, dtype=w_a.dtype)

    def pack(w):
        w4 = w.reshape(nblk // per, per, bs, bs)
        return jnp.einsum("cgij,gh->cgihj", w4, eye).reshape(nblk // per, GATE_TILE, GATE_TILE)
    return jnp.concatenate([pack(w_a), pack(w_x)], axis=-1).astype(BF16)


def kernel(x_prompt, x_sample, state_rnn_conv, state_rnn_h, cache_win_k, cache_win_v, state_ffn_conv,
           norm_mix_g, w_in, rnn_conv_w, rnn_conv_b, w_gate_a, b_gate_a, w_gate_x, b_gate_x, rnn_lambda,
           attn_sinks, rel_bias_table, gn_rnn_g, gn_attn_g, w_out, norm_ffn_g, w_up, ffn_conv_w,
           ffn_conv_b, w_down, norm_final_g):
    batch, seq, d_model = x_prompt.shape
    nb, steps, _ = x_sample.shape
    depth = w_in.shape[0]
    d_rnn = rnn_conv_w.shape[-1]
    d_attn = N_HEADS * HEAD_DIM
    kv_dim = KV_HEADS * HEAD_DIM
    d_ff = w_down.shape[1]
    win = cache_win_k.shape[2]
    assert depth == 1 and d_rnn + d_attn == d_model and w_in.shape[-1] == 2 * d_rnn + d_attn + 2 * kv_dim
    assert win == WINDOW and seq % WINDOW == 0 and w_gate_a.shape[1] == RNN_BLOCKS
    assert rnn_conv_w.shape[1] == 4 and ffn_conv_w.shape[1] == 3 and nb % SUBLANES == 0
    assert steps >= 3 and steps % 2 == 0

    w_in_b = w_in[0].astype(BF16)
    w_out0 = w_out[0]
    w_out_r = w_out0[:d_rnn].astype(BF16)
    w_out_a = _perm_heads(w_out0[d_rnn:], 0).astype(BF16)
    gn_attn_p = _perm_heads(gn_attn_g[0], 0).reshape(1, d_attn)
    w_up_b = w_up[0].astype(BF16)
    w_down_b = w_down[0].astype(BF16)
    wg = _gate_weights(w_gate_a[0], w_gate_x[0])
    row2 = lambda a: a.reshape(1, -1)
    g_mix, g_ffn, g_fin, g_rnn = row2(norm_mix_g[0]), row2(norm_ffn_g[0]), row2(norm_final_g), row2(gn_rnn_g[0])
    cw_r, cb_r = rnn_conv_w[0], row2(rnn_conv_b[0])
    ba, bx, lam = row2(b_gate_a[0]), row2(b_gate_x[0]), row2(rnn_lambda[0])
    cw_f, cb_f = ffn_conv_w[0], row2(ffn_conv_b[0])
    tbl = rel_bias_table.reshape(-1)
    sinks = attn_sinks[0]

    blk = WINDOW
    qi = np.arange(blk)[:, None]
    kj = np.arange(2 * blk)[None, :]
    code_p = jnp.asarray(_t5_bucket_np(blk + qi - kj))
    r = np.arange(N_HEADS * steps)[:, None]
    t_r, h_r = r % steps, r // steps
    npad = 2 * SUBLANES
    code_c = jnp.asarray(_t5_bucket_np(win + t_r - np.arange(win)[None, :]) * N_HEADS + h_r)
    code_n = jnp.asarray(_t5_bucket_np(t_r - np.arange(npad)[None, :]) * N_HEADS + h_r)

    n_p = batch * seq
    tm_a = _pick(PROJ_ROWS, n_p)
    xp2 = x_prompt.reshape(n_p, d_model)
    xr, gr, q, k, v = _inproj(xp2, pl.BlockSpec((tm_a, d_model), lambda i: (i, 0)), n_p // tm_a, tm_a,
                              g_mix, w_in_b, d_rnn, d_attn, kv_dim)
    m_rnn, h_last = _rglru_prompt(xr, gr, cw_r, cb_r, wg, ba, bx, lam, g_rnn, batch, seq,
                                  _pick(RGLRU_ROWS, seq))
    m_attn = _swa_prompt(q, k, v, code_p, tbl, sinks, gn_attn_p, batch, seq)
    y_p, tail_g, tail_v = _ffn_prompt(xp2, m_rnn, m_attn, w_out_r, w_out_a, g_ffn, w_up_b, cw_f, cb_f,
                                      w_down_b, g_fin, batch, seq, _pick(FFN_ROWS, seq),
                                      _pick(FFN_COLS_PROMPT, d_ff))
    y_prompt = y_p.reshape(batch, seq, d_model)
    p_rnn_conv = xr.reshape(batch, seq, d_rnn)[:, seq - 3:, :][None]
    p_rnn_h = h_last[None]
    last_win = lambda a: a.reshape(batch, seq, kv_dim)[:, seq - win:, :].reshape(
        1, batch, win, KV_HEADS, HEAD_DIM)
    p_win_k, p_win_v = last_win(k), last_win(v)
    p_ffn_conv = jnp.concatenate([tail_g[:, SUBLANES - 2:, :], tail_v[:, SUBLANES - 2:, :]], axis=-1)[None]

    n_s = nb * steps
    xs_tm = x_sample.transpose(1, 0, 2).reshape(n_s, d_model)
    xr_s, gr_s, q_s, k_s, v_s = _inproj(xs_tm, pl.BlockSpec((nb, d_model), lambda t: (t, 0)), steps, nb,
                                        g_mix, w_in_b, d_rnn, d_attn, kv_dim)
    conv_tm = state_rnn_conv[0].transpose(1, 0, 2).reshape(3 * nb, d_rnn)
    m_rnn_s, h_new = _rglru_sample(xr_s, gr_s, conv_tm, state_rnn_h[0], cw_r, cb_r, wg, ba, bx, lam, g_rnn,
                                   nb, steps)
    ncg = KV_HEADS // 2
    qs = q_s.reshape(steps, nb, ncg, GQA_GROUP, LANES).transpose(1, 2, 3, 0, 4).reshape(
        nb, ncg, GQA_GROUP * steps, LANES)
    k_tm = k_s.reshape(steps, nb, kv_dim)
    v_tm = v_s.reshape(steps, nb, kv_dim)
    pad = ((0, 0), (0, npad - steps), (0, 0))
    ck_t = cache_win_k[0].transpose(0, 2, 3, 1).reshape(nb, kv_dim, win)
    cv_t = cache_win_v[0].transpose(0, 2, 3, 1).reshape(nb, kv_dim, win)
    bb = SUBLANES
    o_s, ck_new, cv_new = _swa_sample(qs, ck_t, cv_t, jnp.pad(k_tm.transpose(1, 0, 2), pad),
                                      jnp.pad(v_tm.transpose(1, 0, 2), pad), code_c, code_n, tbl, sinks, steps, bb)
    ya_s = o_s.reshape(nb, ncg, GQA_GROUP, steps, LANES).transpose(3, 0, 1, 2, 4).reshape(n_s, d_attn)
    ffn2d = state_ffn_conv[0].reshape(nb, 2 * 2 * d_ff)
    y_s, ns_g, ns_v = _ffn_sample(xs_tm, m_rnn_s, ya_s, gn_attn_p, w_out_r, w_out_a, g_ffn, w_up_b, cw_f, cb_f,
                                  w_down_b, g_fin, ffn2d, nb, steps, _pick(FFN_COLS_SAMPLE, d_ff))
    y_sample = y_s.reshape(steps, nb, d_model).transpose(1, 0, 2)
    s_rnn_conv = xr_s.reshape(steps, nb, d_rnn)[steps - 3:].transpose(1, 0, 2)[None]
    s_rnn_h = h_new[None]

    to_cache = lambda w_t: w_t.reshape(nb, KV_HEADS, HEAD_DIM, win).transpose(0, 3, 1, 2)[None]
    s_win_k, s_win_v = to_cache(ck_new), to_cache(cv_new)
    s_ffn_conv = jnp.concatenate([ns_g, ns_v], axis=-1).transpose(1, 0, 2)[None]

    return (y_prompt, y_sample, p_rnn_conv, p_rnn_h, p_win_k, p_win_v, p_ffn_conv,
            s_rnn_conv, s_rnn_h, s_win_k, s_win_v, s_ffn_conv)
```

```python
import functools
import math

import numpy as np
import jax
import jax.numpy as jnp
from jax import lax
from jax.experimental import pallas as pl
from jax.experimental.pallas import tpu as pltpu

F32 = jnp.float32
BF16 = jnp.bfloat16

HEAD_DIM = 64
KV_HEADS = 4
N_HEADS = 16
GQA_GROUP = N_HEADS // KV_HEADS
RNN_BLOCKS = 16
RG_C = 8.0
WINDOW = 128
N_BUCKETS = 32
MAX_EXACT = N_BUCKETS // 2
REL_MAX_DIST = 128
EPS = 1e-6
NEG_INF = -1e30
ATTN_SCALE = HEAD_DIM ** -0.5

LANES = 128
SUBLANES = 8
HALF_SUBLANES = SUBLANES // 2
VMEM_LIMIT_BYTES = 56 * 1024 * 1024
FFN_VMEM_LIMIT_BYTES = 61 * 1024 * 1024

GATE_TILE = 256

PROJ_ROWS = (512, WINDOW)
RGLRU_ROWS = (1024, 512, WINDOW)
SWA_BLOCKS_PER_STEP = (16, 8, 4, 2, 1)
FFN_ROWS = (1024, 512, 256)
FFN_COLS_PROMPT = (768, 512)
FFN_COLS_SAMPLE = (768, 512)


def _pick(options, n):
    return next(t for t in options if n % t == 0)


def _t5_bucket_np(d):
    n = np.maximum(d, 0)
    nf = np.maximum(n, 1).astype(np.float32)
    large = MAX_EXACT + (np.log(nf / MAX_EXACT) / math.log(REL_MAX_DIST / MAX_EXACT)
                         * (N_BUCKETS - MAX_EXACT)).astype(np.int32)
    large = np.minimum(large, N_BUCKETS - 1)
    return np.where(n < MAX_EXACT, n, large).astype(np.int32)


def _rms(x, g):
    ms = jnp.mean(x * x, axis=-1, keepdims=True)
    return (x * lax.rsqrt(ms + EPS)) * g


GELU_C = math.sqrt(2.0 / math.pi)
GELU_C3 = GELU_C * 0.044715


def _gelu_mul(x, v):
    t = jnp.tanh(x * (GELU_C + GELU_C3 * (x * x)))
    return (0.5 * (x * v)) * (1.0 + t)


def _softplus(x):
    return jnp.maximum(x, 0.0) + jnp.log1p(jnp.exp(-jnp.abs(x)))


def _const_spec(shape):
    nd = len(shape)
    return pl.BlockSpec(shape, lambda *_: (0,) * nd, pipeline_mode=pl.Buffered(1))


def _smem_spec():
    return pl.BlockSpec(memory_space=pltpu.SMEM)


def _inproj_body(x_ref, g_ref, w_ref, xr_ref, gr_ref, q_ref, k_ref, v_ref):
    xn = _rms(x_ref[...], g_ref[...]).astype(BF16)
    d_rnn, d_attn, kv_dim = xr_ref.shape[-1], q_ref.shape[-1], k_ref.shape[-1]
    p = jnp.dot(xn, w_ref[...], preferred_element_type=F32)
    o1, o2, o3, o4 = d_rnn, 2 * d_rnn, 2 * d_rnn + d_attn, 2 * d_rnn + d_attn + kv_dim
    xr_ref[...] = p[:, :o1]
    gr_ref[...] = p[:, o1:o2]
    k_ref[...] = p[:, o3:o4]
    v_ref[...] = p[:, o4:]
    pq = p[:, o2:o3] * ATTN_SCALE
    lane = lax.broadcasted_iota(jnp.int32, (pq.shape[0], LANES), 1)
    blocks = []
    for cg in range(KV_HEADS // 2):
        for g in range(GQA_GROUP):
            ha, hb = 2 * GQA_GROUP * cg + g, 2 * GQA_GROUP * cg + GQA_GROUP + g
            a = pq[:, (ha // 2) * LANES:(ha // 2 + 1) * LANES]
            b = pq[:, (hb // 2) * LANES:(hb // 2 + 1) * LANES]
            if g % 2 == 0:
                blocks.append(jnp.where(lane < HEAD_DIM, a, pltpu.roll(b, HEAD_DIM, 1)))
            else:
                blocks.append(jnp.where(lane < HEAD_DIM, pltpu.roll(a, HEAD_DIM, 1), b))
    q_ref[...] = jnp.concatenate(blocks, axis=1).astype(BF16)


def _inproj(x2d, x_spec, n_steps, tm, g, w, d_rnn, d_attn, kv_dim):
    n = n_steps * tm
    d_model = g.shape[-1]
    row = lambda i: (i, 0)
    out_shape = (
        jax.ShapeDtypeStruct((n, d_rnn), F32),
        jax.ShapeDtypeStruct((n, d_rnn), F32),
        jax.ShapeDtypeStruct((n, d_attn), BF16),
        jax.ShapeDtypeStruct((n, kv_dim), F32),
        jax.ShapeDtypeStruct((n, kv_dim), F32),
    )
    out_specs = (
        pl.BlockSpec((tm, d_rnn), row),
        pl.BlockSpec((tm, d_rnn), row),
        pl.BlockSpec((tm, d_attn), row),
        pl.BlockSpec((tm, kv_dim), row),
        pl.BlockSpec((tm, kv_dim), row),
    )
    return pl.pallas_call(
        _inproj_body,
        grid=(n_steps,),
        in_specs=[x_spec, _const_spec((1, d_model)), _const_spec(w.shape)],
        out_specs=out_specs,
        out_shape=out_shape,
        compiler_params=pltpu.CompilerParams(
            dimension_semantics=("arbitrary",), vmem_limit_bytes=VMEM_LIMIT_BYTES),
        name="inproj",
    )(x2d, g, w)


def _rglru_gates(xc, wg_ref, ba, bx, lam):
    d_rnn = xc.shape[-1]
    pre_a, pre_x = [], []
    for c in range(d_rnn // GATE_TILE):
        xb = xc[:, c * GATE_TILE:(c + 1) * GATE_TILE].astype(BF16)
        pre = jnp.dot(xb, wg_ref[c], preferred_element_type=F32)
        pre_a.append(pre[:, :GATE_TILE])
        pre_x.append(pre[:, GATE_TILE:])
    r = jax.nn.sigmoid(jnp.concatenate(pre_a, axis=1) + ba)
    i = jax.nn.sigmoid(jnp.concatenate(pre_x, axis=1) + bx)
    neg_log_a = r * (RG_C * _softplus(-lam))
    a = jnp.exp(-neg_log_a)
    one_minus_a2 = jnp.tanh(neg_log_a) * (a * a + 1.0)
    root = jnp.where(one_minus_a2 > 0.0, one_minus_a2 * lax.rsqrt(one_minus_a2), 0.0)
    u = root * (i * xc)
    return a, u


def _rglru_prompt_body(xr_ref, gr_ref, cw_ref, cb_ref, wg_ref, ba_ref, bx_ref, lam_ref, gn_ref,
                       out_ref, hlast_ref, xbuf, abuf, ubuf, hbuf, hc_ref, *, tt, conv_w):
    ti = pl.program_id(1)
    d_rnn = xr_ref.shape[-1]

    @pl.when(ti == 0)
    def _():
        xbuf[0:SUBLANES, :] = jnp.zeros((SUBLANES, d_rnn), F32)
        hc_ref[...] = jnp.zeros_like(hc_ref)

    x = xr_ref[...]
    hist = xbuf[...]
    xc = cw_ref[conv_w - 1:conv_w, :] * x
    for k in range(conv_w - 1):
        xc = xc + cw_ref[k:k + 1, :] * _shift_rows(x, hist, conv_w - 1 - k)
    xc = xc + cb_ref[...]
    xbuf[...] = x[tt - SUBLANES:, :]

    a, u = _rglru_gates(xc, wg_ref, ba_ref[...], bx_ref[...], lam_ref[...])
    abuf[...] = a
    ubuf[...] = u

    row = lax.broadcasted_iota(jnp.int32, (SUBLANES, d_rnn), 0)
    row_in_half = row & (HALF_SUBLANES - 1)

    def group(gidx, hc):
        r0 = pl.multiple_of(gidx * SUBLANES, SUBLANES)
        ag = abuf[pl.ds(r0, SUBLANES), :]
        ug = ubuf[pl.ds(r0, SUBLANES), :]
        for k in (1, 2):
            a_prev = jnp.where(row_in_half >= k, pltpu.roll(ag, k, 0), 1.0)
            u_prev = jnp.where(row_in_half >= k, pltpu.roll(ug, k, 0), 0.0)
            ug = ag * u_prev + ug
            ag = ag * a_prev
        h_lo = ag * hc + ug
        h_mid = jnp.broadcast_to(h_lo[HALF_SUBLANES - 1:HALF_SUBLANES, :], (SUBLANES, d_rnn))
        h = jnp.where(row < HALF_SUBLANES, h_lo, ag * h_mid + ug)
        hbuf[pl.ds(r0, SUBLANES), :] = h
        return jnp.broadcast_to(h[SUBLANES - 1:SUBLANES, :], (SUBLANES, d_rnn))

    hc = lax.fori_loop(0, tt // SUBLANES, group, hc_ref[...], unroll=2)
    hc_ref[...] = hc
    hlast_ref[0] = hc

    y = _gelu_mul(gr_ref[...], hbuf[...])
    out_ref[...] = _rms(y, gn_ref[...]).astype(BF16)


def _rglru_prompt(xr, gr, cw, cb, wg, ba, bx, lam, gn, batch, seq, tt):
    d_rnn = xr.shape[-1]
    nt = seq // tt
    conv_w = cw.shape[0]
    tile = lambda b, t: (b * nt + t, 0)
    out, hlast = pl.pallas_call(
        functools.partial(_rglru_prompt_body, tt=tt, conv_w=conv_w),
        grid=(batch, nt),
        in_specs=[
            pl.BlockSpec((tt, d_rnn), tile),
            pl.BlockSpec((tt, d_rnn), tile),
            _const_spec(cw.shape), _const_spec(cb.shape), _const_spec(wg.shape),
            _const_spec(ba.shape), _const_spec(bx.shape), _const_spec(lam.shape), _const_spec(gn.shape),
        ],
        out_specs=(
            pl.BlockSpec((tt, d_rnn), tile),
            pl.BlockSpec((1, SUBLANES, d_rnn), lambda b, t: (b, 0, 0)),
        ),
        out_shape=(
            jax.ShapeDtypeStruct((batch * seq, d_rnn), BF16),
            jax.ShapeDtypeStruct((batch, SUBLANES, d_rnn), F32),
        ),
        scratch_shapes=[
            pltpu.VMEM((SUBLANES, d_rnn), F32),
            pltpu.VMEM((tt, d_rnn), F32),
            pltpu.VMEM((tt, d_rnn), F32),
            pltpu.VMEM((tt, d_rnn), F32),
            pltpu.VMEM((SUBLANES, d_rnn), F32),
        ],
        compiler_params=pltpu.CompilerParams(
            dimension_semantics=("arbitrary", "arbitrary"), vmem_limit_bytes=VMEM_LIMIT_BYTES),
        name="rglru_prompt",
    )(xr, gr, cw, cb, wg, ba, bx, lam, gn)
    return out, hlast[:, 0, :]


def _rglru_sample_body(xr_ref, gr_ref, c0_ref, c1_ref, c2_ref, h0_ref, cw_ref, cb_ref, wg_ref,
                       ba_ref, bx_ref, lam_ref, gn_ref, out_ref, hnew_ref, *, nb, steps):
    x = xr_ref[...]
    hist = [c0_ref[...], c1_ref[...], c2_ref[...]]
    xs = [x[t * nb:(t + 1) * nb, :] for t in range(steps)]
    xp = hist + xs
    conv_w = len(hist) + 1
    xc = cw_ref[conv_w - 1:conv_w, :] * x
    for k in range(conv_w - 1):
        shifted = jnp.concatenate(xp[k:k + steps], axis=0)
        xc = xc + cw_ref[k:k + 1, :] * shifted
    xc = xc + cb_ref[...]
    a, u = _rglru_gates(xc, wg_ref, ba_ref[...], bx_ref[...], lam_ref[...])
    h = h0_ref[...]
    hs = []
    for t in range(steps):
        h = a[t * nb:(t + 1) * nb, :] * h + u[t * nb:(t + 1) * nb, :]
        hs.append(h)
    hnew_ref[...] = h
    y = _gelu_mul(gr_ref[...], jnp.concatenate(hs, axis=0))
    out_ref[...] = _rms(y, gn_ref[...]).astype(BF16)


def _rglru_sample(xr, gr, conv_state2d, h0, cw, cb, wg, ba, bx, lam, gn, nb, steps):
    d_rnn = xr.shape[-1]
    n = nb * steps
    assert cw.shape[0] == 4
    full = lambda shape: pl.BlockSpec(shape, lambda i: (0,) * len(shape))
    return pl.pallas_call(
        functools.partial(_rglru_sample_body, nb=nb, steps=steps),
        grid=(1,),
        in_specs=[
            full((n, d_rnn)), full((n, d_rnn)),
            pl.BlockSpec((nb, d_rnn), lambda i: (0, 0)),
            pl.BlockSpec((nb, d_rnn), lambda i: (1, 0)),
            pl.BlockSpec((nb, d_rnn), lambda i: (2, 0)),
            full((nb, d_rnn)),
            full(cw.shape), full(cb.shape), full(wg.shape), full(ba.shape), full(bx.shape),
            full(lam.shape), full(gn.shape),
        ],
        out_specs=(full((n, d_rnn)), full((nb, d_rnn))),
        out_shape=(jax.ShapeDtypeStruct((n, d_rnn), BF16), jax.ShapeDtypeStruct((nb, d_rnn), F32)),
        compiler_params=pltpu.CompilerParams(
            dimension_semantics=("arbitrary",), vmem_limit_bytes=VMEM_LIMIT_BYTES),
        name="rglru_sample",
    )(xr, gr, conv_state2d, conv_state2d, conv_state2d, h0, cw, cb, wg, ba, bx, lam, gn)


def _swa_prompt_body(code_ref, tbl_ref, sink_ref, q_ref, kp_ref, kc_ref, vp_ref, vc_ref, gn_ref,
                     out_ref, bias_ref):
    j = pl.program_id(1)
    blk = WINDOW
    rows = GQA_GROUP * blk

    @pl.when((pl.program_id(0) == 0) & (j == 0))
    def _():
        code = code_ref[...]
        qi = lax.broadcasted_iota(jnp.int32, code.shape, 0)
        kj = lax.broadcasted_iota(jnp.int32, code.shape, 1)
        dist = blk + qi - kj
        in_window = (dist & -WINDOW) == 0
        for h in range(N_HEADS):
            def pick(b, acc, h=h):
                return jnp.where(code == b, tbl_ref[b * N_HEADS + h], acc)
            bias_h = lax.fori_loop(0, N_BUCKETS, pick, jnp.zeros(code.shape, F32))
            bias_h = jnp.where(in_window, bias_h, NEG_INF)
            kv, g = divmod(h, GQA_GROUP)
            bias_ref[1, kv, g * blk:(g + 1) * blk, :] = bias_h
            bias_ref[0, kv, g * blk:(g + 1) * blk, :] = jnp.where(kj >= blk, bias_h, NEG_INF)

    lane = lax.broadcasted_iota(jnp.int32, (2 * blk, LANES), 1)
    row_g = lax.broadcasted_iota(jnp.int32, (rows, 1), 0) // blk
    sink_cols = []
    for kv in range(KV_HEADS):
        col = jnp.zeros((rows, 1), F32)
        for g in range(GQA_GROUP):
            col = jnp.where(row_g == g, sink_ref[kv * GQA_GROUP + g], col)
        sink_cols.append(col)
    for sub in range(q_ref.shape[0] // blk):
        rows_q = slice(sub * blk, (sub + 1) * blk)
        if sub == 0:
            has_prev = jnp.where(j > 0, 1, 0)
            k_prev, v_prev = kp_ref[...], vp_ref[...]
        else:
            has_prev = 1
            k_prev, v_prev = kc_ref[(sub - 1) * blk:sub * blk, :], vc_ref[(sub - 1) * blk:sub * blk, :]
        kband = jnp.concatenate([k_prev, kc_ref[rows_q, :]], axis=0)
        vband = jnp.concatenate([v_prev, vc_ref[rows_q, :]], axis=0)
        y = _swa_block(q_ref[rows_q, :], kband, vband, bias_ref, has_prev, sink_cols, lane)
        out_ref[rows_q, :] = _rms(y, gn_ref[...]).astype(BF16)


def _swa_block(q, kband, vband, bias_ref, has_prev, sink_cols, lane):
    blk = q.shape[0]
    outs = []
    for cg in range(KV_HEADS // 2):
        qs = jnp.concatenate(
            [q[:, (cg * GQA_GROUP + g) * LANES:(cg * GQA_GROUP + g + 1) * LANES] for g in range(GQA_GROUP)],
            axis=0)
        ka = kband[:, cg * LANES:(cg + 1) * LANES]
        va = vband[:, cg * LANES:(cg + 1) * LANES]
        o = None
        for par in range(2):
            kv = 2 * cg + par
            half = (lane < HEAD_DIM) if par == 0 else (lane >= HEAD_DIM)
            km = jnp.where(half, ka, 0.0).astype(BF16)
            vm = jnp.where(half, va, 0.0).astype(BF16)
            s = lax.dot_general(qs, km, (((1,), (1,)), ((), ())), preferred_element_type=F32)
            s = s + bias_ref[has_prev, kv]
            sink = sink_cols[kv]
            m = jnp.maximum(jnp.max(s, axis=-1, keepdims=True), sink)
            p = jnp.exp(s - m)
            denom = jnp.sum(p, axis=-1, keepdims=True) + jnp.exp(sink - m)
            part = jnp.dot(p.astype(BF16), vm, preferred_element_type=F32) * (1.0 / denom)
            o = part if o is None else o + part
        outs.extend(o[g * blk:(g + 1) * blk, :] for g in range(GQA_GROUP))
    return jnp.concatenate(outs, axis=1)


def _swa_prompt(q, k, v, code, tbl, sinks, gn, batch, seq):
    blk = WINDOW
    per_step = _pick(SWA_BLOCKS_PER_STEP, seq // blk)
    nb = seq // (per_step * blk)
    d_attn = q.shape[-1]
    kv_dim = k.shape[-1]
    cur = lambda b, j: (b * nb + j, 0)
    prev = lambda b, j: (per_step * (b * nb + j) - jnp.where(j > 0, 1, 0), 0)
    return pl.pallas_call(
        _swa_prompt_body,
        grid=(batch, nb),
        in_specs=[
            _const_spec(code.shape), _smem_spec(), _smem_spec(),
            pl.BlockSpec((per_step * blk, d_attn), cur),
            pl.BlockSpec((blk, kv_dim), prev), pl.BlockSpec((per_step * blk, kv_dim), cur),
            pl.BlockSpec((blk, kv_dim), prev), pl.BlockSpec((per_step * blk, kv_dim), cur),
            _const_spec(gn.shape),
        ],
        out_specs=pl.BlockSpec((per_step * blk, d_attn), cur),
        out_shape=jax.ShapeDtypeStruct((batch * seq, d_attn), BF16),
        scratch_shapes=[pltpu.VMEM((2, KV_HEADS, GQA_GROUP * blk, 2 * blk), F32)],
        compiler_params=pltpu.CompilerParams(
            dimension_semantics=("arbitrary", "arbitrary"), vmem_limit_bytes=VMEM_LIMIT_BYTES),
        name="swa_prompt",
    )(code, tbl, sinks, q, k, k, v, v, gn)


def _slide_window(win_ref, new_ref, out_ref, steps):
    bb, kv_dim, win = win_ref.shape
    npad = new_ref.shape[1]
    new_t = new_ref[...].reshape(bb * npad, kv_dim).T
    lane = lax.broadcasted_iota(jnp.int32, (kv_dim, win), 1)
    for b in range(bb):
        placed = pltpu.roll(new_t, (win - steps - b * npad) % win, 1)
        shifted = pltpu.roll(win_ref[b], win - steps, 1)
        out_ref[b] = jnp.where(lane >= win - steps, placed, shifted)


def _swa_sample_body(codec_ref, coden_ref, tbl_ref, sink_ref, q_ref, ck_ref, cv_ref, kn_ref, vn_ref,
                     out_ref, ck_out_ref, cv_out_ref, biasc_ref, biasn_ref, sinkc_ref, *, steps):
    bb = q_ref.shape[0]
    nrow = N_HEADS * steps
    win = ck_ref.shape[2]
    npad = kn_ref.shape[1]

    @pl.when(pl.program_id(0) == 0)
    def _():
        codec = codec_ref[...]
        coden = coden_ref[...]
        hrow = lax.broadcasted_iota(jnp.int32, (nrow, 1), 0) // steps

        def pick(idx, accs):
            ac, an = accs
            val = tbl_ref[idx]
            return jnp.where(codec == idx, val, ac), jnp.where(coden == idx, val, an)
        bc, bn = lax.fori_loop(0, N_BUCKETS * N_HEADS, pick,
                               (jnp.zeros(codec.shape, F32), jnp.zeros(coden.shape, F32)))
        biasc_ref[...] = bc
        biasn_ref[...] = bn

        def pick_sink(h, acc):
            return jnp.where(hrow == h, sink_ref[h], acc)
        sinkc_ref[...] = lax.fori_loop(0, N_HEADS, pick_sink, jnp.zeros((nrow, 1), F32))

    q = q_ref[...].astype(F32)
    lane = lax.broadcasted_iota(jnp.int32, (bb, GQA_GROUP * steps, LANES), 2)
    zeros = jnp.zeros((bb, GQA_GROUP * steps, LANES), F32)
    pieces = []
    for cg in range(KV_HEADS // 2):
        for par in range(2):
            half = (lane < HEAD_DIM) if par == 0 else (lane >= HEAD_DIM)
            qm = jnp.where(half, q[:, cg], zeros)
            pieces.append(jnp.concatenate([qm, zeros] if cg == 0 else [zeros, qm], axis=2))
    qm = jnp.concatenate(pieces, axis=1).astype(BF16)

    ck = ck_ref[...].astype(BF16)
    kn = kn_ref[...].astype(BF16)
    s_c = jnp.einsum("bqd,bdk->bqk", qm, ck, preferred_element_type=F32)
    s_n = jnp.einsum("bqd,bkd->bqk", qm, kn, preferred_element_type=F32)

    t_c = lax.broadcasted_iota(jnp.int32, (nrow, win), 0) % steps
    k_c = lax.broadcasted_iota(jnp.int32, (nrow, win), 1)
    valid_c = k_c > t_c
    t_n = lax.broadcasted_iota(jnp.int32, (nrow, npad), 0) % steps
    k_n = lax.broadcasted_iota(jnp.int32, (nrow, npad), 1)
    valid_n = k_n <= t_n

    s_c = jnp.where(valid_c[None], s_c + biasc_ref[...][None], NEG_INF)
    s_n = jnp.where(valid_n[None], s_n + biasn_ref[...][None], NEG_INF)
    sink = sinkc_ref[...][None]
    m = jnp.maximum(jnp.maximum(jnp.max(s_c, axis=-1, keepdims=True),
                                jnp.max(s_n, axis=-1, keepdims=True)), sink)
    p_c = jnp.exp(s_c - m)
    p_n = jnp.exp(s_n - m)
    denom = (jnp.sum(p_c, axis=-1, keepdims=True) + jnp.sum(p_n, axis=-1, keepdims=True)
             + jnp.exp(sink - m))
    r = 1.0 / denom
    w_c = (p_c * r).astype(BF16)
    w_n = (p_n * r).astype(BF16)
    o = (jnp.einsum("bqk,bdk->bqd", w_c, cv_ref[...].astype(BF16), preferred_element_type=F32)
         + jnp.einsum("bqk,bkd->bqd", w_n, vn_ref[...].astype(BF16), preferred_element_type=F32))
    gt = GQA_GROUP * steps
    lane_o = lax.broadcasted_iota(jnp.int32, (bb, gt, LANES), 2)
    for cg in range(KV_HEADS // 2):
        lo = o[:, cg * 2 * gt:cg * 2 * gt + gt, cg * LANES:(cg + 1) * LANES]
        hi = o[:, cg * 2 * gt + gt:(cg + 1) * 2 * gt, cg * LANES:(cg + 1) * LANES]
        out_ref[:, cg] = jnp.where(lane_o < HEAD_DIM, lo, hi)

    _slide_window(ck_ref, kn_ref, ck_out_ref, steps)
    _slide_window(cv_ref, vn_ref, cv_out_ref, steps)


def _swa_sample(qs, ck, cv, kn, vn, codec, coden, tbl, sinks, steps, bb):
    nbatch, ncg, gt, _ = qs.shape
    kv_dim, win = ck.shape[1], ck.shape[2]
    npad = kn.shape[1]
    nrow = N_HEADS * steps
    assert bb * npad == win
    blk4 = lambda i: (i, 0, 0, 0)
    blk3 = lambda i: (i, 0, 0)
    return pl.pallas_call(
        functools.partial(_swa_sample_body, steps=steps),
        grid=(nbatch // bb,),
        in_specs=[
            _const_spec(codec.shape), _const_spec(coden.shape), _smem_spec(), _smem_spec(),
            pl.BlockSpec((bb, ncg, gt, LANES), blk4),
            pl.BlockSpec((bb, kv_dim, win), blk3), pl.BlockSpec((bb, kv_dim, win), blk3),
            pl.BlockSpec((bb, npad, kv_dim), blk3), pl.BlockSpec((bb, npad, kv_dim), blk3),
        ],
        out_specs=(pl.BlockSpec((bb, ncg, gt, LANES), blk4),
                   pl.BlockSpec((bb, kv_dim, win), blk3), pl.BlockSpec((bb, kv_dim, win), blk3)),
        out_shape=(jax.ShapeDtypeStruct((nbatch, ncg, gt, LANES), F32),
                   jax.ShapeDtypeStruct((nbatch, kv_dim, win), F32),
                   jax.ShapeDtypeStruct((nbatch, kv_dim, win), F32)),
        scratch_shapes=[pltpu.VMEM((nrow, win), F32), pltpu.VMEM((nrow, npad), F32),
                        pltpu.VMEM((nrow, 1), F32)],
        compiler_params=pltpu.CompilerParams(
            dimension_semantics=("arbitrary",), vmem_limit_bytes=VMEM_LIMIT_BYTES),
        name="swa_sample",
    )(codec, coden, tbl, sinks, qs, ck, cv, kn, vn)


def _shift_rows(up, hist, k):
    rolled = pltpu.roll(up, k, 0)
    row = lax.broadcasted_iota(jnp.int32, hist.shape, 0)
    head = jnp.where(row < k, pltpu.roll(hist, k, 0), rolled[0:SUBLANES, :])
    return jnp.concatenate([head, rolled[SUBLANES:, :]], axis=0)


def _ffn_conv(up, hist, cw, cb, *, tm, sample_nb):
    if sample_nb is None:
        prev2 = _shift_rows(up, hist, 2)
        prev1 = _shift_rows(up, hist, 1)
    else:
        s0, s1 = hist
        prev2 = jnp.concatenate([s0, s1] + ([up[:tm - 2 * sample_nb, :]] if tm > 2 * sample_nb else []), axis=0)
        prev1 = jnp.concatenate([s1, up[:tm - sample_nb, :]], axis=0)
    return cw[0:1, :] * prev2 + cw[1:2, :] * prev1 + cw[2:3, :] * up + cb


FFN_SLAB = 256


def _ffn_prompt_body(x_hbm, mr_ref, ma_ref, wor_ref, woa_ref, gf_ref, wg_ref, wv_ref, cwg_ref, cwv_ref, cbg_ref,
                     cbv_ref, wd_ref, gfin_ref, y_ref, tg_ref, tv_ref, hn_ref, car_g, car_v, x_sem,
                     *, tm, tiles_per_seq):
    i = pl.program_id(0)
    j = pl.program_id(1)
    nj = pl.num_programs(1)
    n_slab = tm // FFN_SLAB
    slab = lambda r: slice(r * FFN_SLAB, (r + 1) * FFN_SLAB)

    @pl.when(j == 0)
    def _():
        x_copy = pltpu.make_async_copy(x_hbm.at[pl.ds(pl.multiple_of(i * tm, tm), tm), :], y_ref, x_sem)
        x_copy.start()

        def out_proj(r):
            return (jnp.dot(mr_ref[slab(r), :], wor_ref[...], preferred_element_type=F32)
                    + jnp.dot(ma_ref[slab(r), :], woa_ref[...], preferred_element_type=F32))

        pending = out_proj(0)
        x_copy.wait()
        for r in range(n_slab):
            d = pending
            if r + 1 < n_slab:
                pending = out_proj(r + 1)
            h = y_ref[slab(r), :] + d
            y_ref[slab(r), :] = h
            hn_ref[slab(r), :] = _rms(h, gf_ref[...]).astype(BF16)

    @pl.when((i == 0) & (j == 0))
    def _():
        car_g[...] = jnp.zeros_like(car_g)
        car_v[...] = jnp.zeros_like(car_v)

    cwg, cwv, cbg, cbv = cwg_ref[...], cwv_ref[...], cbg_ref[...], cbv_ref[...]

    def up_proj(r):
        hn = hn_ref[slab(r), :]
        return (jnp.dot(hn, wg_ref[...], preferred_element_type=F32),
                jnp.dot(hn, wv_ref[...], preferred_element_type=F32))

    seq_start = i % tiles_per_seq == 0
    hist_g = jnp.where(seq_start, 0.0, car_g[j])
    hist_v = jnp.where(seq_start, 0.0, car_v[j])
    pending = up_proj(0)
    act = None
    for r in range(n_slab):
        up_g, up_v = pending
        if r + 1 < n_slab:
            pending = up_proj(r + 1)
        if act is not None:
            y_ref[slab(r - 1), :] += jnp.dot(act, wd_ref[...], preferred_element_type=F32)
        c_g = _ffn_conv(up_g, hist_g, cwg, cbg, tm=FFN_SLAB, sample_nb=None)
        c_v = _ffn_conv(up_v, hist_v, cwv, cbv, tm=FFN_SLAB, sample_nb=None)
        hist_g = up_g[FFN_SLAB - SUBLANES:, :]
        hist_v = up_v[FFN_SLAB - SUBLANES:, :]
        act = (jax.nn.gelu(c_g) * c_v).astype(BF16)
    y_ref[slab(n_slab - 1), :] += jnp.dot(act, wd_ref[...], preferred_element_type=F32)
    car_g[j] = hist_g
    car_v[j] = hist_v
    tg_ref[0] = hist_g
    tv_ref[0] = hist_v

    @pl.when(j == nj - 1)
    def _():
        for r in range(n_slab):
            y_ref[slab(r), :] = _rms(y_ref[slab(r), :], gfin_ref[...])


def _ffn_sample_body(x_ref, mr_ref, ma_ref, gna_ref, wor_ref, woa_ref, gf_ref, wg_ref, wv_ref, cwg_ref, cwv_ref,
                     cbg_ref, cbv_ref, wd_ref, gfin_ref, s0g_ref, s1g_ref, s0v_ref, s1v_ref,
                     y_ref, tg_ref, tv_ref, hn_ref, *, nb):
    j = pl.program_id(1)
    nj = pl.num_programs(1)
    tm = x_ref.shape[0]

    @pl.when(j == 0)
    def _():
        ma = _rms(ma_ref[...], gna_ref[...]).astype(BF16)
        h = (x_ref[...] + jnp.dot(mr_ref[...], wor_ref[...], preferred_element_type=F32)
             + jnp.dot(ma, woa_ref[...], preferred_element_type=F32))
        y_ref[...] = h
        hn_ref[...] = _rms(h, gf_ref[...]).astype(BF16)

    rows_per_slab = 2 * nb
    n_slab = tm // rows_per_slab
    slab = lambda r: slice(r * rows_per_slab, (r + 1) * rows_per_slab)
    cwg, cwv, cbg, cbv = cwg_ref[...], cwv_ref[...], cbg_ref[...], cbv_ref[...]

    def up_proj(r):
        hn = hn_ref[slab(r), :]
        return (jnp.dot(hn, wg_ref[...], preferred_element_type=F32),
                jnp.dot(hn, wv_ref[...], preferred_element_type=F32))

    hist_g = (s0g_ref[...], s1g_ref[...])
    hist_v = (s0v_ref[...], s1v_ref[...])
    pending = up_proj(0)
    act = None
    for r in range(n_slab):
        up_g, up_v = pending
        if r + 1 < n_slab:
            pending = up_proj(r + 1)
        if act is not None:
            y_ref[slab(r - 1), :] += jnp.dot(act, wd_ref[...], preferred_element_type=F32)
        c_g = _ffn_conv(up_g, hist_g, cwg, cbg, tm=rows_per_slab, sample_nb=nb)
        c_v = _ffn_conv(up_v, hist_v, cwv, cbv, tm=rows_per_slab, sample_nb=nb)
        hist_g = (up_g[:nb, :], up_g[nb:, :])
        hist_v = (up_v[:nb, :], up_v[nb:, :])
        act = (jax.nn.gelu(c_g) * c_v).astype(BF16)
    y_ref[slab(n_slab - 1), :] += jnp.dot(act, wd_ref[...], preferred_element_type=F32)
    tg_ref[0], tg_ref[1] = hist_g
    tv_ref[0], tv_ref[1] = hist_v

    @pl.when(j == nj - 1)
    def _():
        y_ref[...] = _rms(y_ref[...], gfin_ref[...])


def _ffn_prompt(x, mr, ma, wor, woa, gf, wup, cw, cb, wd, gfin, batch, seq, tm, tf):
    n, d_model = x.shape
    d_half = mr.shape[-1]
    d_ff = wd.shape[0]
    ni, nj = n // tm, d_ff // tf
    tiles_per_seq = seq // tm
    row = lambda i, j: (i, 0)
    gate_col = lambda i, j: (0, j)
    val_col = lambda i, j: (0, nj + j)
    tail = lambda i, j: (i, 0, j)
    y, tg, tv = pl.pallas_call(
        functools.partial(_ffn_prompt_body, tm=tm, tiles_per_seq=tiles_per_seq),
        grid=(ni, nj),
        in_specs=[
            pl.BlockSpec(memory_space=pl.ANY),
            pl.BlockSpec((tm, d_half), row), pl.BlockSpec((tm, d_half), row),
            _const_spec(wor.shape), _const_spec(woa.shape), _const_spec(gf.shape),
            pl.BlockSpec((d_model, tf), gate_col), pl.BlockSpec((d_model, tf), val_col),
            pl.BlockSpec((cw.shape[0], tf), gate_col), pl.BlockSpec((cw.shape[0], tf), val_col),
            pl.BlockSpec((1, tf), gate_col), pl.BlockSpec((1, tf), val_col),
            pl.BlockSpec((tf, d_model), lambda i, j: (j, 0)),
            _const_spec(gfin.shape),
        ],
        out_specs=(
            pl.BlockSpec((tm, d_model), row),
            pl.BlockSpec((1, SUBLANES, tf), tail),
            pl.BlockSpec((1, SUBLANES, tf), tail),
        ),
        out_shape=(
            jax.ShapeDtypeStruct((n, d_model), F32),
            jax.ShapeDtypeStruct((ni, SUBLANES, d_ff), F32),
            jax.ShapeDtypeStruct((ni, SUBLANES, d_ff), F32),
        ),
        scratch_shapes=[
            pltpu.VMEM((tm, d_model), BF16),
            pltpu.VMEM((nj, SUBLANES, tf), F32), pltpu.VMEM((nj, SUBLANES, tf), F32),
            pltpu.SemaphoreType.DMA(()),
        ],
        compiler_params=pltpu.CompilerParams(
            dimension_semantics=("arbitrary", "arbitrary"), vmem_limit_bytes=FFN_VMEM_LIMIT_BYTES),
        name="ffn_prompt",
    )(x, mr, ma, wor, woa, gf, wup, wup, cw, cw, cb, cb, wd, gfin)
    return y, tg[tiles_per_seq - 1::tiles_per_seq], tv[tiles_per_seq - 1::tiles_per_seq]


def _ffn_sample(x, mr, ma, gna, wor, woa, gf, wup, cw, cb, wd, gfin, state2d, nb, steps, tf):
    n, d_model = x.shape
    d_half = mr.shape[-1]
    d_ff = wd.shape[0]
    nj = d_ff // tf
    full2 = lambda shape: pl.BlockSpec(shape, lambda i, j: (0, 0))
    gate_col = lambda i, j: (0, j)
    val_col = lambda i, j: (0, nj + j)
    y, tg, tv = pl.pallas_call(
        functools.partial(_ffn_sample_body, nb=nb),
        grid=(1, nj),
        in_specs=[
            full2((n, d_model)), full2((n, d_half)), full2((n, d_half)), full2(gna.shape),
            _const_spec(wor.shape), _const_spec(woa.shape), _const_spec(gf.shape),
            pl.BlockSpec((d_model, tf), gate_col), pl.BlockSpec((d_model, tf), val_col),
            pl.BlockSpec((cw.shape[0], tf), gate_col), pl.BlockSpec((cw.shape[0], tf), val_col),
            pl.BlockSpec((1, tf), gate_col), pl.BlockSpec((1, tf), val_col),
            pl.BlockSpec((tf, d_model), lambda i, j: (j, 0)),
            _const_spec(gfin.shape),
            pl.BlockSpec((nb, tf), lambda i, j: (0, j)),
            pl.BlockSpec((nb, tf), lambda i, j: (0, 2 * nj + j)),
            pl.BlockSpec((nb, tf), lambda i, j: (0, nj + j)),
            pl.BlockSpec((nb, tf), lambda i, j: (0, 3 * nj + j)),
        ],
        out_specs=(
            full2((n, d_model)),
            pl.BlockSpec((2, nb, tf), lambda i, j: (0, 0, j)),
            pl.BlockSpec((2, nb, tf), lambda i, j: (0, 0, j)),
        ),
        out_shape=(
            jax.ShapeDtypeStruct((n, d_model), F32),
            jax.ShapeDtypeStruct((2, nb, d_ff), F32),
            jax.ShapeDtypeStruct((2, nb, d_ff), F32),
        ),
        scratch_shapes=[pltpu.VMEM((n, d_model), BF16)],
        compiler_params=pltpu.CompilerParams(
            dimension_semantics=("arbitrary", "arbitrary"), vmem_limit_bytes=VMEM_LIMIT_BYTES),
        name="ffn_sample",
    )(x, mr, ma, gna, wor, woa, gf, wup, wup, cw, cw, cb, cb, wd, gfin, state2d, state2d, state2d, state2d)
    return y, tg, tv


def _perm_heads(a, axis):
    shape = a.shape
    a = a.reshape(shape[:axis] + (KV_HEADS // 2, 2, GQA_GROUP, HEAD_DIM) + shape[axis + 1:])
    return jnp.swapaxes(a, axis + 1, axis + 2).reshape(shape)


def _gate_weights(w_a, w_x):
    nblk, bs, _ = w_a.shape
    per = GATE_TILE // bs
    eye = jnp.eye(per, dtype=w_a.dtype)

    def pack(w):
        w4 = w.reshape(nblk // per, per, bs, bs)
        return jnp.einsum("cgij,gh->cgihj", w4, eye).reshape(nblk // per, GATE_TILE, GATE_TILE)
    return jnp.concatenate([pack(w_a), pack(w_x)], axis=-1).astype(BF16)


def kernel(x_prompt, x_sample, state_rnn_conv, state_rnn_h, cache_win_k, cache_win_v, state_ffn_conv,
           norm_mix_g, w_in, rnn_conv_w, rnn_conv_b, w_gate_a, b_gate_a, w_gate_x, b_gate_x, rnn_lambda,
           attn_sinks, rel_bias_table, gn_rnn_g, gn_attn_g, w_out, norm_ffn_g, w_up, ffn_conv_w,
           ffn_conv_b, w_down, norm_final_g):
    batch, seq, d_model = x_prompt.shape
    nb, steps, _ = x_sample.shape
    depth = w_in.shape[0]
    d_rnn = rnn_conv_w.shape[-1]
    d_attn = N_HEADS * HEAD_DIM
    kv_dim = KV_HEADS * HEAD_DIM
    d_ff = w_down.shape[1]
    win = cache_win_k.shape[2]
    assert depth == 1 and d_rnn + d_attn == d_model and w_in.shape[-1] == 2 * d_rnn + d_attn + 2 * kv_dim
    assert win == WINDOW and seq % WINDOW == 0 and w_gate_a.shape[1] == RNN_BLOCKS
    assert rnn_conv_w.shape[1] == 4 and ffn_conv_w.shape[1] == 3 and nb % SUBLANES == 0
    assert steps >= 3 and steps % 2 == 0

    w_in_b = w_in[0].astype(BF16)
    w_out0 = w_out[0]
    w_out_r = w_out0[:d_rnn].astype(BF16)
    w_out_a = _perm_heads(w_out0[d_rnn:], 0).astype(BF16)
    gn_attn_p = _perm_heads(gn_attn_g[0], 0).reshape(1, d_attn)
    w_up_b = w_up[0].astype(BF16)
    w_down_b = w_down[0].astype(BF16)
    wg = _gate_weights(w_gate_a[0], w_gate_x[0])
    row2 = lambda a: a.reshape(1, -1)
    g_mix, g_ffn, g_fin, g_rnn = row2(norm_mix_g[0]), row2(norm_ffn_g[0]), row2(norm_final_g), row2(gn_rnn_g[0])
    cw_r, cb_r = rnn_conv_w[0], row2(rnn_conv_b[0])
    ba, bx, lam = row2(b_gate_a[0]), row2(b_gate_x[0]), row2(rnn_lambda[0])
    cw_f, cb_f = ffn_conv_w[0], row2(ffn_conv_b[0])
    tbl = rel_bias_table.reshape(-1)
    sinks = attn_sinks[0]

    blk = WINDOW
    qi = np.arange(blk)[:, None]
    kj = np.arange(2 * blk)[None, :]
    code_p = jnp.asarray(_t5_bucket_np(blk + qi - kj))
    r = np.arange(N_HEADS * steps)[:, None]
    t_r, h_r = r % steps, r // steps
    npad = 2 * SUBLANES
    code_c = jnp.asarray(_t5_bucket_np(win + t_r - np.arange(win)[None, :]) * N_HEADS + h_r)
    code_n = jnp.asarray(_t5_bucket_np(t_r - np.arange(npad)[None, :]) * N_HEADS + h_r)

    n_p = batch * seq
    tm_a = _pick(PROJ_ROWS, n_p)
    xp2 = x_prompt.reshape(n_p, d_model)
    xr, gr, q, k, v = _inproj(xp2, pl.BlockSpec((tm_a, d_model), lambda i: (i, 0)), n_p // tm_a, tm_a,
                              g_mix, w_in_b, d_rnn, d_attn, kv_dim)
    m_rnn, h_last = _rglru_prompt(xr, gr, cw_r, cb_r, wg, ba, bx, lam, g_rnn, batch, seq,
                                  _pick(RGLRU_ROWS, seq))
    m_attn = _swa_prompt(q, k, v, code_p, tbl, sinks, gn_attn_p, batch, seq)
    y_p, tail_g, tail_v = _ffn_prompt(xp2, m_rnn, m_attn, w_out_r, w_out_a, g_ffn, w_up_b, cw_f, cb_f,
                                      w_down_b, g_fin, batch, seq, _pick(FFN_ROWS, seq),
                                      _pick(FFN_COLS_PROMPT, d_ff))
    y_prompt = y_p.reshape(batch, seq, d_model)
    p_rnn_conv = xr.reshape(batch, seq, d_rnn)[:, seq - 3:, :][None]
    p_rnn_h = h_last[None]
    last_win = lambda a: a.reshape(batch, seq, kv_dim)[:, seq - win:, :].reshape(
        1, batch, win, KV_HEADS, HEAD_DIM)
    p_win_k, p_win_v = last_win(k), last_win(v)
    p_ffn_conv = jnp.concatenate([tail_g[:, SUBLANES - 2:, :], tail_v[:, SUBLANES - 2:, :]], axis=-1)[None]

    n_s = nb * steps
    xs_tm = x_sample.transpose(1, 0, 2).reshape(n_s, d_model)
    xr_s, gr_s, q_s, k_s, v_s = _inproj(xs_tm, pl.BlockSpec((nb, d_model), lambda t: (t, 0)), steps, nb,
                                        g_mix, w_in_b, d_rnn, d_attn, kv_dim)
    conv_tm = state_rnn_conv[0].transpose(1, 0, 2).reshape(3 * nb, d_rnn)
    m_rnn_s, h_new = _rglru_sample(xr_s, gr_s, conv_tm, state_rnn_h[0], cw_r, cb_r, wg, ba, bx, lam, g_rnn,
                                   nb, steps)
    ncg = KV_HEADS // 2
    qs = q_s.reshape(steps, nb, ncg, GQA_GROUP, LANES).transpose(1, 2, 3, 0, 4).reshape(
        nb, ncg, GQA_GROUP * steps, LANES)
    k_tm = k_s.reshape(steps, nb, kv_dim)
    v_tm = v_s.reshape(steps, nb, kv_dim)
    pad = ((0, 0), (0, npad - steps), (0, 0))
    ck_t = cache_win_k[0].transpose(0, 2, 3, 1).reshape(nb, kv_dim, win)
    cv_t = cache_win_v[0].transpose(0, 2, 3, 1).reshape(nb, kv_dim, win)
    bb = SUBLANES
    o_s, ck_new, cv_new = _swa_sample(qs, ck_t, cv_t, jnp.pad(k_tm.transpose(1, 0, 2), pad),
                                      jnp.pad(v_tm.transpose(1, 0, 2), pad), code_c, code_n, tbl, sinks, steps, bb)
    ya_s = o_s.reshape(nb, ncg, GQA_GROUP, steps, LANES).transpose(3, 0, 1, 2, 4).reshape(n_s, d_attn)
    ffn2d = state_ffn_conv[0].reshape(nb, 2 * 2 * d_ff)
    y_s, ns_g, ns_v = _ffn_sample(xs_tm, m_rnn_s, ya_s, gn_attn_p, w_out_r, w_out_a, g_ffn, w_up_b, cw_f, cb_f,
                                  w_down_b, g_fin, ffn2d, nb, steps, _pick(FFN_COLS_SAMPLE, d_ff))
    y_sample = y_s.reshape(steps, nb, d_model).transpose(1, 0, 2)
    s_rnn_conv = xr_s.reshape(steps, nb, d_rnn)[steps - 3:].transpose(1, 0, 2)[None]
    s_rnn_h = h_new[None]

    to_cache = lambda w_t: w_t.reshape(nb, KV_HEADS, HEAD_DIM, win).transpose(0, 3, 1, 2)[None]
    s_win_k, s_win_v = to_cache(ck_new), to_cache(cv_new)
    s_ffn_conv = jnp.concatenate([ns_g, ns_v], axis=-1).transpose(1, 0, 2)[None]

    return (y_prompt, y_sample, p_rnn_conv, p_rnn_h, p_win_k, p_win_v, p_ffn_conv,
            s_rnn_conv, s_rnn_h, s_win_k, s_win_v, s_ffn_conv)
```
